```python
import math
import jax, jax.numpy as jnp
from jax import lax
import numpy as np

D_MODEL = 2048
BATCH = 4
SEQ = 2048
DEPTH = 1
DEC_BATCH = 128
DEC_SEQ = 8
PAST_LEN = 2048
PAGE_SIZE = 128

HEAD_DIM = 64
ATTN_WIDTH = D_MODEL // 2
N_HEADS = ATTN_WIDTH // HEAD_DIM
KV_HEADS = 4
Q_PER_KV = N_HEADS // KV_HEADS
CMP_STRIDE = 16
CMP_BLOCK = 2 * CMP_STRIDE
CMP_HIDDEN = 2 * HEAD_DIM
SEL_BLOCK = 64
N_SELECT = 8
WINDOW = 512
QUERY_BLOCK = 128
N_KV_SLOTS = 6
N_CACHE_SLOTS = 4
SSM_INNER = D_MODEL - ATTN_WIDTH
SSM_HEADDIM = 64
SSM_HEADS = SSM_INNER // SSM_HEADDIM
SSM_GROUPS = 4
SSM_HPG = SSM_HEADS // SSM_GROUPS
SSM_STATE = 128
SSM_CONV = 4
SSD_CHUNK = 128
CONV_DIM = SSM_INNER + 2 * SSM_GROUPS * SSM_STATE
MIX_WIDTH = ATTN_WIDTH + SSM_INNER
IN_WIDTH = ATTN_WIDTH + N_KV_SLOTS * KV_HEADS * HEAD_DIM + 3 * N_HEADS + SSM_INNER + CONV_DIM + SSM_HEADS
D_FF = ((8 * D_MODEL // 3 + 127) // 128) * 128
FFN_CONV = 3
REL_BUCKETS = 32
REL_MAX_DIST = 128
EPS = 1e-6
NEG = -1e30
TINY = 1e-30
FORCE_BONUS = 1e4
SCALE = HEAD_DIM ** -0.5

kernel_name = 'hybrid_nsa_ssd_convffn_step'


def rmsnorm(x, g):
    xf = x.astype(jnp.float32)
    y = xf * lax.rsqrt(jnp.mean(xf * xf, -1, keepdims=True) + EPS)
    return (y * g.astype(jnp.float32)).astype(x.dtype)


def masked_softmax(s, mask):
    m = jnp.max(jnp.where(mask, s, NEG), -1, keepdims=True)
    e = jnp.where(mask, jnp.exp(jnp.where(mask, s, m) - m), 0.0)
    return e / jnp.maximum(jnp.sum(e, -1, keepdims=True), TINY)


def rel_bucket(dist):
    n = jnp.maximum(dist, 0)
    max_exact = REL_BUCKETS // 2
    nf = jnp.maximum(n, 1).astype(jnp.float32)
    large = max_exact + (jnp.log(nf / max_exact) / math.log(REL_MAX_DIST / max_exact)
                         * (REL_BUCKETS - max_exact)).astype(jnp.int32)
    large = jnp.minimum(large, REL_BUCKETS - 1)
    return jnp.where(n < max_exact, n, large)


def causal_dwconv(x_full, w, b):
    width = w.shape[0]
    t = x_full.shape[1] - width + 1
    acc = b
    for k in range(width):
        acc = acc + x_full[:, k:k + t] * w[k]
    return acc


def project(x, g_pre, w_in):
    b, t = x.shape[:2]
    p = rmsnorm(x, g_pre) @ w_in
    o1 = ATTN_WIDTH
    o2 = o1 + N_KV_SLOTS * KV_HEADS * HEAD_DIM
    o3 = o2 + 3 * N_HEADS
    o4 = o3 + SSM_INNER
    o5 = o4 + CONV_DIM
    q, kv, gate, z, xbc, dt = jnp.split(p, [o1, o2, o3, o4, o5], axis=-1)
    q = q.reshape(b, t, KV_HEADS, Q_PER_KV, HEAD_DIM)
    kv = kv.reshape(b, t, N_KV_SLOTS, KV_HEADS, HEAD_DIM)
    gate = jax.nn.sigmoid(gate.astype(jnp.float32)).reshape(b, t, 3, KV_HEADS, Q_PER_KV)
    return q, kv, gate, z, xbc, dt


def compress(rows, pe, w1, b1, w2, b2, n_blk):
    b, _, g, d = rows.shape
    ch = rows[:, :(n_blk + 1) * CMP_STRIDE].reshape(b, n_blk + 1, CMP_STRIDE, g, d)
    pe2 = pe.reshape(2, CMP_STRIDE, 1, d)
    w12 = w1.reshape(2, CMP_STRIDE, d, CMP_HIDDEN)
    hid = (jnp.einsum('bnsgd,sdh->bngh', ch[:, :-1] + pe2[0], w12[0])
           + jnp.einsum('bnsgd,sdh->bngh', ch[:, 1:] + pe2[1], w12[1]) + b1)
    return jnp.einsum('bngh,hd->bngd', jax.nn.gelu(hid), w2) + b2


def overlap_map(n_cmp, n_sel):
    c0 = jnp.arange(n_cmp) * CMP_STRIDE
    s0 = jnp.arange(n_sel) * SEL_BLOCK
    ov = jnp.minimum(c0[:, None] + CMP_BLOCK, s0[None, :] + SEL_BLOCK) - jnp.maximum(c0[:, None], s0[None, :])
    return jnp.clip(ov, 0).astype(jnp.float32) / CMP_BLOCK


def gather_blocks(rows, idx):
    b, t, g, d = rows.shape
    kb = rows.reshape(b, t // SEL_BLOCK, SEL_BLOCK, g, d).transpose(0, 3, 1, 2, 4)
    bi = jnp.arange(b)[:, None, None, None]
    gi = jnp.arange(g)[None, None, :, None]
    return kb[bi, gi, idx]


def pad_rows(rows, n):
    return jnp.pad(rows, ((0, 0), (0, n - rows.shape[1]), (0, 0), (0, 0)))


def nsa_block(q, gate, q_pos, kc, vc, c_end, imp_map, k_sel, v_sel, kw, vw, kw_pos, rel_table):
    b, tq, g, r, d = q.shape
    table = rel_table.astype(jnp.float32).reshape(REL_BUCKETS, g, r)
    dist_c = q_pos[:, None] - c_end[None, :]
    bias_c = table[rel_bucket(dist_c)].transpose(0, 2, 3, 1)[None]
    s_c = jnp.einsum('btgrd,bngd->btgrn', q, kc).astype(jnp.float32) * SCALE + bias_c
    p_c = masked_softmax(s_c, (dist_c >= 0)[None, :, None, None, :])
    o_c = jnp.einsum('btgrn,bngd->btgrd', p_c.astype(vc.dtype), vc).astype(jnp.float32)
    ns = imp_map.shape[1]
    imp = jnp.einsum('btgrn,ns->btgs', p_c, imp_map)
    blk = jnp.arange(ns)
    q_blk = q_pos // SEL_BLOCK
    valid = (blk[None, :] <= q_blk[:, None])[None, :, None, :]
    forced = ((blk[None, :] == 0) | (blk[None, :] == q_blk[:, None]) | (blk[None, :] == q_blk[:, None] - 1))
    score = jnp.where(valid, imp + jnp.where(forced, FORCE_BONUS, 0.0)[None, :, None, :], -jnp.inf)
    k_top = min(N_SELECT, ns)
    _, idx = lax.top_k(score, k_top)
    idx_valid = idx <= q_blk[None, :, None, None]
    ks = gather_blocks(k_sel, idx)
    vs = gather_blocks(v_sel, idx)
    kpos = idx[..., None] * SEL_BLOCK + jnp.arange(SEL_BLOCK)
    dist_s = q_pos[None, :, None, None, None] - kpos
    mask_s = (idx_valid[..., None] & (dist_s >= 0)).reshape(b, tq, g, 1, k_top * SEL_BLOCK)
    bias_s = jnp.moveaxis(table[rel_bucket(dist_s), jnp.arange(g)[None, None, :, None, None]], -1, 3)
    s_s = jnp.einsum('btgrd,btgksd->btgrks', q, ks).astype(jnp.float32) * SCALE + bias_s
    p_s = masked_softmax(s_s.reshape(b, tq, g, r, k_top * SEL_BLOCK), mask_s)
    o_s = jnp.einsum('btgrm,btgmd->btgrd', p_s.astype(vs.dtype),
                     vs.reshape(b, tq, g, k_top * SEL_BLOCK, d)).astype(jnp.float32)
    dist_w = q_pos[:, None] - kw_pos[None, :]
    mask_w = (dist_w >= 0) & (dist_w <= WINDOW) & (kw_pos[None, :] >= 0)
    bias_w = table[rel_bucket(dist_w)].transpose(0, 2, 3, 1)[None]
    s_w = jnp.einsum('btgrd,bkgd->btgrk', q, kw).astype(jnp.float32) * SCALE + bias_w
    p_w = masked_softmax(s_w, mask_w[None, :, None, None, :])
    o_w = jnp.einsum('btgrk,bkgd->btgrd', p_w.astype(vw.dtype), vw).astype(jnp.float32)
    o = (gate[:, :, 0, :, :, None] * o_c + gate[:, :, 1, :, :, None] * o_s
         + gate[:, :, 2, :, :, None] * o_w)
    return o.astype(q.dtype).reshape(b, tq, g * r * d)


def nsa_prompt(q, kv, gate, cmp_pe, cmp_w1, cmp_b1, cmp_w2, cmp_b2, rel_table):
    b, t = q.shape[:2]
    n_cmp = (t - CMP_BLOCK) // CMP_STRIDE + 1
    kc = compress(kv[:, :, 0], cmp_pe[0], cmp_w1[0], cmp_b1[0], cmp_w2[0], cmp_b2[0], n_cmp)
    vc = compress(kv[:, :, 1], cmp_pe[1], cmp_w1[1], cmp_b1[1], cmp_w2[1], cmp_b2[1], n_cmp)
    c_end = jnp.arange(n_cmp) * CMP_STRIDE + CMP_BLOCK - 1
    n_sel = -(-t // SEL_BLOCK)
    imp_map = overlap_map(n_cmp, n_sel)
    k_sel = pad_rows(kv[:, :, 2], n_sel * SEL_BLOCK)
    v_sel = pad_rows(kv[:, :, 3], n_sel * SEL_BLOCK)
    pad_w = ((0, 0), (WINDOW, 0), (0, 0), (0, 0))
    kw_pad = jnp.pad(kv[:, :, 4], pad_w)
    vw_pad = jnp.pad(kv[:, :, 5], pad_w)

    def one_block(i):
        s = i * QUERY_BLOCK
        qb = lax.dynamic_slice_in_dim(q, s, QUERY_BLOCK, 1)
        gb = lax.dynamic_slice_in_dim(gate, s, QUERY_BLOCK, 1)
        q_pos = s + jnp.arange(QUERY_BLOCK)
        kwb = lax.dynamic_slice_in_dim(kw_pad, s, WINDOW + QUERY_BLOCK, 1)
        vwb = lax.dynamic_slice_in_dim(vw_pad, s, WINDOW + QUERY_BLOCK, 1)
        kw_pos = s - WINDOW + jnp.arange(WINDOW + QUERY_BLOCK)
        return nsa_block(qb, gb, q_pos, kc, vc, c_end, imp_map, k_sel, v_sel, kwb, vwb, kw_pos, rel_table)

    out = lax.map(one_block, jnp.arange(t // QUERY_BLOCK))
    return jnp.moveaxis(out, 0, 1).reshape(b, t, ATTN_WIDTH)


def nsa_sample(q, kv, gate, cache_kv_l, win_l, page_table, cmp_pe, cmp_w1, cmp_b1, cmp_w2, cmp_b2, rel_table):
    b, tn = q.shape[:2]
    n_pages = page_table.shape[1]
    past_len = n_pages * PAGE_SIZE
    past = cache_kv_l[page_table].reshape(b, past_len, N_CACHE_SLOTS, KV_HEADS, HEAD_DIM)
    full = jnp.concatenate([past, kv[:, :, :N_CACHE_SLOTS].astype(past.dtype)], 1)
    t_all = past_len + tn
    n_cmp = (t_all - CMP_BLOCK) // CMP_STRIDE + 1
    kc = compress(full[:, :, 0], cmp_pe[0], cmp_w1[0], cmp_b1[0], cmp_w2[0], cmp_b2[0], n_cmp)
    vc = compress(full[:, :, 1], cmp_pe[1], cmp_w1[1], cmp_b1[1], cmp_w2[1], cmp_b2[1], n_cmp)
    c_end = jnp.arange(n_cmp) * CMP_STRIDE + CMP_BLOCK - 1
    n_sel = -(-t_all // SEL_BLOCK)
    imp_map = overlap_map(n_cmp, n_sel)
    k_sel = pad_rows(full[:, :, 2], n_sel * SEL_BLOCK)
    v_sel = pad_rows(full[:, :, 3], n_sel * SEL_BLOCK)
    win_len = win_l.shape[1]
    win_all = jnp.concatenate([win_l, kv[:, :, N_CACHE_SLOTS:].astype(win_l.dtype)], 1)
    kw_pos = past_len - win_len + jnp.arange(win_len + tn)
    q_pos = past_len + jnp.arange(tn)
    o = nsa_block(q, gate, q_pos, kc, vc, c_end, imp_map, k_sel, v_sel,
                  win_all[:, :, 0], win_all[:, :, 1], kw_pos, rel_table)
    return o, win_all[:, -win_len:]


def ssd_scan(x, a, bm, cm, h0):
    b, t = x.shape[:2]
    q = min(SSD_CHUNK, t)
    n_ch = -(-t // q)
    pad = n_ch * q - t

    def chunks(v):
        v = jnp.pad(v, [(0, 0), (0, pad)] + [(0, 0)] * (v.ndim - 2))
        return jnp.moveaxis(v.reshape((b, n_ch, q) + v.shape[2:]), 1, 0)

    causal = jnp.tril(jnp.ones((q, q), bool))[None, :, :, None, None]

    def step(h, inp):
        xc, ac, bc, cc = inp
        acum = jnp.cumsum(ac, axis=1)
        seg = acum[:, :, None] - acum[:, None]
        decay = jnp.where(causal, jnp.exp(jnp.where(causal, seg, 0.0)), 0.0)
        cb = jnp.einsum('bign,bjgn->bijg', cc, bc)
        y = jnp.einsum('bijg,bijgr,bjgrp->bigrp', cb, decay, xc)
        y = y + jnp.einsum('bign,bgrpn->bigrp', cc, h) * jnp.exp(acum)[..., None]
        to_end = jnp.exp(acum[:, -1:] - acum)
        h = (h * jnp.exp(acum[:, -1])[..., None, None]
             + jnp.einsum('bjgn,bjgrp->bgrpn', bc, xc * to_end[..., None]))
        return h, y

    h_t, ys = lax.scan(step, h0, (chunks(x), chunks(a), chunks(bm), chunks(cm)))
    y = jnp.moveaxis(ys, 0, 1).reshape((b, n_ch * q) + x.shape[2:])[:, :t]
    return y, h_t


def ssd_mixer(z, xbc, dt_raw, conv_hist, h0, conv_w, conv_b, dt_bias, a_log, d_skip, norm_g):
    b, t = z.shape[:2]
    f32 = jnp.float32
    xbc_full = jnp.concatenate([conv_hist.astype(xbc.dtype), xbc], 1)
    xbc_c = jax.nn.silu(causal_dwconv(xbc_full, conv_w, conv_b))
    xs, bm, cm = jnp.split(xbc_c, [SSM_INNER, SSM_INNER + SSM_GROUPS * SSM_STATE], axis=-1)
    xs = xs.astype(f32).reshape(b, t, SSM_GROUPS, SSM_HPG, SSM_HEADDIM)
    bm = bm.astype(f32).reshape(b, t, SSM_GROUPS, SSM_STATE)
    cm = cm.astype(f32).reshape(b, t, SSM_GROUPS, SSM_STATE)
    dt = jax.nn.softplus(dt_raw.astype(f32) + dt_bias.astype(f32)).reshape(b, t, SSM_GROUPS, SSM_HPG)
    a = -jnp.exp(a_log.astype(f32)).reshape(SSM_GROUPS, SSM_HPG)
    h0g = h0.astype(f32).reshape(b, SSM_GROUPS, SSM_HPG, SSM_HEADDIM, SSM_STATE)
    y, h_t = ssd_scan(xs * dt[..., None], dt * a, bm, cm, h0g)
    y = y + xs * d_skip.astype(f32).reshape(SSM_GROUPS, SSM_HPG, 1)
    y = y.reshape(b, t, SSM_INNER) * jax.nn.silu(z.astype(f32))
    yg = y.reshape(b, t, SSM_GROUPS, SSM_INNER // SSM_GROUPS)
    yg = yg * lax.rsqrt(jnp.mean(yg * yg, -1, keepdims=True) + EPS)
    y = yg.reshape(b, t, SSM_INNER) * norm_g.astype(f32)
    new_h = h_t.reshape(b, SSM_HEADS, SSM_HEADDIM, SSM_STATE).astype(h0.dtype)
    return y.astype(z.dtype), xbc_full[:, -(SSM_CONV - 1):], new_h


def block_tail(x, attn, ssm, ffn_hist, w_out, g_mix_post, g_ffn_pre, w_gate, w_up, dw_w, dw_b, w_down, g_ffn_post):
    mix = jnp.concatenate([attn, ssm], -1) @ w_out
    x = x + rmsnorm(mix, g_mix_post)
    h = rmsnorm(x, g_ffn_pre)
    gp = h @ w_gate
    full = jnp.concatenate([ffn_hist.astype(gp.dtype), gp], 1)
    act = jax.nn.gelu(causal_dwconv(full, dw_w, dw_b), approximate=True) * (h @ w_up)
    x = x + rmsnorm(act @ w_down, g_ffn_post)
    return x, full[:, -(FFN_CONV - 1):]


def setup_inputs(seed: int = 0) -> dict:
    key = jax.random.key(seed)
    ks = iter(jax.random.split(key, 48))
    f32 = jnp.float32

    def nrm(shape, scale):
        return jax.random.normal(next(ks), shape, f32) * scale

    def gain(shape):
        return 1.0 + nrm(shape, 0.05)

    n_pages = PAST_LEN // PAGE_SIZE
    n_pool = (DEC_BATCH * n_pages * 5) // 4
    win_len = min(WINDOW, PAST_LEN)
    page_table = jax.random.permutation(next(ks), n_pool)[:DEC_BATCH * n_pages]
    page_table = page_table.reshape(DEC_BATCH, n_pages).astype(jnp.int32)
    dt0 = jnp.exp(jax.random.uniform(next(ks), (DEPTH, SSM_HEADS), f32, math.log(1e-3), math.log(1e-1)))
    dt_bias = dt0 + jnp.log(-jnp.expm1(-dt0))
    a_log = jnp.log(jax.random.uniform(next(ks), (DEPTH, SSM_HEADS), f32, 1.0, 16.0))
    return {
        'x_prompt': nrm((BATCH, SEQ, D_MODEL), 1.0),
        'x_sample': nrm((DEC_BATCH, DEC_SEQ, D_MODEL), 1.0),
        'cache_kv': nrm((DEPTH, n_pool, PAGE_SIZE, N_CACHE_SLOTS, KV_HEADS, HEAD_DIM), 1.0),
        'cache_win_kv': nrm((DEPTH, DEC_BATCH, win_len, 2, KV_HEADS, HEAD_DIM), 1.0),
        'state_ssm_conv': nrm((DEPTH, DEC_BATCH, SSM_CONV - 1, CONV_DIM), 1.0),
        'state_ssm': nrm((DEPTH, DEC_BATCH, SSM_HEADS, SSM_HEADDIM, SSM_STATE), 0.5),
        'state_ffn_conv': nrm((DEPTH, DEC_BATCH, FFN_CONV - 1, D_FF), 1.0),
        'page_table': page_table,
        'rel_table': nrm((REL_BUCKETS, N_HEADS), 0.5),
        'ln_mix_pre': gain((DEPTH, D_MODEL)),
        'w_in': nrm((DEPTH, D_MODEL, IN_WIDTH), D_MODEL ** -0.5),
        'cmp_pe': nrm((DEPTH, 2, CMP_BLOCK, HEAD_DIM), 0.1),
        'cmp_w1': nrm((DEPTH, 2, CMP_BLOCK, HEAD_DIM, CMP_HIDDEN), (CMP_BLOCK * HEAD_DIM) ** -0.5),
        'cmp_b1': nrm((DEPTH, 2, CMP_HIDDEN), 0.01),
        'cmp_w2': nrm((DEPTH, 2, CMP_HIDDEN, HEAD_DIM), CMP_HIDDEN ** -0.5),
        'cmp_b2': nrm((DEPTH, 2, HEAD_DIM), 0.01),
        'ssm_conv_w': nrm((DEPTH, SSM_CONV, CONV_DIM), SSM_CONV ** -0.5),
        'ssm_conv_b': nrm((DEPTH, CONV_DIM), 0.01),
        'ssm_dt_bias': dt_bias,
        'ssm_a_log': a_log,
        'ssm_d': gain((DEPTH, SSM_HEADS)),
        'ssm_norm_g': gain((DEPTH, SSM_INNER)),
        'w_out': nrm((DEPTH, MIX_WIDTH, D_MODEL), MIX_WIDTH ** -0.5),
        'ln_mix_post': gain((DEPTH, D_MODEL)),
        'ln_ffn_pre': gain((DEPTH, D_MODEL)),
        'ffn_w_gate': nrm((DEPTH, D_MODEL, D_FF), D_MODEL ** -0.5),
        'ffn_w_up': nrm((DEPTH, D_MODEL, D_FF), D_MODEL ** -0.5),
        'ffn_dw_w': nrm((DEPTH, FFN_CONV, D_FF), FFN_CONV ** -0.5),
        'ffn_dw_b': nrm((DEPTH, D_FF), 0.01),
        'ffn_w_down': nrm((DEPTH, D_FF, D_MODEL), D_FF ** -0.5),
        'ln_ffn_post': gain((DEPTH, D_MODEL)),
    }


def reference(x_prompt, x_sample, cache_kv, cache_win_kv, state_ssm_conv, state_ssm, state_ffn_conv,
              page_table, rel_table, ln_mix_pre, w_in, cmp_pe, cmp_w1, cmp_b1, cmp_w2, cmp_b2,
              ssm_conv_w, ssm_conv_b, ssm_dt_bias, ssm_a_log, ssm_d, ssm_norm_g, w_out, ln_mix_post,
              ln_ffn_pre, ffn_w_gate, ffn_w_up, ffn_dw_w, ffn_dw_b, ffn_w_down, ln_ffn_post):
    xp, xd = x_prompt, x_sample
    bp, tp = xp.shape[:2]
    kv_p, kv_d, win_p, win_d = [], [], [], []
    sc_p, sc_d, ss_p, ss_d, fc_p, fc_d = [], [], [], [], [], []
    for l in range(DEPTH):
        q, kv, gate, z, xbc, dt = project(xp, ln_mix_pre[l], w_in[l])
        attn = nsa_prompt(q, kv, gate, cmp_pe[l], cmp_w1[l], cmp_b1[l], cmp_w2[l], cmp_b2[l], rel_table)
        ssm, conv_new, h_new = ssd_mixer(
            z, xbc, dt, jnp.zeros((bp, SSM_CONV - 1, CONV_DIM), xp.dtype),
            jnp.zeros((bp, SSM_HEADS, SSM_HEADDIM, SSM_STATE), xp.dtype),
            ssm_conv_w[l], ssm_conv_b[l], ssm_dt_bias[l], ssm_a_log[l], ssm_d[l], ssm_norm_g[l])
        xp, ffn_new = block_tail(xp, attn, ssm, jnp.zeros((bp, FFN_CONV - 1, D_FF), xp.dtype),
                                 w_out[l], ln_mix_post[l], ln_ffn_pre[l], ffn_w_gate[l], ffn_w_up[l],
                                 ffn_dw_w[l], ffn_dw_b[l], ffn_w_down[l], ln_ffn_post[l])
        kv_p.append(kv[:, :, :N_CACHE_SLOTS])
        win_p.append(kv[:, tp - min(WINDOW, tp):, N_CACHE_SLOTS:])
        sc_p.append(conv_new)
        ss_p.append(h_new)
        fc_p.append(ffn_new)
        q, kv, gate, z, xbc, dt = project(xd, ln_mix_pre[l], w_in[l])
        attn, win_new = nsa_sample(q, kv, gate, cache_kv[l], cache_win_kv[l], page_table,
                                   cmp_pe[l], cmp_w1[l], cmp_b1[l], cmp_w2[l], cmp_b2[l], rel_table)
        ssm, conv_new, h_new = ssd_mixer(
            z, xbc, dt, state_ssm_conv[l], state_ssm[l],
            ssm_conv_w[l], ssm_conv_b[l], ssm_dt_bias[l], ssm_a_log[l], ssm_d[l], ssm_norm_g[l])
        xd, ffn_new = block_tail(xd, attn, ssm, state_ffn_conv[l],
                                 w_out[l], ln_mix_post[l], ln_ffn_pre[l], ffn_w_gate[l], ffn_w_up[l],
                                 ffn_dw_w[l], ffn_dw_b[l], ffn_w_down[l], ln_ffn_post[l])
        kv_d.append(kv[:, :, :N_CACHE_SLOTS])
        win_d.append(win_new)
        sc_d.append(conv_new)
        ss_d.append(h_new)
        fc_d.append(ffn_new)
    new_kv_p = jnp.stack(kv_p)
    new_kv_d = jnp.stack(kv_d)
    new_win_p = jnp.stack(win_p)
    new_win_d = jnp.stack(win_d)
    new_conv_p = jnp.stack(sc_p)
    new_conv_d = jnp.stack(sc_d)
    new_ssm_p = jnp.stack(ss_p)
    new_ssm_d = jnp.stack(ss_d)
    new_ffn_p = jnp.stack(fc_p)
    new_ffn_d = jnp.stack(fc_d)
    return (xp, xd, new_kv_p, new_kv_d, new_win_p, new_win_d, new_conv_p, new_conv_d,
            new_ssm_p, new_ssm_d, new_ffn_p, new_ffn_d)
```

```python
import functools
import math

import numpy as np
import jax
import jax.numpy as jnp
from jax import lax
from jax.experimental import pallas as pl
from jax.experimental.pallas import tpu as pltpu

F32 = jnp.float32
BF16 = jnp.bfloat16

D_MODEL = 2048
PAGE_SIZE = 128
HEAD_DIM = 64
ATTN_WIDTH = D_MODEL // 2
N_HEADS = ATTN_WIDTH // HEAD_DIM
KV_HEADS = 4
Q_PER_KV = N_HEADS // KV_HEADS
CMP_STRIDE = 16
CMP_BLOCK = 2 * CMP_STRIDE
CMP_HIDDEN = 2 * HEAD_DIM
SEL_BLOCK = 64
N_SELECT = 8
WINDOW = 512
N_KV_SLOTS = 6
N_CACHE_SLOTS = 4
KV_WIDTH = KV_HEADS * HEAD_DIM
SSM_INNER = D_MODEL - ATTN_WIDTH
SSM_HEADDIM = 64
SSM_HEADS = SSM_INNER // SSM_HEADDIM
SSM_GROUPS = 4
SSM_HPG = SSM_HEADS // SSM_GROUPS
SSM_STATE = 128
SSM_CONV = 4
CONV_DIM = SSM_INNER + 2 * SSM_GROUPS * SSM_STATE
D_FF = ((8 * D_MODEL // 3 + 127) // 128) * 128
FFN_CONV = 3
REL_BUCKETS = 32
REL_MAX_DIST = 128
EPS = 1e-6
NEG = -1e30
TINY = 1e-30
FORCE_BONUS = 1e4
SCALE = HEAD_DIM ** -0.5

LANE = 128
SUBLANE = 8
TILE = 128
VMEM_LIMIT = 56 * 1024 * 1024

P_Q = 0
P_Z = ATTN_WIDTH
P_XBC = P_Z + SSM_INNER
P_KV = P_XBC + CONV_DIM
P_GD = P_KV + N_KV_SLOTS * KV_WIDTH
GD_WIDTH = 3 * N_HEADS + SSM_HEADS
P_WIDTH = 6144
DT_COL = 3 * N_HEADS
FF_PAD = 5632


def _cparams(*sem):
    return pltpu.CompilerParams(dimension_semantics=sem, vmem_limit_bytes=VMEM_LIMIT)


def _rms(x, g):
    return x * lax.rsqrt(jnp.mean(x * x, axis=-1, keepdims=True) + EPS) * g


def _split3(x):
    hi = x.astype(BF16)
    r1 = x - hi.astype(F32)
    mid = r1.astype(BF16)
    lo = (r1 - mid.astype(F32)).astype(BF16)
    return hi, mid, lo


def _dot(a, b):
    return jnp.dot(a, b, preferred_element_type=F32)


def _dot_nt(a, b):
    return lax.dot_general(a, b, (((1,), (1,)), ((), ())), preferred_element_type=F32)


def _dot_exact_rhs(x, sel):
    hi, mid, lo = _split3(x)
    return _dot(hi, sel) + _dot(mid, sel) + _dot(lo, sel)


def _dot_exact_lhs(sel, x):
    hi, mid, lo = _split3(x)
    return _dot(sel, hi) + _dot(sel, mid) + _dot(sel, lo)


def _gelu_tanh(x):
    return 0.5 * x * (1.0 + jnp.tanh(math.sqrt(2.0 / math.pi) * (x + 0.044715 * (x * x * x))))


def _silu(x):
    return x * jax.nn.sigmoid(x)


def _proj_body(x_ref, g_ref, w_ref, o_ref, xn_ref):
    @pl.when(pl.program_id(1) == 0)
    def _():
        xn_ref[...] = _rms(x_ref[...], g_ref[...]).astype(BF16)

    o_ref[...] = _dot(xn_ref[...], w_ref[...])


def _proj(x2d, g, w, tm, tn=512):
    m, k = x2d.shape
    n = w.shape[1]
    return pl.pallas_call(
        _proj_body,
        grid=(m // tm, n // tn),
        in_specs=[pl.BlockSpec((tm, k), lambda i, j: (i, 0)),
                  pl.BlockSpec((1, k), lambda i, j: (0, 0)),
                  pl.BlockSpec((k, tn), lambda i, j: (0, j))],
        out_specs=pl.BlockSpec((tm, tn), lambda i, j: (i, j)),
        out_shape=jax.ShapeDtypeStruct((m, n), F32),
        scratch_shapes=[pltpu.VMEM((tm, k), BF16)],
        compiler_params=_cparams("parallel", "arbitrary"),
        name="proj",
    )(x2d, g, w)


def _outproj_body(a_ref, s_ref, x_ref, wa_ref, ws_ref, g1_ref, g2_ref, x1_ref, hn_ref):
    mix = _dot(a_ref[...], wa_ref[...]) + _dot(s_ref[...], ws_ref[...])
    x1 = x_ref[...] + _rms(mix, g1_ref[...])
    x1_ref[...] = x1
    hn_ref[...] = _rms(x1, g2_ref[...]).astype(BF16)


def _outproj(attn, ssm, x2d, w_attn, w_ssm, g_post, g_pre2, tm=512):
    m, d = x2d.shape
    ka = attn.shape[1]
    const = lambda i: (0, 0)
    return pl.pallas_call(
        _outproj_body,
        grid=(m // tm,),
        in_specs=[pl.BlockSpec((tm, ka), lambda i: (i, 0)),
                  pl.BlockSpec((tm, ka), lambda i: (i, 0)),
                  pl.BlockSpec((tm, d), lambda i: (i, 0)),
                  pl.BlockSpec((ka, d), const),
                  pl.BlockSpec((ka, d), const),
                  pl.BlockSpec((1, d), const),
                  pl.BlockSpec((1, d), const)],
        out_specs=[pl.BlockSpec((tm, d), lambda i: (i, 0)),
                   pl.BlockSpec((tm, d), lambda i: (i, 0))],
        out_shape=[jax.ShapeDtypeStruct((m, d), F32), jax.ShapeDtypeStruct((m, d), BF16)],
        compiler_params=_cparams("parallel"),
        name="outproj",
    )(attn, ssm, x2d, w_attn, w_ssm, g_post, g_pre2)


def _ffn_up_body(*refs, seq_len, carry):
    if carry:
        h_ref, wg_ref, wu_ref, dw_ref, db_ref, act_ref, tail_ref, prev_ref = refs
    else:
        h_ref, wg_ref, wu_ref, dw_ref, db_ref, prev_ref, act_ref, tail_ref = refs
    tm = h_ref.shape[0]
    h = h_ref[...]
    gp = _dot(h, wg_ref[...])
    up = _dot(h, wu_ref[...])
    if carry:
        @pl.when(pl.program_id(2) == 0)
        def _():
            prev_ref[...] = jnp.zeros_like(prev_ref)
    prev = prev_ref[...]
    tpos = lax.broadcasted_iota(jnp.int32, (tm, 1), 0) % seq_len
    acc = db_ref[...] + dw_ref[FFN_CONV - 1:FFN_CONV, :] * gp
    for k in range(1, FFN_CONV):
        shifted = jnp.where(tpos >= k, pltpu.roll(gp, k, 0), pltpu.roll(prev, k, 0))
        acc = acc + dw_ref[FFN_CONV - 1 - k:FFN_CONV - k, :] * shifted
    act_ref[...] = (_gelu_tanh(acc) * up).astype(BF16)
    if carry:
        prev_ref[...] = gp
        tail_ref[...] = gp[tm - SUBLANE:, :]
    else:
        tail_ref[...] = gp


def _ffn_up(hn, wg, wu, dw, db, prev, *, n_groups, seq_len, tm, tn=512):
    m, d = hn.shape
    rows = m // n_groups
    ni = rows // tm
    nj = FF_PAD // tn
    carry = prev is None
    in_specs = [pl.BlockSpec((tm, d), lambda b, j, i: (b * ni + i, 0)),
                pl.BlockSpec((d, tn), lambda b, j, i: (0, j)),
                pl.BlockSpec((d, tn), lambda b, j, i: (0, j)),
                pl.BlockSpec((SUBLANE, tn), lambda b, j, i: (0, j)),
                pl.BlockSpec((1, tn), lambda b, j, i: (0, j))]
    args = [hn, wg, wu, dw, db]
    if carry:
        tail_shape = jax.ShapeDtypeStruct((n_groups, SUBLANE, FF_PAD), F32)
        tail_spec = pl.BlockSpec((None, SUBLANE, tn), lambda b, j, i: (b, 0, j))
        scratch = [pltpu.VMEM((tm, tn), F32)]
    else:
        in_specs.append(pl.BlockSpec((tm, tn), lambda b, j, i: (b * ni + i, j)))
        args.append(prev)
        tail_shape = jax.ShapeDtypeStruct((m, FF_PAD), F32)
        tail_spec = pl.BlockSpec((tm, tn), lambda b, j, i: (b * ni + i, j))
        scratch = []
    return pl.pallas_call(
        functools.partial(_ffn_up_body, seq_len=seq_len, carry=carry),
        grid=(n_groups, nj, ni),
        in_specs=in_specs,
        out_specs=[pl.BlockSpec((tm, tn), lambda b, j, i: (b * ni + i, j)), tail_spec],
        out_shape=[jax.ShapeDtypeStruct((m, FF_PAD), BF16), tail_shape],
        scratch_shapes=scratch,
        compiler_params=_cparams("parallel", "parallel", "arbitrary"),
        name="ffn_up",
    )(*args)


def _ffn_down_body(a_ref, w_ref, x_ref, g_ref, o_ref, acc_ref):
    k = pl.program_id(1)

    @pl.when(k == 0)
    def _():
        acc_ref[...] = jnp.zeros_like(acc_ref)

    acc_ref[...] += _dot(a_ref[...], w_ref[...])

    @pl.when(k == pl.num_programs(1) - 1)
    def _():
        o_ref[...] = x_ref[...] + _rms(acc_ref[...], g_ref[...])


def _ffn_down(act, wd, x1, g, tm=1024, tk=512):
    m, d = x1.shape
    return pl.pallas_call(
        _ffn_down_body,
        grid=(m // tm, FF_PAD // tk),
        in_specs=[pl.BlockSpec((tm, tk), lambda i, k: (i, k)),
                  pl.BlockSpec((tk, d), lambda i, k: (k, 0)),
                  pl.BlockSpec((tm, d), lambda i, k: (i, 0)),
                  pl.BlockSpec((1, d), lambda i, k: (0, 0))],
        out_specs=pl.BlockSpec((tm, d), lambda i, k: (i, 0)),
        out_shape=jax.ShapeDtypeStruct((m, d), F32),
        scratch_shapes=[pltpu.VMEM((tm, d), F32)],
        compiler_params=_cparams("parallel", "arbitrary"),
        name="ffn_down",
    )(act, wd, x1, g)


N_CHUNK_TILES = 16
CHUNKS_PER_TILE = TILE // CMP_STRIDE
CMP_COL_BLOCKS = 2 * KV_WIDTH // LANE


def _compress_body(*refs):
    pt_ref = refs[0]
    del pt_ref
    n_in = N_CHUNK_TILES * CMP_COL_BLOCKS
    tiles = refs[1:1 + n_in]
    pe_ref, w1_ref, b1_ref, w2_ref, b2_ref, kc_ref, vc_ref, lhs_a, lhs_b = refs[1 + n_in:]
    out_refs = (kc_ref, vc_ref)
    for slot in range(2):
        outs = []
        for gp in range(KV_HEADS // 2):
            cb = slot * (KV_HEADS // 2) + gp
            for s in range(CMP_STRIDE):
                x = jnp.concatenate(
                    [tiles[j * CMP_COL_BLOCKS + cb][pl.ds(s, CHUNKS_PER_TILE, stride=CMP_STRIDE), :]
                     for j in range(N_CHUNK_TILES)], axis=0)
                lhs_a[:, s * LANE:(s + 1) * LANE] = (x + pe_ref[slot, 0, s:s + 1, :]).astype(BF16)
                lhs_b[:, s * LANE:(s + 1) * LANE] = (x + pe_ref[slot, 1, s:s + 1, :]).astype(BF16)
            first = _dot(lhs_a[...], w1_ref[slot, 0])
            second = _dot(lhs_b[...], w1_ref[slot, 1])
            hid = first + pltpu.roll(second, TILE - 1, 0) + b1_ref[slot]
            act = _gelu_tanh(hid).astype(BF16)
            for gj in range(2):
                outs.append(_dot(act[:, gj * CMP_HIDDEN:(gj + 1) * CMP_HIDDEN], w2_ref[slot]) + b2_ref[slot])
        out_refs[slot][...] = jnp.concatenate(outs, axis=1)


def _compress(src, tile_map, n_seq, page_table, pe, w1, b1, w2, b2):
    in_specs = [pl.BlockSpec((None, TILE, LANE), functools.partial(tile_map, j, cb))
                for j in range(N_CHUNK_TILES) for cb in range(CMP_COL_BLOCKS)]
    full = lambda shape: pl.BlockSpec(shape, lambda b, pt: (0,) * len(shape))
    in_specs += [full(pe.shape), full(w1.shape), full(b1.shape), full(w2.shape), full(b2.shape)]
    out_spec = pl.BlockSpec((None, TILE, KV_WIDTH), lambda b, pt: (b, 0, 0))
    return pl.pallas_call(
        _compress_body,
        grid_spec=pltpu.PrefetchScalarGridSpec(
            num_scalar_prefetch=1, grid=(n_seq,), in_specs=in_specs, out_specs=[out_spec, out_spec],
            scratch_shapes=[pltpu.VMEM((TILE, CMP_STRIDE * LANE), BF16), pltpu.VMEM((TILE, CMP_STRIDE * LANE), BF16)]),
        out_shape=[jax.ShapeDtypeStruct((n_seq, TILE, KV_WIDTH), F32)] * 2,
        compiler_params=_cparams("parallel"),
        name="compress",
    )(page_table, *([src] * (N_CHUNK_TILES * CMP_COL_BLOCKS)), pe, w1, b1, w2, b2)


def _compress_weights(cmp_pe, cmp_w1, cmp_b1, cmp_w2, cmp_b2):
    pe = cmp_pe.reshape(2, 2, CMP_STRIDE, HEAD_DIM)
    pe = jnp.concatenate([pe, pe], axis=-1)
    w = cmp_w1.reshape(2, 2, CMP_STRIDE, HEAD_DIM, CMP_HIDDEN)
    eye = jnp.eye(2, dtype=w.dtype)
    w1 = jnp.einsum('absdh,ij->absidjh', w, eye).reshape(2, 2, CMP_STRIDE * LANE, 2 * CMP_HIDDEN).astype(BF16)
    b1 = jnp.concatenate([cmp_b1, cmp_b1], axis=-1).reshape(2, 1, 2 * CMP_HIDDEN)
    return pe, w1, b1, cmp_w2.astype(BF16), cmp_b2.reshape(2, 1, HEAD_DIM)


def _bucket_thresholds():
    n = np.arange(0, 2 * REL_MAX_DIST)
    max_exact = REL_BUCKETS // 2
    nf = np.maximum(n, 1).astype(np.float32)
    large = max_exact + (np.log(nf / np.float32(max_exact)) / np.float32(math.log(REL_MAX_DIST / max_exact))
                         * np.float32(REL_BUCKETS - max_exact)).astype(np.int32)
    bucket = np.where(n < max_exact, n, np.minimum(large, REL_BUCKETS - 1))
    assert (np.diff(bucket) >= 0).all() and (np.diff(bucket) <= 1).all()
    return [int(np.argmax(bucket >= k)) for k in range(1, REL_BUCKETS)]


BUCKET_START = _bucket_thresholds()


def _rel_bias(dist, tab_ref):
    bias = jnp.broadcast_to(tab_ref[0:1, :], dist.shape)
    for k in range(1, REL_BUCKETS):
        bias = jnp.where(dist >= BUCKET_START[k - 1], tab_ref[k:k + 1, :], bias)
    return bias


def _bias_prompt_body(tab_ref, cmpb_ref, near_ref, edge_ref):
    qb = pl.program_id(0)
    shape = (TILE, Q_PER_KV * TILE)
    row = lax.broadcasted_iota(jnp.int32, shape, 0)
    qi = lax.broadcasted_iota(jnp.int32, shape, 1) % TILE
    dist_c = qb * TILE + qi - (row * CMP_STRIDE + CMP_BLOCK - 1)
    for g in range(KV_HEADS):
        cmpb_ref[g] = _rel_bias(dist_c, tab_ref.at[g])

    @pl.when(qb == 0)
    def _():
        for g in range(KV_HEADS):
            for delta in range(2):
                dist = delta * TILE + qi - row
                b = _rel_bias(dist, tab_ref.at[g])
                near_ref[g, delta] = jnp.where(dist >= 0, b, NEG)
        edge_ref[...] = jnp.where(qi <= row, 0.0, NEG)


def _bias_prompt(tab, n_qb):
    lanes = Q_PER_KV * TILE
    return pl.pallas_call(
        _bias_prompt_body,
        grid=(n_qb,),
        in_specs=[pl.BlockSpec(tab.shape, lambda i: (0, 0, 0))],
        out_specs=[pl.BlockSpec((None, KV_HEADS, TILE, lanes), lambda i: (i, 0, 0, 0)),
                   pl.BlockSpec((KV_HEADS, 2, TILE, lanes), lambda i: (0, 0, 0, 0)),
                   pl.BlockSpec((TILE, lanes), lambda i: (0, 0))],
        out_shape=[jax.ShapeDtypeStruct((n_qb, KV_HEADS, TILE, lanes), F32),
                   jax.ShapeDtypeStruct((KV_HEADS, 2, TILE, lanes), F32),
                   jax.ShapeDtypeStruct((TILE, lanes), F32)],
        compiler_params=_cparams("arbitrary"),
        name="bias_prompt",
    )(tab)


def _bias_sample_body(tab_ref, cmpb_ref, selb_ref, winb_ref, *, past_len, n_new, win_len):
    def lanes_t(rows):
        return lax.broadcasted_iota(jnp.int32, (rows, LANE), 1) % n_new

    def rows_i(rows):
        return lax.broadcasted_iota(jnp.int32, (rows, LANE), 0)

    n = rows_i(TILE)
    dist = past_len + lanes_t(TILE) - (n * CMP_STRIDE + CMP_BLOCK - 1)
    cmpb_ref[...] = jnp.where(dist >= 0, _rel_bias(dist, tab_ref), NEG)
    for j in range(selb_ref.shape[0] // TILE):
        key = rows_i(TILE) + j * TILE
        dist = past_len + lanes_t(TILE) - key
        ok = (dist >= 0) & (key < past_len + n_new)
        selb_ref[j * TILE:(j + 1) * TILE, :] = jnp.where(ok, _rel_bias(dist, tab_ref), NEG)
    for j in range(winb_ref.shape[0] // TILE):
        key = rows_i(TILE) + j * TILE
        dist = win_len + lanes_t(TILE) - key
        ok = (dist >= 0) & (dist <= WINDOW) & (key < win_len + n_new)
        winb_ref[j * TILE:(j + 1) * TILE, :] = jnp.where(ok, _rel_bias(dist, tab_ref), NEG)


def _bias_sample(tab, past_len, n_new, win_len):
    sel_rows = past_len + TILE
    win_rows = win_len + TILE
    return pl.pallas_call(
        functools.partial(_bias_sample_body, past_len=past_len, n_new=n_new, win_len=win_len),
        out_shape=[jax.ShapeDtypeStruct((TILE, LANE), F32),
                   jax.ShapeDtypeStruct((sel_rows, LANE), F32),
                   jax.ShapeDtypeStruct((win_rows, LANE), F32)],
        name="bias_sample",
    )(tab)


def _select_blocks(imp, blk, q_blk):
    valid = blk <= q_blk
    forced = (blk == 0) | (blk == q_blk) | (blk == q_blk - 1)
    score = jnp.where(valid, imp + jnp.where(forced, FORCE_BONUS, 0.0), -jnp.inf)
    rank = jnp.zeros(score.shape, jnp.int32)
    for i in range(score.shape[0]):
        row = score[i:i + 1, :]
        beats = (row > score) | ((row == score) & (blk > i))
        rank = rank + beats.astype(jnp.int32)
    return jnp.where(valid & (rank < N_SELECT), 0.0, NEG)


def _nsa_prompt_body(q_ref, gd_ref, kc_ref, vc_ref, sk_ref, sv_ref, wk_ref, wv_ref, cmpb_ref, near_ref,
                     edge_ref, tab_ref, impt_ref, o_ref, seladd_ref):
    qb = pl.program_id(1)
    lanes = Q_PER_KV * TILE
    q = q_ref[...] * SCALE
    gate_t = jax.nn.sigmoid(gd_ref[...]).T
    shape = (TILE, lanes)
    n_io = lax.broadcasted_iota(jnp.int32, shape, 0)
    qi = lax.broadcasted_iota(jnp.int32, shape, 1) % TILE
    mask_c = qb * TILE + qi - (n_io * CMP_STRIDE + CMP_BLOCK - 1) >= 0
    n_sel = seladd_ref.shape[0]
    blk = lax.broadcasted_iota(jnp.int32, (n_sel, TILE), 0)
    q_blk = (qb * TILE + lax.broadcasted_iota(jnp.int32, (n_sel, TILE), 1)) // SEL_BLOCK
    outs = []
    for g in range(KV_HEADS):
        hs = slice(g * HEAD_DIM, (g + 1) * HEAD_DIM)
        qs = jnp.concatenate(
            [q[:, (g * Q_PER_KV + r) * HEAD_DIM:(g * Q_PER_KV + r + 1) * HEAD_DIM] for r in range(Q_PER_KV)],
            axis=0).astype(BF16)
        far_bias = tab_ref[g, REL_BUCKETS - 1:REL_BUCKETS, :]

        def gate_row(branch, g=g):
            c0 = branch * N_HEADS + g * Q_PER_KV
            return jnp.concatenate([gate_t[c0 + r:c0 + r + 1, :] for r in range(Q_PER_KV)], axis=1)

        s = _dot_nt(kc_ref[:, hs].astype(BF16), qs) + cmpb_ref[g]
        m = jnp.max(jnp.where(mask_c, s, NEG), axis=0, keepdims=True)
        e = jnp.where(mask_c, jnp.exp(jnp.where(mask_c, s, m) - m), 0.0)
        p = e / jnp.maximum(jnp.sum(e, axis=0, keepdims=True), TINY)
        out_t = gate_row(0) * _dot(vc_ref[:, hs].T.astype(BF16), p.astype(BF16))
        psum = p[:, 0:TILE]
        for r in range(1, Q_PER_KV):
            psum = psum + p[:, r * TILE:(r + 1) * TILE]
        sa = _select_blocks(_dot_exact_lhs(impt_ref[...], psum), blk, q_blk)
        seladd_ref[...] = jnp.concatenate([sa] * Q_PER_KV, axis=1)

        def sel_rows(kt):
            half = SEL_BLOCK
            return jnp.concatenate(
                [jnp.broadcast_to(seladd_ref[pl.ds(2 * kt, 1), :], (half, lanes)),
                 jnp.broadcast_to(seladd_ref[pl.ds(2 * kt + 1, 1), :], (half, lanes))], axis=0)

        def tile(carry, k_ref, v_ref, kt, bias, hs=hs, qs=qs):
            m, l, acc = carry
            r0 = pl.multiple_of(kt * TILE, TILE)
            s = _dot_nt(k_ref[pl.ds(r0, TILE), hs].astype(BF16), qs) + bias
            m_new = jnp.maximum(m, jnp.max(s, axis=0, keepdims=True))
            alpha = jnp.exp(m - m_new)
            p = jnp.exp(s - m_new)
            l = alpha * l + jnp.sum(p, axis=0, keepdims=True)
            acc = alpha * acc + _dot(v_ref[pl.ds(r0, TILE), hs].T.astype(BF16), p.astype(BF16))
            return m_new, l, acc

        init = (jnp.full((1, lanes), -jnp.inf, F32), jnp.zeros((1, lanes), F32), jnp.zeros((HEAD_DIM, lanes), F32))

        carry = lax.fori_loop(
            0, jnp.maximum(qb - 1, 0),
            lambda kt, c: tile(c, sk_ref, sv_ref, kt, far_bias + sel_rows(kt)), init)
        kt1 = jnp.maximum(qb - 1, 0)
        carry = tile(carry, sk_ref, sv_ref, kt1, near_ref[g, 1] + sel_rows(kt1) + jnp.where(qb >= 1, 0.0, NEG))
        m, l, acc = tile(carry, sk_ref, sv_ref, qb, near_ref[g, 0] + sel_rows(qb))
        out_t = out_t + gate_row(1) * (acc / jnp.maximum(l, TINY))

        carry = init
        for delta in range(WINDOW // TILE, -1, -1):
            kt = qb - delta
            off = jnp.where(kt >= 0, 0.0, NEG)
            if delta == WINDOW // TILE:
                bias = edge_ref[...] + (far_bias + off)
            elif delta >= 2:
                bias = far_bias + off
            else:
                bias = near_ref[g, delta] + off
            carry = tile(carry, wk_ref, wv_ref, jnp.maximum(kt, 0), bias)
        m, l, acc = carry
        out_t = out_t + gate_row(2) * (acc / jnp.maximum(l, TINY))
        outs += [out_t[:, r * TILE:(r + 1) * TILE].T for r in range(Q_PER_KV)]
    o_ref[...] = jnp.concatenate(outs, axis=1).astype(BF16)


def _nsa_prompt(p3, kc, vc, cmpb, near, edge, tab, impt):
    b, t, _ = p3.shape
    n_qb = t // TILE
    lanes = Q_PER_KV * TILE
    kvcol = lambda slot: (P_KV + slot * KV_WIDTH) // KV_WIDTH
    seq_spec = lambda slot: pl.BlockSpec((None, t, KV_WIDTH), lambda i, j: (i, 0, kvcol(slot)))
    const = lambda shape: pl.BlockSpec(shape, lambda i, j: (0,) * len(shape))
    return pl.pallas_call(
        _nsa_prompt_body,
        grid=(b, n_qb),
        in_specs=[pl.BlockSpec((None, TILE, ATTN_WIDTH), lambda i, j: (i, j, 0)),
                  pl.BlockSpec((None, TILE, LANE), lambda i, j: (i, j, P_GD // LANE)),
                  pl.BlockSpec((None, TILE, KV_WIDTH), lambda i, j: (i, 0, 0)),
                  pl.BlockSpec((None, TILE, KV_WIDTH), lambda i, j: (i, 0, 0)),
                  seq_spec(2), seq_spec(3), seq_spec(4), seq_spec(5),
                  pl.BlockSpec((None, KV_HEADS, TILE, lanes), lambda i, j: (j, 0, 0, 0)),
                  const(near.shape), const(edge.shape), const(tab.shape), const(impt.shape)],
        out_specs=pl.BlockSpec((None, TILE, ATTN_WIDTH), lambda i, j: (i, j, 0)),
        out_shape=jax.ShapeDtypeStruct((b, t, ATTN_WIDTH), BF16),
        scratch_shapes=[pltpu.VMEM((impt.shape[0], lanes), F32)],
        compiler_params=_cparams("parallel", "arbitrary"),
        name="nsa_prompt",
    )(p3, p3, kc, vc, p3, p3, p3, p3, cmpb, near, edge, tab, impt)


def _nsa_sample_body(*refs, past_len, n_new):
    n_pages = past_len // TILE
    pt_ref, q_ref, gd_ref, newkv_ref, newwin_ref, kc_ref, vc_ref = refs[:7]
    pages = refs[7:7 + n_pages]
    (win_ref, cmpb_ref, selb_ref, winb_ref, impt_ref, rsum_ref, gsel_ref,
     o_ref, winout_ref, s_scr) = refs[7 + n_pages:]
    del pt_ref
    win_len = win_ref.shape[0]
    lane = lax.broadcasted_iota(jnp.int32, (1, LANE), 1)
    lane_g = lane // (Q_PER_KV * n_new)

    q = q_ref[...] * SCALE
    qs = jnp.concatenate([q[:, h * HEAD_DIM:(h + 1) * HEAD_DIM] for h in range(N_HEADS)], axis=0)
    qs4 = jnp.concatenate([qs] * KV_HEADS, axis=1)
    row_g = lax.broadcasted_iota(jnp.int32, qs4.shape, 0) // (Q_PER_KV * n_new)
    col_g = lax.broadcasted_iota(jnp.int32, qs4.shape, 1) // HEAD_DIM
    qbd = jnp.where(row_g == col_g, qs4, 0.0).astype(BF16)

    sig = jax.nn.sigmoid(gd_ref[...])
    t_row = lax.broadcasted_iota(jnp.int32, (n_new, LANE), 0)
    t_lane = lax.broadcasted_iota(jnp.int32, (n_new, LANE), 1) % n_new

    def gate_row(branch):
        spread = _dot_exact_rhs(sig, gsel_ref[branch])
        return jnp.sum(jnp.where(t_row == t_lane, spread, 0.0), axis=0, keepdims=True)

    def pick(full_t):
        out = jnp.zeros((HEAD_DIM, LANE), F32)
        for g in range(KV_HEADS):
            out = out + jnp.where(lane_g == g, full_t[g * HEAD_DIM:(g + 1) * HEAD_DIM, :], 0.0)
        return out

    s = _dot_nt(kc_ref[...].astype(BF16), qbd) + cmpb_ref[...]
    m = jnp.max(s, axis=0, keepdims=True)
    e = jnp.exp(s - m)
    p = e / jnp.maximum(jnp.sum(e, axis=0, keepdims=True), TINY)
    out_t = gate_row(0) * pick(_dot(vc_ref[...].T.astype(BF16), p.astype(BF16)))
    imp = _dot_exact_lhs(impt_ref[...], _dot_exact_rhs(p, rsum_ref[...]))
    blk = lax.broadcasted_iota(jnp.int32, imp.shape, 0)
    q_blk = (past_len + lax.broadcasted_iota(jnp.int32, imp.shape, 1) % n_new) // SEL_BLOCK
    sa = _select_blocks(imp, blk, q_blk)

    def attend(k_tiles, v_tiles, bias_ref, masks):
        n_t = len(k_tiles)
        m = jnp.full((1, LANE), -jnp.inf, F32)
        for j in range(n_t):
            s = _dot_nt(k_tiles[j]().astype(BF16), qbd) + bias_ref[j * TILE:(j + 1) * TILE, :]
            if masks is not None:
                s = s + masks[j]
            s_scr[j * TILE:(j + 1) * TILE, :] = s
            m = jnp.maximum(m, jnp.max(s, axis=0, keepdims=True))
        l = jnp.zeros((1, LANE), F32)
        acc = jnp.zeros((KV_WIDTH, LANE), F32)
        for j in range(n_t):
            p = jnp.exp(s_scr[j * TILE:(j + 1) * TILE, :] - m)
            l = l + jnp.sum(p, axis=0, keepdims=True)
            acc = acc + _dot(v_tiles[j]().T.astype(BF16), p.astype(BF16))
        return pick(acc) / jnp.maximum(l, TINY)

    def new_tile(ref):
        return jnp.concatenate([ref[...], jnp.zeros((TILE - n_new, ref.shape[1]), F32)], axis=0)

    half = SEL_BLOCK
    masks = [jnp.concatenate([jnp.broadcast_to(sa[2 * j:2 * j + 1, :], (half, LANE)),
                              jnp.broadcast_to(sa[2 * j + 1:2 * j + 2, :], (half, LANE))], axis=0)
             for j in range(n_pages)]
    masks.append(jnp.broadcast_to(sa[2 * n_pages:2 * n_pages + 1, :], (TILE, LANE)))
    new_kv = new_tile(newkv_ref)
    k_tiles = [functools.partial(lambda r: r[:, 0:KV_WIDTH], pg) for pg in pages] + [lambda: new_kv[:, 0:KV_WIDTH]]
    v_tiles = [functools.partial(lambda r: r[:, KV_WIDTH:2 * KV_WIDTH], pg) for pg in pages] + [
        lambda: new_kv[:, KV_WIDTH:2 * KV_WIDTH]]
    out_t = out_t + gate_row(1) * attend(k_tiles, v_tiles, selb_ref, masks)

    new_win = new_tile(newwin_ref)
    n_wt = win_len // TILE
    k_tiles = [functools.partial(lambda j: win_ref[j * TILE:(j + 1) * TILE, 0:KV_WIDTH], j) for j in range(n_wt)]
    v_tiles = [functools.partial(lambda j: win_ref[j * TILE:(j + 1) * TILE, KV_WIDTH:2 * KV_WIDTH], j)
               for j in range(n_wt)]
    k_tiles.append(lambda: new_win[:, 0:KV_WIDTH])
    v_tiles.append(lambda: new_win[:, KV_WIDTH:2 * KV_WIDTH])
    out_t = out_t + gate_row(2) * attend(k_tiles, v_tiles, winb_ref, None)

    o = out_t.T
    o_ref[...] = jnp.concatenate([o[h * n_new:(h + 1) * n_new, :] for h in range(N_HEADS)], axis=1).astype(BF16)
    winout_ref[0:win_len - n_new, :] = win_ref[n_new:win_len, :]
    winout_ref[win_len - n_new:win_len, :] = newwin_ref[...]


def _nsa_sample(p3, kc, vc, cache, win, page_table, cmpb, selb, winb, impt, rsum, gsel):
    b, n_new, _ = p3.shape
    n_pages = page_table.shape[1]
    past_len = n_pages * PAGE_SIZE
    win_len = win.shape[1]
    kvblk = lambda slot: (P_KV + slot * KV_WIDTH) // (2 * KV_WIDTH)
    const = lambda shape: pl.BlockSpec(shape, lambda i, pt: (0,) * len(shape))
    in_specs = [pl.BlockSpec((None, n_new, ATTN_WIDTH), lambda i, pt: (i, 0, 0)),
                pl.BlockSpec((None, n_new, LANE), lambda i, pt: (i, 0, P_GD // LANE)),
                pl.BlockSpec((None, n_new, 2 * KV_WIDTH), lambda i, pt: (i, 0, kvblk(2))),
                pl.BlockSpec((None, n_new, 2 * KV_WIDTH), lambda i, pt: (i, 0, kvblk(4))),
                pl.BlockSpec((None, TILE, KV_WIDTH), lambda i, pt: (i, 0, 0)),
                pl.BlockSpec((None, TILE, KV_WIDTH), lambda i, pt: (i, 0, 0))]
    in_specs += [pl.BlockSpec((None, PAGE_SIZE, 2 * KV_WIDTH), functools.partial(lambda j, i, pt: (pt[i, j], 0, 1), j))
                 for j in range(n_pages)]
    in_specs += [pl.BlockSpec((None, win_len, 2 * KV_WIDTH), lambda i, pt: (i, 0, 0)),
                 const(cmpb.shape), const(selb.shape), const(winb.shape), const(impt.shape), const(rsum.shape),
                 const(gsel.shape)]
    return pl.pallas_call(
        functools.partial(_nsa_sample_body, past_len=past_len, n_new=n_new),
        grid_spec=pltpu.PrefetchScalarGridSpec(
            num_scalar_prefetch=1, grid=(b,), in_specs=in_specs,
            out_specs=[pl.BlockSpec((None, n_new, ATTN_WIDTH), lambda i, pt: (i, 0, 0)),
                       pl.BlockSpec((None, win_len, 2 * KV_WIDTH), lambda i, pt: (i, 0, 0))],
            scratch_shapes=[pltpu.VMEM((past_len + TILE, LANE), F32)]),
        out_shape=[jax.ShapeDtypeStruct((b, n_new, ATTN_WIDTH), BF16),
                   jax.ShapeDtypeStruct((b, win_len, 2 * KV_WIDTH), F32)],
        compiler_params=_cparams("parallel"),
        name="nsa_sample",
    )(page_table, p3, p3, p3, p3, kc, vc, *([cache] * n_pages), win, cmpb, selb, winb, impt, rsum, gsel)


def _sample_selectors(n_new):
    lanes = np.arange(LANE)
    h, t = lanes // n_new, lanes % n_new
    rsum = ((h[:, None] // Q_PER_KV == h[None, :] // Q_PER_KV) & (t[:, None] == t[None, :])).astype(np.float32)
    cols = np.arange(LANE)
    gsel = np.stack([(cols[:, None] == br * N_HEADS + h[None, :]).astype(np.float32) for br in range(3)])
    return jnp.asarray(rsum, BF16), jnp.asarray(gsel, BF16)


def _dot_tn(a, b):
    return lax.dot_general(a, b, (((0,), (0,)), ((), ())), preferred_element_type=F32)


def _ssd_body(*refs, q_len, carry):
    if carry:
        (z_ref, xbc_ref, gd_ref, cw_ref, cb_ref, dtb_ref, alog_ref, dexp_ref, ng_ref, e16_ref,
         y_ref, hout_ref, prev_ref, h_scr) = refs
    else:
        (z_ref, xbc_ref, gd_ref, prev_ref, h0_ref, cw_ref, cb_ref, dtb_ref, alog_ref, dexp_ref, ng_ref, e16_ref,
         y_ref, hout_ref) = refs
    n_seq = TILE // q_len
    x = xbc_ref[...]
    if carry:
        @pl.when(pl.program_id(1) == 0)
        def _():
            prev_ref[...] = jnp.zeros_like(prev_ref)
            h_scr[...] = jnp.zeros_like(h_scr)
    prev = prev_ref[...]
    tpos = lax.broadcasted_iota(jnp.int32, (TILE, 1), 0) % q_len
    acc = cb_ref[...] + cw_ref[SSM_CONV - 1:SSM_CONV, :] * x
    for k in range(1, SSM_CONV):
        shifted = jnp.where(tpos >= k, pltpu.roll(x, k, 0), pltpu.roll(prev, k, 0))
        acc = acc + cw_ref[SSM_CONV - 1 - k:SSM_CONV - k, :] * shifted
    if carry:
        prev_ref[...] = x
    xc = _silu(acc)
    xs = xc[:, 0:SSM_INNER]
    b_all = xc[:, SSM_INNER:SSM_INNER + SSM_GROUPS * SSM_STATE]
    c_all = xc[:, SSM_INNER + SSM_GROUPS * SSM_STATE:]

    col = lax.broadcasted_iota(jnp.int32, (1, LANE), 1)
    dt_cols = (col >= DT_COL) & (col < DT_COL + SSM_HEADS)
    dt = jnp.where(dt_cols, jax.nn.softplus(gd_ref[...] + dtb_ref[...]), 0.0)
    a = dt * jnp.where(dt_cols, -jnp.exp(alog_ref[...]), 0.0)
    ri = lax.broadcasted_iota(jnp.int32, (TILE, TILE), 0)
    ci = lax.broadcasted_iota(jnp.int32, (TILE, TILE), 1)
    same = (ri // q_len) == (ci // q_len)
    causal = same & (ci <= ri)
    acum = _dot_exact_lhs(causal.astype(BF16), a)
    last_sel = (ci == (ri // q_len) * q_len + (q_len - 1)).astype(BF16)
    alast = _dot_exact_lhs(last_sel, acum)
    acum_t = acum.T
    e16 = e16_ref[...]
    xw = xs * _dot_exact_rhs(dt, e16)
    xwe = xw * _dot_exact_rhs(jnp.exp(alast - acum), e16)
    grow = jnp.exp(_dot_exact_rhs(acum, e16))
    rowseq = lax.broadcasted_iota(jnp.int32, (TILE, 1), 0) // q_len

    ys = []
    for g in range(SSM_GROUPS):
        cg = c_all[:, g * SSM_STATE:(g + 1) * SSM_STATE].astype(BF16)
        bg = b_all[:, g * SSM_STATE:(g + 1) * SSM_STATE].astype(BF16)
        cb = _dot_nt(cg, bg)
        gl = slice(g * SSM_HPG * SSM_HEADDIM, (g + 1) * SSM_HPG * SSM_HEADDIM)
        hl = slice(g * SSM_HPG, (g + 1) * SSM_HPG)
        for r in range(SSM_HPG):
            h = g * SSM_HPG + r
            c = DT_COL + h
            seg = acum[:, c:c + 1] - acum_t[c:c + 1, :]
            decay = jnp.where(causal, jnp.exp(jnp.where(causal, seg, 0.0)), 0.0)
            ys.append(_dot((cb * decay).astype(BF16), xw[:, h * SSM_HEADDIM:(h + 1) * SSM_HEADDIM].astype(BF16)))
        xwe_g = xwe[:, gl]
        if carry:
            h_old = h_scr[hl].reshape(SSM_HPG * SSM_HEADDIM, SSM_STATE)
            y_off = _dot_nt(cg, h_old.astype(BF16))
            scale = jnp.concatenate(
                [jnp.broadcast_to(jnp.exp(alast[0:1, DT_COL + g * SSM_HPG + r:DT_COL + g * SSM_HPG + r + 1]),
                                  (SSM_HEADDIM, SSM_STATE)) for r in range(SSM_HPG)], axis=0)
            h_new = h_old * scale + _dot_tn(xwe_g.astype(BF16), bg)
            h_scr[hl] = h_new.reshape(SSM_HPG, SSM_HEADDIM, SSM_STATE)
        else:
            offs = []
            for s in range(n_seq):
                h_old = h0_ref[s, hl].reshape(SSM_HPG * SSM_HEADDIM, SSM_STATE)
                offs.append(_dot_nt(cg, h_old.astype(BF16))[s * q_len:(s + 1) * q_len, :])
                a_row = alast[s * q_len:s * q_len + 1, :]
                scale = jnp.concatenate(
                    [jnp.broadcast_to(jnp.exp(a_row[:, DT_COL + g * SSM_HPG + r:DT_COL + g * SSM_HPG + r + 1]),
                                      (SSM_HEADDIM, SSM_STATE)) for r in range(SSM_HPG)], axis=0)
                x_s = jnp.where(rowseq == s, xwe_g, 0.0).astype(BF16)
                h_new = h_old * scale + _dot_tn(x_s, bg)
                hout_ref[s, hl] = h_new.reshape(SSM_HPG, SSM_HEADDIM, SSM_STATE)
            y_off = jnp.concatenate(offs, axis=0)
        for r in range(SSM_HPG):
            h = g * SSM_HPG + r
            ys[h] = ys[h] + y_off[:, r * SSM_HEADDIM:(r + 1) * SSM_HEADDIM] * grow[:, h * SSM_HEADDIM:(h + 1) * SSM_HEADDIM]
    if carry:
        @pl.when(pl.program_id(1) == pl.num_programs(1) - 1)
        def _():
            hout_ref[...] = h_scr[...]

    y = jnp.concatenate(ys, axis=1) + xs * dexp_ref[...]
    y = y * _silu(z_ref[...])
    gw = SSM_INNER // SSM_GROUPS
    parts = []
    for g in range(SSM_GROUPS):
        blk = y[:, g * gw:(g + 1) * gw]
        parts.append(blk * lax.rsqrt(jnp.mean(blk * blk, axis=-1, keepdims=True) + EPS))
    y_ref[...] = (jnp.concatenate(parts, axis=1) * ng_ref[...]).astype(BF16)


def _ssd(p3, prev, h0, cw, cb, dtb, alog, dexp, ng, e16, *, q_len):
    n_grp, rows, _ = p3.shape
    carry = prev is None
    n_chunks = rows // TILE
    n_seq = TILE // q_len
    zspec = pl.BlockSpec((None, TILE, SSM_INNER), lambda i, c: (i, c, P_Z // SSM_INNER))
    xspec = pl.BlockSpec((None, TILE, CONV_DIM), lambda i, c: (i, c, P_XBC // CONV_DIM))
    gspec = pl.BlockSpec((None, TILE, LANE), lambda i, c: (i, c, P_GD // LANE))
    const = lambda arr: pl.BlockSpec(arr.shape, lambda i, c: (0,) * arr.ndim)
    params = [cw, cb, dtb, alog, dexp, ng, e16]
    in_specs = [zspec, xspec, gspec]
    args = [p3, p3, p3]
    if carry:
        n_out_seq = n_grp
        hspec = pl.BlockSpec((None, SSM_HEADS, SSM_HEADDIM, SSM_STATE), lambda i, c: (i, 0, 0, 0))
        scratch = [pltpu.VMEM((TILE, CONV_DIM), F32), pltpu.VMEM((SSM_HEADS, SSM_HEADDIM, SSM_STATE), F32)]
    else:
        n_out_seq = n_grp * n_chunks * n_seq
        hspec = pl.BlockSpec((n_seq, SSM_HEADS, SSM_HEADDIM, SSM_STATE), lambda i, c: (i * n_chunks + c, 0, 0, 0))
        in_specs += [pl.BlockSpec((None, TILE, CONV_DIM), lambda i, c: (i, c, 0)), hspec]
        args += [prev, h0]
        scratch = []
    in_specs += [const(a) for a in params]
    return pl.pallas_call(
        functools.partial(_ssd_body, q_len=q_len, carry=carry),
        grid=(n_grp, n_chunks),
        in_specs=in_specs,
        out_specs=[pl.BlockSpec((None, TILE, SSM_INNER), lambda i, c: (i, c, 0)), hspec],
        out_shape=[jax.ShapeDtypeStruct((n_grp, rows, SSM_INNER), BF16),
                   jax.ShapeDtypeStruct((n_out_seq, SSM_HEADS, SSM_HEADDIM, SSM_STATE), F32)],
        scratch_shapes=scratch,
        compiler_params=_cparams("parallel", "arbitrary"),
        name="ssd",
    )(*args, *params)


def _importance_map_t(n_sel):
    n_cmp = TILE - 1
    c0 = np.arange(n_cmp) * CMP_STRIDE
    s0 = np.arange(n_sel) * SEL_BLOCK
    ov = np.minimum(c0[None, :] + CMP_BLOCK, s0[:, None] + SEL_BLOCK) - np.maximum(c0[None, :], s0[:, None])
    m = np.zeros((-(-n_sel // SUBLANE) * SUBLANE, TILE), np.float32)
    m[:n_sel, :n_cmp] = np.clip(ov, 0, None).astype(np.float32) / CMP_BLOCK
    return jnp.asarray(m, BF16)


def _dt_expand():
    e = np.zeros((LANE, SSM_INNER), np.float32)
    for h in range(SSM_HEADS):
        e[DT_COL + h, h * SSM_HEADDIM:(h + 1) * SSM_HEADDIM] = 1.0
    return jnp.asarray(e, BF16)


def _pad_to(x, axis, size):
    pad = [(0, 0)] * x.ndim
    pad[axis] = (0, size - x.shape[axis])
    return jnp.pad(x, pad)


def _history_rows(state, n_rows, width, group):
    b, k, _ = state.shape
    tile = jnp.pad(state, ((0, 0), (n_rows - k, 0), (0, width - state.shape[2])))
    tile = jnp.roll(tile.reshape(b // group, group, n_rows, width), -1, axis=1)
    return tile.reshape(b * n_rows, width)


def kernel(x_prompt, x_sample, cache_kv, cache_win_kv, state_ssm_conv, state_ssm, state_ffn_conv, page_table, rel_table, ln_mix_pre, w_in, cmp_pe, cmp_w1, cmp_b1, cmp_w2, cmp_b2, ssm_conv_w, ssm_conv_b, ssm_dt_bias, ssm_a_log, ssm_d, ssm_norm_g, w_out, ln_mix_post, ln_ffn_pre, ffn_w_gate, ffn_w_up, ffn_dw_w, ffn_dw_b, ffn_w_down, ln_ffn_post):
    bp, tp, d = x_prompt.shape
    bs, ts, _ = x_sample.shape
    n_pool = cache_kv.shape[1]
    n_pages = page_table.shape[1]
    past_len = n_pages * PAGE_SIZE
    win_len = cache_win_kv.shape[2]
    assert d == D_MODEL and tp == N_CHUNK_TILES * TILE and past_len == N_CHUNK_TILES * TILE
    assert N_HEADS * ts == LANE and ts == SUBLANE and tp >= WINDOW and win_len == WINDOW
    seq_per_step = TILE // ts

    rel = rel_table.astype(F32)
    tab_p = jnp.repeat(rel.reshape(REL_BUCKETS, KV_HEADS, Q_PER_KV).transpose(1, 0, 2), TILE, axis=-1)
    tab_s = jnp.repeat(rel, ts, axis=1)
    cmpb_p, near, edge = _bias_prompt(tab_p, tp // TILE)
    cmpb_s, selb, winb = _bias_sample(tab_s, past_len, ts, win_len)
    impt_p = _importance_map_t(-(-tp // SEL_BLOCK))
    impt_s = _importance_map_t(-(-(past_len + ts) // SEL_BLOCK))
    rsum, gsel = _sample_selectors(ts)
    e16 = _dt_expand()
    dummy_pt = jnp.zeros((1, 1), jnp.int32)

    xp = x_prompt.reshape(bp * tp, d)
    xs = x_sample.reshape(bs * ts, d)
    outs = [[] for _ in range(10)]
    for l in range(w_in.shape[0]):
        o1 = ATTN_WIDTH
        o2 = o1 + N_KV_SLOTS * KV_WIDTH
        o3 = o2 + 3 * N_HEADS
        o4 = o3 + SSM_INNER
        o5 = o4 + CONV_DIM
        w = w_in[l]
        wp = jnp.concatenate([w[:, :o1], w[:, o3:o4], w[:, o4:o5], w[:, o1:o2], w[:, o2:o3], w[:, o5:]], axis=1)
        wp = _pad_to(wp, 1, P_WIDTH).astype(BF16)
        g_pre = ln_mix_pre[l].reshape(1, d)
        cw = _compress_weights(cmp_pe[l], cmp_w1[l], cmp_b1[l], cmp_w2[l], cmp_b2[l])
        ssm_w = (_pad_to(ssm_conv_w[l], 0, SUBLANE), ssm_conv_b[l].reshape(1, CONV_DIM),
                 _pad_to(jnp.pad(ssm_dt_bias[l].astype(F32), (DT_COL, 0)), 0, LANE).reshape(1, LANE),
                 _pad_to(jnp.pad(ssm_a_log[l].astype(F32), (DT_COL, 0)), 0, LANE).reshape(1, LANE),
                 jnp.repeat(ssm_d[l].astype(F32), SSM_HEADDIM).reshape(1, SSM_INNER),
                 ssm_norm_g[l].astype(F32).reshape(1, SSM_INNER), e16)
        w_attn = w_out[l, :ATTN_WIDTH].astype(BF16)
        w_ssm = w_out[l, ATTN_WIDTH:].astype(BF16)
        g_post = ln_mix_post[l].reshape(1, d)
        g_pre2 = ln_ffn_pre[l].reshape(1, d)
        wg = _pad_to(ffn_w_gate[l], 1, FF_PAD).astype(BF16)
        wu = _pad_to(ffn_w_up[l], 1, FF_PAD).astype(BF16)
        dw = _pad_to(_pad_to(ffn_dw_w[l], 1, FF_PAD), 0, SUBLANE)
        db = _pad_to(ffn_dw_b[l].reshape(1, D_FF), 1, FF_PAD)
        wd = _pad_to(ffn_w_down[l], 0, FF_PAD).astype(BF16)
        g_post2 = ln_ffn_post[l].reshape(1, d)

        p3 = _proj(xp, g_pre, wp, tm=1024).reshape(bp, tp, P_WIDTH)
        kc, vc = _compress(p3, lambda j, cb, b, pt: (b, j, P_KV // LANE + cb), bp, dummy_pt, *cw)
        attn = _nsa_prompt(p3, kc, vc, cmpb_p, near, edge, tab_p, impt_p)
        ssm, h_new = _ssd(p3, None, None, *ssm_w, q_len=TILE)
        x1, hn = _outproj(attn.reshape(bp * tp, ATTN_WIDTH), ssm.reshape(bp * tp, SSM_INNER), xp,
                          w_attn, w_ssm, g_post, g_pre2)
        act, tail = _ffn_up(hn, wg, wu, dw, db, None, n_groups=bp, seq_len=tp, tm=1024)
        xp = _ffn_down(act, wd, x1, g_post2)
        kv = p3[:, :, P_KV:P_KV + N_KV_SLOTS * KV_WIDTH]
        outs[0].append(kv[:, :, :N_CACHE_SLOTS * KV_WIDTH].reshape(bp, tp, N_CACHE_SLOTS, KV_HEADS, HEAD_DIM))
        outs[2].append(kv[:, tp - WINDOW:, N_CACHE_SLOTS * KV_WIDTH:].reshape(bp, WINDOW, 2, KV_HEADS, HEAD_DIM))
        outs[4].append(p3[:, tp - (SSM_CONV - 1):, P_XBC:P_XBC + CONV_DIM])
        outs[6].append(h_new)
        outs[8].append(tail[:, SUBLANE - (FFN_CONV - 1):, :D_FF])

        p3 = _proj(xs, g_pre, wp, tm=bs * ts).reshape(bs, ts, P_WIDTH)
        cache3 = cache_kv[l].reshape(n_pool, PAGE_SIZE, N_CACHE_SLOTS * KV_WIDTH)
        kc, vc = _compress(cache3, lambda j, cb, b, pt: (pt[b, j], 0, cb), bs, page_table, *cw)
        win3 = cache_win_kv[l].reshape(bs, win_len, 2 * KV_WIDTH)
        attn, win_new = _nsa_sample(p3, kc, vc, cache3, win3, page_table, cmpb_s, selb, winb, impt_s, rsum, gsel)
        prev = _history_rows(state_ssm_conv[l], ts, CONV_DIM, seq_per_step).reshape(1, bs * ts, CONV_DIM)
        ssm, h_new = _ssd(p3.reshape(1, bs * ts, P_WIDTH), prev, state_ssm[l], *ssm_w, q_len=ts)
        x1, hn = _outproj(attn.reshape(bs * ts, ATTN_WIDTH), ssm.reshape(bs * ts, SSM_INNER), xs,
                          w_attn, w_ssm, g_post, g_pre2)
        prev_f = _history_rows(state_ffn_conv[l], ts, FF_PAD, bs)
        act, gp = _ffn_up(hn, wg, wu, dw, db, prev_f, n_groups=1, seq_len=ts, tm=bs * ts)
        xs = _ffn_down(act, wd, x1, g_post2, tm=bs * ts)
        kv = p3[:, :, P_KV:P_KV + N_KV_SLOTS * KV_WIDTH]
        outs[1].append(kv[:, :, :N_CACHE_SLOTS * KV_WIDTH].reshape(bs, ts, N_CACHE_SLOTS, KV_HEADS, HEAD_DIM))
        outs[3].append(win_new.reshape(bs, win_len, 2, KV_HEADS, HEAD_DIM))
        outs[5].append(p3[:, ts - (SSM_CONV - 1):, P_XBC:P_XBC + CONV_DIM])
        outs[7].append(h_new)
        outs[9].append(gp.reshape(bs, ts, FF_PAD)[:, ts - (FFN_CONV - 1):, :D_FF])
    return (xp.reshape(bp, tp, d), xs.reshape(bs, ts, d)) + tuple(jnp.stack(o) for o in outs)
```

```python
import functools
import math

import numpy as np
import jax
import jax.numpy as jnp
from jax import lax
from jax.experimental import pallas as pl
from jax.experimental.pallas import tpu as pltpu

F32 = jnp.float32
BF16 = jnp.bfloat16

D_MODEL = 2048
PAGE_SIZE = 128
HEAD_DIM = 64
ATTN_WIDTH = D_MODEL // 2
N_HEADS = ATTN_WIDTH // HEAD_DIM
KV_HEADS = 4
Q_PER_KV = N_HEADS // KV_HEADS
CMP_STRIDE = 16
CMP_BLOCK = 2 * CMP_STRIDE
CMP_HIDDEN = 2 * HEAD_DIM
SEL_BLOCK = 64
N_SELECT = 8
WINDOW = 512
N_KV_SLOTS = 6
N_CACHE_SLOTS = 4
KV_WIDTH = KV_HEADS * HEAD_DIM
SSM_INNER = D_MODEL - ATTN_WIDTH
SSM_HEADDIM = 64
SSM_HEADS = SSM_INNER // SSM_HEADDIM
SSM_GROUPS = 4
SSM_HPG = SSM_HEADS // SSM_GROUPS
SSM_STATE = 128
SSM_CONV = 4
CONV_DIM = SSM_INNER + 2 * SSM_GROUPS * SSM_STATE
D_FF = ((8 * D_MODEL // 3 + 127) // 128) * 128
FFN_CONV = 3
REL_BUCKETS = 32
REL_MAX_DIST = 128
EPS = 1e-6
NEG = -1e30
TINY = 1e-30
FORCE_BONUS = 1e4
SCALE = HEAD_DIM ** -0.5

LANE = 128
SUBLANE = 8
TILE = 128
VMEM_LIMIT = 56 * 1024 * 1024

P_Q = 0
P_Z = ATTN_WIDTH
P_XBC = P_Z + SSM_INNER
P_KV = P_XBC + CONV_DIM
P_GD = P_KV + N_KV_SLOTS * KV_WIDTH
GD_WIDTH = 3 * N_HEADS + SSM_HEADS
P_WIDTH = 6144
DT_COL = 3 * N_HEADS
FF_PAD = 5632


def _cparams(*sem):
    return pltpu.CompilerParams(dimension_semantics=sem, vmem_limit_bytes=VMEM_LIMIT)


def _rms(x, g):
    return x * lax.rsqrt(jnp.mean(x * x, axis=-1, keepdims=True) + EPS) * g


def _split3(x):
    hi = x.astype(BF16)
    r1 = x - hi.astype(F32)
    mid = r1.astype(BF16)
    lo = (r1 - mid.astype(F32)).astype(BF16)
    return hi, mid, lo


def _dot(a, b):
    return jnp.dot(a, b, preferred_element_type=F32)


def _dot_nt(a, b):
    return lax.dot_general(a, b, (((1,), (1,)), ((), ())), preferred_element_type=F32)


def _dot_exact_rhs(x, sel):
    hi, mid, lo = _split3(x)
    return _dot(hi, sel) + _dot(mid, sel) + _dot(lo, sel)


def _dot_exact_lhs(sel, x):
    hi, mid, lo = _split3(x)
    return _dot(sel, hi) + _dot(sel, mid) + _dot(sel, lo)


def _gelu_tanh(x):
    return 0.5 * x * (1.0 + jnp.tanh(math.sqrt(2.0 / math.pi) * (x + 0.044715 * (x * x * x))))


def _silu(x):
    return x * jax.nn.sigmoid(x)


def _proj_body(x_ref, g_ref, w_ref, o_ref, xn_ref):
    @pl.when(pl.program_id(1) == 0)
    def _():
        xn_ref[...] = _rms(x_ref[...], g_ref[...]).astype(BF16)

    o_ref[...] = _dot(xn_ref[...], w_ref[...])


def _proj(x2d, g, w, tm, tn=512):
    m, k = x2d.shape
    n = w.shape[1]
    return pl.pallas_call(
        _proj_body,
        grid=(m // tm, n // tn),
        in_specs=[pl.BlockSpec((tm, k), lambda i, j: (i, 0)),
                  pl.BlockSpec((1, k), lambda i, j: (0, 0)),
                  pl.BlockSpec((k, tn), lambda i, j: (0, j))],
        out_specs=pl.BlockSpec((tm, tn), lambda i, j: (i, j)),
        out_shape=jax.ShapeDtypeStruct((m, n), F32),
        scratch_shapes=[pltpu.VMEM((tm, k), BF16)],
        compiler_params=_cparams("parallel", "arbitrary"),
        name="proj",
    )(x2d, g, w)


def _outproj_body(a_ref, s_ref, x_ref, wa_ref, ws_ref, g1_ref, g2_ref, x1_ref, hn_ref):
    mix = _dot(a_ref[...], wa_ref[...]) + _dot(s_ref[...], ws_ref[...])
    x1 = x_ref[...] + _rms(mix, g1_ref[...])
    x1_ref[...] = x1
    hn_ref[...] = _rms(x1, g2_ref[...]).astype(BF16)


def _outproj(attn, ssm, x2d, w_attn, w_ssm, g_post, g_pre2, tm=512):
    m, d = x2d.shape
    ka = attn.shape[1]
    const = lambda i: (0, 0)
    return pl.pallas_call(
        _outproj_body,
        grid=(m // tm,),
        in_specs=[pl.BlockSpec((tm, ka), lambda i: (i, 0)),
                  pl.BlockSpec((tm, ka), lambda i: (i, 0)),
                  pl.BlockSpec((tm, d), lambda i: (i, 0)),
                  pl.BlockSpec((ka, d), const),
                  pl.BlockSpec((ka, d), const),
                  pl.BlockSpec((1, d), const),
                  pl.BlockSpec((1, d), const)],
        out_specs=[pl.BlockSpec((tm, d), lambda i: (i, 0)),
                   pl.BlockSpec((tm, d), lambda i: (i, 0))],
        out_shape=[jax.ShapeDtypeStruct((m, d), F32), jax.ShapeDtypeStruct((m, d), BF16)],
        compiler_params=_cparams("parallel"),
        name="outproj",
    )(attn, ssm, x2d, w_attn, w_ssm, g_post, g_pre2)


def _ffn_up_body(*refs, seq_len, carry):
    if carry:
        h_ref, wg_ref, wu_ref, dw_ref, db_ref, act_ref, tail_ref, prev_ref = refs
    else:
        h_ref, wg_ref, wu_ref, dw_ref, db_ref, prev_ref, act_ref, tail_ref = refs
    tm = h_ref.shape[0]
    h = h_ref[...]
    gp = _dot(h, wg_ref[...])
    up = _dot(h, wu_ref[...])
    if carry:
        @pl.when(pl.program_id(2) == 0)
        def _():
            prev_ref[...] = jnp.zeros_like(prev_ref)
    prev = prev_ref[...]
    tpos = lax.broadcasted_iota(jnp.int32, (tm, 1), 0) % seq_len
    acc = db_ref[...] + dw_ref[FFN_CONV - 1:FFN_CONV, :] * gp
    for k in range(1, FFN_CONV):
        shifted = jnp.where(tpos >= k, pltpu.roll(gp, k, 0), pltpu.roll(prev, k, 0))
        acc = acc + dw_ref[FFN_CONV - 1 - k:FFN_CONV - k, :] * shifted
    act_ref[...] = (_gelu_tanh(acc) * up).astype(BF16)
    if carry:
        prev_ref[...] = gp
        tail_ref[...] = gp[tm - SUBLANE:, :]
    else:
        tail_ref[...] = gp


def _ffn_up(hn, wg, wu, dw, db, prev, *, n_groups, seq_len, tm, tn=512):
    m, d = hn.shape
    rows = m // n_groups
    ni = rows // tm
    nj = FF_PAD // tn
    carry = prev is None
    in_specs = [pl.BlockSpec((tm, d), lambda b, j, i: (b * ni + i, 0)),
                pl.BlockSpec((d, tn), lambda b, j, i: (0, j)),
                pl.BlockSpec((d, tn), lambda b, j, i: (0, j)),
                pl.BlockSpec((SUBLANE, tn), lambda b, j, i: (0, j)),
                pl.BlockSpec((1, tn), lambda b, j, i: (0, j))]
    args = [hn, wg, wu, dw, db]
    if carry:
        tail_shape = jax.ShapeDtypeStruct((n_groups, SUBLANE, FF_PAD), F32)
        tail_spec = pl.BlockSpec((None, SUBLANE, tn), lambda b, j, i: (b, 0, j))
        scratch = [pltpu.VMEM((tm, tn), F32)]
    else:
        in_specs.append(pl.BlockSpec((tm, tn), lambda b, j, i: (b * ni + i, j)))
        args.append(prev)
        tail_shape = jax.ShapeDtypeStruct((m, FF_PAD), F32)
        tail_spec = pl.BlockSpec((tm, tn), lambda b, j, i: (b * ni + i, j))
        scratch = []
    return pl.pallas_call(
        functools.partial(_ffn_up_body, seq_len=seq_len, carry=carry),
        grid=(n_groups, nj, ni),
        in_specs=in_specs,
        out_specs=[pl.BlockSpec((tm, tn), lambda b, j, i: (b * ni + i, j)), tail_spec],
        out_shape=[jax.ShapeDtypeStruct((m, FF_PAD), BF16), tail_shape],
        scratch_shapes=scratch,
        compiler_params=_cparams("parallel", "parallel", "arbitrary"),
        name="ffn_up",
    )(*args)


def _ffn_down_body(a_ref, w_ref, x_ref, g_ref, o_ref, acc_ref):
    k = pl.program_id(1)

    @pl.when(k == 0)
    def _():
        acc_ref[...] = jnp.zeros_like(acc_ref)

    acc_ref[...] += _dot(a_ref[...], w_ref[...])

    @pl.when(k == pl.num_programs(1) - 1)
    def _():
        o_ref[...] = x_ref[...] + _rms(acc_ref[...], g_ref[...])


def _ffn_down(act, wd, x1, g, tm=1024, tk=512):
    m, d = x1.shape
    return pl.pallas_call(
        _ffn_down_body,
        grid=(m // tm, FF_PAD // tk),
        in_specs=[pl.BlockSpec((tm, tk), lambda i, k: (i, k)),
                  pl.BlockSpec((tk, d), lambda i, k: (k, 0)),
                  pl.BlockSpec((tm, d), lambda i, k: (i, 0)),
                  pl.BlockSpec((1, d), lambda i, k: (0, 0))],
        out_specs=pl.BlockSpec((tm, d), lambda i, k: (i, 0)),
        out_shape=jax.ShapeDtypeStruct((m, d), F32),
        scratch_shapes=[pltpu.VMEM((tm, d), F32)],
        compiler_params=_cparams("parallel", "arbitrary"),
        name="ffn_down",
    )(act, wd, x1, g)


N_CHUNK_TILES = 16
CHUNKS_PER_TILE = TILE // CMP_STRIDE
CMP_COL_BLOCKS = 2 * KV_WIDTH // LANE


def _compress_body(*refs):
    pt_ref = refs[0]
    del pt_ref
    n_in = N_CHUNK_TILES * CMP_COL_BLOCKS
    tiles = refs[1:1 + n_in]
    pe_ref, w1_ref, b1_ref, w2_ref, b2_ref, kc_ref, vc_ref, lhs_a, lhs_b = refs[1 + n_in:]
    out_refs = (kc_ref, vc_ref)
    for slot in range(2):
        outs = []
        for gp in range(KV_HEADS // 2):
            cb = slot * (KV_HEADS // 2) + gp
            for s in range(CMP_STRIDE):
                x = jnp.concatenate(
                    [tiles[j * CMP_COL_BLOCKS + cb][pl.ds(s, CHUNKS_PER_TILE, stride=CMP_STRIDE), :]
                     for j in range(N_CHUNK_TILES)], axis=0)
                lhs_a[:, s * LANE:(s + 1) * LANE] = (x + pe_ref[slot, 0, s:s + 1, :]).astype(BF16)
                lhs_b[:, s * LANE:(s + 1) * LANE] = (x + pe_ref[slot, 1, s:s + 1, :]).astype(BF16)
            first = _dot(lhs_a[...], w1_ref[slot, 0])
            second = _dot(lhs_b[...], w1_ref[slot, 1])
            hid = first + pltpu.roll(second, TILE - 1, 0) + b1_ref[slot]
            act = _gelu_tanh(hid).astype(BF16)
            for gj in range(2):
                outs.append(_dot(act[:, gj * CMP_HIDDEN:(gj + 1) * CMP_HIDDEN], w2_ref[slot]) + b2_ref[slot])
        out_refs[slot][...] = jnp.concatenate(outs, axis=1)


def _compress(src, tile_map, n_seq, page_table, pe, w1, b1, w2, b2):
    in_specs = [pl.BlockSpec((None, TILE, LANE), functools.partial(tile_map, j, cb))
                for j in range(N_CHUNK_TILES) for cb in range(CMP_COL_BLOCKS)]
    full = lambda shape: pl.BlockSpec(shape, lambda b, pt: (0,) * len(shape))
    in_specs += [full(pe.shape), full(w1.shape), full(b1.shape), full(w2.shape), full(b2.shape)]
    out_spec = pl.BlockSpec((None, TILE, KV_WIDTH), lambda b, pt: (b, 0, 0))
    return pl.pallas_call(
        _compress_body,
        grid_spec=pltpu.PrefetchScalarGridSpec(
            num_scalar_prefetch=1, grid=(n_seq,), in_specs=in_specs, out_specs=[out_spec, out_spec],
            scratch_shapes=[pltpu.VMEM((TILE, CMP_STRIDE * LANE), BF16), pltpu.VMEM((TILE, CMP_STRIDE * LANE), BF16)]),
        out_shape=[jax.ShapeDtypeStruct((n_seq, TILE, KV_WIDTH), F32)] * 2,
        compiler_params=_cparams("parallel"),
        name="compress",
    )(page_table, *([src] * (N_CHUNK_TILES * CMP_COL_BLOCKS)), pe, w1, b1, w2, b2)


def _compress_weights(cmp_pe, cmp_w1, cmp_b1, cmp_w2, cmp_b2):
    pe = cmp_pe.reshape(2, 2, CMP_STRIDE, HEAD_DIM)
    pe = jnp.concatenate([pe, pe], axis=-1)
    w = cmp_w1.reshape(2, 2, CMP_STRIDE, HEAD_DIM, CMP_HIDDEN)
    eye = jnp.eye(2, dtype=w.dtype)
    w1 = jnp.einsum('absdh,ij->absidjh', w, eye).reshape(2, 2, CMP_STRIDE * LANE, 2 * CMP_HIDDEN).astype(BF16)
    b1 = jnp.concatenate([cmp_b1, cmp_b1], axis=-1).reshape(2, 1, 2 * CMP_HIDDEN)
    return pe, w1, b1, cmp_w2.astype(BF16), cmp_b2.reshape(2, 1, HEAD_DIM)


PAGES_PER_PERM = 2


def _pe_bias_body(pe_ref, w1_ref, o_ref):
    for slot in range(2):
        cols = []
        for half in range(2):
            acc = jnp.zeros((2 * SUBLANE, 2 * CMP_HIDDEN), F32)
            for s in range(CMP_STRIDE):
                row = jnp.broadcast_to(pe_ref[slot, half, s:s + 1, :], (2 * SUBLANE, LANE))
                w = w1_ref[slot, s * LANE:(s + 1) * LANE, half * 2 * CMP_HIDDEN:(half + 1) * 2 * CMP_HIDDEN]
                acc = acc + _dot_exact_rhs(row, w)
            cols.append(acc)
        o_ref[slot] = jnp.concatenate(cols, axis=1)


def _pe_bias(pe, w1):
    return pl.pallas_call(
        _pe_bias_body,
        out_shape=jax.ShapeDtypeStruct((2, 2 * SUBLANE, 4 * CMP_HIDDEN), F32),
        name="pe_bias",
    )(pe, w1)


def _compress_paged_body(*refs):
    n_in = N_CHUNK_TILES
    pt_ref = refs[0]
    del pt_ref
    pages = refs[1:1 + n_in]
    perm_ref, w1_ref, peb_ref, b1_ref, w2_ref, b2_ref, kc_ref, vc_ref, xp_ref = refs[1 + n_in:]
    n_perm = N_CHUNK_TILES // PAGES_PER_PERM
    slab = PAGES_PER_PERM * CHUNKS_PER_TILE
    for jp in range(n_perm):
        xt = jnp.concatenate([pages[PAGES_PER_PERM * jp + jj][...] for jj in range(PAGES_PER_PERM)], axis=1)
        xp_ref[jp] = _dot_nt(perm_ref[...], xt.astype(BF16)).astype(BF16)
    out_refs = (kc_ref, vc_ref)
    for slot in range(2):
        outs = []
        for gp in range(KV_HEADS // 2):
            cb = slot * (KV_HEADS // 2) + gp
            acc = jnp.zeros((TILE, 4 * CMP_HIDDEN), F32)
            for sp in range(CMP_STRIDE // 2):
                lhs = jnp.concatenate(
                    [jnp.concatenate([xp_ref[jp, s * slab:(s + 1) * slab, cb * LANE:(cb + 1) * LANE]
                                      for jp in range(n_perm)], axis=0)
                     for s in (2 * sp, 2 * sp + 1)], axis=1)
                acc = acc + _dot(lhs, w1_ref[slot, 2 * sp * LANE:(2 * sp + 2) * LANE, :])
            acc = acc + peb_ref[slot, 0:1, :]
            hid = (acc[:, 0:2 * CMP_HIDDEN] + pltpu.roll(acc[:, 2 * CMP_HIDDEN:], TILE - 1, 0) + b1_ref[slot])
            act = _gelu_tanh(hid).astype(BF16)
            for gj in range(2):
                outs.append(_dot(act[:, gj * CMP_HIDDEN:(gj + 1) * CMP_HIDDEN], w2_ref[slot]) + b2_ref[slot])
        out_refs[slot][...] = jnp.concatenate(outs, axis=1)


def _compress_paged(cache_t, page_table, perm, w1, peb, b1, w2, b2):
    n_seq = page_table.shape[0]
    in_specs = [pl.BlockSpec((None, 2 * KV_WIDTH, PAGE_SIZE), functools.partial(lambda j, b, pt: (pt[b, j], 0, 0), j))
                for j in range(N_CHUNK_TILES)]
    const = lambda arr: pl.BlockSpec(arr.shape, lambda b, pt: (0,) * arr.ndim)
    consts = [perm, w1, peb, b1, w2, b2]
    in_specs += [const(a) for a in consts]
    out_spec = pl.BlockSpec((None, TILE, KV_WIDTH), lambda b, pt: (b, 0, 0))
    rows = PAGES_PER_PERM * PAGE_SIZE
    return pl.pallas_call(
        _compress_paged_body,
        grid_spec=pltpu.PrefetchScalarGridSpec(
            num_scalar_prefetch=1, grid=(n_seq,), in_specs=in_specs, out_specs=[out_spec, out_spec],
            scratch_shapes=[pltpu.VMEM((N_CHUNK_TILES // PAGES_PER_PERM, rows, 2 * KV_WIDTH), BF16)]),
        out_shape=[jax.ShapeDtypeStruct((n_seq, TILE, KV_WIDTH), F32)] * 2,
        compiler_params=_cparams("parallel"),
        name="compress_paged",
    )(page_table, *([cache_t] * N_CHUNK_TILES), *consts)


def _compress_paged_weights(cmp_w1):
    w = cmp_w1.reshape(2, 2, CMP_STRIDE, HEAD_DIM, CMP_HIDDEN)
    eye = jnp.eye(2, dtype=w.dtype)
    w1 = jnp.einsum('absdh,ij->asidbjh', w, eye).reshape(2, CMP_STRIDE * LANE, 4 * CMP_HIDDEN).astype(BF16)
    rows = PAGES_PER_PERM * PAGE_SIZE
    r = np.arange(rows)
    s, page, chunk = r // (PAGES_PER_PERM * CHUNKS_PER_TILE), (r // CHUNKS_PER_TILE) % PAGES_PER_PERM, r % CHUNKS_PER_TILE
    src = page * PAGE_SIZE + chunk * CMP_STRIDE + s
    perm = (np.arange(rows)[None, :] == src[:, None]).astype(np.float32)
    return w1, jnp.asarray(perm, BF16)


def _bucket_thresholds():
    n = np.arange(0, 2 * REL_MAX_DIST)
    max_exact = REL_BUCKETS // 2
    nf = np.maximum(n, 1).astype(np.float32)
    large = max_exact + (np.log(nf / np.float32(max_exact)) / np.float32(math.log(REL_MAX_DIST / max_exact))
                         * np.float32(REL_BUCKETS - max_exact)).astype(np.int32)
    bucket = np.where(n < max_exact, n, np.minimum(large, REL_BUCKETS - 1))
    assert (np.diff(bucket) >= 0).all() and (np.diff(bucket) <= 1).all()
    return [int(np.argmax(bucket >= k)) for k in range(1, REL_BUCKETS)]


BUCKET_START = _bucket_thresholds()


def _rel_bias(dist, tab_ref):
    bias = jnp.broadcast_to(tab_ref[0:1, :], dist.shape)
    for k in range(1, REL_BUCKETS):
        bias = jnp.where(dist >= BUCKET_START[k - 1], tab_ref[k:k + 1, :], bias)
    return bias


def _bias_prompt_body(tab_ref, cmpb_ref, near_ref, edge_ref):
    qb = pl.program_id(0)
    shape = (TILE, Q_PER_KV * TILE)
    row = lax.broadcasted_iota(jnp.int32, shape, 0)
    qi = lax.broadcasted_iota(jnp.int32, shape, 1) % TILE
    dist_c = qb * TILE + qi - (row * CMP_STRIDE + CMP_BLOCK - 1)
    for g in range(KV_HEADS):
        cmpb_ref[g] = _rel_bias(dist_c, tab_ref.at[g])

    @pl.when(qb == 0)
    def _():
        for g in range(KV_HEADS):
            for delta in range(2):
                dist = delta * TILE + qi - row
                b = _rel_bias(dist, tab_ref.at[g])
                near_ref[g, delta] = jnp.where(dist >= 0, b, NEG)
        edge_ref[...] = jnp.where(qi <= row, 0.0, NEG)


def _bias_prompt(tab, n_qb):
    lanes = Q_PER_KV * TILE
    return pl.pallas_call(
        _bias_prompt_body,
        grid=(n_qb,),
        in_specs=[pl.BlockSpec(tab.shape, lambda i: (0, 0, 0))],
        out_specs=[pl.BlockSpec((None, KV_HEADS, TILE, lanes), lambda i: (i, 0, 0, 0)),
                   pl.BlockSpec((KV_HEADS, 2, TILE, lanes), lambda i: (0, 0, 0, 0)),
                   pl.BlockSpec((TILE, lanes), lambda i: (0, 0))],
        out_shape=[jax.ShapeDtypeStruct((n_qb, KV_HEADS, TILE, lanes), F32),
                   jax.ShapeDtypeStruct((KV_HEADS, 2, TILE, lanes), F32),
                   jax.ShapeDtypeStruct((TILE, lanes), F32)],
        compiler_params=_cparams("arbitrary"),
        name="bias_prompt",
    )(tab)


def _bias_sample_body(tab_ref, cmpb_ref, cmpbt_ref, selb_ref, winb_ref, *, past_len, n_new, win_len):
    qt = lax.broadcasted_iota(jnp.int32, (TILE, LANE), 1) % n_new
    row = lax.broadcasted_iota(jnp.int32, (TILE, LANE), 0)
    dist = past_len + qt - (row * CMP_STRIDE + CMP_BLOCK - 1)
    tile = jnp.where(dist >= 0, _rel_bias(dist, tab_ref), NEG)
    cmpbt_ref[...] = tile
    cmpb_ref[...] = tile.T
    for j in range(selb_ref.shape[1] // TILE):
        key = row + j * TILE
        dist = past_len + qt - key
        ok = (dist >= 0) & (key < past_len + n_new)
        selb_ref[:, j * TILE:(j + 1) * TILE] = jnp.where(ok, _rel_bias(dist, tab_ref), NEG).T
    for j in range(winb_ref.shape[1] // TILE):
        key = row + j * TILE
        dist = win_len + qt - key
        ok = (dist >= 0) & (dist <= WINDOW) & (key < win_len + n_new)
        winb_ref[:, j * TILE:(j + 1) * TILE] = jnp.where(ok, _rel_bias(dist, tab_ref), NEG).T


def _bias_sample(tab, past_len, n_new, win_len):
    return pl.pallas_call(
        functools.partial(_bias_sample_body, past_len=past_len, n_new=n_new, win_len=win_len),
        out_shape=[jax.ShapeDtypeStruct((LANE, TILE), F32),
                   jax.ShapeDtypeStruct((TILE, LANE), F32),
                   jax.ShapeDtypeStruct((LANE, past_len + TILE), F32),
                   jax.ShapeDtypeStruct((LANE, win_len + TILE), F32)],
        name="bias_sample",
    )(tab)


def _select_blocks(imp, blk, q_blk):
    valid = blk <= q_blk
    forced = (blk == 0) | (blk == q_blk) | (blk == q_blk - 1)
    score = jnp.where(valid, imp + jnp.where(forced, FORCE_BONUS, 0.0), -jnp.inf)
    rank = jnp.zeros(score.shape, jnp.int32)
    for i in range(score.shape[0]):
        row = score[i:i + 1, :]
        beats = (row > score) | ((row == score) & (blk > i))
        rank = rank + beats.astype(jnp.int32)
    return jnp.where(valid & (rank < N_SELECT), 0.0, NEG)


def _nsa_prompt_body(q_ref, gd_ref, kc_ref, vc_ref, sk_ref, sv_ref, wk_ref, wv_ref, cmpb_ref, near_ref,
                     edge_ref, tab_ref, impt_ref, o_ref, seladd_ref):
    qb = pl.program_id(1)
    lanes = Q_PER_KV * TILE
    q = q_ref[...] * SCALE
    gate_t = jax.nn.sigmoid(gd_ref[...]).T
    shape = (TILE, lanes)
    n_io = lax.broadcasted_iota(jnp.int32, shape, 0)
    qi = lax.broadcasted_iota(jnp.int32, shape, 1) % TILE
    mask_c = qb * TILE + qi - (n_io * CMP_STRIDE + CMP_BLOCK - 1) >= 0
    n_sel = seladd_ref.shape[0]
    blk = lax.broadcasted_iota(jnp.int32, (n_sel, TILE), 0)
    q_blk = (qb * TILE + lax.broadcasted_iota(jnp.int32, (n_sel, TILE), 1)) // SEL_BLOCK
    outs = []
    for g in range(KV_HEADS):
        hs = slice(g * HEAD_DIM, (g + 1) * HEAD_DIM)
        qs = jnp.concatenate(
            [q[:, (g * Q_PER_KV + r) * HEAD_DIM:(g * Q_PER_KV + r + 1) * HEAD_DIM] for r in range(Q_PER_KV)],
            axis=0).astype(BF16)
        far_bias = tab_ref[g, REL_BUCKETS - 1:REL_BUCKETS, :]

        def gate_row(branch, g=g):
            c0 = branch * N_HEADS + g * Q_PER_KV
            return jnp.concatenate([gate_t[c0 + r:c0 + r + 1, :] for r in range(Q_PER_KV)], axis=1)

        s = _dot_nt(kc_ref[:, hs].astype(BF16), qs) + cmpb_ref[g]
        m = jnp.max(jnp.where(mask_c, s, NEG), axis=0, keepdims=True)
        e = jnp.where(mask_c, jnp.exp(jnp.where(mask_c, s, m) - m), 0.0)
        p = e / jnp.maximum(jnp.sum(e, axis=0, keepdims=True), TINY)
        out_t = gate_row(0) * _dot(vc_ref[:, hs].T.astype(BF16), p.astype(BF16))
        psum = p[:, 0:TILE]
        for r in range(1, Q_PER_KV):
            psum = psum + p[:, r * TILE:(r + 1) * TILE]
        sa = _select_blocks(_dot_exact_lhs(impt_ref[...], psum), blk, q_blk)
        seladd_ref[...] = jnp.concatenate([sa] * Q_PER_KV, axis=1)

        def sel_rows(kt):
            half = SEL_BLOCK
            return jnp.concatenate(
                [jnp.broadcast_to(seladd_ref[pl.ds(2 * kt, 1), :], (half, lanes)),
                 jnp.broadcast_to(seladd_ref[pl.ds(2 * kt + 1, 1), :], (half, lanes))], axis=0)

        def tile(carry, k_ref, v_ref, kt, bias, hs=hs, qs=qs):
            m, l, acc = carry
            r0 = pl.multiple_of(kt * TILE, TILE)
            s = _dot_nt(k_ref[pl.ds(r0, TILE), hs].astype(BF16), qs) + bias
            m_new = jnp.maximum(m, jnp.max(s, axis=0, keepdims=True))
            alpha = jnp.exp(m - m_new)
            p = jnp.exp(s - m_new)
            l = alpha * l + jnp.sum(p, axis=0, keepdims=True)
            acc = alpha * acc + _dot(v_ref[pl.ds(r0, TILE), hs].T.astype(BF16), p.astype(BF16))
            return m_new, l, acc

        init = (jnp.full((1, lanes), -jnp.inf, F32), jnp.zeros((1, lanes), F32), jnp.zeros((HEAD_DIM, lanes), F32))

        carry = lax.fori_loop(
            0, jnp.maximum(qb - 1, 0),
            lambda kt, c: tile(c, sk_ref, sv_ref, kt, far_bias + sel_rows(kt)), init)
        kt1 = jnp.maximum(qb - 1, 0)
        carry = tile(carry, sk_ref, sv_ref, kt1, near_ref[g, 1] + sel_rows(kt1) + jnp.where(qb >= 1, 0.0, NEG))
        m, l, acc = tile(carry, sk_ref, sv_ref, qb, near_ref[g, 0] + sel_rows(qb))
        out_t = out_t + gate_row(1) * (acc / jnp.maximum(l, TINY))

        carry = init
        for delta in range(WINDOW // TILE, -1, -1):
            kt = qb - delta
            off = jnp.where(kt >= 0, 0.0, NEG)
            if delta == WINDOW // TILE:
                bias = edge_ref[...] + (far_bias + off)
            elif delta >= 2:
                bias = far_bias + off
            else:
                bias = near_ref[g, delta] + off
            carry = tile(carry, wk_ref, wv_ref, jnp.maximum(kt, 0), bias)
        m, l, acc = carry
        out_t = out_t + gate_row(2) * (acc / jnp.maximum(l, TINY))
        outs += [out_t[:, r * TILE:(r + 1) * TILE].T for r in range(Q_PER_KV)]
    o_ref[...] = jnp.concatenate(outs, axis=1).astype(BF16)


def _nsa_prompt(p3, kc, vc, cmpb, near, edge, tab, impt):
    b, t, _ = p3.shape
    n_qb = t // TILE
    lanes = Q_PER_KV * TILE
    kvcol = lambda slot: (P_KV + slot * KV_WIDTH) // KV_WIDTH
    seq_spec = lambda slot: pl.BlockSpec((None, t, KV_WIDTH), lambda i, j: (i, 0, kvcol(slot)))
    const = lambda shape: pl.BlockSpec(shape, lambda i, j: (0,) * len(shape))
    return pl.pallas_call(
        _nsa_prompt_body,
        grid=(b, n_qb),
        in_specs=[pl.BlockSpec((None, TILE, ATTN_WIDTH), lambda i, j: (i, j, 0)),
                  pl.BlockSpec((None, TILE, LANE), lambda i, j: (i, j, P_GD // LANE)),
                  pl.BlockSpec((None, TILE, KV_WIDTH), lambda i, j: (i, 0, 0)),
                  pl.BlockSpec((None, TILE, KV_WIDTH), lambda i, j: (i, 0, 0)),
                  seq_spec(2), seq_spec(3), seq_spec(4), seq_spec(5),
                  pl.BlockSpec((None, KV_HEADS, TILE, lanes), lambda i, j: (j, 0, 0, 0)),
                  const(near.shape), const(edge.shape), const(tab.shape), const(impt.shape)],
        out_specs=pl.BlockSpec((None, TILE, ATTN_WIDTH), lambda i, j: (i, j, 0)),
        out_shape=jax.ShapeDtypeStruct((b, t, ATTN_WIDTH), BF16),
        scratch_shapes=[pltpu.VMEM((impt.shape[0], lanes), F32)],
        compiler_params=_cparams("parallel", "arbitrary"),
        name="nsa_prompt",
    )(p3, p3, kc, vc, p3, p3, p3, p3, cmpb, near, edge, tab, impt)


def _nsa_sample_body(*refs, past_len, n_new):
    n_pages = past_len // TILE
    pt_ref, q_ref, gd_ref, newkv_ref, newwin_ref, kc_ref, vc_ref = refs[:7]
    pages = refs[7:7 + n_pages]
    (win_ref, cmpb_ref, cmpbt_ref, selb_ref, winb_ref, impt_ref, rsum_ref, expand_ref, place_ref,
     o_ref, winout_ref, s_scr) = refs[7 + n_pages:]
    del pt_ref
    win_len = win_ref.shape[1]
    rows_per_g = Q_PER_KV * n_new
    row_h = lax.broadcasted_iota(jnp.int32, (LANE, 1), 0) // n_new
    row_g = row_h // Q_PER_KV

    q = q_ref[...] * SCALE
    qs = jnp.concatenate([q[:, h * HEAD_DIM:(h + 1) * HEAD_DIM] for h in range(N_HEADS)], axis=0)
    qs4 = jnp.concatenate([qs] * KV_HEADS, axis=1)
    col_g = lax.broadcasted_iota(jnp.int32, qs4.shape, 1) // HEAD_DIM
    qbd = jnp.where(row_g == col_g, qs4, 0.0).astype(BF16)

    sig = jnp.concatenate([jax.nn.sigmoid(gd_ref[...])] * N_HEADS, axis=0)
    gate_lane = lax.broadcasted_iota(jnp.int32, sig.shape, 1)

    def gate_col(branch):
        return jnp.sum(jnp.where(gate_lane == branch * N_HEADS + row_h, sig, 0.0), axis=1, keepdims=True)

    def pick(full):
        out = jnp.zeros((LANE, HEAD_DIM), F32)
        for g in range(KV_HEADS):
            out = out + jnp.where(row_g == g, full[:, g * HEAD_DIM:(g + 1) * HEAD_DIM], 0.0)
        return out

    kc = kc_ref[...].astype(BF16)
    s = _dot_nt(qbd, kc) + cmpb_ref[...]
    m = jnp.max(s, axis=1, keepdims=True)
    e = jnp.exp(s - m)
    p = e / jnp.maximum(jnp.sum(e, axis=1, keepdims=True), TINY)
    out = gate_col(0) * pick(_dot(p.astype(BF16), vc_ref[...].astype(BF16)))

    s_t = _dot_nt(kc, qbd) + cmpbt_ref[...]
    m_t = jnp.max(s_t, axis=0, keepdims=True)
    e_t = jnp.exp(s_t - m_t)
    p_t = e_t / jnp.maximum(jnp.sum(e_t, axis=0, keepdims=True), TINY)
    imp = _dot_exact_lhs(impt_ref[...], _dot_exact_rhs(p_t, rsum_ref[...]))
    blk = lax.broadcasted_iota(jnp.int32, imp.shape, 0)
    q_blk = (past_len + lax.broadcasted_iota(jnp.int32, imp.shape, 1) % n_new) // SEL_BLOCK
    chosen_t = jnp.where(_select_blocks(imp, blk, q_blk) == 0.0, 1.0, 0.0)
    chosen = jnp.concatenate([chosen_t, jnp.zeros((LANE - chosen_t.shape[0], LANE), F32)], axis=0).T
    sel_mask = (_dot(chosen.astype(BF16), expand_ref[...]) - 1.0) * (-NEG)

    def attend(tiles, bias_ref, mask):
        m = jnp.full((LANE, 1), -jnp.inf, F32)
        for j, (score, _) in enumerate(tiles):
            s = score() + bias_ref[:, j * TILE:(j + 1) * TILE]
            if mask is not None:
                s = s + mask[:, j * TILE:(j + 1) * TILE]
            s_scr[:, j * TILE:(j + 1) * TILE] = s
            m = jnp.maximum(m, jnp.max(s, axis=1, keepdims=True))
        l = jnp.zeros((LANE, 1), F32)
        acc = jnp.zeros((LANE, KV_WIDTH), F32)
        for j, (_, pv) in enumerate(tiles):
            p = jnp.exp(s_scr[:, j * TILE:(j + 1) * TILE] - m)
            l = l + jnp.sum(p, axis=1, keepdims=True)
            acc = acc + pv(p.astype(BF16))
        return pick(acc) / jnp.maximum(l, TINY)

    def stored_tile(ref, lanes):
        return (lambda: _dot(qbd, ref[0:KV_WIDTH, lanes].astype(BF16)),
                lambda p: _dot_nt(p, ref[KV_WIDTH:2 * KV_WIDTH, lanes].astype(BF16)))

    def new_tile(ref):
        rows = jnp.concatenate([ref[...], jnp.zeros((TILE - n_new, ref.shape[1]), F32)], axis=0)
        k = rows[:, 0:KV_WIDTH].astype(BF16)
        v = rows[:, KV_WIDTH:2 * KV_WIDTH].astype(BF16)
        return rows, (lambda: _dot_nt(qbd, k), lambda p: _dot(p, v))

    _, new_kv = new_tile(newkv_ref)
    tiles = [stored_tile(pg, slice(None)) for pg in pages] + [new_kv]
    out = out + gate_col(1) * attend(tiles, selb_ref, sel_mask)

    new_rows, new_win = new_tile(newwin_ref)
    n_wt = win_len // TILE
    tiles = [stored_tile(win_ref, slice(j * TILE, (j + 1) * TILE)) for j in range(n_wt)] + [new_win]
    out = out + gate_col(2) * attend(tiles, winb_ref, None)
    o_ref[...] = jnp.concatenate([out[h * n_new:(h + 1) * n_new, :] for h in range(N_HEADS)], axis=1).astype(BF16)

    lane = lax.broadcasted_iota(jnp.int32, (1, LANE), 1)
    hi, mid, lo = _split3(new_rows)
    tail = _dot_tn(hi, place_ref[...]) + _dot_tn(mid, place_ref[...]) + _dot_tn(lo, place_ref[...])
    rolled = [pltpu.roll(win_ref[:, j * TILE:(j + 1) * TILE], TILE - n_new, 1) for j in range(n_wt)]
    for j in range(n_wt):
        nxt = rolled[j + 1] if j + 1 < n_wt else tail
        winout_ref[:, j * TILE:(j + 1) * TILE] = jnp.where(lane < TILE - n_new, rolled[j], nxt)


def _nsa_sample(p3, kc, vc, cache_t, win_t, page_table, cmpb, cmpbt, selb, winb, impt, rsum, expand, place):
    b, n_new, _ = p3.shape
    n_pages = page_table.shape[1]
    past_len = n_pages * PAGE_SIZE
    win_len = win_t.shape[2]
    kvblk = lambda slot: (P_KV + slot * KV_WIDTH) // (2 * KV_WIDTH)
    const = lambda arr: pl.BlockSpec(arr.shape, lambda i, pt: (0,) * arr.ndim)
    in_specs = [pl.BlockSpec((None, n_new, ATTN_WIDTH), lambda i, pt: (i, 0, 0)),
                pl.BlockSpec((None, n_new, LANE), lambda i, pt: (i, 0, P_GD // LANE)),
                pl.BlockSpec((None, n_new, 2 * KV_WIDTH), lambda i, pt: (i, 0, kvblk(2))),
                pl.BlockSpec((None, n_new, 2 * KV_WIDTH), lambda i, pt: (i, 0, kvblk(4))),
                pl.BlockSpec((None, TILE, KV_WIDTH), lambda i, pt: (i, 0, 0)),
                pl.BlockSpec((None, TILE, KV_WIDTH), lambda i, pt: (i, 0, 0))]
    in_specs += [pl.BlockSpec((None, 2 * KV_WIDTH, PAGE_SIZE), functools.partial(lambda j, i, pt: (pt[i, j], 1, 0), j))
                 for j in range(n_pages)]
    consts = [cmpb, cmpbt, selb, winb, impt, rsum, expand, place]
    in_specs += [pl.BlockSpec((None, 2 * KV_WIDTH, win_len), lambda i, pt: (i, 0, 0))] + [const(a) for a in consts]
    return pl.pallas_call(
        functools.partial(_nsa_sample_body, past_len=past_len, n_new=n_new),
        grid_spec=pltpu.PrefetchScalarGridSpec(
            num_scalar_prefetch=1, grid=(b,), in_specs=in_specs,
            out_specs=[pl.BlockSpec((None, n_new, ATTN_WIDTH), lambda i, pt: (i, 0, 0)),
                       pl.BlockSpec((None, 2 * KV_WIDTH, win_len), lambda i, pt: (i, 0, 0))],
            scratch_shapes=[pltpu.VMEM((LANE, past_len + TILE), F32)]),
        out_shape=[jax.ShapeDtypeStruct((b, n_new, ATTN_WIDTH), BF16),
                   jax.ShapeDtypeStruct((b, 2 * KV_WIDTH, win_len), F32)],
        compiler_params=_cparams("parallel"),
        name="nsa_sample",
    )(page_table, p3, p3, p3, p3, kc, vc, *([cache_t] * n_pages), win_t, *consts)


def _sample_selectors(n_new, n_keys):
    lanes = np.arange(LANE)
    h, t = lanes // n_new, lanes % n_new
    rsum = ((h[:, None] // Q_PER_KV == h[None, :] // Q_PER_KV) & (t[:, None] == t[None, :])).astype(np.float32)
    expand = (np.arange(n_keys)[None, :] // SEL_BLOCK == lanes[:, None]).astype(np.float32)
    place = (lanes[None, :] == LANE - n_new + lanes[:, None]).astype(np.float32)
    return jnp.asarray(rsum, BF16), jnp.asarray(expand, BF16), jnp.asarray(place, BF16)


def _dot_tn(a, b):
    return lax.dot_general(a, b, (((0,), (0,)), ((), ())), preferred_element_type=F32)


def _ssd_body(*refs, q_len, carry):
    if carry:
        (z_ref, xbc_ref, gd_ref, cw_ref, cb_ref, dtb_ref, alog_ref, dexp_ref, ng_ref, e16_ref,
         y_ref, hout_ref, prev_ref, h_scr) = refs
    else:
        (z_ref, xbc_ref, gd_ref, prev_ref, h0_ref, cw_ref, cb_ref, dtb_ref, alog_ref, dexp_ref, ng_ref, e16_ref,
         y_ref, hout_ref) = refs
    n_seq = TILE // q_len
    x = xbc_ref[...]
    if carry:
        @pl.when(pl.program_id(1) == 0)
        def _():
            prev_ref[...] = jnp.zeros_like(prev_ref)
            h_scr[...] = jnp.zeros_like(h_scr)
    prev = prev_ref[...]
    tpos = lax.broadcasted_iota(jnp.int32, (TILE, 1), 0) % q_len
    acc = cb_ref[...] + cw_ref[SSM_CONV - 1:SSM_CONV, :] * x
    for k in range(1, SSM_CONV):
        shifted = jnp.where(tpos >= k, pltpu.roll(x, k, 0), pltpu.roll(prev, k, 0))
        acc = acc + cw_ref[SSM_CONV - 1 - k:SSM_CONV - k, :] * shifted
    if carry:
        prev_ref[...] = x
    xc = _silu(acc)
    xs = xc[:, 0:SSM_INNER]
    b_all = xc[:, SSM_INNER:SSM_INNER + SSM_GROUPS * SSM_STATE]
    c_all = xc[:, SSM_INNER + SSM_GROUPS * SSM_STATE:]

    col = lax.broadcasted_iota(jnp.int32, (1, LANE), 1)
    dt_cols = (col >= DT_COL) & (col < DT_COL + SSM_HEADS)
    dt = jnp.where(dt_cols, jax.nn.softplus(gd_ref[...] + dtb_ref[...]), 0.0)
    a = dt * jnp.where(dt_cols, -jnp.exp(alog_ref[...]), 0.0)
    ri = lax.broadcasted_iota(jnp.int32, (TILE, TILE), 0)
    ci = lax.broadcasted_iota(jnp.int32, (TILE, TILE), 1)
    same = (ri // q_len) == (ci // q_len)
    causal = same & (ci <= ri)
    acum = _dot_exact_lhs(causal.astype(BF16), a)
    last_sel = (ci == (ri // q_len) * q_len + (q_len - 1)).astype(BF16)
    alast = _dot_exact_lhs(last_sel, acum)
    acum_t = acum.T
    e16 = e16_ref[...]
    xw = xs * _dot_exact_rhs(dt, e16)
    xwe = xw * _dot_exact_rhs(jnp.exp(alast - acum), e16)
    grow = jnp.exp(_dot_exact_rhs(acum, e16))
    rowseq = lax.broadcasted_iota(jnp.int32, (TILE, 1), 0) // q_len

    ys = []
    for g in range(SSM_GROUPS):
        cg = c_all[:, g * SSM_STATE:(g + 1) * SSM_STATE].astype(BF16)
        bg = b_all[:, g * SSM_STATE:(g + 1) * SSM_STATE].astype(BF16)
        cb = _dot_nt(cg, bg)
        gl = slice(g * SSM_HPG * SSM_HEADDIM, (g + 1) * SSM_HPG * SSM_HEADDIM)
        hl = slice(g * SSM_HPG, (g + 1) * SSM_HPG)
        for r in range(SSM_HPG):
            h = g * SSM_HPG + r
            c = DT_COL + h
            seg = acum[:, c:c + 1] - acum_t[c:c + 1, :]
            decay = jnp.where(causal, jnp.exp(jnp.where(causal, seg, 0.0)), 0.0)
            ys.append(_dot((cb * decay).astype(BF16), xw[:, h * SSM_HEADDIM:(h + 1) * SSM_HEADDIM].astype(BF16)))
        xwe_g = xwe[:, gl]
        if carry:
            h_old = h_scr[hl].reshape(SSM_HPG * SSM_HEADDIM, SSM_STATE)
            y_off = _dot_nt(cg, h_old.astype(BF16))
            scale = jnp.concatenate(
                [jnp.broadcast_to(jnp.exp(alast[0:1, DT_COL + g * SSM_HPG + r:DT_COL + g * SSM_HPG + r + 1]),
                                  (SSM_HEADDIM, SSM_STATE)) for r in range(SSM_HPG)], axis=0)
            h_new = h_old * scale + _dot_tn(xwe_g.astype(BF16), bg)
            h_scr[hl] = h_new.reshape(SSM_HPG, SSM_HEADDIM, SSM_STATE)
        else:
            offs = []
            for s in range(n_seq):
                h_old = h0_ref[s, hl].reshape(SSM_HPG * SSM_HEADDIM, SSM_STATE)
                offs.append(_dot_nt(cg, h_old.astype(BF16))[s * q_len:(s + 1) * q_len, :])
                a_row = alast[s * q_len:s * q_len + 1, :]
                scale = jnp.concatenate(
                    [jnp.broadcast_to(jnp.exp(a_row[:, DT_COL + g * SSM_HPG + r:DT_COL + g * SSM_HPG + r + 1]),
                                      (SSM_HEADDIM, SSM_STATE)) for r in range(SSM_HPG)], axis=0)
                x_s = jnp.where(rowseq == s, xwe_g, 0.0).astype(BF16)
                h_new = h_old * scale + _dot_tn(x_s, bg)
                hout_ref[s, hl] = h_new.reshape(SSM_HPG, SSM_HEADDIM, SSM_STATE)
            y_off = jnp.concatenate(offs, axis=0)
        for r in range(SSM_HPG):
            h = g * SSM_HPG + r
            ys[h] = ys[h] + y_off[:, r * SSM_HEADDIM:(r + 1) * SSM_HEADDIM] * grow[:, h * SSM_HEADDIM:(h + 1) * SSM_HEADDIM]
    if carry:
        @pl.when(pl.program_id(1) == pl.num_programs(1) - 1)
        def _():
            hout_ref[...] = h_scr[...]

    y = jnp.concatenate(ys, axis=1) + xs * dexp_ref[...]
    y = y * _silu(z_ref[...])
    gw = SSM_INNER // SSM_GROUPS
    parts = []
    for g in range(SSM_GROUPS):
        blk = y[:, g * gw:(g + 1) * gw]
        parts.append(blk * lax.rsqrt(jnp.mean(blk * blk, axis=-1, keepdims=True) + EPS))
    y_ref[...] = (jnp.concatenate(parts, axis=1) * ng_ref[...]).astype(BF16)


def _ssd(p3, prev, h0, cw, cb, dtb, alog, dexp, ng, e16, *, q_len):
    n_grp, rows, _ = p3.shape
    carry = prev is None
    n_chunks = rows // TILE
    n_seq = TILE // q_len
    zspec = pl.BlockSpec((None, TILE, SSM_INNER), lambda i, c: (i, c, P_Z // SSM_INNER))
    xspec = pl.BlockSpec((None, TILE, CONV_DIM), lambda i, c: (i, c, P_XBC // CONV_DIM))
    gspec = pl.BlockSpec((None, TILE, LANE), lambda i, c: (i, c, P_GD // LANE))
    const = lambda arr: pl.BlockSpec(arr.shape, lambda i, c: (0,) * arr.ndim)
    params = [cw, cb, dtb, alog, dexp, ng, e16]
    in_specs = [zspec, xspec, gspec]
    args = [p3, p3, p3]
    if carry:
        n_out_seq = n_grp
        hspec = pl.BlockSpec((None, SSM_HEADS, SSM_HEADDIM, SSM_STATE), lambda i, c: (i, 0, 0, 0))
        scratch = [pltpu.VMEM((TILE, CONV_DIM), F32), pltpu.VMEM((SSM_HEADS, SSM_HEADDIM, SSM_STATE), F32)]
    else:
        n_out_seq = n_grp * n_chunks * n_seq
        hspec = pl.BlockSpec((n_seq, SSM_HEADS, SSM_HEADDIM, SSM_STATE), lambda i, c: (i * n_chunks + c, 0, 0, 0))
        in_specs += [pl.BlockSpec((None, TILE, CONV_DIM), lambda i, c: (i, c, 0)), hspec]
        args += [prev, h0]
        scratch = []
    in_specs += [const(a) for a in params]
    return pl.pallas_call(
        functools.partial(_ssd_body, q_len=q_len, carry=carry),
        grid=(n_grp, n_chunks),
        in_specs=in_specs,
        out_specs=[pl.BlockSpec((None, TILE, SSM_INNER), lambda i, c: (i, c, 0)), hspec],
        out_shape=[jax.ShapeDtypeStruct((n_grp, rows, SSM_INNER), BF16),
                   jax.ShapeDtypeStruct((n_out_seq, SSM_HEADS, SSM_HEADDIM, SSM_STATE), F32)],
        scratch_shapes=scratch,
        compiler_params=_cparams("parallel", "arbitrary"),
        name="ssd",
    )(*args, *params)


def _importance_map_t(n_sel):
    n_cmp = TILE - 1
    c0 = np.arange(n_cmp) * CMP_STRIDE
    s0 = np.arange(n_sel) * SEL_BLOCK
    ov = np.minimum(c0[None, :] + CMP_BLOCK, s0[:, None] + SEL_BLOCK) - np.maximum(c0[None, :], s0[:, None])
    m = np.zeros((-(-n_sel // SUBLANE) * SUBLANE, TILE), np.float32)
    m[:n_sel, :n_cmp] = np.clip(ov, 0, None).astype(np.float32) / CMP_BLOCK
    return jnp.asarray(m, BF16)


def _dt_expand():
    e = np.zeros((LANE, SSM_INNER), np.float32)
    for h in range(SSM_HEADS):
        e[DT_COL + h, h * SSM_HEADDIM:(h + 1) * SSM_HEADDIM] = 1.0
    return jnp.asarray(e, BF16)


def _pad_to(x, axis, size):
    pad = [(0, 0)] * x.ndim
    pad[axis] = (0, size - x.shape[axis])
    return jnp.pad(x, pad)


def _history_rows(state, n_rows, width, group):
    b, k, _ = state.shape
    tile = jnp.pad(state, ((0, 0), (n_rows - k, 0), (0, width - state.shape[2])))
    tile = jnp.roll(tile.reshape(b // group, group, n_rows, width), -1, axis=1)
    return tile.reshape(b * n_rows, width)


def kernel(x_prompt, x_sample, cache_kv, cache_win_kv, state_ssm_conv, state_ssm, state_ffn_conv, page_table, rel_table, ln_mix_pre, w_in, cmp_pe, cmp_w1, cmp_b1, cmp_w2, cmp_b2, ssm_conv_w, ssm_conv_b, ssm_dt_bias, ssm_a_log, ssm_d, ssm_norm_g, w_out, ln_mix_post, ln_ffn_pre, ffn_w_gate, ffn_w_up, ffn_dw_w, ffn_dw_b, ffn_w_down, ln_ffn_post):
    bp, tp, d = x_prompt.shape
    bs, ts, _ = x_sample.shape
    n_pool = cache_kv.shape[1]
    n_pages = page_table.shape[1]
    past_len = n_pages * PAGE_SIZE
    win_len = cache_win_kv.shape[2]
    assert d == D_MODEL and tp == N_CHUNK_TILES * TILE and past_len == N_CHUNK_TILES * TILE
    assert N_HEADS * ts == LANE and ts == SUBLANE and tp >= WINDOW and win_len == WINDOW
    seq_per_step = TILE // ts

    rel = rel_table.astype(F32)
    tab_p = jnp.repeat(rel.reshape(REL_BUCKETS, KV_HEADS, Q_PER_KV).transpose(1, 0, 2), TILE, axis=-1)
    tab_s = jnp.repeat(rel, ts, axis=1)
    cmpb_p, near, edge = _bias_prompt(tab_p, tp // TILE)
    cmpb_s, cmpbt_s, selb, winb = _bias_sample(tab_s, past_len, ts, win_len)
    impt_p = _importance_map_t(-(-tp // SEL_BLOCK))
    impt_s = _importance_map_t(-(-(past_len + ts) // SEL_BLOCK))
    rsum, expand, place = _sample_selectors(ts, past_len + TILE)
    e16 = _dt_expand()
    dummy_pt = jnp.zeros((1, 1), jnp.int32)

    xp = x_prompt.reshape(bp * tp, d)
    xs = x_sample.reshape(bs * ts, d)
    outs = [[] for _ in range(10)]
    for l in range(w_in.shape[0]):
        o1 = ATTN_WIDTH
        o2 = o1 + N_KV_SLOTS * KV_WIDTH
        o3 = o2 + 3 * N_HEADS
        o4 = o3 + SSM_INNER
        o5 = o4 + CONV_DIM
        w = w_in[l]
        wp = jnp.concatenate([w[:, :o1], w[:, o3:o4], w[:, o4:o5], w[:, o1:o2], w[:, o2:o3], w[:, o5:]], axis=1)
        wp = _pad_to(wp, 1, P_WIDTH).astype(BF16)
        g_pre = ln_mix_pre[l].reshape(1, d)
        cw = _compress_weights(cmp_pe[l], cmp_w1[l], cmp_b1[l], cmp_w2[l], cmp_b2[l])
        ssm_w = (_pad_to(ssm_conv_w[l], 0, SUBLANE), ssm_conv_b[l].reshape(1, CONV_DIM),
                 _pad_to(jnp.pad(ssm_dt_bias[l].astype(F32), (DT_COL, 0)), 0, LANE).reshape(1, LANE),
                 _pad_to(jnp.pad(ssm_a_log[l].astype(F32), (DT_COL, 0)), 0, LANE).reshape(1, LANE),
                 jnp.repeat(ssm_d[l].astype(F32), SSM_HEADDIM).reshape(1, SSM_INNER),
                 ssm_norm_g[l].astype(F32).reshape(1, SSM_INNER), e16)
        w_attn = w_out[l, :ATTN_WIDTH].astype(BF16)
        w_ssm = w_out[l, ATTN_WIDTH:].astype(BF16)
        g_post = ln_mix_post[l].reshape(1, d)
        g_pre2 = ln_ffn_pre[l].reshape(1, d)
        wg = _pad_to(ffn_w_gate[l], 1, FF_PAD).astype(BF16)
        wu = _pad_to(ffn_w_up[l], 1, FF_PAD).astype(BF16)
        dw = _pad_to(_pad_to(ffn_dw_w[l], 1, FF_PAD), 0, SUBLANE)
        db = _pad_to(ffn_dw_b[l].reshape(1, D_FF), 1, FF_PAD)
        wd = _pad_to(ffn_w_down[l], 0, FF_PAD).astype(BF16)
        g_post2 = ln_ffn_post[l].reshape(1, d)

        p3 = _proj(xp, g_pre, wp, tm=1024).reshape(bp, tp, P_WIDTH)
        kc, vc = _compress(p3, lambda j, cb, b, pt: (b, j, P_KV // LANE + cb), bp, dummy_pt, *cw)
        attn = _nsa_prompt(p3, kc, vc, cmpb_p, near, edge, tab_p, impt_p)
        ssm, h_new = _ssd(p3, None, None, *ssm_w, q_len=TILE)
        x1, hn = _outproj(attn.reshape(bp * tp, ATTN_WIDTH), ssm.reshape(bp * tp, SSM_INNER), xp,
                          w_attn, w_ssm, g_post, g_pre2)
        act, tail = _ffn_up(hn, wg, wu, dw, db, None, n_groups=bp, seq_len=tp, tm=1024)
        xp = _ffn_down(act, wd, x1, g_post2)
        kv = p3[:, :, P_KV:P_KV + N_KV_SLOTS * KV_WIDTH]
        outs[0].append(kv[:, :, :N_CACHE_SLOTS * KV_WIDTH].reshape(bp, tp, N_CACHE_SLOTS, KV_HEADS, HEAD_DIM))
        outs[2].append(kv[:, tp - WINDOW:, N_CACHE_SLOTS * KV_WIDTH:].reshape(bp, WINDOW, 2, KV_HEADS, HEAD_DIM))
        outs[4].append(p3[:, tp - (SSM_CONV - 1):, P_XBC:P_XBC + CONV_DIM])
        outs[6].append(h_new)
        outs[8].append(tail[:, SUBLANE - (FFN_CONV - 1):, :D_FF])

        p3 = _proj(xs, g_pre, wp, tm=bs * ts).reshape(bs, ts, P_WIDTH)
        cache_t = cache_kv[l].transpose(0, 2, 3, 4, 1).reshape(n_pool, N_CACHE_SLOTS * KV_WIDTH, PAGE_SIZE)
        win_t = cache_win_kv[l].transpose(0, 2, 3, 4, 1).reshape(bs, 2 * KV_WIDTH, win_len)
        w1p, perm = _compress_paged_weights(cmp_w1[l])
        kc, vc = _compress_paged(cache_t, page_table, perm, w1p, _pe_bias(cw[0], w1p), cw[2], cw[3], cw[4])
        attn, win_new = _nsa_sample(p3, kc, vc, cache_t, win_t, page_table, cmpb_s, cmpbt_s, selb, winb,
                                    impt_s, rsum, expand, place)
        win_new = win_new.reshape(bs, 2, KV_HEADS, HEAD_DIM, win_len).transpose(0, 4, 1, 2, 3)
        prev = _history_rows(state_ssm_conv[l], ts, CONV_DIM, seq_per_step).reshape(1, bs * ts, CONV_DIM)
        ssm, h_new = _ssd(p3.reshape(1, bs * ts, P_WIDTH), prev, state_ssm[l], *ssm_w, q_len=ts)
        x1, hn = _outproj(attn.reshape(bs * ts, ATTN_WIDTH), ssm.reshape(bs * ts, SSM_INNER), xs,
                          w_attn, w_ssm, g_post, g_pre2)
        prev_f = _history_rows(state_ffn_conv[l], ts, FF_PAD, bs)
        act, gp = _ffn_up(hn, wg, wu, dw, db, prev_f, n_groups=1, seq_len=ts, tm=bs * ts)
        xs = _ffn_down(act, wd, x1, g_post2, tm=bs * ts)
        kv = p3[:, :, P_KV:P_KV + N_KV_SLOTS * KV_WIDTH]
        outs[1].append(kv[:, :, :N_CACHE_SLOTS * KV_WIDTH].reshape(bs, ts, N_CACHE_SLOTS, KV_HEADS, HEAD_DIM))
        outs[3].append(win_new)
        outs[5].append(p3[:, ts - (SSM_CONV - 1):, P_XBC:P_XBC + CONV_DIM])
        outs[7].append(h_new)
        outs[9].append(gp.reshape(bs, ts, FF_PAD)[:, ts - (FFN_CONV - 1):, :D_FF])
    return (xp.reshape(bp, tp, d), xs.reshape(bs, ts, d)) + tuple(jnp.stack(o) for o in outs)
```

```python
import functools
import math

import numpy as np
import jax
import jax.numpy as jnp
from jax import lax
from jax.experimental import pallas as pl
from jax.experimental.pallas import tpu as pltpu

F32 = jnp.float32
BF16 = jnp.bfloat16

D_MODEL = 2048
PAGE_SIZE = 128
HEAD_DIM = 64
ATTN_WIDTH = D_MODEL // 2
N_HEADS = ATTN_WIDTH // HEAD_DIM
KV_HEADS = 4
Q_PER_KV = N_HEADS // KV_HEADS
CMP_STRIDE = 16
CMP_BLOCK = 2 * CMP_STRIDE
CMP_HIDDEN = 2 * HEAD_DIM
SEL_BLOCK = 64
N_SELECT = 8
WINDOW = 512
N_KV_SLOTS = 6
N_CACHE_SLOTS = 4
KV_WIDTH = KV_HEADS * HEAD_DIM
SSM_INNER = D_MODEL - ATTN_WIDTH
SSM_HEADDIM = 64
SSM_HEADS = SSM_INNER // SSM_HEADDIM
SSM_GROUPS = 4
SSM_HPG = SSM_HEADS // SSM_GROUPS
SSM_STATE = 128
SSM_CONV = 4
CONV_DIM = SSM_INNER + 2 * SSM_GROUPS * SSM_STATE
D_FF = ((8 * D_MODEL // 3 + 127) // 128) * 128
FFN_CONV = 3
REL_BUCKETS = 32
REL_MAX_DIST = 128
EPS = 1e-6
NEG = -1e30
TINY = 1e-30
FORCE_BONUS = 1e4
SCALE = HEAD_DIM ** -0.5

LANE = 128
SUBLANE = 8
TILE = 128
VMEM_LIMIT = 56 * 1024 * 1024

P_Q = 0
P_Z = ATTN_WIDTH
P_XBC = P_Z + SSM_INNER
P_KV = P_XBC + CONV_DIM
P_GD = P_KV + N_KV_SLOTS * KV_WIDTH
GD_WIDTH = 3 * N_HEADS + SSM_HEADS
P_WIDTH = 6144
DT_COL = 3 * N_HEADS
FF_PAD = 5632
FFN_SUB = 256


def _cparams(*sem):
    return pltpu.CompilerParams(dimension_semantics=sem, vmem_limit_bytes=VMEM_LIMIT)


def _rms(x, g):
    return x * lax.rsqrt(jnp.mean(x * x, axis=-1, keepdims=True) + EPS) * g


def _split3(x):
    hi = x.astype(BF16)
    r1 = x - hi.astype(F32)
    mid = r1.astype(BF16)
    lo = (r1 - mid.astype(F32)).astype(BF16)
    return hi, mid, lo


def _dot(a, b):
    return jnp.dot(a, b, preferred_element_type=F32)


def _dot_nt(a, b):
    return lax.dot_general(a, b, (((1,), (1,)), ((), ())), preferred_element_type=F32)


def _dot_exact_rhs(x, sel):
    hi, mid, lo = _split3(x)
    return _dot(hi, sel) + _dot(mid, sel) + _dot(lo, sel)


def _dot_exact_lhs(sel, x):
    hi, mid, lo = _split3(x)
    return _dot(sel, hi) + _dot(sel, mid) + _dot(sel, lo)


def _gelu_tanh(x):
    return 0.5 * x * (1.0 + jnp.tanh(math.sqrt(2.0 / math.pi) * (x + 0.044715 * (x * x * x))))


def _silu(x):
    return x * jax.nn.sigmoid(x)


def _proj_body(x_ref, g_ref, w_ref, o_ref, xn_ref):
    @pl.when(pl.program_id(1) == 0)
    def _():
        xn_ref[...] = _rms(x_ref[...], g_ref[...]).astype(BF16)

    o_ref[...] = _dot(xn_ref[...], w_ref[...])


def _proj(x2d, g, w, tm, tn=512):
    m, k = x2d.shape
    n = w.shape[1]
    return pl.pallas_call(
        _proj_body,
        grid=(m // tm, n // tn),
        in_specs=[pl.BlockSpec((tm, k), lambda i, j: (i, 0)),
                  pl.BlockSpec((1, k), lambda i, j: (0, 0)),
                  pl.BlockSpec((k, tn), lambda i, j: (0, j))],
        out_specs=pl.BlockSpec((tm, tn), lambda i, j: (i, j)),
        out_shape=jax.ShapeDtypeStruct((m, n), F32),
        scratch_shapes=[pltpu.VMEM((tm, k), BF16)],
        compiler_params=_cparams("parallel", "arbitrary"),
        name="proj",
    )(x2d, g, w)


def _outproj_body(a_ref, s_ref, x_ref, wa_ref, ws_ref, g1_ref, g2_ref, x1_ref, hn_ref):
    mix = _dot(a_ref[...], wa_ref[...]) + _dot(s_ref[...], ws_ref[...])
    x1 = x_ref[...] + _rms(mix, g1_ref[...])
    x1_ref[...] = x1
    hn_ref[...] = _rms(x1, g2_ref[...]).astype(BF16)


def _outproj(attn, ssm, x2d, w_attn, w_ssm, g_post, g_pre2, tm=512):
    m, d = x2d.shape
    ka = attn.shape[1]
    const = lambda i: (0, 0)
    return pl.pallas_call(
        _outproj_body,
        grid=(m // tm,),
        in_specs=[pl.BlockSpec((tm, ka), lambda i: (i, 0)),
                  pl.BlockSpec((tm, ka), lambda i: (i, 0)),
                  pl.BlockSpec((tm, d), lambda i: (i, 0)),
                  pl.BlockSpec((ka, d), const),
                  pl.BlockSpec((ka, d), const),
                  pl.BlockSpec((1, d), const),
                  pl.BlockSpec((1, d), const)],
        out_specs=[pl.BlockSpec((tm, d), lambda i: (i, 0)),
                   pl.BlockSpec((tm, d), lambda i: (i, 0))],
        out_shape=[jax.ShapeDtypeStruct((m, d), F32), jax.ShapeDtypeStruct((m, d), BF16)],
        compiler_params=_cparams("parallel"),
        name="outproj",
    )(attn, ssm, x2d, w_attn, w_ssm, g_post, g_pre2)


def _ffn_up_body(*refs, seq_len, carry):
    if carry:
        h_ref, wg_ref, wu_ref, dw_ref, db_ref, act_ref, tail_ref, prev_ref = refs
    else:
        h_ref, wg_ref, wu_ref, dw_ref, db_ref, prev_ref, act_ref, tail_ref = refs
    tm, tn = act_ref.shape
    h = h_ref[...]
    if carry:
        @pl.when(pl.program_id(2) == 0)
        def _():
            prev_ref[...] = jnp.zeros_like(prev_ref)
    tpos = lax.broadcasted_iota(jnp.int32, (tm, 1), 0) % seq_len

    def project(cs):
        return _dot(h, wg_ref[:, cs]), _dot(h, wu_ref[:, cs])

    def gate(cs, gp, up):
        prev = prev_ref[:, cs]
        acc = db_ref[:, cs] + dw_ref[FFN_CONV - 1:FFN_CONV, cs] * gp
        for k in range(1, FFN_CONV):
            shifted = jnp.where(tpos >= k, pltpu.roll(gp, k, 0), pltpu.roll(prev, k, 0))
            acc = acc + dw_ref[FFN_CONV - 1 - k:FFN_CONV - k, cs] * shifted
        act_ref[:, cs] = (_gelu_tanh(acc) * up).astype(BF16)
        if carry:
            prev_ref[:, cs] = gp
            tail_ref[:, cs] = gp[tm - SUBLANE:, :]
        else:
            tail_ref[:, cs] = gp

    pending = None
    for c in range(tn // FFN_SUB):
        cs = slice(c * FFN_SUB, (c + 1) * FFN_SUB)
        cur = (cs,) + project(cs)
        if pending is not None:
            gate(*pending)
        pending = cur
    gate(*pending)


def _ffn_up(hn, wg, wu, dw, db, prev, *, n_groups, seq_len, tm, tn=512):
    m, d = hn.shape
    rows = m // n_groups
    ni = rows // tm
    nj = FF_PAD // tn
    carry = prev is None
    in_specs = [pl.BlockSpec((tm, d), lambda b, j, i: (b * ni + i, 0)),
                pl.BlockSpec((d, tn), lambda b, j, i: (0, j)),
                pl.BlockSpec((d, tn), lambda b, j, i: (0, j)),
                pl.BlockSpec((SUBLANE, tn), lambda b, j, i: (0, j)),
                pl.BlockSpec((1, tn), lambda b, j, i: (0, j))]
    args = [hn, wg, wu, dw, db]
    if carry:
        tail_shape = jax.ShapeDtypeStruct((n_groups, SUBLANE, FF_PAD), F32)
        tail_spec = pl.BlockSpec((None, SUBLANE, tn), lambda b, j, i: (b, 0, j))
        scratch = [pltpu.VMEM((tm, tn), F32)]
    else:
        in_specs.append(pl.BlockSpec((tm, tn), lambda b, j, i: (b * ni + i, j)))
        args.append(prev)
        tail_shape = jax.ShapeDtypeStruct((m, FF_PAD), F32)
        tail_spec = pl.BlockSpec((tm, tn), lambda b, j, i: (b * ni + i, j))
        scratch = []
    return pl.pallas_call(
        functools.partial(_ffn_up_body, seq_len=seq_len, carry=carry),
        grid=(n_groups, nj, ni),
        in_specs=in_specs,
        out_specs=[pl.BlockSpec((tm, tn), lambda b, j, i: (b * ni + i, j)), tail_spec],
        out_shape=[jax.ShapeDtypeStruct((m, FF_PAD), BF16), tail_shape],
        scratch_shapes=scratch,
        compiler_params=_cparams("parallel", "parallel", "arbitrary"),
        name="ffn_up",
    )(*args)


def _ffn_down_body(a_ref, w_ref, x_ref, g_ref, o_ref, acc_ref):
    k = pl.program_id(1)

    @pl.when(k == 0)
    def _():
        acc_ref[...] = jnp.zeros_like(acc_ref)

    acc_ref[...] += _dot(a_ref[...], w_ref[...])

    @pl.when(k == pl.num_programs(1) - 1)
    def _():
        o_ref[...] = x_ref[...] + _rms(acc_ref[...], g_ref[...])


def _ffn_down(act, wd, x1, g, tm=1024, tk=512):
    m, d = x1.shape
    return pl.pallas_call(
        _ffn_down_body,
        grid=(m // tm, FF_PAD // tk),
        in_specs=[pl.BlockSpec((tm, tk), lambda i, k: (i, k)),
                  pl.BlockSpec((tk, d), lambda i, k: (k, 0)),
                  pl.BlockSpec((tm, d), lambda i, k: (i, 0)),
                  pl.BlockSpec((1, d), lambda i, k: (0, 0))],
        out_specs=pl.BlockSpec((tm, d), lambda i, k: (i, 0)),
        out_shape=jax.ShapeDtypeStruct((m, d), F32),
        scratch_shapes=[pltpu.VMEM((tm, d), F32)],
        compiler_params=_cparams("parallel", "arbitrary"),
        name="ffn_down",
    )(act, wd, x1, g)


N_CHUNK_TILES = 16
CHUNKS_PER_TILE = TILE // CMP_STRIDE
CMP_COL_BLOCKS = 2 * KV_WIDTH // LANE


def _compress_body(*refs):
    pt_ref = refs[0]
    del pt_ref
    n_in = N_CHUNK_TILES * CMP_COL_BLOCKS
    tiles = refs[1:1 + n_in]
    pe_ref, w1_ref, b1_ref, w2_ref, b2_ref, kc_ref, vc_ref, lhs_a, lhs_b = refs[1 + n_in:]
    out_refs = (kc_ref, vc_ref)
    for slot in range(2):
        outs = []
        for gp in range(KV_HEADS // 2):
            cb = slot * (KV_HEADS // 2) + gp
            for s in range(CMP_STRIDE):
                x = jnp.concatenate(
                    [tiles[j * CMP_COL_BLOCKS + cb][pl.ds(s, CHUNKS_PER_TILE, stride=CMP_STRIDE), :]
                     for j in range(N_CHUNK_TILES)], axis=0)
                lhs_a[:, s * LANE:(s + 1) * LANE] = (x + pe_ref[slot, 0, s:s + 1, :]).astype(BF16)
                lhs_b[:, s * LANE:(s + 1) * LANE] = (x + pe_ref[slot, 1, s:s + 1, :]).astype(BF16)
            first = _dot(lhs_a[...], w1_ref[slot, 0])
            second = _dot(lhs_b[...], w1_ref[slot, 1])
            hid = first + pltpu.roll(second, TILE - 1, 0) + b1_ref[slot]
            act = _gelu_tanh(hid).astype(BF16)
            for gj in range(2):
                outs.append(_dot(act[:, gj * CMP_HIDDEN:(gj + 1) * CMP_HIDDEN], w2_ref[slot]) + b2_ref[slot])
        out_refs[slot][...] = jnp.concatenate(outs, axis=1)


def _compress(src, tile_map, n_seq, page_table, pe, w1, b1, w2, b2):
    in_specs = [pl.BlockSpec((None, TILE, LANE), functools.partial(tile_map, j, cb))
                for j in range(N_CHUNK_TILES) for cb in range(CMP_COL_BLOCKS)]
    full = lambda shape: pl.BlockSpec(shape, lambda b, pt: (0,) * len(shape))
    in_specs += [full(pe.shape), full(w1.shape), full(b1.shape), full(w2.shape), full(b2.shape)]
    out_spec = pl.BlockSpec((None, TILE, KV_WIDTH), lambda b, pt: (b, 0, 0))
    return pl.pallas_call(
        _compress_body,
        grid_spec=pltpu.PrefetchScalarGridSpec(
            num_scalar_prefetch=1, grid=(n_seq,), in_specs=in_specs, out_specs=[out_spec, out_spec],
            scratch_shapes=[pltpu.VMEM((TILE, CMP_STRIDE * LANE), BF16), pltpu.VMEM((TILE, CMP_STRIDE * LANE), BF16)]),
        out_shape=[jax.ShapeDtypeStruct((n_seq, TILE, KV_WIDTH), F32)] * 2,
        compiler_params=_cparams("parallel"),
        name="compress",
    )(page_table, *([src] * (N_CHUNK_TILES * CMP_COL_BLOCKS)), pe, w1, b1, w2, b2)


def _compress_weights(cmp_pe, cmp_w1, cmp_b1, cmp_w2, cmp_b2):
    pe = cmp_pe.reshape(2, 2, CMP_STRIDE, HEAD_DIM)
    pe = jnp.concatenate([pe, pe], axis=-1)
    w = cmp_w1.reshape(2, 2, CMP_STRIDE, HEAD_DIM, CMP_HIDDEN)
    eye = jnp.eye(2, dtype=w.dtype)
    w1 = jnp.einsum('absdh,ij->absidjh', w, eye).reshape(2, 2, CMP_STRIDE * LANE, 2 * CMP_HIDDEN).astype(BF16)
    b1 = jnp.concatenate([cmp_b1, cmp_b1], axis=-1).reshape(2, 1, 2 * CMP_HIDDEN)
    return pe, w1, b1, cmp_w2.astype(BF16), cmp_b2.reshape(2, 1, HEAD_DIM)


PAGES_PER_PERM = 2


def _pe_bias_body(pe_ref, w1_ref, o_ref):
    for slot in range(2):
        cols = []
        for half in range(2):
            acc = jnp.zeros((2 * SUBLANE, 2 * CMP_HIDDEN), F32)
            for s in range(CMP_STRIDE):
                row = jnp.broadcast_to(pe_ref[slot, half, s:s + 1, :], (2 * SUBLANE, LANE))
                w = w1_ref[slot, s * LANE:(s + 1) * LANE, half * 2 * CMP_HIDDEN:(half + 1) * 2 * CMP_HIDDEN]
                acc = acc + _dot_exact_rhs(row, w)
            cols.append(acc)
        o_ref[slot] = jnp.concatenate(cols, axis=1)


def _pe_bias(pe, w1):
    return pl.pallas_call(
        _pe_bias_body,
        out_shape=jax.ShapeDtypeStruct((2, 2 * SUBLANE, 4 * CMP_HIDDEN), F32),
        name="pe_bias",
    )(pe, w1)


def _compress_paged_body(*refs):
    n_in = N_CHUNK_TILES
    pt_ref = refs[0]
    del pt_ref
    pages = refs[1:1 + n_in]
    perm_ref, w1_ref, peb_ref, b1_ref, w2_ref, b2_ref, kc_ref, vc_ref, xp_ref = refs[1 + n_in:]
    n_perm = N_CHUNK_TILES // PAGES_PER_PERM
    slab = PAGES_PER_PERM * CHUNKS_PER_TILE
    for jp in range(n_perm):
        xt = jnp.concatenate([pages[PAGES_PER_PERM * jp + jj][...] for jj in range(PAGES_PER_PERM)], axis=1)
        xp_ref[jp] = _dot_nt(perm_ref[...], xt.astype(BF16)).astype(BF16)
    out_refs = (kc_ref, vc_ref)
    for slot in range(2):
        outs = []
        for gp in range(KV_HEADS // 2):
            cb = slot * (KV_HEADS // 2) + gp
            acc = jnp.zeros((TILE, 4 * CMP_HIDDEN), F32)
            for sp in range(CMP_STRIDE // 2):
                lhs = jnp.concatenate(
                    [jnp.concatenate([xp_ref[jp, s * slab:(s + 1) * slab, cb * LANE:(cb + 1) * LANE]
                                      for jp in range(n_perm)], axis=0)
                     for s in (2 * sp, 2 * sp + 1)], axis=1)
                acc = acc + _dot(lhs, w1_ref[slot, 2 * sp * LANE:(2 * sp + 2) * LANE, :])
            acc = acc + peb_ref[slot, 0:1, :]
            hid = (acc[:, 0:2 * CMP_HIDDEN] + pltpu.roll(acc[:, 2 * CMP_HIDDEN:], TILE - 1, 0) + b1_ref[slot])
            act = _gelu_tanh(hid).astype(BF16)
            for gj in range(2):
                outs.append(_dot(act[:, gj * CMP_HIDDEN:(gj + 1) * CMP_HIDDEN], w2_ref[slot]) + b2_ref[slot])
        out_refs[slot][...] = jnp.concatenate(outs, axis=1)


def _compress_paged(cache_t, page_table, perm, w1, peb, b1, w2, b2):
    n_seq = page_table.shape[0]
    in_specs = [pl.BlockSpec((None, 2 * KV_WIDTH, PAGE_SIZE), functools.partial(lambda j, b, pt: (pt[b, j], 0, 0), j))
                for j in range(N_CHUNK_TILES)]
    const = lambda arr: pl.BlockSpec(arr.shape, lambda b, pt: (0,) * arr.ndim)
    consts = [perm, w1, peb, b1, w2, b2]
    in_specs += [const(a) for a in consts]
    out_spec = pl.BlockSpec((None, TILE, KV_WIDTH), lambda b, pt: (b, 0, 0))
    rows = PAGES_PER_PERM * PAGE_SIZE
    return pl.pallas_call(
        _compress_paged_body,
        grid_spec=pltpu.PrefetchScalarGridSpec(
            num_scalar_prefetch=1, grid=(n_seq,), in_specs=in_specs, out_specs=[out_spec, out_spec],
            scratch_shapes=[pltpu.VMEM((N_CHUNK_TILES // PAGES_PER_PERM, rows, 2 * KV_WIDTH), BF16)]),
        out_shape=[jax.ShapeDtypeStruct((n_seq, TILE, KV_WIDTH), F32)] * 2,
        compiler_params=_cparams("parallel"),
        name="compress_paged",
    )(page_table, *([cache_t] * N_CHUNK_TILES), *consts)


def _compress_paged_weights(cmp_w1):
    w = cmp_w1.reshape(2, 2, CMP_STRIDE, HEAD_DIM, CMP_HIDDEN)
    eye = jnp.eye(2, dtype=w.dtype)
    w1 = jnp.einsum('absdh,ij->asidbjh', w, eye).reshape(2, CMP_STRIDE * LANE, 4 * CMP_HIDDEN).astype(BF16)
    rows = PAGES_PER_PERM * PAGE_SIZE
    r = np.arange(rows)
    s, page, chunk = r // (PAGES_PER_PERM * CHUNKS_PER_TILE), (r // CHUNKS_PER_TILE) % PAGES_PER_PERM, r % CHUNKS_PER_TILE
    src = page * PAGE_SIZE + chunk * CMP_STRIDE + s
    perm = (np.arange(rows)[None, :] == src[:, None]).astype(np.float32)
    return w1, jnp.asarray(perm, BF16)


def _bucket_thresholds():
    n = np.arange(0, 2 * REL_MAX_DIST)
    max_exact = REL_BUCKETS // 2
    nf = np.maximum(n, 1).astype(np.float32)
    large = max_exact + (np.log(nf / np.float32(max_exact)) / np.float32(math.log(REL_MAX_DIST / max_exact))
                         * np.float32(REL_BUCKETS - max_exact)).astype(np.int32)
    bucket = np.where(n < max_exact, n, np.minimum(large, REL_BUCKETS - 1))
    assert (np.diff(bucket) >= 0).all() and (np.diff(bucket) <= 1).all()
    return [int(np.argmax(bucket >= k)) for k in range(1, REL_BUCKETS)]


BUCKET_START = _bucket_thresholds()


def _rel_bias(dist, tab_ref):
    bias = jnp.broadcast_to(tab_ref[0:1, :], dist.shape)
    for k in range(1, REL_BUCKETS):
        bias = jnp.where(dist >= BUCKET_START[k - 1], tab_ref[k:k + 1, :], bias)
    return bias


def _bias_prompt_body(tab_ref, cmpb_ref, near_ref, edge_ref):
    qb = pl.program_id(0)
    shape = (TILE, Q_PER_KV * TILE)
    row = lax.broadcasted_iota(jnp.int32, shape, 0)
    qi = lax.broadcasted_iota(jnp.int32, shape, 1) % TILE
    dist_c = qb * TILE + qi - (row * CMP_STRIDE + CMP_BLOCK - 1)
    for g in range(KV_HEADS):
        cmpb_ref[g] = jnp.where(dist_c >= 0, _rel_bias(dist_c, tab_ref.at[g]), NEG)

    @pl.when(qb == 0)
    def _():
        for g in range(KV_HEADS):
            for delta in range(2):
                dist = delta * TILE + qi - row
                b = _rel_bias(dist, tab_ref.at[g])
                near_ref[g, delta] = jnp.where(dist >= 0, b, NEG)
        edge_ref[...] = jnp.where(qi <= row, 0.0, NEG)


def _bias_prompt(tab, n_qb):
    lanes = Q_PER_KV * TILE
    return pl.pallas_call(
        _bias_prompt_body,
        grid=(n_qb,),
        in_specs=[pl.BlockSpec(tab.shape, lambda i: (0, 0, 0))],
        out_specs=[pl.BlockSpec((None, KV_HEADS, TILE, lanes), lambda i: (i, 0, 0, 0)),
                   pl.BlockSpec((KV_HEADS, 2, TILE, lanes), lambda i: (0, 0, 0, 0)),
                   pl.BlockSpec((TILE, lanes), lambda i: (0, 0))],
        out_shape=[jax.ShapeDtypeStruct((n_qb, KV_HEADS, TILE, lanes), F32),
                   jax.ShapeDtypeStruct((KV_HEADS, 2, TILE, lanes), F32),
                   jax.ShapeDtypeStruct((TILE, lanes), F32)],
        compiler_params=_cparams("arbitrary"),
        name="bias_prompt",
    )(tab)


def _bias_sample_body(tab_ref, cmpb_ref, cmpbt_ref, selb_ref, winb_ref, *, past_len, n_new, win_len):
    qt = lax.broadcasted_iota(jnp.int32, (TILE, LANE), 1) % n_new
    row = lax.broadcasted_iota(jnp.int32, (TILE, LANE), 0)
    dist = past_len + qt - (row * CMP_STRIDE + CMP_BLOCK - 1)
    tile = jnp.where(dist >= 0, _rel_bias(dist, tab_ref), NEG)
    cmpbt_ref[...] = tile
    cmpb_ref[...] = tile.T
    for j in range(selb_ref.shape[1] // TILE):
        key = row + j * TILE
        dist = past_len + qt - key
        ok = (dist >= 0) & (key < past_len + n_new)
        selb_ref[:, j * TILE:(j + 1) * TILE] = jnp.where(ok, _rel_bias(dist, tab_ref), NEG).T
    for j in range(winb_ref.shape[1] // TILE):
        key = row + j * TILE
        dist = win_len + qt - key
        ok = (dist >= 0) & (dist <= WINDOW) & (key < win_len + n_new)
        winb_ref[:, j * TILE:(j + 1) * TILE] = jnp.where(ok, _rel_bias(dist, tab_ref), NEG).T


def _bias_sample(tab, past_len, n_new, win_len):
    return pl.pallas_call(
        functools.partial(_bias_sample_body, past_len=past_len, n_new=n_new, win_len=win_len),
        out_shape=[jax.ShapeDtypeStruct((LANE, TILE), F32),
                   jax.ShapeDtypeStruct((TILE, LANE), F32),
                   jax.ShapeDtypeStruct((LANE, past_len + TILE), F32),
                   jax.ShapeDtypeStruct((LANE, win_len + TILE), F32)],
        name="bias_sample",
    )(tab)


def _select_blocks(imp, blk, q_blk):
    valid = blk <= q_blk
    forced = (blk == 0) | (blk == q_blk) | (blk == q_blk - 1)
    score = jnp.where(valid, imp + jnp.where(forced, FORCE_BONUS, 0.0), -jnp.inf)
    rank = jnp.zeros(score.shape, jnp.int32)
    for i in range(score.shape[0]):
        row = score[i:i + 1, :]
        beats = (row > score) | ((row == score) & (blk > i))
        rank = rank + beats.astype(jnp.int32)
    return jnp.where(valid & (rank < N_SELECT), 0.0, NEG)


def _nsa_prompt_body(q_ref, gd_ref, kc_ref, vc_ref, sk_ref, sv_ref, wk_ref, wv_ref, cmpb_ref, near_ref,
                     edge_ref, tab_ref, impt_ref, o_ref, qs_ref, seladd_ref, selfar_ref, m_ref, l_ref, acc_ref,
                     out_ref):
    qb = pl.program_id(1)
    lanes = Q_PER_KV * TILE
    q = q_ref[...] * SCALE
    gate_t = jax.nn.sigmoid(gd_ref[...]).T
    n_sel = seladd_ref.shape[1]
    blk = lax.broadcasted_iota(jnp.int32, (n_sel, TILE), 0)
    q_blk = (qb * TILE + lax.broadcasted_iota(jnp.int32, (n_sel, TILE), 1)) // SEL_BLOCK
    groups = range(KV_HEADS)

    def hs(g):
        return slice(g * HEAD_DIM, (g + 1) * HEAD_DIM)

    def gate_row(branch, g):
        c0 = branch * N_HEADS + g * Q_PER_KV
        return jnp.concatenate([gate_t[c0 + r:c0 + r + 1, :] for r in range(Q_PER_KV)], axis=1)

    def far_bias(g):
        return tab_ref[g, REL_BUCKETS - 1:REL_BUCKETS, :]

    for g in groups:
        qs = jnp.concatenate(
            [q[:, (g * Q_PER_KV + r) * HEAD_DIM:(g * Q_PER_KV + r + 1) * HEAD_DIM] for r in range(Q_PER_KV)],
            axis=0).astype(BF16)
        qs_ref[g] = qs
        s = _dot_nt(kc_ref[:, hs(g)].astype(BF16), qs) + cmpb_ref[g]
        m = jnp.max(s, axis=0, keepdims=True)
        e = jnp.exp(s - m)
        norm = jnp.where(m > 0.5 * NEG, 1.0 / jnp.maximum(jnp.sum(e, axis=0, keepdims=True), TINY), 0.0)
        p = e * norm
        out_ref[g] = gate_row(0, g) * _dot(vc_ref[:, hs(g)].T.astype(BF16), p.astype(BF16))
        psum = p[:, 0:TILE]
        for r in range(1, Q_PER_KV):
            psum = psum + p[:, r * TILE:(r + 1) * TILE]
        sa = _select_blocks(_dot_exact_lhs(impt_ref[...], psum), blk, q_blk)
        sa = jnp.concatenate([sa] * Q_PER_KV, axis=1)
        seladd_ref[g] = sa
        selfar_ref[g] = sa + far_bias(g)

    def sel_rows(ref, g, kt):
        half = SEL_BLOCK
        return jnp.concatenate(
            [jnp.broadcast_to(ref[g, pl.ds(2 * kt, 1), :], (half, lanes)),
             jnp.broadcast_to(ref[g, pl.ds(2 * kt + 1, 1), :], (half, lanes))], axis=0)

    def reset():
        m_ref[...] = jnp.full(m_ref.shape, -jnp.inf, F32)
        l_ref[...] = jnp.zeros(l_ref.shape, F32)
        acc_ref[...] = jnp.zeros(acc_ref.shape, F32)

    def tile(k_ref, v_ref, kt, bias_of):
        r0 = pl.multiple_of(kt * TILE, TILE)
        k = k_ref[pl.ds(r0, TILE), :].astype(BF16)
        vt = v_ref[pl.ds(r0, TILE), :].T.astype(BF16)
        scores = [_dot_nt(k[:, hs(g)], qs_ref[g]) + bias_of(g) for g in groups]
        probs, alphas = [], []
        for g in groups:
            m = m_ref[g]
            m_new = jnp.maximum(m, jnp.max(scores[g], axis=0, keepdims=True))
            alpha = jnp.exp(m - m_new)
            p = jnp.exp(scores[g] - m_new)
            m_ref[g] = m_new
            l_ref[g] = alpha * l_ref[g] + jnp.sum(p, axis=0, keepdims=True)
            probs.append(p.astype(BF16))
            alphas.append(alpha)
        pvs = [_dot(vt[hs(g), :], probs[g]) for g in groups]
        for g in groups:
            acc_ref[g] = alphas[g] * acc_ref[g] + pvs[g]

    def finish(branch, g):
        return gate_row(branch, g) * (acc_ref[g] / jnp.maximum(l_ref[g], TINY))

    reset()

    def far_body(kt, carry):
        tile(sk_ref, sv_ref, kt, lambda g: sel_rows(selfar_ref, g, kt))
        return carry

    lax.fori_loop(0, jnp.maximum(qb - 1, 0), far_body, 0)
    kt1 = jnp.maximum(qb - 1, 0)
    off1 = jnp.where(qb >= 1, 0.0, NEG)
    tile(sk_ref, sv_ref, kt1, lambda g: near_ref[g, 1] + (sel_rows(seladd_ref, g, kt1) + off1))
    tile(sk_ref, sv_ref, qb, lambda g: near_ref[g, 0] + sel_rows(seladd_ref, g, qb))
    for g in groups:
        out_ref[g] = out_ref[g] + finish(1, g)

    reset()
    for delta in range(WINDOW // TILE, -1, -1):
        kt = qb - delta
        off = jnp.where(kt >= 0, 0.0, NEG)
        if delta == WINDOW // TILE:
            bias_of = lambda g, off=off: edge_ref[...] + (far_bias(g) + off)
        elif delta >= 2:
            bias_of = lambda g, off=off: far_bias(g) + off
        else:
            bias_of = lambda g, off=off, delta=delta: near_ref[g, delta] + off
        tile(wk_ref, wv_ref, jnp.maximum(kt, 0), bias_of)
    outs = []
    for g in groups:
        out_t = out_ref[g] + finish(2, g)
        outs += [out_t[:, r * TILE:(r + 1) * TILE].T for r in range(Q_PER_KV)]
    o_ref[...] = jnp.concatenate(outs, axis=1).astype(BF16)


def _nsa_prompt(p3, kc, vc, cmpb, near, edge, tab, impt):
    b, t, _ = p3.shape
    n_qb = t // TILE
    lanes = Q_PER_KV * TILE
    kvcol = lambda slot: (P_KV + slot * KV_WIDTH) // KV_WIDTH
    seq_spec = lambda slot: pl.BlockSpec((None, t, KV_WIDTH), lambda i, j: (i, 0, kvcol(slot)))
    const = lambda shape: pl.BlockSpec(shape, lambda i, j: (0,) * len(shape))
    return pl.pallas_call(
        _nsa_prompt_body,
        grid=(b, n_qb),
        in_specs=[pl.BlockSpec((None, TILE, ATTN_WIDTH), lambda i, j: (i, j, 0)),
                  pl.BlockSpec((None, TILE, LANE), lambda i, j: (i, j, P_GD // LANE)),
                  pl.BlockSpec((None, TILE, KV_WIDTH), lambda i, j: (i, 0, 0)),
                  pl.BlockSpec((None, TILE, KV_WIDTH), lambda i, j: (i, 0, 0)),
                  seq_spec(2), seq_spec(3), seq_spec(4), seq_spec(5),
                  pl.BlockSpec((None, KV_HEADS, TILE, lanes), lambda i, j: (j, 0, 0, 0)),
                  const(near.shape), const(edge.shape), const(tab.shape), const(impt.shape)],
        out_specs=pl.BlockSpec((None, TILE, ATTN_WIDTH), lambda i, j: (i, j, 0)),
        out_shape=jax.ShapeDtypeStruct((b, t, ATTN_WIDTH), BF16),
        scratch_shapes=[pltpu.VMEM((KV_HEADS, lanes, HEAD_DIM), BF16),
                        pltpu.VMEM((KV_HEADS, impt.shape[0], lanes), F32),
                        pltpu.VMEM((KV_HEADS, impt.shape[0], lanes), F32),
                        pltpu.VMEM((KV_HEADS, 1, lanes), F32),
                        pltpu.VMEM((KV_HEADS, 1, lanes), F32),
                        pltpu.VMEM((KV_HEADS, HEAD_DIM, lanes), F32),
                        pltpu.VMEM((KV_HEADS, HEAD_DIM, lanes), F32)],
        compiler_params=_cparams("parallel", "arbitrary"),
        name="nsa_prompt",
    )(p3, p3, kc, vc, p3, p3, p3, p3, cmpb, near, edge, tab, impt)


def _nsa_sample_body(*refs, past_len, n_new):
    n_pages = past_len // TILE
    pt_ref, q_ref, gd_ref, newkv_ref, newwin_ref, kc_ref, vc_ref = refs[:7]
    pages = refs[7:7 + n_pages]
    (win_ref, cmpb_ref, cmpbt_ref, selb_ref, winb_ref, impt_ref, rsum_ref, expand_ref, place_ref,
     o_ref, winout_ref, s_scr) = refs[7 + n_pages:]
    del pt_ref
    win_len = win_ref.shape[1]
    rows_per_g = Q_PER_KV * n_new
    row_h = lax.broadcasted_iota(jnp.int32, (LANE, 1), 0) // n_new
    row_g = row_h // Q_PER_KV

    q = q_ref[...] * SCALE
    qs = jnp.concatenate([q[:, h * HEAD_DIM:(h + 1) * HEAD_DIM] for h in range(N_HEADS)], axis=0)
    qs4 = jnp.concatenate([qs] * KV_HEADS, axis=1)
    col_g = lax.broadcasted_iota(jnp.int32, qs4.shape, 1) // HEAD_DIM
    qbd = jnp.where(row_g == col_g, qs4, 0.0).astype(BF16)

    sig = jnp.concatenate([jax.nn.sigmoid(gd_ref[...])] * N_HEADS, axis=0)
    gate_lane = lax.broadcasted_iota(jnp.int32, sig.shape, 1)

    def gate_col(branch):
        return jnp.sum(jnp.where(gate_lane == branch * N_HEADS + row_h, sig, 0.0), axis=1, keepdims=True)

    def pick(full):
        out = jnp.zeros((LANE, HEAD_DIM), F32)
        for g in range(KV_HEADS):
            out = out + jnp.where(row_g == g, full[:, g * HEAD_DIM:(g + 1) * HEAD_DIM], 0.0)
        return out

    kc = kc_ref[...].astype(BF16)
    s = _dot_nt(qbd, kc) + cmpb_ref[...]
    m = jnp.max(s, axis=1, keepdims=True)
    e = jnp.exp(s - m)
    p = e / jnp.maximum(jnp.sum(e, axis=1, keepdims=True), TINY)
    out = gate_col(0) * pick(_dot(p.astype(BF16), vc_ref[...].astype(BF16)))

    s_t = _dot_nt(kc, qbd) + cmpbt_ref[...]
    m_t = jnp.max(s_t, axis=0, keepdims=True)
    e_t = jnp.exp(s_t - m_t)
    p_t = e_t / jnp.maximum(jnp.sum(e_t, axis=0, keepdims=True), TINY)
    imp = _dot_exact_lhs(impt_ref[...], _dot_exact_rhs(p_t, rsum_ref[...]))
    blk = lax.broadcasted_iota(jnp.int32, imp.shape, 0)
    q_blk = (past_len + lax.broadcasted_iota(jnp.int32, imp.shape, 1) % n_new) // SEL_BLOCK
    chosen_t = jnp.where(_select_blocks(imp, blk, q_blk) == 0.0, 1.0, 0.0)
    chosen = jnp.concatenate([chosen_t, jnp.zeros((LANE - chosen_t.shape[0], LANE), F32)], axis=0).T
    sel_mask = (_dot(chosen.astype(BF16), expand_ref[...]) - 1.0) * (-NEG)

    def attend(tiles, bias_ref, mask):
        m = jnp.full((LANE, 1), -jnp.inf, F32)
        for j, (score, _) in enumerate(tiles):
            s = score() + bias_ref[:, j * TILE:(j + 1) * TILE]
            if mask is not None:
                s = s + mask[:, j * TILE:(j + 1) * TILE]
            s_scr[:, j * TILE:(j + 1) * TILE] = s
            m = jnp.maximum(m, jnp.max(s, axis=1, keepdims=True))
        l = jnp.zeros((LANE, 1), F32)
        acc = jnp.zeros((LANE, KV_WIDTH), F32)
        for j, (_, pv) in enumerate(tiles):
            p = jnp.exp(s_scr[:, j * TILE:(j + 1) * TILE] - m)
            l = l + jnp.sum(p, axis=1, keepdims=True)
            acc = acc + pv(p.astype(BF16))
        return pick(acc) / jnp.maximum(l, TINY)

    def stored_tile(ref, lanes):
        return (lambda: _dot(qbd, ref[0:KV_WIDTH, lanes].astype(BF16)),
                lambda p: _dot_nt(p, ref[KV_WIDTH:2 * KV_WIDTH, lanes].astype(BF16)))

    def new_tile(ref):
        rows = jnp.concatenate([ref[...], jnp.zeros((TILE - n_new, ref.shape[1]), F32)], axis=0)
        k = rows[:, 0:KV_WIDTH].astype(BF16)
        v = rows[:, KV_WIDTH:2 * KV_WIDTH].astype(BF16)
        return rows, (lambda: _dot_nt(qbd, k), lambda p: _dot(p, v))

    _, new_kv = new_tile(newkv_ref)
    tiles = [stored_tile(pg, slice(None)) for pg in pages] + [new_kv]
    out = out + gate_col(1) * attend(tiles, selb_ref, sel_mask)

    new_rows, new_win = new_tile(newwin_ref)
    n_wt = win_len // TILE
    tiles = [stored_tile(win_ref, slice(j * TILE, (j + 1) * TILE)) for j in range(n_wt)] + [new_win]
    out = out + gate_col(2) * attend(tiles, winb_ref, None)
    o_ref[...] = jnp.concatenate([out[h * n_new:(h + 1) * n_new, :] for h in range(N_HEADS)], axis=1).astype(BF16)

    lane = lax.broadcasted_iota(jnp.int32, (1, LANE), 1)
    hi, mid, lo = _split3(new_rows)
    tail = _dot_tn(hi, place_ref[...]) + _dot_tn(mid, place_ref[...]) + _dot_tn(lo, place_ref[...])
    rolled = [pltpu.roll(win_ref[:, j * TILE:(j + 1) * TILE], TILE - n_new, 1) for j in range(n_wt)]
    for j in range(n_wt):
        nxt = rolled[j + 1] if j + 1 < n_wt else tail
        winout_ref[:, j * TILE:(j + 1) * TILE] = jnp.where(lane < TILE - n_new, rolled[j], nxt)


def _nsa_sample(p3, kc, vc, cache_t, win_t, page_table, cmpb, cmpbt, selb, winb, impt, rsum, expand, place):
    b, n_new, _ = p3.shape
    n_pages = page_table.shape[1]
    past_len = n_pages * PAGE_SIZE
    win_len = win_t.shape[2]
    kvblk = lambda slot: (P_KV + slot * KV_WIDTH) // (2 * KV_WIDTH)
    const = lambda arr: pl.BlockSpec(arr.shape, lambda i, pt: (0,) * arr.ndim)
    in_specs = [pl.BlockSpec((None, n_new, ATTN_WIDTH), lambda i, pt: (i, 0, 0)),
                pl.BlockSpec((None, n_new, LANE), lambda i, pt: (i, 0, P_GD // LANE)),
                pl.BlockSpec((None, n_new, 2 * KV_WIDTH), lambda i, pt: (i, 0, kvblk(2))),
                pl.BlockSpec((None, n_new, 2 * KV_WIDTH), lambda i, pt: (i, 0, kvblk(4))),
                pl.BlockSpec((None, TILE, KV_WIDTH), lambda i, pt: (i, 0, 0)),
                pl.BlockSpec((None, TILE, KV_WIDTH), lambda i, pt: (i, 0, 0))]
    in_specs += [pl.BlockSpec((None, 2 * KV_WIDTH, PAGE_SIZE), functools.partial(lambda j, i, pt: (pt[i, j], 1, 0), j))
                 for j in range(n_pages)]
    consts = [cmpb, cmpbt, selb, winb, impt, rsum, expand, place]
    in_specs += [pl.BlockSpec((None, 2 * KV_WIDTH, win_len), lambda i, pt: (i, 0, 0))] + [const(a) for a in consts]
    return pl.pallas_call(
        functools.partial(_nsa_sample_body, past_len=past_len, n_new=n_new),
        grid_spec=pltpu.PrefetchScalarGridSpec(
            num_scalar_prefetch=1, grid=(b,), in_specs=in_specs,
            out_specs=[pl.BlockSpec((None, n_new, ATTN_WIDTH), lambda i, pt: (i, 0, 0)),
                       pl.BlockSpec((None, 2 * KV_WIDTH, win_len), lambda i, pt: (i, 0, 0))],
            scratch_shapes=[pltpu.VMEM((LANE, past_len + TILE), F32)]),
        out_shape=[jax.ShapeDtypeStruct((b, n_new, ATTN_WIDTH), BF16),
                   jax.ShapeDtypeStruct((b, 2 * KV_WIDTH, win_len), F32)],
        compiler_params=_cparams("parallel"),
        name="nsa_sample",
    )(page_table, p3, p3, p3, p3, kc, vc, *([cache_t] * n_pages), win_t, *consts)


def _sample_selectors(n_new, n_keys):
    lanes = np.arange(LANE)
    h, t = lanes // n_new, lanes % n_new
    rsum = ((h[:, None] // Q_PER_KV == h[None, :] // Q_PER_KV) & (t[:, None] == t[None, :])).astype(np.float32)
    expand = (np.arange(n_keys)[None, :] // SEL_BLOCK == lanes[:, None]).astype(np.float32)
    place = (lanes[None, :] == LANE - n_new + lanes[:, None]).astype(np.float32)
    return jnp.asarray(rsum, BF16), jnp.asarray(expand, BF16), jnp.asarray(place, BF16)


def _dot_tn(a, b):
    return lax.dot_general(a, b, (((0,), (0,)), ((), ())), preferred_element_type=F32)


def _ssd_body(*refs, q_len, carry):
    if carry:
        (z_ref, xbc_ref, gd_ref, cw_ref, cb_ref, dtb_ref, alog_ref, dexp_ref, ng_ref, e16_ref,
         y_ref, hout_ref, prev_ref, h_scr) = refs
    else:
        (z_ref, xbc_ref, gd_ref, prev_ref, h0_ref, cw_ref, cb_ref, dtb_ref, alog_ref, dexp_ref, ng_ref, e16_ref,
         y_ref, hout_ref) = refs
    n_seq = TILE // q_len
    x = xbc_ref[...]
    if carry:
        @pl.when(pl.program_id(1) == 0)
        def _():
            prev_ref[...] = jnp.zeros_like(prev_ref)
            h_scr[...] = jnp.zeros_like(h_scr)
    prev = prev_ref[...]
    tpos = lax.broadcasted_iota(jnp.int32, (TILE, 1), 0) % q_len
    acc = cb_ref[...] + cw_ref[SSM_CONV - 1:SSM_CONV, :] * x
    for k in range(1, SSM_CONV):
        shifted = jnp.where(tpos >= k, pltpu.roll(x, k, 0), pltpu.roll(prev, k, 0))
        acc = acc + cw_ref[SSM_CONV - 1 - k:SSM_CONV - k, :] * shifted
    if carry:
        prev_ref[...] = x
    xc = _silu(acc)
    xs = xc[:, 0:SSM_INNER]
    b_all = xc[:, SSM_INNER:SSM_INNER + SSM_GROUPS * SSM_STATE]
    c_all = xc[:, SSM_INNER + SSM_GROUPS * SSM_STATE:]

    col = lax.broadcasted_iota(jnp.int32, (1, LANE), 1)
    dt_cols = (col >= DT_COL) & (col < DT_COL + SSM_HEADS)
    dt = jnp.where(dt_cols, jax.nn.softplus(gd_ref[...] + dtb_ref[...]), 0.0)
    a = dt * jnp.where(dt_cols, -jnp.exp(alog_ref[...]), 0.0)
    ri = lax.broadcasted_iota(jnp.int32, (TILE, TILE), 0)
    ci = lax.broadcasted_iota(jnp.int32, (TILE, TILE), 1)
    same = (ri // q_len) == (ci // q_len)
    causal = same & (ci <= ri)
    acum = _dot_exact_lhs(causal.astype(BF16), a)
    last_sel = (ci == (ri // q_len) * q_len + (q_len - 1)).astype(BF16)
    alast = _dot_exact_lhs(last_sel, acum)
    acum_t = acum.T
    e16 = e16_ref[...]
    xw = xs * _dot_exact_rhs(dt, e16)
    xwe = xw * _dot_exact_rhs(jnp.exp(alast - acum), e16)
    grow = jnp.exp(_dot_exact_rhs(acum, e16))
    rowseq = lax.broadcasted_iota(jnp.int32, (TILE, 1), 0) // q_len

    ys = []
    for g in range(SSM_GROUPS):
        cg = c_all[:, g * SSM_STATE:(g + 1) * SSM_STATE].astype(BF16)
        bg = b_all[:, g * SSM_STATE:(g + 1) * SSM_STATE].astype(BF16)
        cb = _dot_nt(cg, bg)
        gl = slice(g * SSM_HPG * SSM_HEADDIM, (g + 1) * SSM_HPG * SSM_HEADDIM)
        hl = slice(g * SSM_HPG, (g + 1) * SSM_HPG)
        for r in range(SSM_HPG):
            h = g * SSM_HPG + r
            c = DT_COL + h
            seg = acum[:, c:c + 1] - acum_t[c:c + 1, :]
            decay = jnp.where(causal, jnp.exp(jnp.where(causal, seg, 0.0)), 0.0)
            ys.append(_dot((cb * decay).astype(BF16), xw[:, h * SSM_HEADDIM:(h + 1) * SSM_HEADDIM].astype(BF16)))
        xwe_g = xwe[:, gl]
        if carry:
            h_old = h_scr[hl].reshape(SSM_HPG * SSM_HEADDIM, SSM_STATE)
            y_off = _dot_nt(cg, h_old.astype(BF16))
            scale = jnp.concatenate(
                [jnp.broadcast_to(jnp.exp(alast[0:1, DT_COL + g * SSM_HPG + r:DT_COL + g * SSM_HPG + r + 1]),
                                  (SSM_HEADDIM, SSM_STATE)) for r in range(SSM_HPG)], axis=0)
            h_new = h_old * scale + _dot_tn(xwe_g.astype(BF16), bg)
            h_scr[hl] = h_new.reshape(SSM_HPG, SSM_HEADDIM, SSM_STATE)
        else:
            offs = []
            for s in range(n_seq):
                h_old = h0_ref[s, hl].reshape(SSM_HPG * SSM_HEADDIM, SSM_STATE)
                offs.append(_dot_nt(cg, h_old.astype(BF16))[s * q_len:(s + 1) * q_len, :])
                a_row = alast[s * q_len:s * q_len + 1, :]
                scale = jnp.concatenate(
                    [jnp.broadcast_to(jnp.exp(a_row[:, DT_COL + g * SSM_HPG + r:DT_COL + g * SSM_HPG + r + 1]),
                                      (SSM_HEADDIM, SSM_STATE)) for r in range(SSM_HPG)], axis=0)
                x_s = jnp.where(rowseq == s, xwe_g, 0.0).astype(BF16)
                h_new = h_old * scale + _dot_tn(x_s, bg)
                hout_ref[s, hl] = h_new.reshape(SSM_HPG, SSM_HEADDIM, SSM_STATE)
            y_off = jnp.concatenate(offs, axis=0)
        for r in range(SSM_HPG):
            h = g * SSM_HPG + r
            ys[h] = ys[h] + y_off[:, r * SSM_HEADDIM:(r + 1) * SSM_HEADDIM] * grow[:, h * SSM_HEADDIM:(h + 1) * SSM_HEADDIM]
    if carry:
        @pl.when(pl.program_id(1) == pl.num_programs(1) - 1)
        def _():
            hout_ref[...] = h_scr[...]

    y = jnp.concatenate(ys, axis=1) + xs * dexp_ref[...]
    y = y * _silu(z_ref[...])
    gw = SSM_INNER // SSM_GROUPS
    parts = []
    for g in range(SSM_GROUPS):
        blk = y[:, g * gw:(g + 1) * gw]
        parts.append(blk * lax.rsqrt(jnp.mean(blk * blk, axis=-1, keepdims=True) + EPS))
    y_ref[...] = (jnp.concatenate(parts, axis=1) * ng_ref[...]).astype(BF16)


def _ssd(p3, prev, h0, cw, cb, dtb, alog, dexp, ng, e16, *, q_len):
    n_grp, rows, _ = p3.shape
    carry = prev is None
    n_chunks = rows // TILE
    n_seq = TILE // q_len
    zspec = pl.BlockSpec((None, TILE, SSM_INNER), lambda i, c: (i, c, P_Z // SSM_INNER))
    xspec = pl.BlockSpec((None, TILE, CONV_DIM), lambda i, c: (i, c, P_XBC // CONV_DIM))
    gspec = pl.BlockSpec((None, TILE, LANE), lambda i, c: (i, c, P_GD // LANE))
    const = lambda arr: pl.BlockSpec(arr.shape, lambda i, c: (0,) * arr.ndim)
    params = [cw, cb, dtb, alog, dexp, ng, e16]
    in_specs = [zspec, xspec, gspec]
    args = [p3, p3, p3]
    if carry:
        n_out_seq = n_grp
        hspec = pl.BlockSpec((None, SSM_HEADS, SSM_HEADDIM, SSM_STATE), lambda i, c: (i, 0, 0, 0))
        scratch = [pltpu.VMEM((TILE, CONV_DIM), F32), pltpu.VMEM((SSM_HEADS, SSM_HEADDIM, SSM_STATE), F32)]
    else:
        n_out_seq = n_grp * n_chunks * n_seq
        hspec = pl.BlockSpec((n_seq, SSM_HEADS, SSM_HEADDIM, SSM_STATE), lambda i, c: (i * n_chunks + c, 0, 0, 0))
        in_specs += [pl.BlockSpec((None, TILE, CONV_DIM), lambda i, c: (i, c, 0)), hspec]
        args += [prev, h0]
        scratch = []
    in_specs += [const(a) for a in params]
    return pl.pallas_call(
        functools.partial(_ssd_body, q_len=q_len, carry=carry),
        grid=(n_grp, n_chunks),
        in_specs=in_specs,
        out_specs=[pl.BlockSpec((None, TILE, SSM_INNER), lambda i, c: (i, c, 0)), hspec],
        out_shape=[jax.ShapeDtypeStruct((n_grp, rows, SSM_INNER), BF16),
                   jax.ShapeDtypeStruct((n_out_seq, SSM_HEADS, SSM_HEADDIM, SSM_STATE), F32)],
        scratch_shapes=scratch,
        compiler_params=_cparams("parallel", "arbitrary"),
        name="ssd",
    )(*args, *params)


def _importance_map_t(n_sel):
    n_cmp = TILE - 1
    c0 = np.arange(n_cmp) * CMP_STRIDE
    s0 = np.arange(n_sel) * SEL_BLOCK
    ov = np.minimum(c0[None, :] + CMP_BLOCK, s0[:, None] + SEL_BLOCK) - np.maximum(c0[None, :], s0[:, None])
    m = np.zeros((-(-n_sel // SUBLANE) * SUBLANE, TILE), np.float32)
    m[:n_sel, :n_cmp] = np.clip(ov, 0, None).astype(np.float32) / CMP_BLOCK
    return jnp.asarray(m, BF16)


def _dt_expand():
    e = np.zeros((LANE, SSM_INNER), np.float32)
    for h in range(SSM_HEADS):
        e[DT_COL + h, h * SSM_HEADDIM:(h + 1) * SSM_HEADDIM] = 1.0
    return jnp.asarray(e, BF16)


def _pad_to(x, axis, size):
    pad = [(0, 0)] * x.ndim
    pad[axis] = (0, size - x.shape[axis])
    return jnp.pad(x, pad)


def _history_rows(state, n_rows, width, group):
    b, k, _ = state.shape
    tile = jnp.pad(state, ((0, 0), (n_rows - k, 0), (0, width - state.shape[2])))
    tile = jnp.roll(tile.reshape(b // group, group, n_rows, width), -1, axis=1)
    return tile.reshape(b * n_rows, width)


def kernel(x_prompt, x_sample, cache_kv, cache_win_kv, state_ssm_conv, state_ssm, state_ffn_conv, page_table, rel_table, ln_mix_pre, w_in, cmp_pe, cmp_w1, cmp_b1, cmp_w2, cmp_b2, ssm_conv_w, ssm_conv_b, ssm_dt_bias, ssm_a_log, ssm_d, ssm_norm_g, w_out, ln_mix_post, ln_ffn_pre, ffn_w_gate, ffn_w_up, ffn_dw_w, ffn_dw_b, ffn_w_down, ln_ffn_post):
    bp, tp, d = x_prompt.shape
    bs, ts, _ = x_sample.shape
    n_pool = cache_kv.shape[1]
    n_pages = page_table.shape[1]
    past_len = n_pages * PAGE_SIZE
    win_len = cache_win_kv.shape[2]
    assert d == D_MODEL and tp == N_CHUNK_TILES * TILE and past_len == N_CHUNK_TILES * TILE
    assert N_HEADS * ts == LANE and ts == SUBLANE and tp >= WINDOW and win_len == WINDOW
    seq_per_step = TILE // ts

    rel = rel_table.astype(F32)
    tab_p = jnp.repeat(rel.reshape(REL_BUCKETS, KV_HEADS, Q_PER_KV).transpose(1, 0, 2), TILE, axis=-1)
    tab_s = jnp.repeat(rel, ts, axis=1)
    cmpb_p, near, edge = _bias_prompt(tab_p, tp // TILE)
    cmpb_s, cmpbt_s, selb, winb = _bias_sample(tab_s, past_len, ts, win_len)
    impt_p = _importance_map_t(-(-tp // SEL_BLOCK))
    impt_s = _importance_map_t(-(-(past_len + ts) // SEL_BLOCK))
    rsum, expand, place = _sample_selectors(ts, past_len + TILE)
    e16 = _dt_expand()
    dummy_pt = jnp.zeros((1, 1), jnp.int32)

    xp = x_prompt.reshape(bp * tp, d)
    xs = x_sample.reshape(bs * ts, d)
    outs = [[] for _ in range(10)]
    for l in range(w_in.shape[0]):
        o1 = ATTN_WIDTH
        o2 = o1 + N_KV_SLOTS * KV_WIDTH
        o3 = o2 + 3 * N_HEADS
        o4 = o3 + SSM_INNER
        o5 = o4 + CONV_DIM
        w = w_in[l]
        wp = jnp.concatenate([w[:, :o1], w[:, o3:o4], w[:, o4:o5], w[:, o1:o2], w[:, o2:o3], w[:, o5:]], axis=1)
        wp = _pad_to(wp, 1, P_WIDTH).astype(BF16)
        g_pre = ln_mix_pre[l].reshape(1, d)
        cw = _compress_weights(cmp_pe[l], cmp_w1[l], cmp_b1[l], cmp_w2[l], cmp_b2[l])
        ssm_w = (_pad_to(ssm_conv_w[l], 0, SUBLANE), ssm_conv_b[l].reshape(1, CONV_DIM),
                 _pad_to(jnp.pad(ssm_dt_bias[l].astype(F32), (DT_COL, 0)), 0, LANE).reshape(1, LANE),
                 _pad_to(jnp.pad(ssm_a_log[l].astype(F32), (DT_COL, 0)), 0, LANE).reshape(1, LANE),
                 jnp.repeat(ssm_d[l].astype(F32), SSM_HEADDIM).reshape(1, SSM_INNER),
                 ssm_norm_g[l].astype(F32).reshape(1, SSM_INNER), e16)
        w_attn = w_out[l, :ATTN_WIDTH].astype(BF16)
        w_ssm = w_out[l, ATTN_WIDTH:].astype(BF16)
        g_post = ln_mix_post[l].reshape(1, d)
        g_pre2 = ln_ffn_pre[l].reshape(1, d)
        wg = _pad_to(ffn_w_gate[l], 1, FF_PAD).astype(BF16)
        wu = _pad_to(ffn_w_up[l], 1, FF_PAD).astype(BF16)
        dw = _pad_to(_pad_to(ffn_dw_w[l], 1, FF_PAD), 0, SUBLANE)
        db = _pad_to(ffn_dw_b[l].reshape(1, D_FF), 1, FF_PAD)
        wd = _pad_to(ffn_w_down[l], 0, FF_PAD).astype(BF16)
        g_post2 = ln_ffn_post[l].reshape(1, d)

        p3 = _proj(xp, g_pre, wp, tm=1024).reshape(bp, tp, P_WIDTH)
        kc, vc = _compress(p3, lambda j, cb, b, pt: (b, j, P_KV // LANE + cb), bp, dummy_pt, *cw)
        attn = _nsa_prompt(p3, kc, vc, cmpb_p, near, edge, tab_p, impt_p)
        ssm, h_new = _ssd(p3, None, None, *ssm_w, q_len=TILE)
        x1, hn = _outproj(attn.reshape(bp * tp, ATTN_WIDTH), ssm.reshape(bp * tp, SSM_INNER), xp,
                          w_attn, w_ssm, g_post, g_pre2)
        act, tail = _ffn_up(hn, wg, wu, dw, db, None, n_groups=bp, seq_len=tp, tm=1024)
        xp = _ffn_down(act, wd, x1, g_post2)
        kv = p3[:, :, P_KV:P_KV + N_KV_SLOTS * KV_WIDTH]
        outs[0].append(kv[:, :, :N_CACHE_SLOTS * KV_WIDTH].reshape(bp, tp, N_CACHE_SLOTS, KV_HEADS, HEAD_DIM))
        outs[2].append(kv[:, tp - WINDOW:, N_CACHE_SLOTS * KV_WIDTH:].reshape(bp, WINDOW, 2, KV_HEADS, HEAD_DIM))
        outs[4].append(p3[:, tp - (SSM_CONV - 1):, P_XBC:P_XBC + CONV_DIM])
        outs[6].append(h_new)
        outs[8].append(tail[:, SUBLANE - (FFN_CONV - 1):, :D_FF])

        p3 = _proj(xs, g_pre, wp, tm=bs * ts).reshape(bs, ts, P_WIDTH)
        cache_t = cache_kv[l].transpose(0, 2, 3, 4, 1).reshape(n_pool, N_CACHE_SLOTS * KV_WIDTH, PAGE_SIZE)
        win_t = cache_win_kv[l].transpose(0, 2, 3, 4, 1).reshape(bs, 2 * KV_WIDTH, win_len)
        w1p, perm = _compress_paged_weights(cmp_w1[l])
        kc, vc = _compress_paged(cache_t, page_table, perm, w1p, _pe_bias(cw[0], w1p), cw[2], cw[3], cw[4])
        attn, win_new = _nsa_sample(p3, kc, vc, cache_t, win_t, page_table, cmpb_s, cmpbt_s, selb, winb,
                                    impt_s, rsum, expand, place)
        win_new = win_new.reshape(bs, 2, KV_HEADS, HEAD_DIM, win_len).transpose(0, 4, 1, 2, 3)
        prev = _history_rows(state_ssm_conv[l], ts, CONV_DIM, seq_per_step).reshape(1, bs * ts, CONV_DIM)
        ssm, h_new = _ssd(p3.reshape(1, bs * ts, P_WIDTH), prev, state_ssm[l], *ssm_w, q_len=ts)
        x1, hn = _outproj(attn.reshape(bs * ts, ATTN_WIDTH), ssm.reshape(bs * ts, SSM_INNER), xs,
                          w_attn, w_ssm, g_post, g_pre2)
        prev_f = _history_rows(state_ffn_conv[l], ts, FF_PAD, bs)
        act, gp = _ffn_up(hn, wg, wu, dw, db, prev_f, n_groups=1, seq_len=ts, tm=bs * ts)
        xs = _ffn_down(act, wd, x1, g_post2, tm=bs * ts)
        kv = p3[:, :, P_KV:P_KV + N_KV_SLOTS * KV_WIDTH]
        outs[1].append(kv[:, :, :N_CACHE_SLOTS * KV_WIDTH].reshape(bs, ts, N_CACHE_SLOTS, KV_HEADS, HEAD_DIM))
        outs[3].append(win_new)
        outs[5].append(p3[:, ts - (SSM_CONV - 1):, P_XBC:P_XBC + CONV_DIM])
        outs[7].append(h_new)
        outs[9].append(gp.reshape(bs, ts, FF_PAD)[:, ts - (FFN_CONV - 1):, :D_FF])
    return (xp.reshape(bp, tp, d), xs.reshape(bs, ts, d)) + tuple(jnp.stack(o) for o in outs)
```

```python
import functools
import math

import numpy as np
import jax
import jax.numpy as jnp
from jax import lax
from jax.experimental import pallas as pl
from jax.experimental.pallas import tpu as pltpu

F32 = jnp.float32
BF16 = jnp.bfloat16

D_MODEL = 2048
PAGE_SIZE = 128
HEAD_DIM = 64
ATTN_WIDTH = D_MODEL // 2
N_HEADS = ATTN_WIDTH // HEAD_DIM
KV_HEADS = 4
Q_PER_KV = N_HEADS // KV_HEADS
CMP_STRIDE = 16
CMP_BLOCK = 2 * CMP_STRIDE
CMP_HIDDEN = 2 * HEAD_DIM
SEL_BLOCK = 64
N_SELECT = 8
WINDOW = 512
N_KV_SLOTS = 6
N_CACHE_SLOTS = 4
KV_WIDTH = KV_HEADS * HEAD_DIM
SSM_INNER = D_MODEL - ATTN_WIDTH
SSM_HEADDIM = 64
SSM_HEADS = SSM_INNER // SSM_HEADDIM
SSM_GROUPS = 4
SSM_HPG = SSM_HEADS // SSM_GROUPS
SSM_STATE = 128
SSM_CONV = 4
CONV_DIM = SSM_INNER + 2 * SSM_GROUPS * SSM_STATE
D_FF = ((8 * D_MODEL // 3 + 127) // 128) * 128
FFN_CONV = 3
REL_BUCKETS = 32
REL_MAX_DIST = 128
EPS = 1e-6
NEG = -1e30
TINY = 1e-30
FORCE_BONUS = 1e4
SCALE = HEAD_DIM ** -0.5

LANE = 128
SUBLANE = 8
TILE = 128
VMEM_LIMIT = 56 * 1024 * 1024

P_Q = 0
P_Z = ATTN_WIDTH
P_XBC = P_Z + SSM_INNER
P_KV = P_XBC + CONV_DIM
P_GD = P_KV + N_KV_SLOTS * KV_WIDTH
GD_WIDTH = 3 * N_HEADS + SSM_HEADS
P_WIDTH = 6144
DT_COL = 3 * N_HEADS
FF_PAD = 5632
FFN_SUB = 256


def _cparams(*sem):
    return pltpu.CompilerParams(dimension_semantics=sem, vmem_limit_bytes=VMEM_LIMIT)


def _rms(x, g):
    return x * lax.rsqrt(jnp.mean(x * x, axis=-1, keepdims=True) + EPS) * g


def _split3(x):
    hi = x.astype(BF16)
    r1 = x - hi.astype(F32)
    mid = r1.astype(BF16)
    lo = (r1 - mid.astype(F32)).astype(BF16)
    return hi, mid, lo


def _dot(a, b):
    return jnp.dot(a, b, preferred_element_type=F32)


def _dot_nt(a, b):
    return lax.dot_general(a, b, (((1,), (1,)), ((), ())), preferred_element_type=F32)


def _dot_exact_rhs(x, sel):
    hi, mid, lo = _split3(x)
    return _dot(hi, sel) + _dot(mid, sel) + _dot(lo, sel)


def _dot_exact_lhs(sel, x):
    hi, mid, lo = _split3(x)
    return _dot(sel, hi) + _dot(sel, mid) + _dot(sel, lo)


def _gelu_tanh(x):
    return 0.5 * x * (1.0 + jnp.tanh(math.sqrt(2.0 / math.pi) * (x + 0.044715 * (x * x * x))))


def _silu(x):
    return x * jax.nn.sigmoid(x)


def _proj_body(x_ref, g_ref, w_ref, o_ref, xn_ref):
    @pl.when(pl.program_id(1) == 0)
    def _():
        xn_ref[...] = _rms(x_ref[...], g_ref[...]).astype(BF16)

    o_ref[...] = _dot_nt(xn_ref[...], w_ref[...])


def _proj(x2d, g, w_t, tm, tn=1024):
    m, k = x2d.shape
    n = w_t.shape[0]
    w = w_t
    return pl.pallas_call(
        _proj_body,
        grid=(m // tm, n // tn),
        in_specs=[pl.BlockSpec((tm, k), lambda i, j: (i, 0)),
                  pl.BlockSpec((1, k), lambda i, j: (0, 0)),
                  pl.BlockSpec((tn, k), lambda i, j: (j, 0))],
        out_specs=pl.BlockSpec((tm, tn), lambda i, j: (i, j)),
        out_shape=jax.ShapeDtypeStruct((m, n), F32),
        scratch_shapes=[pltpu.VMEM((tm, k), BF16)],
        compiler_params=_cparams("parallel", "arbitrary"),
        name="proj",
    )(x2d, g, w)


def _outproj_body(a_ref, s_ref, x_ref, wa_ref, ws_ref, g1_ref, g2_ref, x1_ref, hn_ref):
    mix = _dot(a_ref[...], wa_ref[...]) + _dot(s_ref[...], ws_ref[...])
    x1 = x_ref[...] + _rms(mix, g1_ref[...])
    x1_ref[...] = x1
    hn_ref[...] = _rms(x1, g2_ref[...]).astype(BF16)


def _outproj(attn, ssm, x2d, w_attn, w_ssm, g_post, g_pre2, tm=512):
    m, d = x2d.shape
    ka = attn.shape[1]
    const = lambda i: (0, 0)
    return pl.pallas_call(
        _outproj_body,
        grid=(m // tm,),
        in_specs=[pl.BlockSpec((tm, ka), lambda i: (i, 0)),
                  pl.BlockSpec((tm, ka), lambda i: (i, 0)),
                  pl.BlockSpec((tm, d), lambda i: (i, 0)),
                  pl.BlockSpec((ka, d), const),
                  pl.BlockSpec((ka, d), const),
                  pl.BlockSpec((1, d), const),
                  pl.BlockSpec((1, d), const)],
        out_specs=[pl.BlockSpec((tm, d), lambda i: (i, 0)),
                   pl.BlockSpec((tm, d), lambda i: (i, 0))],
        out_shape=[jax.ShapeDtypeStruct((m, d), F32), jax.ShapeDtypeStruct((m, d), BF16)],
        compiler_params=_cparams("parallel"),
        name="outproj",
    )(attn, ssm, x2d, w_attn, w_ssm, g_post, g_pre2)


def _shift_rows(x, prev, k, seq_len):
    rows, cols = x.shape
    if seq_len == rows:
        t = lax.broadcasted_iota(jnp.int32, (rows, 1), 0)
        return jnp.where(t >= k, pltpu.roll(x, k, 0), 0.0 if prev is None else pltpu.roll(prev, k, 0))
    assert seq_len == SUBLANE and prev is not None
    shape = (rows // seq_len, seq_len, cols)
    t = lax.broadcasted_iota(jnp.int32, shape, 1)
    out = jnp.where(t >= k, pltpu.roll(x.reshape(shape), k, 1), pltpu.roll(prev.reshape(shape), k, 1))
    return out.reshape(rows, cols)


def _ffn_up_body(*refs, seq_len, has_prev):
    if has_prev:
        h_ref, wg_ref, wu_ref, dw_ref, db_ref, prev_ref, act_ref, tail_ref = refs
    else:
        h_ref, wg_ref, wu_ref, dw_ref, db_ref, act_ref, tail_ref = refs
    tm, tn = act_ref.shape
    h = h_ref[...]

    def project(cs):
        return _dot(h, wg_ref[:, cs]), _dot(h, wu_ref[:, cs])

    def gate(cs, gp, up):
        prev = prev_ref[:, cs] if has_prev else None
        acc = db_ref[:, cs] + dw_ref[FFN_CONV - 1:FFN_CONV, cs] * gp
        for k in range(1, FFN_CONV):
            acc = acc + dw_ref[FFN_CONV - 1 - k:FFN_CONV - k, cs] * _shift_rows(gp, prev, k, seq_len)
        act_ref[:, cs] = (_gelu_tanh(acc) * up).astype(BF16)
        tail_ref[:, cs] = gp if has_prev else gp[tm - SUBLANE:, :]

    pending = None
    for c in range(tn // FFN_SUB):
        cs = slice(c * FFN_SUB, (c + 1) * FFN_SUB)
        cur = (cs,) + project(cs)
        if pending is not None:
            gate(*pending)
        pending = cur
    gate(*pending)


def _ffn_up(hn, wg, wu, dw, db, prev, *, seq_len, tm, tn=512):
    m, d = hn.shape
    ni = m // tm
    nj = FF_PAD // tn
    has_prev = prev is not None
    in_specs = [pl.BlockSpec((tm, d), lambda i, j: (i, 0)),
                pl.BlockSpec((d, tn), lambda i, j: (0, j)),
                pl.BlockSpec((d, tn), lambda i, j: (0, j)),
                pl.BlockSpec((SUBLANE, tn), lambda i, j: (0, j)),
                pl.BlockSpec((1, tn), lambda i, j: (0, j))]
    args = [hn, wg, wu, dw, db]
    if has_prev:
        in_specs.append(pl.BlockSpec((tm, tn), lambda i, j: (i, j)))
        args.append(prev)
        tail_shape = jax.ShapeDtypeStruct((m, FF_PAD), F32)
        tail_spec = pl.BlockSpec((tm, tn), lambda i, j: (i, j))
    else:
        tail_shape = jax.ShapeDtypeStruct((ni, SUBLANE, FF_PAD), F32)
        tail_spec = pl.BlockSpec((None, SUBLANE, tn), lambda i, j: (i, 0, j))
    return pl.pallas_call(
        functools.partial(_ffn_up_body, seq_len=seq_len, has_prev=has_prev),
        grid=(ni, nj),
        in_specs=in_specs,
        out_specs=[pl.BlockSpec((tm, tn), lambda i, j: (i, j)), tail_spec],
        out_shape=[jax.ShapeDtypeStruct((m, FF_PAD), BF16), tail_shape],
        compiler_params=_cparams("parallel", "parallel"),
        name="ffn_up",
    )(*args)


def _ffn_down_body(a_ref, w_ref, x_ref, g_ref, o_ref, acc_ref):
    k = pl.program_id(1)

    @pl.when(k == 0)
    def _():
        acc_ref[...] = jnp.zeros_like(acc_ref)

    acc_ref[...] += _dot(a_ref[...], w_ref[...])

    @pl.when(k == pl.num_programs(1) - 1)
    def _():
        o_ref[...] = x_ref[...] + _rms(acc_ref[...], g_ref[...])


def _ffn_down(act, wd, x1, g, tm=1024, tk=512):
    m, d = x1.shape
    return pl.pallas_call(
        _ffn_down_body,
        grid=(m // tm, FF_PAD // tk),
        in_specs=[pl.BlockSpec((tm, tk), lambda i, k: (i, k)),
                  pl.BlockSpec((tk, d), lambda i, k: (k, 0)),
                  pl.BlockSpec((tm, d), lambda i, k: (i, 0)),
                  pl.BlockSpec((1, d), lambda i, k: (0, 0))],
        out_specs=pl.BlockSpec((tm, d), lambda i, k: (i, 0)),
        out_shape=jax.ShapeDtypeStruct((m, d), F32),
        scratch_shapes=[pltpu.VMEM((tm, d), F32)],
        compiler_params=_cparams("parallel", "arbitrary"),
        name="ffn_down",
    )(act, wd, x1, g)


N_CHUNK_TILES = 16
CHUNKS_PER_TILE = TILE // CMP_STRIDE
CMP_COL_BLOCKS = 2 * KV_WIDTH // LANE


def _compress_body(*refs):
    pt_ref = refs[0]
    del pt_ref
    n_in = N_CHUNK_TILES * CMP_COL_BLOCKS
    tiles = refs[1:1 + n_in]
    pe_ref, w1_ref, b1_ref, w2_ref, b2_ref, kc_ref, vc_ref, lhs_a, lhs_b = refs[1 + n_in:]
    out_refs = (kc_ref, vc_ref)
    for slot in range(2):
        outs = []
        for gp in range(KV_HEADS // 2):
            cb = slot * (KV_HEADS // 2) + gp
            for s in range(CMP_STRIDE):
                x = jnp.concatenate(
                    [tiles[j * CMP_COL_BLOCKS + cb][pl.ds(s, CHUNKS_PER_TILE, stride=CMP_STRIDE), :]
                     for j in range(N_CHUNK_TILES)], axis=0)
                lhs_a[:, s * LANE:(s + 1) * LANE] = (x + pe_ref[slot, 0, s:s + 1, :]).astype(BF16)
                lhs_b[:, s * LANE:(s + 1) * LANE] = (x + pe_ref[slot, 1, s:s + 1, :]).astype(BF16)
            first = _dot(lhs_a[...], w1_ref[slot, 0])
            second = _dot(lhs_b[...], w1_ref[slot, 1])
            hid = first + pltpu.roll(second, TILE - 1, 0) + b1_ref[slot]
            act = _gelu_tanh(hid).astype(BF16)
            for gj in range(2):
                outs.append(_dot(act[:, gj * CMP_HIDDEN:(gj + 1) * CMP_HIDDEN], w2_ref[slot]) + b2_ref[slot])
        out_refs[slot][...] = jnp.concatenate(outs, axis=1)


def _compress(src, tile_map, n_seq, page_table, pe, w1, b1, w2, b2):
    in_specs = [pl.BlockSpec((None, TILE, LANE), functools.partial(tile_map, j, cb))
                for j in range(N_CHUNK_TILES) for cb in range(CMP_COL_BLOCKS)]
    full = lambda shape: pl.BlockSpec(shape, lambda b, pt: (0,) * len(shape))
    in_specs += [full(pe.shape), full(w1.shape), full(b1.shape), full(w2.shape), full(b2.shape)]
    out_spec = pl.BlockSpec((None, TILE, KV_WIDTH), lambda b, pt: (b, 0, 0))
    return pl.pallas_call(
        _compress_body,
        grid_spec=pltpu.PrefetchScalarGridSpec(
            num_scalar_prefetch=1, grid=(n_seq,), in_specs=in_specs, out_specs=[out_spec, out_spec],
            scratch_shapes=[pltpu.VMEM((TILE, CMP_STRIDE * LANE), BF16), pltpu.VMEM((TILE, CMP_STRIDE * LANE), BF16)]),
        out_shape=[jax.ShapeDtypeStruct((n_seq, TILE, KV_WIDTH), F32)] * 2,
        compiler_params=_cparams("parallel"),
        name="compress",
    )(page_table, *([src] * (N_CHUNK_TILES * CMP_COL_BLOCKS)), pe, w1, b1, w2, b2)


def _compress_weights(cmp_pe, cmp_w1, cmp_b1, cmp_w2, cmp_b2):
    pe = cmp_pe.reshape(2, 2, CMP_STRIDE, HEAD_DIM)
    pe = jnp.concatenate([pe, pe], axis=-1)
    w = cmp_w1.reshape(2, 2, CMP_STRIDE, HEAD_DIM, CMP_HIDDEN)
    eye = jnp.eye(2, dtype=w.dtype)
    w1 = jnp.einsum('absdh,ij->absidjh', w, eye).reshape(2, 2, CMP_STRIDE * LANE, 2 * CMP_HIDDEN).astype(BF16)
    b1 = jnp.concatenate([cmp_b1, cmp_b1], axis=-1).reshape(2, 1, 2 * CMP_HIDDEN)
    return pe, w1, b1, cmp_w2.astype(BF16), cmp_b2.reshape(2, 1, HEAD_DIM)


PAGES_PER_PERM = 2


def _pe_bias_body(pe_ref, w1_ref, o_ref):
    for slot in range(2):
        cols = []
        for half in range(2):
            acc = jnp.zeros((2 * SUBLANE, 2 * CMP_HIDDEN), F32)
            for s in range(CMP_STRIDE):
                row = jnp.broadcast_to(pe_ref[slot, half, s:s + 1, :], (2 * SUBLANE, LANE))
                w = w1_ref[slot, s * LANE:(s + 1) * LANE, half * 2 * CMP_HIDDEN:(half + 1) * 2 * CMP_HIDDEN]
                acc = acc + _dot_exact_rhs(row, w)
            cols.append(acc)
        o_ref[slot] = jnp.concatenate(cols, axis=1)


def _pe_bias(pe, w1):
    return pl.pallas_call(
        _pe_bias_body,
        out_shape=jax.ShapeDtypeStruct((2, 2 * SUBLANE, 4 * CMP_HIDDEN), F32),
        name="pe_bias",
    )(pe, w1)


def _compress_paged_body(*refs):
    n_in = N_CHUNK_TILES
    pt_ref = refs[0]
    del pt_ref
    pages = refs[1:1 + n_in]
    perm_ref, w1_ref, peb_ref, b1_ref, w2_ref, b2_ref, kc_ref, vc_ref, xp_ref = refs[1 + n_in:]
    n_perm = N_CHUNK_TILES // PAGES_PER_PERM
    slab = PAGES_PER_PERM * CHUNKS_PER_TILE
    for jp in range(n_perm):
        xt = jnp.concatenate([pages[PAGES_PER_PERM * jp + jj][...] for jj in range(PAGES_PER_PERM)], axis=1)
        xp_ref[jp] = _dot_nt(perm_ref[...], xt.astype(BF16)).astype(BF16)
    out_refs = (kc_ref, vc_ref)
    for slot in range(2):
        outs = []
        for gp in range(KV_HEADS // 2):
            cb = slot * (KV_HEADS // 2) + gp
            acc = jnp.zeros((TILE, 4 * CMP_HIDDEN), F32)
            for sp in range(CMP_STRIDE // 2):
                lhs = jnp.concatenate(
                    [jnp.concatenate([xp_ref[jp, s * slab:(s + 1) * slab, cb * LANE:(cb + 1) * LANE]
                                      for jp in range(n_perm)], axis=0)
                     for s in (2 * sp, 2 * sp + 1)], axis=1)
                acc = acc + _dot(lhs, w1_ref[slot, 2 * sp * LANE:(2 * sp + 2) * LANE, :])
            acc = acc + peb_ref[slot, 0:1, :]
            hid = (acc[:, 0:2 * CMP_HIDDEN] + pltpu.roll(acc[:, 2 * CMP_HIDDEN:], TILE - 1, 0) + b1_ref[slot])
            act = _gelu_tanh(hid).astype(BF16)
            for gj in range(2):
                outs.append(_dot(act[:, gj * CMP_HIDDEN:(gj + 1) * CMP_HIDDEN], w2_ref[slot]) + b2_ref[slot])
        out_refs[slot][...] = jnp.concatenate(outs, axis=1)


def _compress_paged(cache_t, page_table, perm, w1, peb, b1, w2, b2):
    n_seq = page_table.shape[0]
    in_specs = [pl.BlockSpec((None, 2 * KV_WIDTH, PAGE_SIZE), functools.partial(lambda j, b, pt: (pt[b, j], 0, 0), j))
                for j in range(N_CHUNK_TILES)]
    const = lambda arr: pl.BlockSpec(arr.shape, lambda b, pt: (0,) * arr.ndim)
    consts = [perm, w1, peb, b1, w2, b2]
    in_specs += [const(a) for a in consts]
    out_spec = pl.BlockSpec((None, TILE, KV_WIDTH), lambda b, pt: (b, 0, 0))
    rows = PAGES_PER_PERM * PAGE_SIZE
    return pl.pallas_call(
        _compress_paged_body,
        grid_spec=pltpu.PrefetchScalarGridSpec(
            num_scalar_prefetch=1, grid=(n_seq,), in_specs=in_specs, out_specs=[out_spec, out_spec],
            scratch_shapes=[pltpu.VMEM((N_CHUNK_TILES // PAGES_PER_PERM, rows, 2 * KV_WIDTH), BF16)]),
        out_shape=[jax.ShapeDtypeStruct((n_seq, TILE, KV_WIDTH), F32)] * 2,
        compiler_params=_cparams("parallel"),
        name="compress_paged",
    )(page_table, *([cache_t] * N_CHUNK_TILES), *consts)


def _compress_paged_weights(cmp_w1):
    w = cmp_w1.reshape(2, 2, CMP_STRIDE, HEAD_DIM, CMP_HIDDEN)
    eye = jnp.eye(2, dtype=w.dtype)
    w1 = jnp.einsum('absdh,ij->asidbjh', w, eye).reshape(2, CMP_STRIDE * LANE, 4 * CMP_HIDDEN).astype(BF16)
    rows = PAGES_PER_PERM * PAGE_SIZE
    r = np.arange(rows)
    s, page, chunk = r // (PAGES_PER_PERM * CHUNKS_PER_TILE), (r // CHUNKS_PER_TILE) % PAGES_PER_PERM, r % CHUNKS_PER_TILE
    src = page * PAGE_SIZE + chunk * CMP_STRIDE + s
    perm = (np.arange(rows)[None, :] == src[:, None]).astype(np.float32)
    return w1, jnp.asarray(perm, BF16)


def _bucket_thresholds():
    n = np.arange(0, 2 * REL_MAX_DIST)
    max_exact = REL_BUCKETS // 2
    nf = np.maximum(n, 1).astype(np.float32)
    large = max_exact + (np.log(nf / np.float32(max_exact)) / np.float32(math.log(REL_MAX_DIST / max_exact))
                         * np.float32(REL_BUCKETS - max_exact)).astype(np.int32)
    bucket = np.where(n < max_exact, n, np.minimum(large, REL_BUCKETS - 1))
    assert (np.diff(bucket) >= 0).all() and (np.diff(bucket) <= 1).all()
    return [int(np.argmax(bucket >= k)) for k in range(1, REL_BUCKETS)]


BUCKET_START = _bucket_thresholds()
CMP_BAND = 2 * SUBLANE
assert (BUCKET_START[-1] - 1 + CMP_BLOCK - 1) // CMP_STRIDE <= SUBLANE
assert (TILE - CMP_BLOCK) // CMP_STRIDE < SUBLANE


def _rel_bias(dist, tab_ref):
    bias = jnp.broadcast_to(tab_ref[0:1, :], dist.shape)
    for k in range(1, REL_BUCKETS):
        bias = jnp.where(dist >= BUCKET_START[k - 1], tab_ref[k:k + 1, :], bias)
    return bias


def _bias_prompt_body(tab_ref, cmpb_ref, near_ref, edge_ref):
    qb = pl.program_id(0)
    shape = (TILE, Q_PER_KV * TILE)
    row = lax.broadcasted_iota(jnp.int32, shape, 0)
    qi = lax.broadcasted_iota(jnp.int32, shape, 1) % TILE
    dist_c = qb * TILE + qi - (row * CMP_STRIDE + CMP_BLOCK - 1)
    band0 = pl.multiple_of(jnp.maximum(qb * (TILE // CMP_STRIDE) - SUBLANE, 0), SUBLANE)
    band_shape = (CMP_BAND, Q_PER_KV * TILE)
    band_row = band0 + lax.broadcasted_iota(jnp.int32, band_shape, 0)
    band_qi = lax.broadcasted_iota(jnp.int32, band_shape, 1) % TILE
    dist_b = qb * TILE + band_qi - (band_row * CMP_STRIDE + CMP_BLOCK - 1)
    for g in range(KV_HEADS):
        cmpb_ref[g] = jnp.where(dist_c >= 0, tab_ref[g, REL_BUCKETS - 1:REL_BUCKETS, :], NEG)
        cmpb_ref[g, pl.ds(band0, CMP_BAND), :] = jnp.where(dist_b >= 0, _rel_bias(dist_b, tab_ref.at[g]), NEG)

    @pl.when(qb == 0)
    def _():
        for g in range(KV_HEADS):
            for delta in range(2):
                dist = delta * TILE + qi - row
                b = _rel_bias(dist, tab_ref.at[g])
                near_ref[g, delta] = jnp.where(dist >= 0, b, NEG)
        edge_ref[...] = jnp.where(qi <= row, 0.0, NEG)


def _bias_prompt(tab, n_qb):
    lanes = Q_PER_KV * TILE
    return pl.pallas_call(
        _bias_prompt_body,
        grid=(n_qb,),
        in_specs=[pl.BlockSpec(tab.shape, lambda i: (0, 0, 0))],
        out_specs=[pl.BlockSpec((None, KV_HEADS, TILE, lanes), lambda i: (i, 0, 0, 0)),
                   pl.BlockSpec((KV_HEADS, 2, TILE, lanes), lambda i: (0, 0, 0, 0)),
                   pl.BlockSpec((TILE, lanes), lambda i: (0, 0))],
        out_shape=[jax.ShapeDtypeStruct((n_qb, KV_HEADS, TILE, lanes), F32),
                   jax.ShapeDtypeStruct((KV_HEADS, 2, TILE, lanes), F32),
                   jax.ShapeDtypeStruct((TILE, lanes), F32)],
        compiler_params=_cparams("arbitrary"),
        name="bias_prompt",
    )(tab)


def _bias_sample_body(tab_ref, cmpbt_ref, selb_ref, winb_ref, *, past_len, n_new, win_len):
    qt = lax.broadcasted_iota(jnp.int32, (TILE, LANE), 1) % n_new
    row = lax.broadcasted_iota(jnp.int32, (TILE, LANE), 0)
    dist = past_len + qt - (row * CMP_STRIDE + CMP_BLOCK - 1)
    cmpbt_ref[...] = jnp.where(dist >= 0, _rel_bias(dist, tab_ref), NEG)
    for j in range(selb_ref.shape[1] // TILE):
        key = row + j * TILE
        dist = past_len + qt - key
        ok = (dist >= 0) & (key < past_len + n_new)
        selb_ref[:, j * TILE:(j + 1) * TILE] = jnp.where(ok, _rel_bias(dist, tab_ref), NEG).T
    for j in range(winb_ref.shape[1] // TILE):
        key = row + j * TILE
        dist = win_len + qt - key
        ok = (dist >= 0) & (dist <= WINDOW) & (key < win_len + n_new)
        winb_ref[:, j * TILE:(j + 1) * TILE] = jnp.where(ok, _rel_bias(dist, tab_ref), NEG).T


def _bias_sample(tab, past_len, n_new, win_len):
    return pl.pallas_call(
        functools.partial(_bias_sample_body, past_len=past_len, n_new=n_new, win_len=win_len),
        out_shape=[jax.ShapeDtypeStruct((TILE, LANE), F32),
                   jax.ShapeDtypeStruct((LANE, past_len + TILE), F32),
                   jax.ShapeDtypeStruct((LANE, win_len + TILE), F32)],
        name="bias_sample",
    )(tab)


def _select_blocks(imp, blk, q_blk):
    valid = blk <= q_blk
    forced = (blk == 0) | (blk == q_blk) | (blk == q_blk - 1)
    score = jnp.where(valid, imp + jnp.where(forced, FORCE_BONUS, 0.0), -jnp.inf)
    rank = jnp.zeros(score.shape, jnp.int32)
    for i in range(score.shape[0]):
        row = score[i:i + 1, :]
        beats = (row > score) | ((row == score) & (blk > i))
        rank = rank + beats.astype(jnp.int32)
    return jnp.where(valid & (rank < N_SELECT), 0.0, NEG)


def _nsa_prompt_body(q_ref, gd_ref, kc_ref, vc_ref, sk_ref, sv_ref, wk_ref, wv_ref, cmpb_ref, near_ref,
                     edge_ref, tab_ref, impt_ref, o_ref, qs_ref, seladd_ref, selfar_ref, m_ref, l_ref, acc_ref,
                     out_ref):
    qb = pl.program_id(1)
    lanes = Q_PER_KV * TILE
    q = q_ref[...] * SCALE
    gate_t = jax.nn.sigmoid(gd_ref[...]).T
    n_sel = seladd_ref.shape[1]
    blk = lax.broadcasted_iota(jnp.int32, (n_sel, TILE), 0)
    q_blk = (qb * TILE + lax.broadcasted_iota(jnp.int32, (n_sel, TILE), 1)) // SEL_BLOCK
    groups = range(KV_HEADS)

    def hs(g):
        return slice(g * HEAD_DIM, (g + 1) * HEAD_DIM)

    def gate_row(branch, g):
        c0 = branch * N_HEADS + g * Q_PER_KV
        return jnp.concatenate([gate_t[c0 + r:c0 + r + 1, :] for r in range(Q_PER_KV)], axis=1)

    def far_bias(g):
        return tab_ref[g, REL_BUCKETS - 1:REL_BUCKETS, :]

    for g in groups:
        qs = jnp.concatenate(
            [q[:, (g * Q_PER_KV + r) * HEAD_DIM:(g * Q_PER_KV + r + 1) * HEAD_DIM] for r in range(Q_PER_KV)],
            axis=0).astype(BF16)
        qs_ref[g] = qs
        s = _dot_nt(kc_ref[:, hs(g)].astype(BF16), qs) + cmpb_ref[g]
        m = jnp.max(s, axis=0, keepdims=True)
        e = jnp.exp(s - m)
        norm = jnp.where(m > 0.5 * NEG, 1.0 / jnp.maximum(jnp.sum(e, axis=0, keepdims=True), TINY), 0.0)
        p = e * norm
        out_ref[g] = gate_row(0, g) * _dot(vc_ref[:, hs(g)].T.astype(BF16), p.astype(BF16))
        psum = p[:, 0:TILE]
        for r in range(1, Q_PER_KV):
            psum = psum + p[:, r * TILE:(r + 1) * TILE]
        sa = _select_blocks(_dot_exact_lhs(impt_ref[...], psum), blk, q_blk)
        sa = jnp.concatenate([sa] * Q_PER_KV, axis=1)
        seladd_ref[g] = sa
        selfar_ref[g] = sa + far_bias(g)

    def sel_rows(ref, g, kt):
        half = SEL_BLOCK
        return jnp.concatenate(
            [jnp.broadcast_to(ref[g, pl.ds(2 * kt, 1), :], (half, lanes)),
             jnp.broadcast_to(ref[g, pl.ds(2 * kt + 1, 1), :], (half, lanes))], axis=0)

    def reset():
        m_ref[...] = jnp.full(m_ref.shape, -jnp.inf, F32)
        l_ref[...] = jnp.zeros(l_ref.shape, F32)
        acc_ref[...] = jnp.zeros(acc_ref.shape, F32)

    def tile(k_ref, v_ref, kt, bias_of):
        r0 = pl.multiple_of(kt * TILE, TILE)
        k = k_ref[pl.ds(r0, TILE), :].astype(BF16)
        vt = v_ref[pl.ds(r0, TILE), :].T.astype(BF16)
        scores = [_dot_nt(k[:, hs(g)], qs_ref[g]) + bias_of(g) for g in groups]
        probs, alphas = [], []
        for g in groups:
            m = m_ref[g]
            m_new = jnp.maximum(m, jnp.max(scores[g], axis=0, keepdims=True))
            alpha = jnp.exp(m - m_new)
            p = jnp.exp(scores[g] - m_new)
            m_ref[g] = m_new
            l_ref[g] = alpha * l_ref[g] + jnp.sum(p, axis=0, keepdims=True)
            probs.append(p.astype(BF16))
            alphas.append(alpha)
        pvs = [_dot(vt[hs(g), :], probs[g]) for g in groups]
        for g in groups:
            acc_ref[g] = alphas[g] * acc_ref[g] + pvs[g]

    def finish(branch, g):
        return gate_row(branch, g) * (acc_ref[g] / jnp.maximum(l_ref[g], TINY))

    reset()

    def far_body(kt, carry):
        tile(sk_ref, sv_ref, kt, lambda g: sel_rows(selfar_ref, g, kt))
        return carry

    lax.fori_loop(0, jnp.maximum(qb - 1, 0), far_body, 0)
    kt1 = jnp.maximum(qb - 1, 0)
    off1 = jnp.where(qb >= 1, 0.0, NEG)
    tile(sk_ref, sv_ref, kt1, lambda g: near_ref[g, 1] + (sel_rows(seladd_ref, g, kt1) + off1))
    tile(sk_ref, sv_ref, qb, lambda g: near_ref[g, 0] + sel_rows(seladd_ref, g, qb))
    for g in groups:
        out_ref[g] = out_ref[g] + finish(1, g)

    reset()
    for delta in range(WINDOW // TILE, -1, -1):
        kt = qb - delta
        off = jnp.where(kt >= 0, 0.0, NEG)
        if delta == WINDOW // TILE:
            bias_of = lambda g, off=off: edge_ref[...] + (far_bias(g) + off)
        elif delta >= 2:
            bias_of = lambda g, off=off: far_bias(g) + off
        else:
            bias_of = lambda g, off=off, delta=delta: near_ref[g, delta] + off
        tile(wk_ref, wv_ref, jnp.maximum(kt, 0), bias_of)
    outs = []
    for g in groups:
        out_t = out_ref[g] + finish(2, g)
        outs += [out_t[:, r * TILE:(r + 1) * TILE].T for r in range(Q_PER_KV)]
    o_ref[...] = jnp.concatenate(outs, axis=1).astype(BF16)


def _nsa_prompt(p3, kc, vc, cmpb, near, edge, tab, impt):
    b, t, _ = p3.shape
    n_qb = t // TILE
    lanes = Q_PER_KV * TILE
    kvcol = lambda slot: (P_KV + slot * KV_WIDTH) // KV_WIDTH
    seq_spec = lambda slot: pl.BlockSpec((None, t, KV_WIDTH), lambda i, j: (i, 0, kvcol(slot)))
    const = lambda shape: pl.BlockSpec(shape, lambda i, j: (0,) * len(shape))
    return pl.pallas_call(
        _nsa_prompt_body,
        grid=(b, n_qb),
        in_specs=[pl.BlockSpec((None, TILE, ATTN_WIDTH), lambda i, j: (i, j, 0)),
                  pl.BlockSpec((None, TILE, LANE), lambda i, j: (i, j, P_GD // LANE)),
                  pl.BlockSpec((None, TILE, KV_WIDTH), lambda i, j: (i, 0, 0)),
                  pl.BlockSpec((None, TILE, KV_WIDTH), lambda i, j: (i, 0, 0)),
                  seq_spec(2), seq_spec(3), seq_spec(4), seq_spec(5),
                  pl.BlockSpec((None, KV_HEADS, TILE, lanes), lambda i, j: (j, 0, 0, 0)),
                  const(near.shape), const(edge.shape), const(tab.shape), const(impt.shape)],
        out_specs=pl.BlockSpec((None, TILE, ATTN_WIDTH), lambda i, j: (i, j, 0)),
        out_shape=jax.ShapeDtypeStruct((b, t, ATTN_WIDTH), BF16),
        scratch_shapes=[pltpu.VMEM((KV_HEADS, lanes, HEAD_DIM), BF16),
                        pltpu.VMEM((KV_HEADS, impt.shape[0], lanes), F32),
                        pltpu.VMEM((KV_HEADS, impt.shape[0], lanes), F32),
                        pltpu.VMEM((KV_HEADS, 1, lanes), F32),
                        pltpu.VMEM((KV_HEADS, 1, lanes), F32),
                        pltpu.VMEM((KV_HEADS, HEAD_DIM, lanes), F32),
                        pltpu.VMEM((KV_HEADS, HEAD_DIM, lanes), F32)],
        compiler_params=_cparams("parallel", "arbitrary"),
        name="nsa_prompt",
    )(p3, p3, kc, vc, p3, p3, p3, p3, cmpb, near, edge, tab, impt)


def _nsa_sample_body(*refs, past_len, n_new):
    n_pages = past_len // TILE
    pt_ref, q_ref, gd_ref, newkv_ref, newwin_ref, kc_ref, vc_ref = refs[:7]
    pages = refs[7:7 + n_pages]
    (win_ref, cmpbt_ref, selb_ref, winb_ref, impt_ref, rsum_ref, gsel_ref, expand_ref, place_ref,
     o_ref, winout_ref, s_scr) = refs[7 + n_pages:]
    del pt_ref
    win_len = win_ref.shape[1]
    rows_per_g = Q_PER_KV * n_new
    row_g = lax.broadcasted_iota(jnp.int32, (LANE, 1), 0) // rows_per_g

    q = q_ref[...] * SCALE
    qs = jnp.concatenate([q[:, h * HEAD_DIM:(h + 1) * HEAD_DIM] for h in range(N_HEADS)], axis=0)
    qs4 = jnp.concatenate([qs] * KV_HEADS, axis=1)
    col_g = lax.broadcasted_iota(jnp.int32, qs4.shape, 1) // HEAD_DIM
    qbd = jnp.where(row_g == col_g, qs4, 0.0).astype(BF16)

    sig = jax.nn.sigmoid(gd_ref[...])
    t_row = lax.broadcasted_iota(jnp.int32, (n_new, LANE), 0)
    t_lane = lax.broadcasted_iota(jnp.int32, (n_new, LANE), 1) % n_new
    lane_g = lax.broadcasted_iota(jnp.int32, (1, LANE), 1) // rows_per_g
    ones_rows = jnp.ones((2 * SUBLANE, TILE), BF16)

    def gate_row(branch):
        spread = _dot_exact_rhs(sig, gsel_ref[branch])
        return jnp.sum(jnp.where(t_row == t_lane, spread, 0.0), axis=0, keepdims=True)

    def pick(full_t):
        out = jnp.zeros((HEAD_DIM, LANE), F32)
        for g in range(KV_HEADS):
            out = out + jnp.where(lane_g == g, full_t[g * HEAD_DIM:(g + 1) * HEAD_DIM, :], 0.0)
        return out

    kc = kc_ref[...].astype(BF16)
    s_t = _dot_nt(kc, qbd) + cmpbt_ref[...]
    m_t = jnp.max(s_t, axis=0, keepdims=True)
    e_t = jnp.exp(s_t - m_t)
    p_t = e_t / jnp.maximum(jnp.sum(e_t, axis=0, keepdims=True), TINY)
    out_t = gate_row(0) * pick(_dot(vc_ref[...].T.astype(BF16), p_t.astype(BF16)))
    imp = _dot_exact_lhs(impt_ref[...], _dot_exact_rhs(p_t, rsum_ref[...]))
    blk = lax.broadcasted_iota(jnp.int32, imp.shape, 0)
    q_blk = (past_len + lax.broadcasted_iota(jnp.int32, imp.shape, 1) % n_new) // SEL_BLOCK
    chosen_t = jnp.where(_select_blocks(imp, blk, q_blk) == 0.0, 1.0, 0.0)
    chosen = jnp.concatenate([chosen_t, jnp.zeros((LANE - chosen_t.shape[0], LANE), F32)], axis=0).T
    sel_mask = (_dot(chosen.astype(BF16), expand_ref[...]) - 1.0) * (-NEG)

    def attend(tiles, bias_ref, mask):
        m = jnp.full((LANE, 1), -jnp.inf, F32)
        for j, (score, _) in enumerate(tiles):
            s = score() + bias_ref[:, j * TILE:(j + 1) * TILE]
            if mask is not None:
                s = s + mask[:, j * TILE:(j + 1) * TILE]
            s_scr[:, j * TILE:(j + 1) * TILE] = s
            m = jnp.maximum(m, jnp.max(s, axis=1, keepdims=True))
        acc = jnp.zeros((KV_WIDTH + 2 * SUBLANE, LANE), F32)
        for j, (_, vt) in enumerate(tiles):
            p = jnp.exp(s_scr[:, j * TILE:(j + 1) * TILE] - m).astype(BF16)
            acc = acc + _dot_nt(jnp.concatenate([vt(), ones_rows], axis=0), p)
        return pick(acc) / jnp.maximum(acc[KV_WIDTH:KV_WIDTH + 1, :], TINY)

    def stored_tile(ref, lanes):
        return (lambda: _dot(qbd, ref[0:KV_WIDTH, lanes].astype(BF16)),
                lambda: ref[KV_WIDTH:2 * KV_WIDTH, lanes].astype(BF16))

    def new_tile(ref):
        rows = jnp.concatenate([ref[...], jnp.zeros((TILE - n_new, ref.shape[1]), F32)], axis=0)
        k = rows[:, 0:KV_WIDTH].astype(BF16)
        vt = rows[:, KV_WIDTH:2 * KV_WIDTH].T.astype(BF16)
        return rows, (lambda: _dot_nt(qbd, k), lambda: vt)

    _, new_kv = new_tile(newkv_ref)
    tiles = [stored_tile(pg, slice(None)) for pg in pages] + [new_kv]
    out_t = out_t + gate_row(1) * attend(tiles, selb_ref, sel_mask)

    new_rows, new_win = new_tile(newwin_ref)
    n_wt = win_len // TILE
    tiles = [stored_tile(win_ref, slice(j * TILE, (j + 1) * TILE)) for j in range(n_wt)] + [new_win]
    out_t = out_t + gate_row(2) * attend(tiles, winb_ref, None)
    out = out_t.T
    o_ref[...] = jnp.concatenate([out[h * n_new:(h + 1) * n_new, :] for h in range(N_HEADS)], axis=1).astype(BF16)

    lane = lax.broadcasted_iota(jnp.int32, (1, LANE), 1)
    hi, mid, lo = _split3(new_rows)
    tail = _dot_tn(hi, place_ref[...]) + _dot_tn(mid, place_ref[...]) + _dot_tn(lo, place_ref[...])
    rolled = [pltpu.roll(win_ref[:, j * TILE:(j + 1) * TILE], TILE - n_new, 1) for j in range(n_wt)]
    for j in range(n_wt):
        nxt = rolled[j + 1] if j + 1 < n_wt else tail
        winout_ref[:, j * TILE:(j + 1) * TILE] = jnp.where(lane < TILE - n_new, rolled[j], nxt)


def _nsa_sample(p3, kc, vc, cache_t, win_t, page_table, cmpbt, selb, winb, impt, rsum, gsel, expand, place):
    b, n_new, _ = p3.shape
    n_pages = page_table.shape[1]
    past_len = n_pages * PAGE_SIZE
    win_len = win_t.shape[2]
    kvblk = lambda slot: (P_KV + slot * KV_WIDTH) // (2 * KV_WIDTH)
    const = lambda arr: pl.BlockSpec(arr.shape, lambda i, pt: (0,) * arr.ndim)
    in_specs = [pl.BlockSpec((None, n_new, ATTN_WIDTH), lambda i, pt: (i, 0, 0)),
                pl.BlockSpec((None, n_new, LANE), lambda i, pt: (i, 0, P_GD // LANE)),
                pl.BlockSpec((None, n_new, 2 * KV_WIDTH), lambda i, pt: (i, 0, kvblk(2))),
                pl.BlockSpec((None, n_new, 2 * KV_WIDTH), lambda i, pt: (i, 0, kvblk(4))),
                pl.BlockSpec((None, TILE, KV_WIDTH), lambda i, pt: (i, 0, 0)),
                pl.BlockSpec((None, TILE, KV_WIDTH), lambda i, pt: (i, 0, 0))]
    in_specs += [pl.BlockSpec((None, 2 * KV_WIDTH, PAGE_SIZE), functools.partial(lambda j, i, pt: (pt[i, j], 1, 0), j))
                 for j in range(n_pages)]
    consts = [cmpbt, selb, winb, impt, rsum, gsel, expand, place]
    in_specs += [pl.BlockSpec((None, 2 * KV_WIDTH, win_len), lambda i, pt: (i, 0, 0))] + [const(a) for a in consts]
    return pl.pallas_call(
        functools.partial(_nsa_sample_body, past_len=past_len, n_new=n_new),
        grid_spec=pltpu.PrefetchScalarGridSpec(
            num_scalar_prefetch=1, grid=(b,), in_specs=in_specs,
            out_specs=[pl.BlockSpec((None, n_new, ATTN_WIDTH), lambda i, pt: (i, 0, 0)),
                       pl.BlockSpec((None, 2 * KV_WIDTH, win_len), lambda i, pt: (i, 0, 0))],
            scratch_shapes=[pltpu.VMEM((LANE, past_len + TILE), F32)]),
        out_shape=[jax.ShapeDtypeStruct((b, n_new, ATTN_WIDTH), BF16),
                   jax.ShapeDtypeStruct((b, 2 * KV_WIDTH, win_len), F32)],
        compiler_params=_cparams("parallel"),
        name="nsa_sample",
    )(page_table, p3, p3, p3, p3, kc, vc, *([cache_t] * n_pages), win_t, *consts)


def _sample_selectors(n_new, n_keys):
    lanes = np.arange(LANE)
    h, t = lanes // n_new, lanes % n_new
    rsum = ((h[:, None] // Q_PER_KV == h[None, :] // Q_PER_KV) & (t[:, None] == t[None, :])).astype(np.float32)
    gsel = np.stack([(lanes[:, None] == br * N_HEADS + h[None, :]).astype(np.float32) for br in range(3)])
    expand = (np.arange(n_keys)[None, :] // SEL_BLOCK == lanes[:, None]).astype(np.float32)
    place = (lanes[None, :] == LANE - n_new + lanes[:, None]).astype(np.float32)
    return tuple(jnp.asarray(a, BF16) for a in (rsum, gsel, expand, place))


def _dot_tn(a, b):
    return lax.dot_general(a, b, (((0,), (0,)), ((), ())), preferred_element_type=F32)


def _ssd_body(*refs, q_len, carry):
    if carry:
        (z_ref, xbc_ref, gd_ref, cw_ref, cb_ref, dtb_ref, alog_ref, dexp_ref, ng_ref, e16_ref,
         y_ref, hout_ref, prev_ref, h_scr) = refs
    else:
        (z_ref, xbc_ref, gd_ref, prev_ref, h0_ref, cw_ref, cb_ref, dtb_ref, alog_ref, dexp_ref, ng_ref, e16_ref,
         y_ref, hout_ref) = refs
    n_seq = TILE // q_len
    x = xbc_ref[...]
    if carry:
        @pl.when(pl.program_id(1) == 0)
        def _():
            prev_ref[...] = jnp.zeros_like(prev_ref)
            h_scr[...] = jnp.zeros_like(h_scr)
    prev = prev_ref[...]
    acc = cb_ref[...] + cw_ref[SSM_CONV - 1:SSM_CONV, :] * x
    for k in range(1, SSM_CONV):
        acc = acc + cw_ref[SSM_CONV - 1 - k:SSM_CONV - k, :] * _shift_rows(x, prev, k, q_len)
    if carry:
        prev_ref[...] = x
    xc = _silu(acc)
    xs = xc[:, 0:SSM_INNER]
    b_all = xc[:, SSM_INNER:SSM_INNER + SSM_GROUPS * SSM_STATE]
    c_all = xc[:, SSM_INNER + SSM_GROUPS * SSM_STATE:]

    col = lax.broadcasted_iota(jnp.int32, (1, LANE), 1)
    dt_cols = (col >= DT_COL) & (col < DT_COL + SSM_HEADS)
    dt = jnp.where(dt_cols, jax.nn.softplus(gd_ref[...] + dtb_ref[...]), 0.0)
    a = dt * jnp.where(dt_cols, -jnp.exp(alog_ref[...]), 0.0)
    ri = lax.broadcasted_iota(jnp.int32, (TILE, TILE), 0)
    ci = lax.broadcasted_iota(jnp.int32, (TILE, TILE), 1)
    same = (ri // q_len) == (ci // q_len)
    causal = same & (ci <= ri)
    acum = _dot_exact_lhs(causal.astype(BF16), a)
    last_sel = (ci == (ri // q_len) * q_len + (q_len - 1)).astype(BF16)
    alast = _dot_exact_lhs(last_sel, acum)
    acum_t = acum.T
    e16 = e16_ref[...]
    xw = xs * _dot_exact_rhs(dt, e16)
    xwe = xw * _dot_exact_rhs(jnp.exp(alast - acum), e16)
    grow = jnp.exp(_dot_exact_rhs(acum, e16))
    rowseq = lax.broadcasted_iota(jnp.int32, (TILE, 1), 0) // q_len

    ys = []
    for g in range(SSM_GROUPS):
        cg = c_all[:, g * SSM_STATE:(g + 1) * SSM_STATE].astype(BF16)
        bg = b_all[:, g * SSM_STATE:(g + 1) * SSM_STATE].astype(BF16)
        cb = _dot_nt(cg, bg)
        gl = slice(g * SSM_HPG * SSM_HEADDIM, (g + 1) * SSM_HPG * SSM_HEADDIM)
        hl = slice(g * SSM_HPG, (g + 1) * SSM_HPG)
        for r in range(SSM_HPG):
            h = g * SSM_HPG + r
            c = DT_COL + h
            seg = acum[:, c:c + 1] - acum_t[c:c + 1, :]
            decay = jnp.where(causal, jnp.exp(jnp.where(causal, seg, 0.0)), 0.0)
            ys.append(_dot((cb * decay).astype(BF16), xw[:, h * SSM_HEADDIM:(h + 1) * SSM_HEADDIM].astype(BF16)))
        xwe_g = xwe[:, gl]
        if carry:
            h_old = h_scr[hl].reshape(SSM_HPG * SSM_HEADDIM, SSM_STATE)
            y_off = _dot_nt(cg, h_old.astype(BF16))
            scale = jnp.concatenate(
                [jnp.broadcast_to(jnp.exp(alast[0:1, DT_COL + g * SSM_HPG + r:DT_COL + g * SSM_HPG + r + 1]),
                                  (SSM_HEADDIM, SSM_STATE)) for r in range(SSM_HPG)], axis=0)
            h_new = h_old * scale + _dot_tn(xwe_g.astype(BF16), bg)
            h_scr[hl] = h_new.reshape(SSM_HPG, SSM_HEADDIM, SSM_STATE)
        else:
            offs = []
            for s in range(n_seq):
                h_old = h0_ref[s, hl].reshape(SSM_HPG * SSM_HEADDIM, SSM_STATE)
                offs.append(_dot_nt(cg, h_old.astype(BF16))[s * q_len:(s + 1) * q_len, :])
                a_row = alast[s * q_len:s * q_len + 1, :]
                scale = jnp.concatenate(
                    [jnp.broadcast_to(jnp.exp(a_row[:, DT_COL + g * SSM_HPG + r:DT_COL + g * SSM_HPG + r + 1]),
                                      (SSM_HEADDIM, SSM_STATE)) for r in range(SSM_HPG)], axis=0)
                x_s = jnp.where(rowseq == s, xwe_g, 0.0).astype(BF16)
                h_new = h_old * scale + _dot_tn(x_s, bg)
                hout_ref[s, hl] = h_new.reshape(SSM_HPG, SSM_HEADDIM, SSM_STATE)
            y_off = jnp.concatenate(offs, axis=0)
        for r in range(SSM_HPG):
            h = g * SSM_HPG + r
            ys[h] = ys[h] + y_off[:, r * SSM_HEADDIM:(r + 1) * SSM_HEADDIM] * grow[:, h * SSM_HEADDIM:(h + 1) * SSM_HEADDIM]
    if carry:
        @pl.when(pl.program_id(1) == pl.num_programs(1) - 1)
        def _():
            hout_ref[...] = h_scr[...]

    y = jnp.concatenate(ys, axis=1) + xs * dexp_ref[...]
    y = y * _silu(z_ref[...])
    gw = SSM_INNER // SSM_GROUPS
    parts = []
    for g in range(SSM_GROUPS):
        blk = y[:, g * gw:(g + 1) * gw]
        parts.append(blk * lax.rsqrt(jnp.mean(blk * blk, axis=-1, keepdims=True) + EPS))
    y_ref[...] = (jnp.concatenate(parts, axis=1) * ng_ref[...]).astype(BF16)


def _ssd(p3, prev, h0, cw, cb, dtb, alog, dexp, ng, e16, *, q_len):
    n_grp, rows, _ = p3.shape
    carry = prev is None
    n_chunks = rows // TILE
    n_seq = TILE // q_len
    zspec = pl.BlockSpec((None, TILE, SSM_INNER), lambda i, c: (i, c, P_Z // SSM_INNER))
    xspec = pl.BlockSpec((None, TILE, CONV_DIM), lambda i, c: (i, c, P_XBC // CONV_DIM))
    gspec = pl.BlockSpec((None, TILE, LANE), lambda i, c: (i, c, P_GD // LANE))
    const = lambda arr: pl.BlockSpec(arr.shape, lambda i, c: (0,) * arr.ndim)
    params = [cw, cb, dtb, alog, dexp, ng, e16]
    in_specs = [zspec, xspec, gspec]
    args = [p3, p3, p3]
    if carry:
        n_out_seq = n_grp
        hspec = pl.BlockSpec((None, SSM_HEADS, SSM_HEADDIM, SSM_STATE), lambda i, c: (i, 0, 0, 0))
        scratch = [pltpu.VMEM((TILE, CONV_DIM), F32), pltpu.VMEM((SSM_HEADS, SSM_HEADDIM, SSM_STATE), F32)]
    else:
        n_out_seq = n_grp * n_chunks * n_seq
        hspec = pl.BlockSpec((n_seq, SSM_HEADS, SSM_HEADDIM, SSM_STATE), lambda i, c: (i * n_chunks + c, 0, 0, 0))
        in_specs += [pl.BlockSpec((None, TILE, CONV_DIM), lambda i, c: (i, c, 0)), hspec]
        args += [prev, h0]
        scratch = []
    in_specs += [const(a) for a in params]
    return pl.pallas_call(
        functools.partial(_ssd_body, q_len=q_len, carry=carry),
        grid=(n_grp, n_chunks),
        in_specs=in_specs,
        out_specs=[pl.BlockSpec((None, TILE, SSM_INNER), lambda i, c: (i, c, 0)), hspec],
        out_shape=[jax.ShapeDtypeStruct((n_grp, rows, SSM_INNER), BF16),
                   jax.ShapeDtypeStruct((n_out_seq, SSM_HEADS, SSM_HEADDIM, SSM_STATE), F32)],
        scratch_shapes=scratch,
        compiler_params=_cparams("parallel", "arbitrary"),
        name="ssd",
    )(*args, *params)


def _importance_map_t(n_sel):
    n_cmp = TILE - 1
    c0 = np.arange(n_cmp) * CMP_STRIDE
    s0 = np.arange(n_sel) * SEL_BLOCK
    ov = np.minimum(c0[None, :] + CMP_BLOCK, s0[:, None] + SEL_BLOCK) - np.maximum(c0[None, :], s0[:, None])
    m = np.zeros((-(-n_sel // SUBLANE) * SUBLANE, TILE), np.float32)
    m[:n_sel, :n_cmp] = np.clip(ov, 0, None).astype(np.float32) / CMP_BLOCK
    return jnp.asarray(m, BF16)


def _dt_expand():
    e = np.zeros((LANE, SSM_INNER), np.float32)
    for h in range(SSM_HEADS):
        e[DT_COL + h, h * SSM_HEADDIM:(h + 1) * SSM_HEADDIM] = 1.0
    return jnp.asarray(e, BF16)


def _pad_to(x, axis, size):
    pad = [(0, 0)] * x.ndim
    pad[axis] = (0, size - x.shape[axis])
    return jnp.pad(x, pad)


def _history_rows(state, n_rows, width):
    b, k, _ = state.shape
    tile = jnp.pad(state, ((0, 0), (n_rows - k, 0), (0, width - state.shape[2])))
    return tile.reshape(b * n_rows, width)


def kernel(x_prompt, x_sample, cache_kv, cache_win_kv, state_ssm_conv, state_ssm, state_ffn_conv, page_table, rel_table, ln_mix_pre, w_in, cmp_pe, cmp_w1, cmp_b1, cmp_w2, cmp_b2, ssm_conv_w, ssm_conv_b, ssm_dt_bias, ssm_a_log, ssm_d, ssm_norm_g, w_out, ln_mix_post, ln_ffn_pre, ffn_w_gate, ffn_w_up, ffn_dw_w, ffn_dw_b, ffn_w_down, ln_ffn_post):
    bp, tp, d = x_prompt.shape
    bs, ts, _ = x_sample.shape
    n_pool = cache_kv.shape[1]
    n_pages = page_table.shape[1]
    past_len = n_pages * PAGE_SIZE
    win_len = cache_win_kv.shape[2]
    assert d == D_MODEL and tp == N_CHUNK_TILES * TILE and past_len == N_CHUNK_TILES * TILE
    assert N_HEADS * ts == LANE and ts == SUBLANE and tp >= WINDOW and win_len == WINDOW

    rel = rel_table.astype(F32)
    tab_p = jnp.repeat(rel.reshape(REL_BUCKETS, KV_HEADS, Q_PER_KV).transpose(1, 0, 2), TILE, axis=-1)
    tab_s = jnp.repeat(rel, ts, axis=1)
    cmpb_p, near, edge = _bias_prompt(tab_p, tp // TILE)
    cmpbt_s, selb, winb = _bias_sample(tab_s, past_len, ts, win_len)
    impt_p = _importance_map_t(-(-tp // SEL_BLOCK))
    impt_s = _importance_map_t(-(-(past_len + ts) // SEL_BLOCK))
    rsum, gsel, expand, place = _sample_selectors(ts, past_len + TILE)
    e16 = _dt_expand()
    dummy_pt = jnp.zeros((1, 1), jnp.int32)

    xp = x_prompt.reshape(bp * tp, d)
    xs = x_sample.reshape(bs * ts, d)
    outs = [[] for _ in range(10)]
    for l in range(w_in.shape[0]):
        o1 = ATTN_WIDTH
        o2 = o1 + N_KV_SLOTS * KV_WIDTH
        o3 = o2 + 3 * N_HEADS
        o4 = o3 + SSM_INNER
        o5 = o4 + CONV_DIM
        w = w_in[l].T
        wp = jnp.concatenate([w[:o1], w[o3:o4], w[o4:o5], w[o1:o2], w[o2:o3], w[o5:],
                              jnp.zeros((P_WIDTH - w.shape[0], d), w.dtype)], axis=0).astype(BF16)
        g_pre = ln_mix_pre[l].reshape(1, d)
        cw = _compress_weights(cmp_pe[l], cmp_w1[l], cmp_b1[l], cmp_w2[l], cmp_b2[l])
        ssm_w = (_pad_to(ssm_conv_w[l], 0, SUBLANE), ssm_conv_b[l].reshape(1, CONV_DIM),
                 _pad_to(jnp.pad(ssm_dt_bias[l].astype(F32), (DT_COL, 0)), 0, LANE).reshape(1, LANE),
                 _pad_to(jnp.pad(ssm_a_log[l].astype(F32), (DT_COL, 0)), 0, LANE).reshape(1, LANE),
                 jnp.repeat(ssm_d[l].astype(F32), SSM_HEADDIM).reshape(1, SSM_INNER),
                 ssm_norm_g[l].astype(F32).reshape(1, SSM_INNER), e16)
        w_attn = w_out[l, :ATTN_WIDTH].astype(BF16)
        w_ssm = w_out[l, ATTN_WIDTH:].astype(BF16)
        g_post = ln_mix_post[l].reshape(1, d)
        g_pre2 = ln_ffn_pre[l].reshape(1, d)
        wg = _pad_to(ffn_w_gate[l], 1, FF_PAD).astype(BF16)
        wu = _pad_to(ffn_w_up[l], 1, FF_PAD).astype(BF16)
        dw = _pad_to(_pad_to(ffn_dw_w[l], 1, FF_PAD), 0, SUBLANE)
        db = _pad_to(ffn_dw_b[l].reshape(1, D_FF), 1, FF_PAD)
        wd = _pad_to(ffn_w_down[l], 0, FF_PAD).astype(BF16)
        g_post2 = ln_ffn_post[l].reshape(1, d)

        p3 = _proj(xp, g_pre, wp, tm=1024).reshape(bp, tp, P_WIDTH)
        kc, vc = _compress(p3, lambda j, cb, b, pt: (b, j, P_KV // LANE + cb), bp, dummy_pt, *cw)
        attn = _nsa_prompt(p3, kc, vc, cmpb_p, near, edge, tab_p, impt_p)
        ssm, h_new = _ssd(p3, None, None, *ssm_w, q_len=TILE)
        x1, hn = _outproj(attn.reshape(bp * tp, ATTN_WIDTH), ssm.reshape(bp * tp, SSM_INNER), xp,
                          w_attn, w_ssm, g_post, g_pre2)
        act, tail = _ffn_up(hn, wg, wu, dw, db, None, seq_len=tp, tm=tp)
        xp = _ffn_down(act, wd, x1, g_post2)
        kv = p3[:, :, P_KV:P_KV + N_KV_SLOTS * KV_WIDTH]
        outs[0].append(kv[:, :, :N_CACHE_SLOTS * KV_WIDTH].reshape(bp, tp, N_CACHE_SLOTS, KV_HEADS, HEAD_DIM))
        outs[2].append(kv[:, tp - WINDOW:, N_CACHE_SLOTS * KV_WIDTH:].reshape(bp, WINDOW, 2, KV_HEADS, HEAD_DIM))
        outs[4].append(p3[:, tp - (SSM_CONV - 1):, P_XBC:P_XBC + CONV_DIM])
        outs[6].append(h_new)
        outs[8].append(tail[:, SUBLANE - (FFN_CONV - 1):, :D_FF])

        p3 = _proj(xs, g_pre, wp, tm=bs * ts).reshape(bs, ts, P_WIDTH)
        cache_t = cache_kv[l].transpose(0, 2, 3, 4, 1).reshape(n_pool, N_CACHE_SLOTS * KV_WIDTH, PAGE_SIZE)
        win_t = cache_win_kv[l].transpose(0, 2, 3, 4, 1).reshape(bs, 2 * KV_WIDTH, win_len)
        w1p, perm = _compress_paged_weights(cmp_w1[l])
        kc, vc = _compress_paged(cache_t, page_table, perm, w1p, _pe_bias(cw[0], w1p), cw[2], cw[3], cw[4])
        attn, win_new = _nsa_sample(p3, kc, vc, cache_t, win_t, page_table, cmpbt_s, selb, winb,
                                    impt_s, rsum, gsel, expand, place)
        win_new = win_new.reshape(bs, 2, KV_HEADS, HEAD_DIM, win_len).transpose(0, 4, 1, 2, 3)
        prev = _history_rows(state_ssm_conv[l], ts, CONV_DIM).reshape(1, bs * ts, CONV_DIM)
        ssm, h_new = _ssd(p3.reshape(1, bs * ts, P_WIDTH), prev, state_ssm[l], *ssm_w, q_len=ts)
        x1, hn = _outproj(attn.reshape(bs * ts, ATTN_WIDTH), ssm.reshape(bs * ts, SSM_INNER), xs,
                          w_attn, w_ssm, g_post, g_pre2)
        prev_f = _history_rows(state_ffn_conv[l], ts, FF_PAD)
        act, gp = _ffn_up(hn, wg, wu, dw, db, prev_f, seq_len=ts, tm=bs * ts)
        xs = _ffn_down(act, wd, x1, g_post2, tm=bs * ts)
        kv = p3[:, :, P_KV:P_KV + N_KV_SLOTS * KV_WIDTH]
        outs[1].append(kv[:, :, :N_CACHE_SLOTS * KV_WIDTH].reshape(bs, ts, N_CACHE_SLOTS, KV_HEADS, HEAD_DIM))
        outs[3].append(win_new)
        outs[5].append(p3[:, ts - (SSM_CONV - 1):, P_XBC:P_XBC + CONV_DIM])
        outs[7].append(h_new)
        outs[9].append(gp.reshape(bs, ts, FF_PAD)[:, ts - (FFN_CONV - 1):, :D_FF])
    return (xp.reshape(bp, tp, d), xs.reshape(bs, ts, d)) + tuple(jnp.stack(o) for o in outs)
```

```python
import functools
import math

import numpy as np
import jax
import jax.numpy as jnp
from jax import lax
from jax.experimental import pallas as pl
from jax.experimental.pallas import tpu as pltpu

F32 = jnp.float32
BF16 = jnp.bfloat16

D_MODEL = 2048
PAGE_SIZE = 128
HEAD_DIM = 64
ATTN_WIDTH = D_MODEL // 2
N_HEADS = ATTN_WIDTH // HEAD_DIM
KV_HEADS = 4
Q_PER_KV = N_HEADS // KV_HEADS
CMP_STRIDE = 16
CMP_BLOCK = 2 * CMP_STRIDE
CMP_HIDDEN = 2 * HEAD_DIM
SEL_BLOCK = 64
N_SELECT = 8
WINDOW = 512
N_KV_SLOTS = 6
N_CACHE_SLOTS = 4
KV_WIDTH = KV_HEADS * HEAD_DIM
SSM_INNER = D_MODEL - ATTN_WIDTH
SSM_HEADDIM = 64
SSM_HEADS = SSM_INNER // SSM_HEADDIM
SSM_GROUPS = 4
SSM_HPG = SSM_HEADS // SSM_GROUPS
SSM_STATE = 128
SSM_CONV = 4
CONV_DIM = SSM_INNER + 2 * SSM_GROUPS * SSM_STATE
D_FF = ((8 * D_MODEL // 3 + 127) // 128) * 128
FFN_CONV = 3
REL_BUCKETS = 32
REL_MAX_DIST = 128
EPS = 1e-6
NEG = -1e30
TINY = 1e-30
FORCE_BONUS = 1e4
SCALE = HEAD_DIM ** -0.5
LOG2E = math.log2(math.e)

LANE = 128
SUBLANE = 8
TILE = 128
VMEM_LIMIT = 56 * 1024 * 1024

P_Q = 0
P_Z = ATTN_WIDTH
P_XBC = P_Z + SSM_INNER
P_KV = P_XBC + CONV_DIM
P_GD = P_KV + N_KV_SLOTS * KV_WIDTH
GD_WIDTH = 3 * N_HEADS + SSM_HEADS
P_WIDTH = 6144
DT_COL = 3 * N_HEADS
FF_PAD = 5632
FFN_SUB = 256
FFN_ROWS = 256
FFN_STRIP = 32


def _cparams(*sem):
    return pltpu.CompilerParams(dimension_semantics=sem, vmem_limit_bytes=VMEM_LIMIT)


def _rms(x, g):
    return x * lax.rsqrt(jnp.mean(x * x, axis=-1, keepdims=True) + EPS) * g


def _split3(x):
    hi = x.astype(BF16)
    r1 = x - hi.astype(F32)
    mid = r1.astype(BF16)
    lo = (r1 - mid.astype(F32)).astype(BF16)
    return hi, mid, lo


def _dot(a, b):
    return jnp.dot(a, b, preferred_element_type=F32)


def _dot_nt(a, b):
    return lax.dot_general(a, b, (((1,), (1,)), ((), ())), preferred_element_type=F32)


def _dot_exact_rhs(x, sel):
    hi, mid, lo = _split3(x)
    return _dot(hi, sel) + _dot(mid, sel) + _dot(lo, sel)


def _dot_exact_lhs(sel, x):
    hi, mid, lo = _split3(x)
    return _dot(sel, hi) + _dot(sel, mid) + _dot(sel, lo)


def _gelu_tanh(x):
    return 0.5 * x * (1.0 + jnp.tanh(math.sqrt(2.0 / math.pi) * (x + 0.044715 * (x * x * x))))


def _silu(x):
    return x * jax.nn.sigmoid(x)


def _proj_body(x_ref, g_ref, w_ref, o_ref, xn_ref):
    @pl.when(pl.program_id(1) == 0)
    def _():
        xn_ref[...] = _rms(x_ref[...], g_ref[...]).astype(BF16)

    o_ref[...] = _dot_nt(xn_ref[...], w_ref[...])


def _proj(x2d, g, w_t, tm, tn=1024):
    m, k = x2d.shape
    n = w_t.shape[0]
    w = w_t
    return pl.pallas_call(
        _proj_body,
        grid=(m // tm, n // tn),
        in_specs=[pl.BlockSpec((tm, k), lambda i, j: (i, 0)),
                  pl.BlockSpec((1, k), lambda i, j: (0, 0)),
                  pl.BlockSpec((tn, k), lambda i, j: (j, 0))],
        out_specs=pl.BlockSpec((tm, tn), lambda i, j: (i, j)),
        out_shape=jax.ShapeDtypeStruct((m, n), F32),
        scratch_shapes=[pltpu.VMEM((tm, k), BF16)],
        compiler_params=_cparams("parallel", "arbitrary"),
        name="proj",
    )(x2d, g, w)


def _outproj_body(a_ref, s_ref, x_ref, wa_ref, ws_ref, g1_ref, g2_ref, x1_ref, hn_ref):
    mix = _dot(a_ref[...], wa_ref[...]) + _dot(s_ref[...], ws_ref[...])
    x1 = x_ref[...] + _rms(mix, g1_ref[...])
    x1_ref[...] = x1
    hn_ref[...] = _rms(x1, g2_ref[...]).astype(BF16)


def _outproj(attn, ssm, x2d, w_attn, w_ssm, g_post, g_pre2, tm=512):
    m, d = x2d.shape
    ka = attn.shape[1]
    const = lambda i: (0, 0)
    return pl.pallas_call(
        _outproj_body,
        grid=(m // tm,),
        in_specs=[pl.BlockSpec((tm, ka), lambda i: (i, 0)),
                  pl.BlockSpec((tm, ka), lambda i: (i, 0)),
                  pl.BlockSpec((tm, d), lambda i: (i, 0)),
                  pl.BlockSpec((ka, d), const),
                  pl.BlockSpec((ka, d), const),
                  pl.BlockSpec((1, d), const),
                  pl.BlockSpec((1, d), const)],
        out_specs=[pl.BlockSpec((tm, d), lambda i: (i, 0)),
                   pl.BlockSpec((tm, d), lambda i: (i, 0))],
        out_shape=[jax.ShapeDtypeStruct((m, d), F32), jax.ShapeDtypeStruct((m, d), BF16)],
        compiler_params=_cparams("parallel"),
        name="outproj",
    )(attn, ssm, x2d, w_attn, w_ssm, g_post, g_pre2)


def _shift_rows(x, prev, k, seq_len):
    rows, cols = x.shape
    if seq_len >= rows:
        t = lax.broadcasted_iota(jnp.int32, (rows, 1), 0)
        return jnp.where(t >= k, pltpu.roll(x, k, 0), 0.0 if prev is None else pltpu.roll(prev, k, 0))
    assert seq_len == SUBLANE and prev is not None
    shape = (rows // seq_len, seq_len, cols)
    t = lax.broadcasted_iota(jnp.int32, shape, 1)
    out = jnp.where(t >= k, pltpu.roll(x.reshape(shape), k, 1), pltpu.roll(prev.reshape(shape), k, 1))
    return out.reshape(rows, cols)


def _ffn_up_body(*refs, seq_len, has_prev):
    if has_prev:
        h_ref, wg_ref, wu_ref, dw_ref, db_ref, prev_ref, act_ref, tail_ref, gp_ref, up_ref = refs
    else:
        h_ref, wg_ref, wu_ref, dw_ref, db_ref, act_ref, tail_ref, gp_ref, up_ref = refs
    assert seq_len == SUBLANE if has_prev else seq_len == act_ref.shape[0]
    tm, tn = act_ref.shape
    rows = up_ref.shape[1]
    n_row = tm // rows
    n_buf = gp_ref.shape[0]
    strip3 = (FFN_STRIP // SUBLANE, SUBLANE, FFN_SUB)
    t8 = lax.broadcasted_iota(jnp.int32, strip3, 1)

    def project(buf, rs, cs, before):
        h = h_ref[rs, :]
        gp_ref[buf, SUBLANE:, :] = _dot(h, wg_ref[:, cs])
        up_ref[buf] = _dot(h, wu_ref[:, cs])
        if not has_prev:
            gp_ref[buf, 0:SUBLANE, :] = (jnp.zeros((SUBLANE, FFN_SUB), F32) if before is None
                                         else gp_ref[before, rows:rows + SUBLANE, :])

    def gate(r, rs, cs, buf):
        taps = [dw_ref[FFN_CONV - 1 - k:FFN_CONV - k, cs] for k in range(FFN_CONV)]
        bias = db_ref[:, cs]
        for r0 in range(0, rows, FFN_STRIP):
            x = gp_ref[buf, SUBLANE + r0:SUBLANE + r0 + FFN_STRIP, :]
            if has_prev:
                above = prev_ref[rs.start + r0:rs.start + r0 + FFN_STRIP, cs]
            else:
                above = gp_ref[buf, r0:r0 + FFN_STRIP, :]
            x3, above3 = x.reshape(strip3), above.reshape(strip3)
            acc = bias + taps[0] * x
            for k in range(1, FFN_CONV):
                shifted = jnp.where(t8 >= k, pltpu.roll(x3, k, 1), pltpu.roll(above3, k, 1))
                acc = acc + taps[k] * shifted.reshape(FFN_STRIP, FFN_SUB)
            out_rows = slice(rs.start + r0, rs.start + r0 + FFN_STRIP)
            act_ref[out_rows, cs] = (_gelu_tanh(acc) * up_ref[buf, r0:r0 + FFN_STRIP, :]).astype(BF16)
            if has_prev:
                tail_ref[out_rows, cs] = x
        if not has_prev and r == n_row - 1:
            tail_ref[:, cs] = gp_ref[buf, rows:rows + SUBLANE, :]

    pending = None
    unit = 0
    for c in range(tn // FFN_SUB):
        cs = slice(c * FFN_SUB, (c + 1) * FFN_SUB)
        before = None
        for r in range(n_row):
            rs = slice(r * rows, (r + 1) * rows)
            buf = unit % n_buf
            project(buf, rs, cs, before)
            if pending is not None:
                gate(*pending)
            pending = (r, rs, cs, buf)
            before = buf
            unit += 1
    gate(*pending)


def _ffn_up(hn, wg, wu, dw, db, prev, *, seq_len, tm, tn=512):
    m, d = hn.shape
    ni = m // tm
    nj = FF_PAD // tn
    has_prev = prev is not None
    in_specs = [pl.BlockSpec((tm, d), lambda i, j: (i, 0)),
                pl.BlockSpec((d, tn), lambda i, j: (0, j)),
                pl.BlockSpec((d, tn), lambda i, j: (0, j)),
                pl.BlockSpec((SUBLANE, tn), lambda i, j: (0, j)),
                pl.BlockSpec((1, tn), lambda i, j: (0, j))]
    args = [hn, wg, wu, dw, db]
    if has_prev:
        in_specs.append(pl.BlockSpec((tm, tn), lambda i, j: (i, j)))
        args.append(prev)
        tail_shape = jax.ShapeDtypeStruct((m, FF_PAD), F32)
        tail_spec = pl.BlockSpec((tm, tn), lambda i, j: (i, j))
    else:
        tail_shape = jax.ShapeDtypeStruct((ni, SUBLANE, FF_PAD), F32)
        tail_spec = pl.BlockSpec((None, SUBLANE, tn), lambda i, j: (i, 0, j))
    return pl.pallas_call(
        functools.partial(_ffn_up_body, seq_len=seq_len, has_prev=has_prev),
        grid=(ni, nj),
        in_specs=in_specs,
        out_specs=[pl.BlockSpec((tm, tn), lambda i, j: (i, j)), tail_spec],
        out_shape=[jax.ShapeDtypeStruct((m, FF_PAD), BF16), tail_shape],
        scratch_shapes=[pltpu.VMEM((3, SUBLANE + min(FFN_ROWS, tm), FFN_SUB), F32),
                        pltpu.VMEM((3, min(FFN_ROWS, tm), FFN_SUB), F32)],
        compiler_params=_cparams("parallel", "parallel"),
        name="ffn_up",
    )(*args)


def _ffn_down_body(a_ref, w_ref, x_ref, g_ref, o_ref, acc_ref):
    k = pl.program_id(1)

    @pl.when(k == 0)
    def _():
        acc_ref[...] = jnp.zeros_like(acc_ref)

    acc_ref[...] += _dot(a_ref[...], w_ref[...])

    @pl.when(k == pl.num_programs(1) - 1)
    def _():
        o_ref[...] = x_ref[...] + _rms(acc_ref[...], g_ref[...])


def _ffn_down(act, wd, x1, g, tm=1024, tk=512):
    m, d = x1.shape
    return pl.pallas_call(
        _ffn_down_body,
        grid=(m // tm, FF_PAD // tk),
        in_specs=[pl.BlockSpec((tm, tk), lambda i, k: (i, k)),
                  pl.BlockSpec((tk, d), lambda i, k: (k, 0)),
                  pl.BlockSpec((tm, d), lambda i, k: (i, 0)),
                  pl.BlockSpec((1, d), lambda i, k: (0, 0))],
        out_specs=pl.BlockSpec((tm, d), lambda i, k: (i, 0)),
        out_shape=jax.ShapeDtypeStruct((m, d), F32),
        scratch_shapes=[pltpu.VMEM((tm, d), F32)],
        compiler_params=_cparams("parallel", "arbitrary"),
        name="ffn_down",
    )(act, wd, x1, g)


N_CHUNK_TILES = 16
CHUNKS_PER_TILE = TILE // CMP_STRIDE
CMP_COL_BLOCKS = 2 * KV_WIDTH // LANE


def _compress_body(*refs):
    pt_ref = refs[0]
    del pt_ref
    n_in = N_CHUNK_TILES * CMP_COL_BLOCKS
    tiles = refs[1:1 + n_in]
    pe_ref, w1_ref, b1_ref, w2_ref, b2_ref, kc_ref, vc_ref, lhs_a, lhs_b = refs[1 + n_in:]
    out_refs = (kc_ref, vc_ref)
    for slot in range(2):
        outs = []
        for gp in range(KV_HEADS // 2):
            cb = slot * (KV_HEADS // 2) + gp
            for s in range(CMP_STRIDE):
                x = jnp.concatenate(
                    [tiles[j * CMP_COL_BLOCKS + cb][pl.ds(s, CHUNKS_PER_TILE, stride=CMP_STRIDE), :]
                     for j in range(N_CHUNK_TILES)], axis=0)
                lhs_a[:, s * LANE:(s + 1) * LANE] = (x + pe_ref[slot, 0, s:s + 1, :]).astype(BF16)
                lhs_b[:, s * LANE:(s + 1) * LANE] = (x + pe_ref[slot, 1, s:s + 1, :]).astype(BF16)
            first = _dot(lhs_a[...], w1_ref[slot, 0])
            second = _dot(lhs_b[...], w1_ref[slot, 1])
            hid = first + pltpu.roll(second, TILE - 1, 0) + b1_ref[slot]
            act = _gelu_tanh(hid).astype(BF16)
            for gj in range(2):
                outs.append(_dot(act[:, gj * CMP_HIDDEN:(gj + 1) * CMP_HIDDEN], w2_ref[slot]) + b2_ref[slot])
        out_refs[slot][...] = jnp.concatenate(outs, axis=1)


def _compress(src, tile_map, n_seq, page_table, pe, w1, b1, w2, b2):
    in_specs = [pl.BlockSpec((None, TILE, LANE), functools.partial(tile_map, j, cb))
                for j in range(N_CHUNK_TILES) for cb in range(CMP_COL_BLOCKS)]
    full = lambda shape: pl.BlockSpec(shape, lambda b, pt: (0,) * len(shape))
    in_specs += [full(pe.shape), full(w1.shape), full(b1.shape), full(w2.shape), full(b2.shape)]
    out_spec = pl.BlockSpec((None, TILE, KV_WIDTH), lambda b, pt: (b, 0, 0))
    return pl.pallas_call(
        _compress_body,
        grid_spec=pltpu.PrefetchScalarGridSpec(
            num_scalar_prefetch=1, grid=(n_seq,), in_specs=in_specs, out_specs=[out_spec, out_spec],
            scratch_shapes=[pltpu.VMEM((TILE, CMP_STRIDE * LANE), BF16), pltpu.VMEM((TILE, CMP_STRIDE * LANE), BF16)]),
        out_shape=[jax.ShapeDtypeStruct((n_seq, TILE, KV_WIDTH), F32)] * 2,
        compiler_params=_cparams("parallel"),
        name="compress",
    )(page_table, *([src] * (N_CHUNK_TILES * CMP_COL_BLOCKS)), pe, w1, b1, w2, b2)


def _compress_weights(cmp_pe, cmp_w1, cmp_b1, cmp_w2, cmp_b2):
    pe = cmp_pe.reshape(2, 2, CMP_STRIDE, HEAD_DIM)
    pe = jnp.concatenate([pe, pe], axis=-1)
    w = cmp_w1.reshape(2, 2, CMP_STRIDE, HEAD_DIM, CMP_HIDDEN)
    eye = jnp.eye(2, dtype=w.dtype)
    w1 = jnp.einsum('absdh,ij->absidjh', w, eye).reshape(2, 2, CMP_STRIDE * LANE, 2 * CMP_HIDDEN).astype(BF16)
    b1 = jnp.concatenate([cmp_b1, cmp_b1], axis=-1).reshape(2, 1, 2 * CMP_HIDDEN)
    return pe, w1, b1, cmp_w2.astype(BF16), cmp_b2.reshape(2, 1, HEAD_DIM)


PAGES_PER_PERM = 2


def _pe_bias_body(pe_ref, w1_ref, o_ref):
    for slot in range(2):
        cols = []
        for half in range(2):
            acc = jnp.zeros((2 * SUBLANE, 2 * CMP_HIDDEN), F32)
            for s in range(CMP_STRIDE):
                row = jnp.broadcast_to(pe_ref[slot, half, s:s + 1, :], (2 * SUBLANE, LANE))
                w = w1_ref[slot, s * LANE:(s + 1) * LANE, half * 2 * CMP_HIDDEN:(half + 1) * 2 * CMP_HIDDEN]
                acc = acc + _dot_exact_rhs(row, w)
            cols.append(acc)
        o_ref[slot] = jnp.concatenate(cols, axis=1)


def _pe_bias(pe, w1):
    return pl.pallas_call(
        _pe_bias_body,
        out_shape=jax.ShapeDtypeStruct((2, 2 * SUBLANE, 4 * CMP_HIDDEN), F32),
        name="pe_bias",
    )(pe, w1)


def _compress_paged_body(*refs):
    n_in = N_CHUNK_TILES
    pt_ref = refs[0]
    del pt_ref
    pages = refs[1:1 + n_in]
    perm_ref, w1_ref, peb_ref, b1_ref, w2_ref, b2_ref, kc_ref, vc_ref, xp_ref, hid_ref = refs[1 + n_in:]
    n_perm = N_CHUNK_TILES // PAGES_PER_PERM
    slab = PAGES_PER_PERM * CHUNKS_PER_TILE
    for jp in range(n_perm):
        xt = jnp.concatenate([pages[PAGES_PER_PERM * jp + jj][...] for jj in range(PAGES_PER_PERM)], axis=1)
        xp_ref[jp] = _dot_nt(perm_ref[...], xt.astype(BF16)).astype(BF16)
    out_refs = (kc_ref, vc_ref)

    def first_layer(slot, gp):
        cb = slot * (KV_HEADS // 2) + gp
        lhs = jnp.concatenate(
            [jnp.concatenate([xp_ref[jp, s * slab:(s + 1) * slab, cb * LANE:(cb + 1) * LANE]
                              for jp in range(n_perm)], axis=0)
             for s in range(CMP_STRIDE)], axis=1)
        hid_ref[gp] = _dot(lhs, w1_ref[slot])
        return gp

    def second_layer(slot, gp, buf):
        acc = hid_ref[buf] + peb_ref[slot, 0:1, :]
        hid = (acc[:, 0:2 * CMP_HIDDEN] + pltpu.roll(acc[:, 2 * CMP_HIDDEN:], TILE - 1, 0) + b1_ref[slot])
        act = _gelu_tanh(hid).astype(BF16)
        for gj in range(2):
            g = 2 * gp + gj
            out_refs[slot][:, g * HEAD_DIM:(g + 1) * HEAD_DIM] = (
                _dot(act[:, gj * CMP_HIDDEN:(gj + 1) * CMP_HIDDEN], w2_ref[slot]) + b2_ref[slot])

    pending = None
    for slot in range(2):
        for gp in range(KV_HEADS // 2):
            acc = first_layer(slot, gp)
            if pending is not None:
                second_layer(*pending)
            pending = (slot, gp, acc)
    second_layer(*pending)


def _compress_paged(cache_t, page_table, perm, w1, peb, b1, w2, b2):
    n_seq = page_table.shape[0]
    in_specs = [pl.BlockSpec((None, 2 * KV_WIDTH, PAGE_SIZE), functools.partial(lambda j, b, pt: (pt[b, j], 0, 0), j))
                for j in range(N_CHUNK_TILES)]
    const = lambda arr: pl.BlockSpec(arr.shape, lambda b, pt: (0,) * arr.ndim)
    consts = [perm, w1, peb, b1, w2, b2]
    in_specs += [const(a) for a in consts]
    out_spec = pl.BlockSpec((None, TILE, KV_WIDTH), lambda b, pt: (b, 0, 0))
    rows = PAGES_PER_PERM * PAGE_SIZE
    return pl.pallas_call(
        _compress_paged_body,
        grid_spec=pltpu.PrefetchScalarGridSpec(
            num_scalar_prefetch=1, grid=(n_seq,), in_specs=in_specs, out_specs=[out_spec, out_spec],
            scratch_shapes=[pltpu.VMEM((N_CHUNK_TILES // PAGES_PER_PERM, rows, 2 * KV_WIDTH), BF16),
                            pltpu.VMEM((KV_HEADS // 2, TILE, 4 * CMP_HIDDEN), F32)]),
        out_shape=[jax.ShapeDtypeStruct((n_seq, TILE, KV_WIDTH), F32)] * 2,
        compiler_params=_cparams("parallel"),
        name="compress_paged",
    )(page_table, *([cache_t] * N_CHUNK_TILES), *consts)


def _compress_paged_weights(cmp_w1):
    w = cmp_w1.reshape(2, 2, CMP_STRIDE, HEAD_DIM, CMP_HIDDEN)
    eye = jnp.eye(2, dtype=w.dtype)
    w1 = jnp.einsum('absdh,ij->asidbjh', w, eye).reshape(2, CMP_STRIDE * LANE, 4 * CMP_HIDDEN).astype(BF16)
    rows = PAGES_PER_PERM * PAGE_SIZE
    r = np.arange(rows)
    s, page, chunk = r // (PAGES_PER_PERM * CHUNKS_PER_TILE), (r // CHUNKS_PER_TILE) % PAGES_PER_PERM, r % CHUNKS_PER_TILE
    src = page * PAGE_SIZE + chunk * CMP_STRIDE + s
    perm = (np.arange(rows)[None, :] == src[:, None]).astype(np.float32)
    return w1, jnp.asarray(perm, BF16)


def _bucket_thresholds():
    n = np.arange(0, 2 * REL_MAX_DIST)
    max_exact = REL_BUCKETS // 2
    nf = np.maximum(n, 1).astype(np.float32)
    large = max_exact + (np.log(nf / np.float32(max_exact)) / np.float32(math.log(REL_MAX_DIST / max_exact))
                         * np.float32(REL_BUCKETS - max_exact)).astype(np.int32)
    bucket = np.where(n < max_exact, n, np.minimum(large, REL_BUCKETS - 1))
    assert (np.diff(bucket) >= 0).all() and (np.diff(bucket) <= 1).all()
    return [int(np.argmax(bucket >= k)) for k in range(1, REL_BUCKETS)]


BUCKET_START = _bucket_thresholds()
CMP_BAND = 2 * SUBLANE
assert (BUCKET_START[-1] - 1 + CMP_BLOCK - 1) // CMP_STRIDE <= SUBLANE
assert (TILE - CMP_BLOCK) // CMP_STRIDE < SUBLANE


def _rel_bias(dist, tab_ref):
    bias = jnp.broadcast_to(tab_ref[0:1, :], dist.shape)
    for k in range(1, REL_BUCKETS):
        bias = jnp.where(dist >= BUCKET_START[k - 1], tab_ref[k:k + 1, :], bias)
    return bias


def _bias_prompt_body(tab_ref, cmpb_ref, near_ref, edge_ref):
    qb = pl.program_id(0)
    shape = (TILE, Q_PER_KV * TILE)
    row = lax.broadcasted_iota(jnp.int32, shape, 0)
    qi = lax.broadcasted_iota(jnp.int32, shape, 1) % TILE
    dist_c = qb * TILE + qi - (row * CMP_STRIDE + CMP_BLOCK - 1)
    band0 = pl.multiple_of(jnp.maximum(qb * (TILE // CMP_STRIDE) - SUBLANE, 0), SUBLANE)
    band_shape = (CMP_BAND, Q_PER_KV * TILE)
    band_row = band0 + lax.broadcasted_iota(jnp.int32, band_shape, 0)
    band_qi = lax.broadcasted_iota(jnp.int32, band_shape, 1) % TILE
    dist_b = qb * TILE + band_qi - (band_row * CMP_STRIDE + CMP_BLOCK - 1)
    for g in range(KV_HEADS):
        cmpb_ref[g] = jnp.where(dist_c >= 0, tab_ref[g, REL_BUCKETS - 1:REL_BUCKETS, :], NEG)
        cmpb_ref[g, pl.ds(band0, CMP_BAND), :] = jnp.where(dist_b >= 0, _rel_bias(dist_b, tab_ref.at[g]), NEG)

    @pl.when(qb == 0)
    def _():
        for g in range(KV_HEADS):
            for delta in range(2):
                dist = delta * TILE + qi - row
                b = _rel_bias(dist, tab_ref.at[g])
                near_ref[g, delta] = jnp.where(dist >= 0, b * LOG2E, NEG)
        edge_ref[...] = jnp.where(qi <= row, 0.0, NEG)


def _bias_prompt(tab, n_qb):
    lanes = Q_PER_KV * TILE
    return pl.pallas_call(
        _bias_prompt_body,
        grid=(n_qb,),
        in_specs=[pl.BlockSpec(tab.shape, lambda i: (0, 0, 0))],
        out_specs=[pl.BlockSpec((None, KV_HEADS, TILE, lanes), lambda i: (i, 0, 0, 0)),
                   pl.BlockSpec((KV_HEADS, 2, TILE, lanes), lambda i: (0, 0, 0, 0)),
                   pl.BlockSpec((TILE, lanes), lambda i: (0, 0))],
        out_shape=[jax.ShapeDtypeStruct((n_qb, KV_HEADS, TILE, lanes), F32),
                   jax.ShapeDtypeStruct((KV_HEADS, 2, TILE, lanes), F32),
                   jax.ShapeDtypeStruct((TILE, lanes), F32)],
        compiler_params=_cparams("arbitrary"),
        name="bias_prompt",
    )(tab)


def _bias_sample_body(tab_ref, cmpbt_ref, selb_ref, winb_ref, *, past_len, n_new, win_len):
    qt = lax.broadcasted_iota(jnp.int32, (TILE, LANE), 1) % n_new
    row = lax.broadcasted_iota(jnp.int32, (TILE, LANE), 0)
    dist = past_len + qt - (row * CMP_STRIDE + CMP_BLOCK - 1)
    cmpbt_ref[...] = jnp.where(dist >= 0, _rel_bias(dist, tab_ref), NEG)
    for j in range(selb_ref.shape[1] // TILE):
        key = row + j * TILE
        dist = past_len + qt - key
        ok = (dist >= 0) & (key < past_len + n_new)
        selb_ref[:, j * TILE:(j + 1) * TILE] = jnp.where(ok, _rel_bias(dist, tab_ref), NEG).T
    for j in range(winb_ref.shape[1] // TILE):
        key = row + j * TILE
        dist = win_len + qt - key
        ok = (dist >= 0) & (dist <= WINDOW) & (key < win_len + n_new)
        winb_ref[:, j * TILE:(j + 1) * TILE] = jnp.where(ok, _rel_bias(dist, tab_ref), NEG).T


def _bias_sample(tab, past_len, n_new, win_len):
    return pl.pallas_call(
        functools.partial(_bias_sample_body, past_len=past_len, n_new=n_new, win_len=win_len),
        out_shape=[jax.ShapeDtypeStruct((TILE, LANE), F32),
                   jax.ShapeDtypeStruct((LANE, past_len + TILE), F32),
                   jax.ShapeDtypeStruct((LANE, win_len + TILE), F32)],
        name="bias_sample",
    )(tab)


def _select_blocks(imp, blk, q_blk):
    valid = blk <= q_blk
    forced = (blk == 0) | (blk == q_blk) | (blk == q_blk - 1)
    score = jnp.where(valid, imp + jnp.where(forced, FORCE_BONUS, 0.0), -jnp.inf)
    rank = jnp.zeros(score.shape, jnp.int32)
    for i in range(score.shape[0]):
        row = score[i:i + 1, :]
        beats = (row > score) | ((row == score) & (blk > i))
        rank = rank + beats.astype(jnp.int32)
    return jnp.where(valid & (rank < N_SELECT), 0.0, NEG)


def _nsa_prompt_body(q_ref, gd_ref, kc_ref, vc_ref, sk_ref, sv_ref, wk_ref, wv_ref, cmpb_ref, near_ref,
                     edge_ref, tab_ref, impt_ref, o_ref, qs_ref, seladd_ref, selfar_ref, m_ref, l_ref, acc_ref,
                     out_ref):
    qb = pl.program_id(1)
    lanes = Q_PER_KV * TILE
    q = q_ref[...] * SCALE
    gate_t = jax.nn.sigmoid(gd_ref[...]).T
    n_sel = seladd_ref.shape[1]
    blk = lax.broadcasted_iota(jnp.int32, (n_sel, TILE), 0)
    q_blk = (qb * TILE + lax.broadcasted_iota(jnp.int32, (n_sel, TILE), 1)) // SEL_BLOCK
    groups = range(KV_HEADS)

    def hs(g):
        return slice(g * HEAD_DIM, (g + 1) * HEAD_DIM)

    def gate_row(branch, g):
        c0 = branch * N_HEADS + g * Q_PER_KV
        return jnp.concatenate([gate_t[c0 + r:c0 + r + 1, :] for r in range(Q_PER_KV)], axis=1)

    def far_bias(g):
        return tab_ref[g, REL_BUCKETS - 1:REL_BUCKETS, :] * LOG2E

    def stack_heads(x, g):
        return jnp.concatenate(
            [x[:, (g * Q_PER_KV + r) * HEAD_DIM:(g * Q_PER_KV + r + 1) * HEAD_DIM] for r in range(Q_PER_KV)],
            axis=0).astype(BF16)

    q2 = q * LOG2E
    for g in groups:
        qs = stack_heads(q, g)
        qs_ref[g] = stack_heads(q2, g)
        s = _dot_nt(kc_ref[:, hs(g)].astype(BF16), qs) + cmpb_ref[g]
        m = jnp.max(s, axis=0, keepdims=True)
        e = jnp.exp(s - m)
        norm = jnp.where(m > 0.5 * NEG, 1.0 / jnp.maximum(jnp.sum(e, axis=0, keepdims=True), TINY), 0.0)
        p = e * norm
        out_ref[g] = gate_row(0, g) * _dot(vc_ref[:, hs(g)].T.astype(BF16), p.astype(BF16))
        psum = p[:, 0:TILE]
        for r in range(1, Q_PER_KV):
            psum = psum + p[:, r * TILE:(r + 1) * TILE]
        sa = _select_blocks(_dot_exact_lhs(impt_ref[...], psum), blk, q_blk)
        sa = jnp.concatenate([sa] * Q_PER_KV, axis=1)
        seladd_ref[g] = sa
        selfar_ref[g] = sa + far_bias(g)

    def sel_rows(ref, g, kt):
        half = SEL_BLOCK
        return jnp.concatenate(
            [jnp.broadcast_to(ref[g, pl.ds(2 * kt, 1), :], (half, lanes)),
             jnp.broadcast_to(ref[g, pl.ds(2 * kt + 1, 1), :], (half, lanes))], axis=0)

    def reset():
        m_ref[...] = jnp.full(m_ref.shape, -jnp.inf, F32)
        l_ref[...] = jnp.zeros(l_ref.shape, F32)
        acc_ref[...] = jnp.zeros(acc_ref.shape, F32)

    def tile(k_ref, v_ref, kt, bias_of):
        r0 = pl.multiple_of(kt * TILE, TILE)
        k = k_ref[pl.ds(r0, TILE), :].astype(BF16)
        vt = v_ref[pl.ds(r0, TILE), :].T.astype(BF16)
        scores = [_dot_nt(k[:, hs(g)], qs_ref[g]) + bias_of(g) for g in groups]
        probs, alphas = [], []
        for g in groups:
            m = m_ref[g]
            m_new = jnp.maximum(m, jnp.max(scores[g], axis=0, keepdims=True))
            alpha = jnp.exp2(m - m_new)
            p = jnp.exp2(scores[g] - m_new)
            m_ref[g] = m_new
            l_ref[g] = alpha * l_ref[g] + jnp.sum(p, axis=0, keepdims=True)
            probs.append(p.astype(BF16))
            alphas.append(alpha)
        pvs = [_dot(vt[hs(g), :], probs[g]) for g in groups]
        for g in groups:
            acc_ref[g] = alphas[g] * acc_ref[g] + pvs[g]

    def finish(branch, g):
        return gate_row(branch, g) * (acc_ref[g] / jnp.maximum(l_ref[g], TINY))

    reset()

    def far_body(kt, carry):
        tile(sk_ref, sv_ref, kt, lambda g: sel_rows(selfar_ref, g, kt))
        return carry

    lax.fori_loop(0, jnp.maximum(qb - 1, 0), far_body, 0)
    kt1 = jnp.maximum(qb - 1, 0)
    off1 = jnp.where(qb >= 1, 0.0, NEG)
    tile(sk_ref, sv_ref, kt1, lambda g: near_ref[g, 1] + (sel_rows(seladd_ref, g, kt1) + off1))
    tile(sk_ref, sv_ref, qb, lambda g: near_ref[g, 0] + sel_rows(seladd_ref, g, qb))
    for g in groups:
        out_ref[g] = out_ref[g] + finish(1, g)

    reset()
    for delta in range(WINDOW // TILE, -1, -1):
        kt = qb - delta
        off = jnp.where(kt >= 0, 0.0, NEG)
        if delta == WINDOW // TILE:
            bias_of = lambda g, off=off: edge_ref[...] + (far_bias(g) + off)
        elif delta >= 2:
            bias_of = lambda g, off=off: far_bias(g) + off
        else:
            bias_of = lambda g, off=off, delta=delta: near_ref[g, delta] + off
        tile(wk_ref, wv_ref, jnp.maximum(kt, 0), bias_of)
    outs = []
    for g in groups:
        out_t = out_ref[g] + finish(2, g)
        outs += [out_t[:, r * TILE:(r + 1) * TILE].T for r in range(Q_PER_KV)]
    o_ref[...] = jnp.concatenate(outs, axis=1).astype(BF16)


def _nsa_prompt(p3, kc, vc, cmpb, near, edge, tab, impt):
    b, t, _ = p3.shape
    n_qb = t // TILE
    lanes = Q_PER_KV * TILE
    kvcol = lambda slot: (P_KV + slot * KV_WIDTH) // KV_WIDTH
    seq_spec = lambda slot: pl.BlockSpec((None, t, KV_WIDTH), lambda i, j: (i, 0, kvcol(slot)))
    const = lambda shape: pl.BlockSpec(shape, lambda i, j: (0,) * len(shape))
    return pl.pallas_call(
        _nsa_prompt_body,
        grid=(b, n_qb),
        in_specs=[pl.BlockSpec((None, TILE, ATTN_WIDTH), lambda i, j: (i, j, 0)),
                  pl.BlockSpec((None, TILE, LANE), lambda i, j: (i, j, P_GD // LANE)),
                  pl.BlockSpec((None, TILE, KV_WIDTH), lambda i, j: (i, 0, 0)),
                  pl.BlockSpec((None, TILE, KV_WIDTH), lambda i, j: (i, 0, 0)),
                  seq_spec(2), seq_spec(3), seq_spec(4), seq_spec(5),
                  pl.BlockSpec((None, KV_HEADS, TILE, lanes), lambda i, j: (j, 0, 0, 0)),
                  const(near.shape), const(edge.shape), const(tab.shape), const(impt.shape)],
        out_specs=pl.BlockSpec((None, TILE, ATTN_WIDTH), lambda i, j: (i, j, 0)),
        out_shape=jax.ShapeDtypeStruct((b, t, ATTN_WIDTH), BF16),
        scratch_shapes=[pltpu.VMEM((KV_HEADS, lanes, HEAD_DIM), BF16),
                        pltpu.VMEM((KV_HEADS, impt.shape[0], lanes), F32),
                        pltpu.VMEM((KV_HEADS, impt.shape[0], lanes), F32),
                        pltpu.VMEM((KV_HEADS, 1, lanes), F32),
                        pltpu.VMEM((KV_HEADS, 1, lanes), F32),
                        pltpu.VMEM((KV_HEADS, HEAD_DIM, lanes), F32),
                        pltpu.VMEM((KV_HEADS, HEAD_DIM, lanes), F32)],
        compiler_params=_cparams("parallel", "arbitrary"),
        name="nsa_prompt",
    )(p3, p3, kc, vc, p3, p3, p3, p3, cmpb, near, edge, tab, impt)


def _nsa_sample_body(*refs, past_len, n_new):
    n_pages = past_len // TILE
    pt_ref, q_ref, gd_ref, newkv_ref, newwin_ref, kc_ref, vc_ref = refs[:7]
    pages = refs[7:7 + n_pages]
    (win_ref, cmpbt_ref, selb_ref, winb_ref, impt_ref, rsum_ref, gsel_ref, expand_ref, place_ref,
     o_ref, winout_ref, s_scr) = refs[7 + n_pages:]
    del pt_ref
    win_len = win_ref.shape[1]
    rows_per_g = Q_PER_KV * n_new
    row_g = lax.broadcasted_iota(jnp.int32, (LANE, 1), 0) // rows_per_g

    q = q_ref[...] * SCALE
    qs = jnp.concatenate([q[:, h * HEAD_DIM:(h + 1) * HEAD_DIM] for h in range(N_HEADS)], axis=0)
    qs4 = jnp.concatenate([qs] * KV_HEADS, axis=1)
    col_g = lax.broadcasted_iota(jnp.int32, qs4.shape, 1) // HEAD_DIM
    qbd = jnp.where(row_g == col_g, qs4, 0.0).astype(BF16)

    sig = jax.nn.sigmoid(gd_ref[...])
    t_row = lax.broadcasted_iota(jnp.int32, (n_new, LANE), 0)
    t_lane = lax.broadcasted_iota(jnp.int32, (n_new, LANE), 1) % n_new
    lane_g = lax.broadcasted_iota(jnp.int32, (1, LANE), 1) // rows_per_g
    ones_rows = jnp.ones((2 * SUBLANE, TILE), BF16)

    def gate_row(branch):
        spread = _dot_exact_rhs(sig, gsel_ref[branch])
        return jnp.sum(jnp.where(t_row == t_lane, spread, 0.0), axis=0, keepdims=True)

    def pick(full_t):
        out = jnp.zeros((HEAD_DIM, LANE), F32)
        for g in range(KV_HEADS):
            out = out + jnp.where(lane_g == g, full_t[g * HEAD_DIM:(g + 1) * HEAD_DIM, :], 0.0)
        return out

    kc = kc_ref[...].astype(BF16)
    s_t = _dot_nt(kc, qbd) + cmpbt_ref[...]
    m_t = jnp.max(s_t, axis=0, keepdims=True)
    e_t = jnp.exp(s_t - m_t)
    p_t = e_t / jnp.maximum(jnp.sum(e_t, axis=0, keepdims=True), TINY)
    out_t = gate_row(0) * pick(_dot(vc_ref[...].T.astype(BF16), p_t.astype(BF16)))
    imp = _dot_exact_lhs(impt_ref[...], _dot_exact_rhs(p_t, rsum_ref[...]))
    blk = lax.broadcasted_iota(jnp.int32, imp.shape, 0)
    q_blk = (past_len + lax.broadcasted_iota(jnp.int32, imp.shape, 1) % n_new) // SEL_BLOCK
    chosen_t = jnp.where(_select_blocks(imp, blk, q_blk) == 0.0, 1.0, 0.0)
    chosen = jnp.concatenate([chosen_t, jnp.zeros((LANE - chosen_t.shape[0], LANE), F32)], axis=0).T
    sel_mask = (_dot(chosen.astype(BF16), expand_ref[...]) - 1.0) * (-NEG)

    def attend(tiles, bias_ref, mask):
        m = jnp.full((LANE, 1), -jnp.inf, F32)
        for j, (score, _) in enumerate(tiles):
            s = score() + bias_ref[:, j * TILE:(j + 1) * TILE]
            if mask is not None:
                s = s + mask[:, j * TILE:(j + 1) * TILE]
            s_scr[:, j * TILE:(j + 1) * TILE] = s
            m = jnp.maximum(m, jnp.max(s, axis=1, keepdims=True))
        acc = jnp.zeros((KV_WIDTH + 2 * SUBLANE, LANE), F32)
        for j, (_, vt) in enumerate(tiles):
            p = jnp.exp(s_scr[:, j * TILE:(j + 1) * TILE] - m).astype(BF16)
            acc = acc + _dot_nt(jnp.concatenate([vt(), ones_rows], axis=0), p)
        return pick(acc) / jnp.maximum(acc[KV_WIDTH:KV_WIDTH + 1, :], TINY)

    def stored_tile(ref, lanes):
        return (lambda: _dot(qbd, ref[0:KV_WIDTH, lanes].astype(BF16)),
                lambda: ref[KV_WIDTH:2 * KV_WIDTH, lanes].astype(BF16))

    def new_tile(ref):
        rows = jnp.concatenate([ref[...], jnp.zeros((TILE - n_new, ref.shape[1]), F32)], axis=0)
        k = rows[:, 0:KV_WIDTH].astype(BF16)
        vt = rows[:, KV_WIDTH:2 * KV_WIDTH].T.astype(BF16)
        return rows, (lambda: _dot_nt(qbd, k), lambda: vt)

    _, new_kv = new_tile(newkv_ref)
    tiles = [stored_tile(pg, slice(None)) for pg in pages] + [new_kv]
    out_t = out_t + gate_row(1) * attend(tiles, selb_ref, sel_mask)

    new_rows, new_win = new_tile(newwin_ref)
    n_wt = win_len // TILE
    tiles = [stored_tile(win_ref, slice(j * TILE, (j + 1) * TILE)) for j in range(n_wt)] + [new_win]
    out_t = out_t + gate_row(2) * attend(tiles, winb_ref, None)
    out = out_t.T
    o_ref[...] = jnp.concatenate([out[h * n_new:(h + 1) * n_new, :] for h in range(N_HEADS)], axis=1).astype(BF16)

    lane = lax.broadcasted_iota(jnp.int32, (1, LANE), 1)
    hi, mid, lo = _split3(new_rows)
    tail = _dot_tn(hi, place_ref[...]) + _dot_tn(mid, place_ref[...]) + _dot_tn(lo, place_ref[...])
    rolled = [pltpu.roll(win_ref[:, j * TILE:(j + 1) * TILE], TILE - n_new, 1) for j in range(n_wt)]
    for j in range(n_wt):
        nxt = rolled[j + 1] if j + 1 < n_wt else tail
        winout_ref[:, j * TILE:(j + 1) * TILE] = jnp.where(lane < TILE - n_new, rolled[j], nxt)


def _nsa_sample(p3, kc, vc, cache_t, win_t, page_table, cmpbt, selb, winb, impt, rsum, gsel, expand, place):
    b, n_new, _ = p3.shape
    n_pages = page_table.shape[1]
    past_len = n_pages * PAGE_SIZE
    win_len = win_t.shape[2]
    kvblk = lambda slot: (P_KV + slot * KV_WIDTH) // (2 * KV_WIDTH)
    const = lambda arr: pl.BlockSpec(arr.shape, lambda i, pt: (0,) * arr.ndim)
    in_specs = [pl.BlockSpec((None, n_new, ATTN_WIDTH), lambda i, pt: (i, 0, 0)),
                pl.BlockSpec((None, n_new, LANE), lambda i, pt: (i, 0, P_GD // LANE)),
                pl.BlockSpec((None, n_new, 2 * KV_WIDTH), lambda i, pt: (i, 0, kvblk(2))),
                pl.BlockSpec((None, n_new, 2 * KV_WIDTH), lambda i, pt: (i, 0, kvblk(4))),
                pl.BlockSpec((None, TILE, KV_WIDTH), lambda i, pt: (i, 0, 0)),
                pl.BlockSpec((None, TILE, KV_WIDTH), lambda i, pt: (i, 0, 0))]
    in_specs += [pl.BlockSpec((None, 2 * KV_WIDTH, PAGE_SIZE), functools.partial(lambda j, i, pt: (pt[i, j], 1, 0), j))
                 for j in range(n_pages)]
    consts = [cmpbt, selb, winb, impt, rsum, gsel, expand, place]
    in_specs += [pl.BlockSpec((None, 2 * KV_WIDTH, win_len), lambda i, pt: (i, 0, 0))] + [const(a) for a in consts]
    return pl.pallas_call(
        functools.partial(_nsa_sample_body, past_len=past_len, n_new=n_new),
        grid_spec=pltpu.PrefetchScalarGridSpec(
            num_scalar_prefetch=1, grid=(b,), in_specs=in_specs,
            out_specs=[pl.BlockSpec((None, n_new, ATTN_WIDTH), lambda i, pt: (i, 0, 0)),
                       pl.BlockSpec((None, 2 * KV_WIDTH, win_len), lambda i, pt: (i, 0, 0))],
            scratch_shapes=[pltpu.VMEM((LANE, past_len + TILE), F32)]),
        out_shape=[jax.ShapeDtypeStruct((b, n_new, ATTN_WIDTH), BF16),
                   jax.ShapeDtypeStruct((b, 2 * KV_WIDTH, win_len), F32)],
        compiler_params=_cparams("parallel"),
        name="nsa_sample",
    )(page_table, p3, p3, p3, p3, kc, vc, *([cache_t] * n_pages), win_t, *consts)


def _sample_selectors(n_new, n_keys):
    lanes = np.arange(LANE)
    h, t = lanes // n_new, lanes % n_new
    rsum = ((h[:, None] // Q_PER_KV == h[None, :] // Q_PER_KV) & (t[:, None] == t[None, :])).astype(np.float32)
    gsel = np.stack([(lanes[:, None] == br * N_HEADS + h[None, :]).astype(np.float32) for br in range(3)])
    expand = (np.arange(n_keys)[None, :] // SEL_BLOCK == lanes[:, None]).astype(np.float32)
    place = (lanes[None, :] == LANE - n_new + lanes[:, None]).astype(np.float32)
    return tuple(jnp.asarray(a, BF16) for a in (rsum, gsel, expand, place))


def _dot_tn(a, b):
    return lax.dot_general(a, b, (((0,), (0,)), ((), ())), preferred_element_type=F32)


def _ssd_body(*refs, q_len, carry):
    if carry:
        (z_ref, xbc_ref, gd_ref, cw_ref, cb_ref, dtb_ref, alog_ref, dexp_ref, ng_ref, e16_ref,
         y_ref, hout_ref, prev_ref, h_scr) = refs
    else:
        (z_ref, xbc_ref, gd_ref, prev_ref, h0_ref, cw_ref, cb_ref, dtb_ref, alog_ref, dexp_ref, ng_ref, e16_ref,
         y_ref, hout_ref) = refs
    n_seq = TILE // q_len
    x = xbc_ref[...]
    if carry:
        @pl.when(pl.program_id(1) == 0)
        def _():
            prev_ref[...] = jnp.zeros_like(prev_ref)
            h_scr[...] = jnp.zeros_like(h_scr)
    prev = prev_ref[...]
    acc = cb_ref[...] + cw_ref[SSM_CONV - 1:SSM_CONV, :] * x
    for k in range(1, SSM_CONV):
        acc = acc + cw_ref[SSM_CONV - 1 - k:SSM_CONV - k, :] * _shift_rows(x, prev, k, q_len)
    if carry:
        prev_ref[...] = x
    xc = _silu(acc)
    xs = xc[:, 0:SSM_INNER]
    b_all = xc[:, SSM_INNER:SSM_INNER + SSM_GROUPS * SSM_STATE]
    c_all = xc[:, SSM_INNER + SSM_GROUPS * SSM_STATE:]

    col = lax.broadcasted_iota(jnp.int32, (1, LANE), 1)
    dt_cols = (col >= DT_COL) & (col < DT_COL + SSM_HEADS)
    dt = jnp.where(dt_cols, jax.nn.softplus(gd_ref[...] + dtb_ref[...]), 0.0)
    a = dt * jnp.where(dt_cols, -jnp.exp(alog_ref[...]), 0.0)
    ri = lax.broadcasted_iota(jnp.int32, (TILE, TILE), 0)
    ci = lax.broadcasted_iota(jnp.int32, (TILE, TILE), 1)
    same = (ri // q_len) == (ci // q_len)
    causal = same & (ci <= ri)
    acum = _dot_exact_lhs(causal.astype(BF16), a)
    last_sel = (ci == (ri // q_len) * q_len + (q_len - 1)).astype(BF16)
    alast = _dot_exact_lhs(last_sel, acum)
    acum_t = acum.T
    e16 = e16_ref[...]
    xw = xs * _dot_exact_rhs(dt, e16)
    xwe = xw * _dot_exact_rhs(jnp.exp(alast - acum), e16)
    grow = jnp.exp(_dot_exact_rhs(acum, e16))
    rowseq = lax.broadcasted_iota(jnp.int32, (TILE, 1), 0) // q_len

    ys = []
    for g in range(SSM_GROUPS):
        cg = c_all[:, g * SSM_STATE:(g + 1) * SSM_STATE].astype(BF16)
        bg = b_all[:, g * SSM_STATE:(g + 1) * SSM_STATE].astype(BF16)
        cb = _dot_nt(cg, bg)
        gl = slice(g * SSM_HPG * SSM_HEADDIM, (g + 1) * SSM_HPG * SSM_HEADDIM)
        hl = slice(g * SSM_HPG, (g + 1) * SSM_HPG)
        for r in range(SSM_HPG):
            h = g * SSM_HPG + r
            c = DT_COL + h
            seg = acum[:, c:c + 1] - acum_t[c:c + 1, :]
            decay = jnp.where(causal, jnp.exp(jnp.where(causal, seg, 0.0)), 0.0)
            ys.append(_dot((cb * decay).astype(BF16), xw[:, h * SSM_HEADDIM:(h + 1) * SSM_HEADDIM].astype(BF16)))
        xwe_g = xwe[:, gl]
        if carry:
            h_old = h_scr[hl].reshape(SSM_HPG * SSM_HEADDIM, SSM_STATE)
            y_off = _dot_nt(cg, h_old.astype(BF16))
            scale = jnp.concatenate(
                [jnp.broadcast_to(jnp.exp(alast[0:1, DT_COL + g * SSM_HPG + r:DT_COL + g * SSM_HPG + r + 1]),
                                  (SSM_HEADDIM, SSM_STATE)) for r in range(SSM_HPG)], axis=0)
            h_new = h_old * scale + _dot_tn(xwe_g.astype(BF16), bg)
            h_scr[hl] = h_new.reshape(SSM_HPG, SSM_HEADDIM, SSM_STATE)
        else:
            offs = []
            for s in range(n_seq):
                h_old = h0_ref[s, hl].reshape(SSM_HPG * SSM_HEADDIM, SSM_STATE)
                offs.append(_dot_nt(cg, h_old.astype(BF16))[s * q_len:(s + 1) * q_len, :])
                a_row = alast[s * q_len:s * q_len + 1, :]
                scale = jnp.concatenate(
                    [jnp.broadcast_to(jnp.exp(a_row[:, DT_COL + g * SSM_HPG + r:DT_COL + g * SSM_HPG + r + 1]),
                                      (SSM_HEADDIM, SSM_STATE)) for r in range(SSM_HPG)], axis=0)
                x_s = jnp.where(rowseq == s, xwe_g, 0.0).astype(BF16)
                h_new = h_old * scale + _dot_tn(x_s, bg)
                hout_ref[s, hl] = h_new.reshape(SSM_HPG, SSM_HEADDIM, SSM_STATE)
            y_off = jnp.concatenate(offs, axis=0)
        for r in range(SSM_HPG):
            h = g * SSM_HPG + r
            ys[h] = ys[h] + y_off[:, r * SSM_HEADDIM:(r + 1) * SSM_HEADDIM] * grow[:, h * SSM_HEADDIM:(h + 1) * SSM_HEADDIM]
    if carry:
        @pl.when(pl.program_id(1) == pl.num_programs(1) - 1)
        def _():
            hout_ref[...] = h_scr[...]

    y = jnp.concatenate(ys, axis=1) + xs * dexp_ref[...]
    y = y * _silu(z_ref[...])
    gw = SSM_INNER // SSM_GROUPS
    parts = []
    for g in range(SSM_GROUPS):
        blk = y[:, g * gw:(g + 1) * gw]
        parts.append(blk * lax.rsqrt(jnp.mean(blk * blk, axis=-1, keepdims=True) + EPS))
    y_ref[...] = (jnp.concatenate(parts, axis=1) * ng_ref[...]).astype(BF16)


def _ssd(p3, prev, h0, cw, cb, dtb, alog, dexp, ng, e16, *, q_len):
    n_grp, rows, _ = p3.shape
    carry = prev is None
    n_chunks = rows // TILE
    n_seq = TILE // q_len
    zspec = pl.BlockSpec((None, TILE, SSM_INNER), lambda i, c: (i, c, P_Z // SSM_INNER))
    xspec = pl.BlockSpec((None, TILE, CONV_DIM), lambda i, c: (i, c, P_XBC // CONV_DIM))
    gspec = pl.BlockSpec((None, TILE, LANE), lambda i, c: (i, c, P_GD // LANE))
    const = lambda arr: pl.BlockSpec(arr.shape, lambda i, c: (0,) * arr.ndim)
    params = [cw, cb, dtb, alog, dexp, ng, e16]
    in_specs = [zspec, xspec, gspec]
    args = [p3, p3, p3]
    if carry:
        n_out_seq = n_grp
        hspec = pl.BlockSpec((None, SSM_HEADS, SSM_HEADDIM, SSM_STATE), lambda i, c: (i, 0, 0, 0))
        scratch = [pltpu.VMEM((TILE, CONV_DIM), F32), pltpu.VMEM((SSM_HEADS, SSM_HEADDIM, SSM_STATE), F32)]
    else:
        n_out_seq = n_grp * n_chunks * n_seq
        hspec = pl.BlockSpec((n_seq, SSM_HEADS, SSM_HEADDIM, SSM_STATE), lambda i, c: (i * n_chunks + c, 0, 0, 0))
        in_specs += [pl.BlockSpec((None, TILE, CONV_DIM), lambda i, c: (i, c, 0)), hspec]
        args += [prev, h0]
        scratch = []
    in_specs += [const(a) for a in params]
    return pl.pallas_call(
        functools.partial(_ssd_body, q_len=q_len, carry=carry),
        grid=(n_grp, n_chunks),
        in_specs=in_specs,
        out_specs=[pl.BlockSpec((None, TILE, SSM_INNER), lambda i, c: (i, c, 0)), hspec],
        out_shape=[jax.ShapeDtypeStruct((n_grp, rows, SSM_INNER), BF16),
                   jax.ShapeDtypeStruct((n_out_seq, SSM_HEADS, SSM_HEADDIM, SSM_STATE), F32)],
        scratch_shapes=scratch,
        compiler_params=_cparams("parallel", "arbitrary"),
        name="ssd",
    )(*args, *params)


def _importance_map_t(n_sel):
    n_cmp = TILE - 1
    c0 = np.arange(n_cmp) * CMP_STRIDE
    s0 = np.arange(n_sel) * SEL_BLOCK
    ov = np.minimum(c0[None, :] + CMP_BLOCK, s0[:, None] + SEL_BLOCK) - np.maximum(c0[None, :], s0[:, None])
    m = np.zeros((-(-n_sel // SUBLANE) * SUBLANE, TILE), np.float32)
    m[:n_sel, :n_cmp] = np.clip(ov, 0, None).astype(np.float32) / CMP_BLOCK
    return jnp.asarray(m, BF16)


def _dt_expand():
    e = np.zeros((LANE, SSM_INNER), np.float32)
    for h in range(SSM_HEADS):
        e[DT_COL + h, h * SSM_HEADDIM:(h + 1) * SSM_HEADDIM] = 1.0
    return jnp.asarray(e, BF16)


def _pad_to(x, axis, size):
    pad = [(0, 0)] * x.ndim
    pad[axis] = (0, size - x.shape[axis])
    return jnp.pad(x, pad)


def _history_rows(state, n_rows, width):
    b, k, _ = state.shape
    tile = jnp.pad(state, ((0, 0), (n_rows - k, 0), (0, width - state.shape[2])))
    return tile.reshape(b * n_rows, width)


def kernel(x_prompt, x_sample, cache_kv, cache_win_kv, state_ssm_conv, state_ssm, state_ffn_conv, page_table, rel_table, ln_mix_pre, w_in, cmp_pe, cmp_w1, cmp_b1, cmp_w2, cmp_b2, ssm_conv_w, ssm_conv_b, ssm_dt_bias, ssm_a_log, ssm_d, ssm_norm_g, w_out, ln_mix_post, ln_ffn_pre, ffn_w_gate, ffn_w_up, ffn_dw_w, ffn_dw_b, ffn_w_down, ln_ffn_post):
    bp, tp, d = x_prompt.shape
    bs, ts, _ = x_sample.shape
    n_pool = cache_kv.shape[1]
    n_pages = page_table.shape[1]
    past_len = n_pages * PAGE_SIZE
    win_len = cache_win_kv.shape[2]
    assert d == D_MODEL and tp == N_CHUNK_TILES * TILE and past_len == N_CHUNK_TILES * TILE
    assert N_HEADS * ts == LANE and ts == SUBLANE and tp >= WINDOW and win_len == WINDOW

    rel = rel_table.astype(F32)
    tab_p = jnp.repeat(rel.reshape(REL_BUCKETS, KV_HEADS, Q_PER_KV).transpose(1, 0, 2), TILE, axis=-1)
    tab_s = jnp.repeat(rel, ts, axis=1)
    cmpb_p, near, edge = _bias_prompt(tab_p, tp // TILE)
    cmpbt_s, selb, winb = _bias_sample(tab_s, past_len, ts, win_len)
    impt_p = _importance_map_t(-(-tp // SEL_BLOCK))
    impt_s = _importance_map_t(-(-(past_len + ts) // SEL_BLOCK))
    rsum, gsel, expand, place = _sample_selectors(ts, past_len + TILE)
    e16 = _dt_expand()
    dummy_pt = jnp.zeros((1, 1), jnp.int32)

    xp = x_prompt.reshape(bp * tp, d)
    xs = x_sample.reshape(bs * ts, d)
    outs = [[] for _ in range(10)]
    for l in range(w_in.shape[0]):
        o1 = ATTN_WIDTH
        o2 = o1 + N_KV_SLOTS * KV_WIDTH
        o3 = o2 + 3 * N_HEADS
        o4 = o3 + SSM_INNER
        o5 = o4 + CONV_DIM
        w = w_in[l].T
        wp = jnp.concatenate([w[:o1], w[o3:o4], w[o4:o5], w[o1:o2], w[o2:o3], w[o5:],
                              jnp.zeros((P_WIDTH - w.shape[0], d), w.dtype)], axis=0).astype(BF16)
        g_pre = ln_mix_pre[l].reshape(1, d)
        cw = _compress_weights(cmp_pe[l], cmp_w1[l], cmp_b1[l], cmp_w2[l], cmp_b2[l])
        ssm_w = (_pad_to(ssm_conv_w[l], 0, SUBLANE), ssm_conv_b[l].reshape(1, CONV_DIM),
                 _pad_to(jnp.pad(ssm_dt_bias[l].astype(F32), (DT_COL, 0)), 0, LANE).reshape(1, LANE),
                 _pad_to(jnp.pad(ssm_a_log[l].astype(F32), (DT_COL, 0)), 0, LANE).reshape(1, LANE),
                 jnp.repeat(ssm_d[l].astype(F32), SSM_HEADDIM).reshape(1, SSM_INNER),
                 ssm_norm_g[l].astype(F32).reshape(1, SSM_INNER), e16)
        w_attn = w_out[l, :ATTN_WIDTH].astype(BF16)
        w_ssm = w_out[l, ATTN_WIDTH:].astype(BF16)
        g_post = ln_mix_post[l].reshape(1, d)
        g_pre2 = ln_ffn_pre[l].reshape(1, d)
        wg = _pad_to(ffn_w_gate[l], 1, FF_PAD).astype(BF16)
        wu = _pad_to(ffn_w_up[l], 1, FF_PAD).astype(BF16)
        dw = _pad_to(_pad_to(ffn_dw_w[l], 1, FF_PAD), 0, SUBLANE)
        db = _pad_to(ffn_dw_b[l].reshape(1, D_FF), 1, FF_PAD)
        wd = _pad_to(ffn_w_down[l], 0, FF_PAD).astype(BF16)
        g_post2 = ln_ffn_post[l].reshape(1, d)

        p3 = _proj(xp, g_pre, wp, tm=1024).reshape(bp, tp, P_WIDTH)
        kc, vc = _compress(p3, lambda j, cb, b, pt: (b, j, P_KV // LANE + cb), bp, dummy_pt, *cw)
        attn = _nsa_prompt(p3, kc, vc, cmpb_p, near, edge, tab_p, impt_p)
        ssm, h_new = _ssd(p3, None, None, *ssm_w, q_len=TILE)
        x1, hn = _outproj(attn.reshape(bp * tp, ATTN_WIDTH), ssm.reshape(bp * tp, SSM_INNER), xp,
                          w_attn, w_ssm, g_post, g_pre2)
        act, tail = _ffn_up(hn, wg, wu, dw, db, None, seq_len=tp, tm=tp)
        xp = _ffn_down(act, wd, x1, g_post2)
        kv = p3[:, :, P_KV:P_KV + N_KV_SLOTS * KV_WIDTH]
        outs[0].append(kv[:, :, :N_CACHE_SLOTS * KV_WIDTH].reshape(bp, tp, N_CACHE_SLOTS, KV_HEADS, HEAD_DIM))
        outs[2].append(kv[:, tp - WINDOW:, N_CACHE_SLOTS * KV_WIDTH:].reshape(bp, WINDOW, 2, KV_HEADS, HEAD_DIM))
        outs[4].append(p3[:, tp - (SSM_CONV - 1):, P_XBC:P_XBC + CONV_DIM])
        outs[6].append(h_new)
        outs[8].append(tail[:, SUBLANE - (FFN_CONV - 1):, :D_FF])

        p3 = _proj(xs, g_pre, wp, tm=bs * ts).reshape(bs, ts, P_WIDTH)
        cache_t = cache_kv[l].transpose(0, 2, 3, 4, 1).reshape(n_pool, N_CACHE_SLOTS * KV_WIDTH, PAGE_SIZE)
        win_t = cache_win_kv[l].transpose(0, 2, 3, 4, 1).reshape(bs, 2 * KV_WIDTH, win_len)
        w1p, perm = _compress_paged_weights(cmp_w1[l])
        kc, vc = _compress_paged(cache_t, page_table, perm, w1p, _pe_bias(cw[0], w1p), cw[2], cw[3], cw[4])
        attn, win_new = _nsa_sample(p3, kc, vc, cache_t, win_t, page_table, cmpbt_s, selb, winb,
                                    impt_s, rsum, gsel, expand, place)
        win_new = win_new.reshape(bs, 2, KV_HEADS, HEAD_DIM, win_len).transpose(0, 4, 1, 2, 3)
        prev = _history_rows(state_ssm_conv[l], ts, CONV_DIM).reshape(1, bs * ts, CONV_DIM)
        ssm, h_new = _ssd(p3.reshape(1, bs * ts, P_WIDTH), prev, state_ssm[l], *ssm_w, q_len=ts)
        x1, hn = _outproj(attn.reshape(bs * ts, ATTN_WIDTH), ssm.reshape(bs * ts, SSM_INNER), xs,
                          w_attn, w_ssm, g_post, g_pre2)
        prev_f = _history_rows(state_ffn_conv[l], ts, FF_PAD)
        act, gp = _ffn_up(hn, wg, wu, dw, db, prev_f, seq_len=ts, tm=bs * ts)
        xs = _ffn_down(act, wd, x1, g_post2, tm=bs * ts)
        kv = p3[:, :, P_KV:P_KV + N_KV_SLOTS * KV_WIDTH]
        outs[1].append(kv[:, :, :N_CACHE_SLOTS * KV_WIDTH].reshape(bs, ts, N_CACHE_SLOTS, KV_HEADS, HEAD_DIM))
        outs[3].append(win_new)
        outs[5].append(p3[:, ts - (SSM_CONV - 1):, P_XBC:P_XBC + CONV_DIM])
        outs[7].append(h_new)
        outs[9].append(gp.reshape(bs, ts, FF_PAD)[:, ts - (FFN_CONV - 1):, :D_FF])
    return (xp.reshape(bp, tp, d), xs.reshape(bs, ts, d)) + tuple(jnp.stack(o) for o in outs)
```

```python
import functools
import math

import numpy as np
import jax
import jax.numpy as jnp
from jax import lax
from jax.experimental import pallas as pl
from jax.experimental.pallas import tpu as pltpu

F32 = jnp.float32
BF16 = jnp.bfloat16

D_MODEL = 2048
PAGE_SIZE = 128
HEAD_DIM = 64
ATTN_WIDTH = D_MODEL // 2
N_HEADS = ATTN_WIDTH // HEAD_DIM
KV_HEADS = 4
Q_PER_KV = N_HEADS // KV_HEADS
CMP_STRIDE = 16
CMP_BLOCK = 2 * CMP_STRIDE
CMP_HIDDEN = 2 * HEAD_DIM
SEL_BLOCK = 64
N_SELECT = 8
WINDOW = 512
N_KV_SLOTS = 6
N_CACHE_SLOTS = 4
KV_WIDTH = KV_HEADS * HEAD_DIM
SSM_INNER = D_MODEL - ATTN_WIDTH
SSM_HEADDIM = 64
SSM_HEADS = SSM_INNER // SSM_HEADDIM
SSM_GROUPS = 4
SSM_HPG = SSM_HEADS // SSM_GROUPS
SSM_STATE = 128
SSM_CONV = 4
CONV_DIM = SSM_INNER + 2 * SSM_GROUPS * SSM_STATE
D_FF = ((8 * D_MODEL // 3 + 127) // 128) * 128
FFN_CONV = 3
REL_BUCKETS = 32
REL_MAX_DIST = 128
EPS = 1e-6
NEG = -1e30
TINY = 1e-30
FORCE_BONUS = 1e4
SCALE = HEAD_DIM ** -0.5
LOG2E = math.log2(math.e)

LANE = 128
SUBLANE = 8
TILE = 128
VMEM_LIMIT = 56 * 1024 * 1024

P_Q = 0
P_Z = ATTN_WIDTH
P_XBC = P_Z + SSM_INNER
P_KV = P_XBC + CONV_DIM
P_GD = P_KV + N_KV_SLOTS * KV_WIDTH
GD_WIDTH = 3 * N_HEADS + SSM_HEADS
P_WIDTH = 6144
DT_COL = 3 * N_HEADS
FF_PAD = 5632
FFN_SUB = 256


def _cparams(*sem):
    return pltpu.CompilerParams(dimension_semantics=sem, vmem_limit_bytes=VMEM_LIMIT)


def _rms(x, g):
    return x * lax.rsqrt(jnp.mean(x * x, axis=-1, keepdims=True) + EPS) * g


def _split3(x):
    hi = x.astype(BF16)
    r1 = x - hi.astype(F32)
    mid = r1.astype(BF16)
    lo = (r1 - mid.astype(F32)).astype(BF16)
    return hi, mid, lo


def _dot(a, b):
    return jnp.dot(a, b, preferred_element_type=F32)


def _dot_nt(a, b):
    return lax.dot_general(a, b, (((1,), (1,)), ((), ())), preferred_element_type=F32)


def _dot_exact_rhs(x, sel):
    hi, mid, lo = _split3(x)
    return _dot(hi, sel) + _dot(mid, sel) + _dot(lo, sel)


def _dot_exact_lhs(sel, x):
    hi, mid, lo = _split3(x)
    return _dot(sel, hi) + _dot(sel, mid) + _dot(sel, lo)


def _gelu_tanh(x):
    return 0.5 * x * (1.0 + jnp.tanh(math.sqrt(2.0 / math.pi) * (x + 0.044715 * (x * x * x))))


def _silu(x):
    return x * jax.nn.sigmoid(x)


CAST_ROWS = 256


def _cast_pad_cols_body(w_ref, o_ref):
    n = w_ref.shape[1]
    o_ref[:, 0:n] = w_ref[...].astype(BF16)
    o_ref[:, n:] = jnp.zeros((o_ref.shape[0], o_ref.shape[1] - n), BF16)


def _cast_pad_cols(w, n_out):
    m, n = w.shape
    return pl.pallas_call(
        _cast_pad_cols_body,
        grid=(m // CAST_ROWS,),
        in_specs=[pl.BlockSpec((CAST_ROWS, n), lambda i: (i, 0))],
        out_specs=pl.BlockSpec((CAST_ROWS, n_out), lambda i: (i, 0)),
        out_shape=jax.ShapeDtypeStruct((m, n_out), BF16),
        compiler_params=_cparams("parallel"),
        name="cast_pad_cols",
    )(w)


def _cast_pad_rows_body(w_ref, o_ref, *, n_blocks):
    live = pl.program_id(0) < n_blocks
    o_ref[...] = jnp.where(live, w_ref[...], 0.0).astype(BF16)


def _cast_pad_rows(w, m_out):
    m, n = w.shape
    n_blocks = m // TILE
    return pl.pallas_call(
        functools.partial(_cast_pad_rows_body, n_blocks=n_blocks),
        grid=(m_out // TILE,),
        in_specs=[pl.BlockSpec((TILE, n), lambda i: (jnp.minimum(i, n_blocks - 1), 0))],
        out_specs=pl.BlockSpec((TILE, n), lambda i: (i, 0)),
        out_shape=jax.ShapeDtypeStruct((m_out, n), BF16),
        compiler_params=_cparams("parallel"),
        name="cast_pad_rows",
    )(w)


def _proj_body(x_ref, g_ref, w_ref, o_ref, xn_ref):
    @pl.when(pl.program_id(1) == 0)
    def _():
        xn_ref[...] = _rms(x_ref[...], g_ref[...]).astype(BF16)

    o_ref[...] = _dot_nt(xn_ref[...], w_ref[...])


def _proj(x2d, g, w_t, tm, tn=1024):
    m, k = x2d.shape
    n = w_t.shape[0]
    w = w_t
    return pl.pallas_call(
        _proj_body,
        grid=(m // tm, n // tn),
        in_specs=[pl.BlockSpec((tm, k), lambda i, j: (i, 0)),
                  pl.BlockSpec((1, k), lambda i, j: (0, 0)),
                  pl.BlockSpec((tn, k), lambda i, j: (j, 0))],
        out_specs=pl.BlockSpec((tm, tn), lambda i, j: (i, j)),
        out_shape=jax.ShapeDtypeStruct((m, n), F32),
        scratch_shapes=[pltpu.VMEM((tm, k), BF16)],
        compiler_params=_cparams("parallel", "arbitrary"),
        name="proj",
    )(x2d, g, w)


def _outproj_body(a_ref, s_ref, x_ref, wa_ref, ws_ref, g1_ref, g2_ref, x1_ref, hn_ref):
    mix = _dot(a_ref[...], wa_ref[...]) + _dot(s_ref[...], ws_ref[...])
    x1 = x_ref[...] + _rms(mix, g1_ref[...])
    x1_ref[...] = x1
    hn_ref[...] = _rms(x1, g2_ref[...]).astype(BF16)


def _outproj(attn, ssm, x2d, w_attn, w_ssm, g_post, g_pre2, tm=512):
    m, d = x2d.shape
    ka = attn.shape[1]
    const = lambda i: (0, 0)
    return pl.pallas_call(
        _outproj_body,
        grid=(m // tm,),
        in_specs=[pl.BlockSpec((tm, ka), lambda i: (i, 0)),
                  pl.BlockSpec((tm, ka), lambda i: (i, 0)),
                  pl.BlockSpec((tm, d), lambda i: (i, 0)),
                  pl.BlockSpec((ka, d), const),
                  pl.BlockSpec((ka, d), const),
                  pl.BlockSpec((1, d), const),
                  pl.BlockSpec((1, d), const)],
        out_specs=[pl.BlockSpec((tm, d), lambda i: (i, 0)),
                   pl.BlockSpec((tm, d), lambda i: (i, 0))],
        out_shape=[jax.ShapeDtypeStruct((m, d), F32), jax.ShapeDtypeStruct((m, d), BF16)],
        compiler_params=_cparams("parallel"),
        name="outproj",
    )(attn, ssm, x2d, w_attn, w_ssm, g_post, g_pre2)


def _shift_rows(x, prev, k, seq_len):
    rows, cols = x.shape
    if seq_len >= rows:
        t = lax.broadcasted_iota(jnp.int32, (rows, 1), 0)
        return jnp.where(t >= k, pltpu.roll(x, k, 0), 0.0 if prev is None else pltpu.roll(prev, k, 0))
    assert seq_len == SUBLANE and prev is not None
    shape = (rows // seq_len, seq_len, cols)
    t = lax.broadcasted_iota(jnp.int32, shape, 1)
    out = jnp.where(t >= k, pltpu.roll(x.reshape(shape), k, 1), pltpu.roll(prev.reshape(shape), k, 1))
    return out.reshape(rows, cols)


def _ffn_up_body(*refs, seq_len, has_prev):
    if has_prev:
        h_ref, wg_ref, wu_ref, dw_ref, db_ref, prev_ref, act_ref, tail_ref = refs
    else:
        h_ref, wg_ref, wu_ref, dw_ref, db_ref, act_ref, tail_ref = refs
    tm, tn = act_ref.shape
    h = h_ref[...]

    def project(cs):
        return _dot(h, wg_ref[:, cs]), _dot(h, wu_ref[:, cs])

    def gate(cs, gp, up):
        prev = prev_ref[:, cs] if has_prev else None
        acc = db_ref[:, cs] + dw_ref[FFN_CONV - 1:FFN_CONV, cs] * gp
        for k in range(1, FFN_CONV):
            acc = acc + dw_ref[FFN_CONV - 1 - k:FFN_CONV - k, cs] * _shift_rows(gp, prev, k, seq_len)
        act_ref[:, cs] = (_gelu_tanh(acc) * up).astype(BF16)
        tail_ref[:, cs] = gp if has_prev else gp[tm - SUBLANE:, :]

    pending = None
    for c in range(tn // FFN_SUB):
        cs = slice(c * FFN_SUB, (c + 1) * FFN_SUB)
        cur = (cs,) + project(cs)
        if pending is not None:
            gate(*pending)
        pending = cur
    gate(*pending)


def _ffn_up(hn, wg, wu, dw, db, prev, *, seq_len, tm, tn=512):
    m, d = hn.shape
    ni = m // tm
    nj = FF_PAD // tn
    has_prev = prev is not None
    in_specs = [pl.BlockSpec((tm, d), lambda i, j: (i, 0)),
                pl.BlockSpec((d, tn), lambda i, j: (0, j)),
                pl.BlockSpec((d, tn), lambda i, j: (0, j)),
                pl.BlockSpec((SUBLANE, tn), lambda i, j: (0, j)),
                pl.BlockSpec((1, tn), lambda i, j: (0, j))]
    args = [hn, wg, wu, dw, db]
    if has_prev:
        in_specs.append(pl.BlockSpec((tm, tn), lambda i, j: (i, j)))
        args.append(prev)
        tail_shape = jax.ShapeDtypeStruct((m, FF_PAD), F32)
        tail_spec = pl.BlockSpec((tm, tn), lambda i, j: (i, j))
    else:
        tail_shape = jax.ShapeDtypeStruct((ni, SUBLANE, FF_PAD), F32)
        tail_spec = pl.BlockSpec((None, SUBLANE, tn), lambda i, j: (i, 0, j))
    return pl.pallas_call(
        functools.partial(_ffn_up_body, seq_len=seq_len, has_prev=has_prev),
        grid=(ni, nj),
        in_specs=in_specs,
        out_specs=[pl.BlockSpec((tm, tn), lambda i, j: (i, j)), tail_spec],
        out_shape=[jax.ShapeDtypeStruct((m, FF_PAD), BF16), tail_shape],
        compiler_params=_cparams("parallel", "parallel"),
        name="ffn_up",
    )(*args)


def _ffn_down_body(a_ref, w_ref, x_ref, g_ref, o_ref, acc_ref):
    k = pl.program_id(1)

    @pl.when(k == 0)
    def _():
        acc_ref[...] = jnp.zeros_like(acc_ref)

    acc_ref[...] += _dot(a_ref[...], w_ref[...])

    @pl.when(k == pl.num_programs(1) - 1)
    def _():
        o_ref[...] = x_ref[...] + _rms(acc_ref[...], g_ref[...])


def _ffn_down(act, wd, x1, g, tm=1024, tk=512):
    m, d = x1.shape
    return pl.pallas_call(
        _ffn_down_body,
        grid=(m // tm, FF_PAD // tk),
        in_specs=[pl.BlockSpec((tm, tk), lambda i, k: (i, k)),
                  pl.BlockSpec((tk, d), lambda i, k: (k, 0)),
                  pl.BlockSpec((tm, d), lambda i, k: (i, 0)),
                  pl.BlockSpec((1, d), lambda i, k: (0, 0))],
        out_specs=pl.BlockSpec((tm, d), lambda i, k: (i, 0)),
        out_shape=jax.ShapeDtypeStruct((m, d), F32),
        scratch_shapes=[pltpu.VMEM((tm, d), F32)],
        compiler_params=_cparams("parallel", "arbitrary"),
        name="ffn_down",
    )(act, wd, x1, g)


N_CHUNK_TILES = 16
CHUNKS_PER_TILE = TILE // CMP_STRIDE
CMP_COL_BLOCKS = 2 * KV_WIDTH // LANE


def _compress_body(*refs):
    pt_ref = refs[0]
    del pt_ref
    n_in = N_CHUNK_TILES * CMP_COL_BLOCKS
    tiles = refs[1:1 + n_in]
    pe_ref, w1_ref, b1_ref, w2_ref, b2_ref, kc_ref, vc_ref, lhs_a, lhs_b = refs[1 + n_in:]
    out_refs = (kc_ref, vc_ref)
    for slot in range(2):
        outs = []
        for gp in range(KV_HEADS // 2):
            cb = slot * (KV_HEADS // 2) + gp
            for s in range(CMP_STRIDE):
                x = jnp.concatenate(
                    [tiles[j * CMP_COL_BLOCKS + cb][pl.ds(s, CHUNKS_PER_TILE, stride=CMP_STRIDE), :]
                     for j in range(N_CHUNK_TILES)], axis=0)
                lhs_a[:, s * LANE:(s + 1) * LANE] = (x + pe_ref[slot, 0, s:s + 1, :]).astype(BF16)
                lhs_b[:, s * LANE:(s + 1) * LANE] = (x + pe_ref[slot, 1, s:s + 1, :]).astype(BF16)
            first = _dot(lhs_a[...], w1_ref[slot, 0])
            second = _dot(lhs_b[...], w1_ref[slot, 1])
            hid = first + pltpu.roll(second, TILE - 1, 0) + b1_ref[slot]
            act = _gelu_tanh(hid).astype(BF16)
            for gj in range(2):
                outs.append(_dot(act[:, gj * CMP_HIDDEN:(gj + 1) * CMP_HIDDEN], w2_ref[slot]) + b2_ref[slot])
        out_refs[slot][...] = jnp.concatenate(outs, axis=1)


def _compress(src, tile_map, n_seq, page_table, pe, w1, b1, w2, b2):
    in_specs = [pl.BlockSpec((None, TILE, LANE), functools.partial(tile_map, j, cb))
                for j in range(N_CHUNK_TILES) for cb in range(CMP_COL_BLOCKS)]
    full = lambda shape: pl.BlockSpec(shape, lambda b, pt: (0,) * len(shape))
    in_specs += [full(pe.shape), full(w1.shape), full(b1.shape), full(w2.shape), full(b2.shape)]
    out_spec = pl.BlockSpec((None, TILE, KV_WIDTH), lambda b, pt: (b, 0, 0))
    return pl.pallas_call(
        _compress_body,
        grid_spec=pltpu.PrefetchScalarGridSpec(
            num_scalar_prefetch=1, grid=(n_seq,), in_specs=in_specs, out_specs=[out_spec, out_spec],
            scratch_shapes=[pltpu.VMEM((TILE, CMP_STRIDE * LANE), BF16), pltpu.VMEM((TILE, CMP_STRIDE * LANE), BF16)]),
        out_shape=[jax.ShapeDtypeStruct((n_seq, TILE, KV_WIDTH), F32)] * 2,
        compiler_params=_cparams("parallel"),
        name="compress",
    )(page_table, *([src] * (N_CHUNK_TILES * CMP_COL_BLOCKS)), pe, w1, b1, w2, b2)


def _compress_weights(cmp_pe, cmp_w1, cmp_b1, cmp_w2, cmp_b2):
    pe = cmp_pe.reshape(2, 2, CMP_STRIDE, HEAD_DIM)
    pe = jnp.concatenate([pe, pe], axis=-1)
    w = cmp_w1.reshape(2, 2, CMP_STRIDE, HEAD_DIM, CMP_HIDDEN)
    eye = jnp.eye(2, dtype=w.dtype)
    w1 = jnp.einsum('absdh,ij->absidjh', w, eye).reshape(2, 2, CMP_STRIDE * LANE, 2 * CMP_HIDDEN).astype(BF16)
    b1 = jnp.concatenate([cmp_b1, cmp_b1], axis=-1).reshape(2, 1, 2 * CMP_HIDDEN)
    return pe, w1, b1, cmp_w2.astype(BF16), cmp_b2.reshape(2, 1, HEAD_DIM)


PAGES_PER_PERM = 2


def _pe_bias_body(pe_ref, w1_ref, o_ref):
    for slot in range(2):
        cols = []
        for half in range(2):
            acc = jnp.zeros((2 * SUBLANE, 2 * CMP_HIDDEN), F32)
            for s in range(CMP_STRIDE):
                row = jnp.broadcast_to(pe_ref[slot, half, s:s + 1, :], (2 * SUBLANE, LANE))
                w = w1_ref[slot, s * LANE:(s + 1) * LANE, half * 2 * CMP_HIDDEN:(half + 1) * 2 * CMP_HIDDEN]
                acc = acc + _dot_exact_rhs(row, w)
            cols.append(acc)
        o_ref[slot] = jnp.concatenate(cols, axis=1)


def _pe_bias(pe, w1):
    return pl.pallas_call(
        _pe_bias_body,
        out_shape=jax.ShapeDtypeStruct((2, 2 * SUBLANE, 4 * CMP_HIDDEN), F32),
        name="pe_bias",
    )(pe, w1)


def _compress_paged_body(*refs):
    n_in = N_CHUNK_TILES
    pt_ref = refs[0]
    del pt_ref
    pages = refs[1:1 + n_in]
    perm_ref, w1_ref, peb_ref, b1_ref, w2_ref, b2_ref, kc_ref, vc_ref, xp_ref, hid_ref = refs[1 + n_in:]
    n_perm = N_CHUNK_TILES // PAGES_PER_PERM
    slab = PAGES_PER_PERM * CHUNKS_PER_TILE
    for jp in range(n_perm):
        xt = jnp.concatenate([pages[PAGES_PER_PERM * jp + jj][...] for jj in range(PAGES_PER_PERM)], axis=1)
        xp_ref[jp] = _dot_nt(perm_ref[...], xt.astype(BF16)).astype(BF16)
    out_refs = (kc_ref, vc_ref)

    def first_layer(slot, gp):
        cb = slot * (KV_HEADS // 2) + gp
        lhs = jnp.concatenate(
            [jnp.concatenate([xp_ref[jp, s * slab:(s + 1) * slab, cb * LANE:(cb + 1) * LANE]
                              for jp in range(n_perm)], axis=0)
             for s in range(CMP_STRIDE)], axis=1)
        hid_ref[gp] = _dot(lhs, w1_ref[slot])
        return gp

    def second_layer(slot, gp, buf):
        acc = hid_ref[buf] + peb_ref[slot, 0:1, :]
        hid = (acc[:, 0:2 * CMP_HIDDEN] + pltpu.roll(acc[:, 2 * CMP_HIDDEN:], TILE - 1, 0) + b1_ref[slot])
        act = _gelu_tanh(hid).astype(BF16)
        for gj in range(2):
            g = 2 * gp + gj
            out_refs[slot][:, g * HEAD_DIM:(g + 1) * HEAD_DIM] = (
                _dot(act[:, gj * CMP_HIDDEN:(gj + 1) * CMP_HIDDEN], w2_ref[slot]) + b2_ref[slot])

    pending = None
    for slot in range(2):
        for gp in range(KV_HEADS // 2):
            acc = first_layer(slot, gp)
            if pending is not None:
                second_layer(*pending)
            pending = (slot, gp, acc)
    second_layer(*pending)


def _compress_paged(cache_t, page_table, perm, w1, peb, b1, w2, b2):
    n_seq = page_table.shape[0]
    in_specs = [pl.BlockSpec((None, 2 * KV_WIDTH, PAGE_SIZE), functools.partial(lambda j, b, pt: (pt[b, j], 0, 0), j))
                for j in range(N_CHUNK_TILES)]
    const = lambda arr: pl.BlockSpec(arr.shape, lambda b, pt: (0,) * arr.ndim)
    consts = [perm, w1, peb, b1, w2, b2]
    in_specs += [const(a) for a in consts]
    out_spec = pl.BlockSpec((None, TILE, KV_WIDTH), lambda b, pt: (b, 0, 0))
    rows = PAGES_PER_PERM * PAGE_SIZE
    return pl.pallas_call(
        _compress_paged_body,
        grid_spec=pltpu.PrefetchScalarGridSpec(
            num_scalar_prefetch=1, grid=(n_seq,), in_specs=in_specs, out_specs=[out_spec, out_spec],
            scratch_shapes=[pltpu.VMEM((N_CHUNK_TILES // PAGES_PER_PERM, rows, 2 * KV_WIDTH), BF16),
                            pltpu.VMEM((KV_HEADS // 2, TILE, 4 * CMP_HIDDEN), F32)]),
        out_shape=[jax.ShapeDtypeStruct((n_seq, TILE, KV_WIDTH), F32)] * 2,
        compiler_params=_cparams("parallel"),
        name="compress_paged",
    )(page_table, *([cache_t] * N_CHUNK_TILES), *consts)


def _compress_paged_weights(cmp_w1):
    w = cmp_w1.reshape(2, 2, CMP_STRIDE, HEAD_DIM, CMP_HIDDEN)
    eye = jnp.eye(2, dtype=w.dtype)
    w1 = jnp.einsum('absdh,ij->asidbjh', w, eye).reshape(2, CMP_STRIDE * LANE, 4 * CMP_HIDDEN).astype(BF16)
    rows = PAGES_PER_PERM * PAGE_SIZE
    r = np.arange(rows)
    s, page, chunk = r // (PAGES_PER_PERM * CHUNKS_PER_TILE), (r // CHUNKS_PER_TILE) % PAGES_PER_PERM, r % CHUNKS_PER_TILE
    src = page * PAGE_SIZE + chunk * CMP_STRIDE + s
    perm = (np.arange(rows)[None, :] == src[:, None]).astype(np.float32)
    return w1, jnp.asarray(perm, BF16)


def _bucket_thresholds():
    n = np.arange(0, 2 * REL_MAX_DIST)
    max_exact = REL_BUCKETS // 2
    nf = np.maximum(n, 1).astype(np.float32)
    large = max_exact + (np.log(nf / np.float32(max_exact)) / np.float32(math.log(REL_MAX_DIST / max_exact))
                         * np.float32(REL_BUCKETS - max_exact)).astype(np.int32)
    bucket = np.where(n < max_exact, n, np.minimum(large, REL_BUCKETS - 1))
    assert (np.diff(bucket) >= 0).all() and (np.diff(bucket) <= 1).all()
    return [int(np.argmax(bucket >= k)) for k in range(1, REL_BUCKETS)]


BUCKET_START = _bucket_thresholds()
CMP_BAND = 2 * SUBLANE
assert (BUCKET_START[-1] - 1 + CMP_BLOCK - 1) // CMP_STRIDE <= SUBLANE
assert (TILE - CMP_BLOCK) // CMP_STRIDE < SUBLANE


def _rel_bias(dist, tab_ref):
    bias = jnp.broadcast_to(tab_ref[0:1, :], dist.shape)
    for k in range(1, REL_BUCKETS):
        bias = jnp.where(dist >= BUCKET_START[k - 1], tab_ref[k:k + 1, :], bias)
    return bias


def _bias_prompt_body(tab_ref, cmpb_ref, near_ref, edge_ref):
    qb = pl.program_id(0)
    shape = (TILE, Q_PER_KV * TILE)
    row = lax.broadcasted_iota(jnp.int32, shape, 0)
    qi = lax.broadcasted_iota(jnp.int32, shape, 1) % TILE
    dist_c = qb * TILE + qi - (row * CMP_STRIDE + CMP_BLOCK - 1)
    band0 = pl.multiple_of(jnp.maximum(qb * (TILE // CMP_STRIDE) - SUBLANE, 0), SUBLANE)
    band_shape = (CMP_BAND, Q_PER_KV * TILE)
    band_row = band0 + lax.broadcasted_iota(jnp.int32, band_shape, 0)
    band_qi = lax.broadcasted_iota(jnp.int32, band_shape, 1) % TILE
    dist_b = qb * TILE + band_qi - (band_row * CMP_STRIDE + CMP_BLOCK - 1)
    for g in range(KV_HEADS):
        cmpb_ref[g] = jnp.where(dist_c >= 0, tab_ref[g, REL_BUCKETS - 1:REL_BUCKETS, :], NEG)
        cmpb_ref[g, pl.ds(band0, CMP_BAND), :] = jnp.where(dist_b >= 0, _rel_bias(dist_b, tab_ref.at[g]), NEG)

    @pl.when(qb == 0)
    def _():
        for g in range(KV_HEADS):
            for delta in range(2):
                dist = delta * TILE + qi - row
                b = _rel_bias(dist, tab_ref.at[g])
                near_ref[g, delta] = jnp.where(dist >= 0, b * LOG2E, NEG)
        edge_ref[...] = jnp.where(qi <= row, 0.0, NEG)


def _bias_prompt(tab, n_qb):
    lanes = Q_PER_KV * TILE
    return pl.pallas_call(
        _bias_prompt_body,
        grid=(n_qb,),
        in_specs=[pl.BlockSpec(tab.shape, lambda i: (0, 0, 0))],
        out_specs=[pl.BlockSpec((None, KV_HEADS, TILE, lanes), lambda i: (i, 0, 0, 0)),
                   pl.BlockSpec((KV_HEADS, 2, TILE, lanes), lambda i: (0, 0, 0, 0)),
                   pl.BlockSpec((TILE, lanes), lambda i: (0, 0))],
        out_shape=[jax.ShapeDtypeStruct((n_qb, KV_HEADS, TILE, lanes), F32),
                   jax.ShapeDtypeStruct((KV_HEADS, 2, TILE, lanes), F32),
                   jax.ShapeDtypeStruct((TILE, lanes), F32)],
        compiler_params=_cparams("arbitrary"),
        name="bias_prompt",
    )(tab)


def _bias_sample_body(tab_ref, cmpbt_ref, selb_ref, winb_ref, *, past_len, n_new, win_len):
    qt = lax.broadcasted_iota(jnp.int32, (TILE, LANE), 1) % n_new
    row = lax.broadcasted_iota(jnp.int32, (TILE, LANE), 0)
    dist = past_len + qt - (row * CMP_STRIDE + CMP_BLOCK - 1)
    cmpbt_ref[...] = jnp.where(dist >= 0, _rel_bias(dist, tab_ref), NEG)
    for j in range(selb_ref.shape[1] // TILE):
        key = row + j * TILE
        dist = past_len + qt - key
        ok = (dist >= 0) & (key < past_len + n_new)
        selb_ref[:, j * TILE:(j + 1) * TILE] = jnp.where(ok, _rel_bias(dist, tab_ref), NEG).T
    for j in range(winb_ref.shape[1] // TILE):
        key = row + j * TILE
        dist = win_len + qt - key
        ok = (dist >= 0) & (dist <= WINDOW) & (key < win_len + n_new)
        winb_ref[:, j * TILE:(j + 1) * TILE] = jnp.where(ok, _rel_bias(dist, tab_ref), NEG).T


def _bias_sample(tab, past_len, n_new, win_len):
    return pl.pallas_call(
        functools.partial(_bias_sample_body, past_len=past_len, n_new=n_new, win_len=win_len),
        out_shape=[jax.ShapeDtypeStruct((TILE, LANE), F32),
                   jax.ShapeDtypeStruct((LANE, past_len + TILE), F32),
                   jax.ShapeDtypeStruct((LANE, win_len + TILE), F32)],
        name="bias_sample",
    )(tab)


def _select_blocks(imp, blk, q_blk):
    valid = blk <= q_blk
    forced = (blk == 0) | (blk == q_blk) | (blk == q_blk - 1)
    score = jnp.where(valid, imp + jnp.where(forced, FORCE_BONUS, 0.0), -jnp.inf)
    rank = jnp.zeros(score.shape, jnp.int32)
    for i in range(score.shape[0]):
        row = score[i:i + 1, :]
        beats = (row > score) | ((row == score) & (blk > i))
        rank = rank + beats.astype(jnp.int32)
    return jnp.where(valid & (rank < N_SELECT), 0.0, NEG)


def _nsa_prompt_body(q_ref, gd_ref, kc_ref, vc_ref, sk_ref, sv_ref, wk_ref, wv_ref, cmpb_ref, near_ref,
                     edge_ref, tab_ref, impt_ref, o_ref, qs_ref, seladd_ref, selfar_ref, m_ref, l_ref, acc_ref,
                     out_ref):
    qb = pl.program_id(1)
    lanes = Q_PER_KV * TILE
    q = q_ref[...] * SCALE
    gate_t = jax.nn.sigmoid(gd_ref[...]).T
    n_sel = seladd_ref.shape[1]
    blk = lax.broadcasted_iota(jnp.int32, (n_sel, TILE), 0)
    q_blk = (qb * TILE + lax.broadcasted_iota(jnp.int32, (n_sel, TILE), 1)) // SEL_BLOCK
    groups = range(KV_HEADS)

    def hs(g):
        return slice(g * HEAD_DIM, (g + 1) * HEAD_DIM)

    def gate_row(branch, g):
        c0 = branch * N_HEADS + g * Q_PER_KV
        return jnp.concatenate([gate_t[c0 + r:c0 + r + 1, :] for r in range(Q_PER_KV)], axis=1)

    def far_bias(g):
        return tab_ref[g, REL_BUCKETS - 1:REL_BUCKETS, :] * LOG2E

    def stack_heads(x, g):
        return jnp.concatenate(
            [x[:, (g * Q_PER_KV + r) * HEAD_DIM:(g * Q_PER_KV + r + 1) * HEAD_DIM] for r in range(Q_PER_KV)],
            axis=0).astype(BF16)

    q2 = q * LOG2E
    for g in groups:
        qs = stack_heads(q, g)
        qs_ref[g] = stack_heads(q2, g)
        s = _dot_nt(kc_ref[:, hs(g)].astype(BF16), qs) + cmpb_ref[g]
        m = jnp.max(s, axis=0, keepdims=True)
        e = jnp.exp(s - m)
        norm = jnp.where(m > 0.5 * NEG, 1.0 / jnp.maximum(jnp.sum(e, axis=0, keepdims=True), TINY), 0.0)
        p = e * norm
        out_ref[g] = gate_row(0, g) * _dot(vc_ref[:, hs(g)].T.astype(BF16), p.astype(BF16))
        psum = p[:, 0:TILE]
        for r in range(1, Q_PER_KV):
            psum = psum + p[:, r * TILE:(r + 1) * TILE]
        sa = _select_blocks(_dot_exact_lhs(impt_ref[...], psum), blk, q_blk)
        sa = jnp.concatenate([sa] * Q_PER_KV, axis=1)
        seladd_ref[g] = sa
        selfar_ref[g] = sa + far_bias(g)

    def sel_rows(ref, g, kt):
        half = SEL_BLOCK
        return jnp.concatenate(
            [jnp.broadcast_to(ref[g, pl.ds(2 * kt, 1), :], (half, lanes)),
             jnp.broadcast_to(ref[g, pl.ds(2 * kt + 1, 1), :], (half, lanes))], axis=0)

    def reset():
        m_ref[...] = jnp.full(m_ref.shape, -jnp.inf, F32)
        l_ref[...] = jnp.zeros(l_ref.shape, F32)
        acc_ref[...] = jnp.zeros(acc_ref.shape, F32)

    def tiles(k_ref, v_ref, parts):
        starts = [pl.multiple_of(kt * TILE, TILE) for kt, _ in parts]
        k = jnp.concatenate([k_ref[pl.ds(r0, TILE), :] for r0 in starts], axis=0).astype(BF16)
        vt = jnp.concatenate([v_ref[pl.ds(r0, TILE), :] for r0 in starts], axis=0).T.astype(BF16)
        scores = []
        for g in groups:
            s = _dot_nt(k[:, hs(g)], qs_ref[g])
            scores.append([s[i * TILE:(i + 1) * TILE] + bias_of(g) for i, (_, bias_of) in enumerate(parts)])
        probs, alphas = [], []
        for g in groups:
            m = m_ref[g]
            m_new = m
            for s in scores[g]:
                m_new = jnp.maximum(m_new, jnp.max(s, axis=0, keepdims=True))
            alpha = jnp.exp2(m - m_new)
            ps = [jnp.exp2(s - m_new) for s in scores[g]]
            total = jnp.sum(ps[0], axis=0, keepdims=True)
            for p in ps[1:]:
                total = total + jnp.sum(p, axis=0, keepdims=True)
            m_ref[g] = m_new
            l_ref[g] = alpha * l_ref[g] + total
            probs.append(jnp.concatenate([p.astype(BF16) for p in ps], axis=0))
            alphas.append(alpha)
        pvs = [_dot(vt[hs(g), :], probs[g]) for g in groups]
        for g in groups:
            acc_ref[g] = alphas[g] * acc_ref[g] + pvs[g]

    def finish(branch, g):
        return gate_row(branch, g) * (acc_ref[g] / jnp.maximum(l_ref[g], TINY))

    reset()

    n_far = jnp.maximum(qb - 1, 0)

    def far_body(i, carry):
        hi = n_far - 1 - 2 * i
        lo = jnp.maximum(hi - 1, 0)
        lo_off = jnp.where(hi >= 1, 0.0, NEG)
        tiles(sk_ref, sv_ref, [(lo, lambda g: sel_rows(selfar_ref, g, lo) + lo_off),
                               (hi, lambda g: sel_rows(selfar_ref, g, hi))])
        return carry

    lax.fori_loop(0, (n_far + 1) // 2, far_body, 0)
    kt1 = jnp.maximum(qb - 1, 0)
    off1 = jnp.where(qb >= 1, 0.0, NEG)
    tiles(sk_ref, sv_ref, [(kt1, lambda g: near_ref[g, 1] + (sel_rows(seladd_ref, g, kt1) + off1)),
                           (qb, lambda g: near_ref[g, 0] + sel_rows(seladd_ref, g, qb))])
    for g in groups:
        out_ref[g] = out_ref[g] + finish(1, g)

    reset()
    parts = []
    for delta in range(WINDOW // TILE, -1, -1):
        kt = qb - delta
        off = jnp.where(kt >= 0, 0.0, NEG)
        if delta == WINDOW // TILE:
            bias_of = lambda g, off=off: edge_ref[...] + (far_bias(g) + off)
        elif delta >= 2:
            bias_of = lambda g, off=off: far_bias(g) + off
        else:
            bias_of = lambda g, off=off, delta=delta: near_ref[g, delta] + off
        parts.append((jnp.maximum(kt, 0), bias_of))
    tiles(wk_ref, wv_ref, parts[:3])
    tiles(wk_ref, wv_ref, parts[3:])
    outs = []
    for g in groups:
        out_t = out_ref[g] + finish(2, g)
        outs += [out_t[:, r * TILE:(r + 1) * TILE].T for r in range(Q_PER_KV)]
    o_ref[...] = jnp.concatenate(outs, axis=1).astype(BF16)


def _nsa_prompt(p3, kc, vc, cmpb, near, edge, tab, impt):
    b, t, _ = p3.shape
    n_qb = t // TILE
    lanes = Q_PER_KV * TILE
    kvcol = lambda slot: (P_KV + slot * KV_WIDTH) // KV_WIDTH
    seq_spec = lambda slot: pl.BlockSpec((None, t, KV_WIDTH), lambda i, j: (i, 0, kvcol(slot)))
    const = lambda shape: pl.BlockSpec(shape, lambda i, j: (0,) * len(shape))
    return pl.pallas_call(
        _nsa_prompt_body,
        grid=(b, n_qb),
        in_specs=[pl.BlockSpec((None, TILE, ATTN_WIDTH), lambda i, j: (i, j, 0)),
                  pl.BlockSpec((None, TILE, LANE), lambda i, j: (i, j, P_GD // LANE)),
                  pl.BlockSpec((None, TILE, KV_WIDTH), lambda i, j: (i, 0, 0)),
                  pl.BlockSpec((None, TILE, KV_WIDTH), lambda i, j: (i, 0, 0)),
                  seq_spec(2), seq_spec(3), seq_spec(4), seq_spec(5),
                  pl.BlockSpec((None, KV_HEADS, TILE, lanes), lambda i, j: (j, 0, 0, 0)),
                  const(near.shape), const(edge.shape), const(tab.shape), const(impt.shape)],
        out_specs=pl.BlockSpec((None, TILE, ATTN_WIDTH), lambda i, j: (i, j, 0)),
        out_shape=jax.ShapeDtypeStruct((b, t, ATTN_WIDTH), BF16),
        scratch_shapes=[pltpu.VMEM((KV_HEADS, lanes, HEAD_DIM), BF16),
                        pltpu.VMEM((KV_HEADS, impt.shape[0], lanes), F32),
                        pltpu.VMEM((KV_HEADS, impt.shape[0], lanes), F32),
                        pltpu.VMEM((KV_HEADS, 1, lanes), F32),
                        pltpu.VMEM((KV_HEADS, 1, lanes), F32),
                        pltpu.VMEM((KV_HEADS, HEAD_DIM, lanes), F32),
                        pltpu.VMEM((KV_HEADS, HEAD_DIM, lanes), F32)],
        compiler_params=_cparams("parallel", "arbitrary"),
        name="nsa_prompt",
    )(p3, p3, kc, vc, p3, p3, p3, p3, cmpb, near, edge, tab, impt)


def _nsa_sample_body(*refs, past_len, n_new):
    n_pages = past_len // TILE
    pt_ref, q_ref, gd_ref, newkv_ref, newwin_ref, kc_ref, vc_ref = refs[:7]
    pages = refs[7:7 + n_pages]
    (win_ref, cmpbt_ref, selb_ref, winb_ref, impt_ref, rsum_ref, gsel_ref, expand_ref, place_ref,
     o_ref, winout_ref, s_scr) = refs[7 + n_pages:]
    del pt_ref
    win_len = win_ref.shape[1]
    rows_per_g = Q_PER_KV * n_new
    row_g = lax.broadcasted_iota(jnp.int32, (LANE, 1), 0) // rows_per_g

    q = q_ref[...] * SCALE
    qs = jnp.concatenate([q[:, h * HEAD_DIM:(h + 1) * HEAD_DIM] for h in range(N_HEADS)], axis=0)
    qs4 = jnp.concatenate([qs] * KV_HEADS, axis=1)
    col_g = lax.broadcasted_iota(jnp.int32, qs4.shape, 1) // HEAD_DIM
    qbd = jnp.where(row_g == col_g, qs4, 0.0).astype(BF16)

    sig = jax.nn.sigmoid(gd_ref[...])
    t_row = lax.broadcasted_iota(jnp.int32, (n_new, LANE), 0)
    t_lane = lax.broadcasted_iota(jnp.int32, (n_new, LANE), 1) % n_new
    lane_g = lax.broadcasted_iota(jnp.int32, (1, LANE), 1) // rows_per_g
    ones_rows = jnp.ones((2 * SUBLANE, TILE), BF16)

    def gate_row(branch):
        spread = _dot_exact_rhs(sig, gsel_ref[branch])
        return jnp.sum(jnp.where(t_row == t_lane, spread, 0.0), axis=0, keepdims=True)

    def pick(full_t):
        out = jnp.zeros((HEAD_DIM, LANE), F32)
        for g in range(KV_HEADS):
            out = out + jnp.where(lane_g == g, full_t[g * HEAD_DIM:(g + 1) * HEAD_DIM, :], 0.0)
        return out

    kc = kc_ref[...].astype(BF16)
    s_t = _dot_nt(kc, qbd) + cmpbt_ref[...]
    m_t = jnp.max(s_t, axis=0, keepdims=True)
    e_t = jnp.exp(s_t - m_t)
    p_t = e_t / jnp.maximum(jnp.sum(e_t, axis=0, keepdims=True), TINY)
    out_t = gate_row(0) * pick(_dot(vc_ref[...].T.astype(BF16), p_t.astype(BF16)))
    imp = _dot_exact_lhs(impt_ref[...], _dot_exact_rhs(p_t, rsum_ref[...]))
    blk = lax.broadcasted_iota(jnp.int32, imp.shape, 0)
    q_blk = (past_len + lax.broadcasted_iota(jnp.int32, imp.shape, 1) % n_new) // SEL_BLOCK
    chosen_t = jnp.where(_select_blocks(imp, blk, q_blk) == 0.0, 1.0, 0.0)
    chosen = jnp.concatenate([chosen_t, jnp.zeros((LANE - chosen_t.shape[0], LANE), F32)], axis=0).T
    sel_mask = (_dot(chosen.astype(BF16), expand_ref[...]) - 1.0) * (-NEG)

    def attend(tiles, bias_ref, mask):
        m = jnp.full((LANE, 1), -jnp.inf, F32)
        for j, (score, _) in enumerate(tiles):
            s = score() + bias_ref[:, j * TILE:(j + 1) * TILE]
            if mask is not None:
                s = s + mask[:, j * TILE:(j + 1) * TILE]
            s_scr[:, j * TILE:(j + 1) * TILE] = s
            m = jnp.maximum(m, jnp.max(s, axis=1, keepdims=True))
        acc = jnp.zeros((KV_WIDTH + 2 * SUBLANE, LANE), F32)
        for j, (_, vt) in enumerate(tiles):
            p = jnp.exp(s_scr[:, j * TILE:(j + 1) * TILE] - m).astype(BF16)
            acc = acc + _dot_nt(jnp.concatenate([vt(), ones_rows], axis=0), p)
        return pick(acc) / jnp.maximum(acc[KV_WIDTH:KV_WIDTH + 1, :], TINY)

    def stored_tile(ref, lanes):
        return (lambda: _dot(qbd, ref[0:KV_WIDTH, lanes].astype(BF16)),
                lambda: ref[KV_WIDTH:2 * KV_WIDTH, lanes].astype(BF16))

    def new_tile(ref):
        rows = jnp.concatenate([ref[...], jnp.zeros((TILE - n_new, ref.shape[1]), F32)], axis=0)
        k = rows[:, 0:KV_WIDTH].astype(BF16)
        vt = rows[:, KV_WIDTH:2 * KV_WIDTH].T.astype(BF16)
        return rows, (lambda: _dot_nt(qbd, k), lambda: vt)

    _, new_kv = new_tile(newkv_ref)
    tiles = [stored_tile(pg, slice(None)) for pg in pages] + [new_kv]
    out_t = out_t + gate_row(1) * attend(tiles, selb_ref, sel_mask)

    new_rows, new_win = new_tile(newwin_ref)
    n_wt = win_len // TILE
    tiles = [stored_tile(win_ref, slice(j * TILE, (j + 1) * TILE)) for j in range(n_wt)] + [new_win]
    out_t = out_t + gate_row(2) * attend(tiles, winb_ref, None)
    out = out_t.T
    o_ref[...] = jnp.concatenate([out[h * n_new:(h + 1) * n_new, :] for h in range(N_HEADS)], axis=1).astype(BF16)

    lane = lax.broadcasted_iota(jnp.int32, (1, LANE), 1)
    hi, mid, lo = _split3(new_rows)
    tail = _dot_tn(hi, place_ref[...]) + _dot_tn(mid, place_ref[...]) + _dot_tn(lo, place_ref[...])
    rolled = [pltpu.roll(win_ref[:, j * TILE:(j + 1) * TILE], TILE - n_new, 1) for j in range(n_wt)]
    for j in range(n_wt):
        nxt = rolled[j + 1] if j + 1 < n_wt else tail
        winout_ref[:, j * TILE:(j + 1) * TILE] = jnp.where(lane < TILE - n_new, rolled[j], nxt)


def _nsa_sample(p3, kc, vc, cache_t, win_t, page_table, cmpbt, selb, winb, impt, rsum, gsel, expand, place):
    b, n_new, _ = p3.shape
    n_pages = page_table.shape[1]
    past_len = n_pages * PAGE_SIZE
    win_len = win_t.shape[2]
    kvblk = lambda slot: (P_KV + slot * KV_WIDTH) // (2 * KV_WIDTH)
    const = lambda arr: pl.BlockSpec(arr.shape, lambda i, pt: (0,) * arr.ndim)
    in_specs = [pl.BlockSpec((None, n_new, ATTN_WIDTH), lambda i, pt: (i, 0, 0)),
                pl.BlockSpec((None, n_new, LANE), lambda i, pt: (i, 0, P_GD // LANE)),
                pl.BlockSpec((None, n_new, 2 * KV_WIDTH), lambda i, pt: (i, 0, kvblk(2))),
                pl.BlockSpec((None, n_new, 2 * KV_WIDTH), lambda i, pt: (i, 0, kvblk(4))),
                pl.BlockSpec((None, TILE, KV_WIDTH), lambda i, pt: (i, 0, 0)),
                pl.BlockSpec((None, TILE, KV_WIDTH), lambda i, pt: (i, 0, 0))]
    in_specs += [pl.BlockSpec((None, 2 * KV_WIDTH, PAGE_SIZE), functools.partial(lambda j, i, pt: (pt[i, j], 1, 0), j))
                 for j in range(n_pages)]
    consts = [cmpbt, selb, winb, impt, rsum, gsel, expand, place]
    in_specs += [pl.BlockSpec((None, 2 * KV_WIDTH, win_len), lambda i, pt: (i, 0, 0))] + [const(a) for a in consts]
    return pl.pallas_call(
        functools.partial(_nsa_sample_body, past_len=past_len, n_new=n_new),
        grid_spec=pltpu.PrefetchScalarGridSpec(
            num_scalar_prefetch=1, grid=(b,), in_specs=in_specs,
            out_specs=[pl.BlockSpec((None, n_new, ATTN_WIDTH), lambda i, pt: (i, 0, 0)),
                       pl.BlockSpec((None, 2 * KV_WIDTH, win_len), lambda i, pt: (i, 0, 0))],
            scratch_shapes=[pltpu.VMEM((LANE, past_len + TILE), F32)]),
        out_shape=[jax.ShapeDtypeStruct((b, n_new, ATTN_WIDTH), BF16),
                   jax.ShapeDtypeStruct((b, 2 * KV_WIDTH, win_len), F32)],
        compiler_params=_cparams("parallel"),
        name="nsa_sample",
    )(page_table, p3, p3, p3, p3, kc, vc, *([cache_t] * n_pages), win_t, *consts)


def _sample_selectors(n_new, n_keys):
    lanes = np.arange(LANE)
    h, t = lanes // n_new, lanes % n_new
    rsum = ((h[:, None] // Q_PER_KV == h[None, :] // Q_PER_KV) & (t[:, None] == t[None, :])).astype(np.float32)
    gsel = np.stack([(lanes[:, None] == br * N_HEADS + h[None, :]).astype(np.float32) for br in range(3)])
    expand = (np.arange(n_keys)[None, :] // SEL_BLOCK == lanes[:, None]).astype(np.float32)
    place = (lanes[None, :] == LANE - n_new + lanes[:, None]).astype(np.float32)
    return tuple(jnp.asarray(a, BF16) for a in (rsum, gsel, expand, place))


def _dot_tn(a, b):
    return lax.dot_general(a, b, (((0,), (0,)), ((), ())), preferred_element_type=F32)


def _ssd_body(*refs, q_len, carry):
    if carry:
        (z_ref, xbc_ref, gd_ref, cw_ref, cb_ref, dtb_ref, alog_ref, dexp_ref, ng_ref, e16_ref,
         y_ref, hout_ref, prev_ref, h_scr) = refs
    else:
        (z_ref, xbc_ref, gd_ref, prev_ref, h0_ref, cw_ref, cb_ref, dtb_ref, alog_ref, dexp_ref, ng_ref, e16_ref,
         y_ref, hout_ref) = refs
    n_seq = TILE // q_len
    x = xbc_ref[...]
    if carry:
        @pl.when(pl.program_id(1) == 0)
        def _():
            prev_ref[...] = jnp.zeros_like(prev_ref)
            h_scr[...] = jnp.zeros_like(h_scr)
    prev = prev_ref[...]
    acc = cb_ref[...] + cw_ref[SSM_CONV - 1:SSM_CONV, :] * x
    for k in range(1, SSM_CONV):
        acc = acc + cw_ref[SSM_CONV - 1 - k:SSM_CONV - k, :] * _shift_rows(x, prev, k, q_len)
    if carry:
        prev_ref[...] = x
    xc = _silu(acc)
    xs = xc[:, 0:SSM_INNER]
    b_all = xc[:, SSM_INNER:SSM_INNER + SSM_GROUPS * SSM_STATE]
    c_all = xc[:, SSM_INNER + SSM_GROUPS * SSM_STATE:]

    col = lax.broadcasted_iota(jnp.int32, (1, LANE), 1)
    dt_cols = (col >= DT_COL) & (col < DT_COL + SSM_HEADS)
    dt = jnp.where(dt_cols, jax.nn.softplus(gd_ref[...] + dtb_ref[...]), 0.0)
    a = dt * jnp.where(dt_cols, -jnp.exp(alog_ref[...]), 0.0)
    ri = lax.broadcasted_iota(jnp.int32, (TILE, TILE), 0)
    ci = lax.broadcasted_iota(jnp.int32, (TILE, TILE), 1)
    same = (ri // q_len) == (ci // q_len)
    causal = same & (ci <= ri)
    acum = _dot_exact_lhs(causal.astype(BF16), a)
    last_sel = (ci == (ri // q_len) * q_len + (q_len - 1)).astype(BF16)
    alast = _dot_exact_lhs(last_sel, acum)
    acum_t = acum.T
    e16 = e16_ref[...]
    xw = xs * _dot_exact_rhs(dt, e16)
    xwe = xw * _dot_exact_rhs(jnp.exp(alast - acum), e16)
    grow = jnp.exp(_dot_exact_rhs(acum, e16))
    rowseq = lax.broadcasted_iota(jnp.int32, (TILE, 1), 0) // q_len

    ys = []
    for g in range(SSM_GROUPS):
        cg = c_all[:, g * SSM_STATE:(g + 1) * SSM_STATE].astype(BF16)
        bg = b_all[:, g * SSM_STATE:(g + 1) * SSM_STATE].astype(BF16)
        cb = _dot_nt(cg, bg)
        gl = slice(g * SSM_HPG * SSM_HEADDIM, (g + 1) * SSM_HPG * SSM_HEADDIM)
        hl = slice(g * SSM_HPG, (g + 1) * SSM_HPG)
        for r in range(SSM_HPG):
            h = g * SSM_HPG + r
            c = DT_COL + h
            seg = acum[:, c:c + 1] - acum_t[c:c + 1, :]
            decay = jnp.where(causal, jnp.exp(jnp.where(causal, seg, 0.0)), 0.0)
            ys.append(_dot((cb * decay).astype(BF16), xw[:, h * SSM_HEADDIM:(h + 1) * SSM_HEADDIM].astype(BF16)))
        xwe_g = xwe[:, gl]
        if carry:
            h_old = h_scr[hl].reshape(SSM_HPG * SSM_HEADDIM, SSM_STATE)
            y_off = _dot_nt(cg, h_old.astype(BF16))
            scale = jnp.concatenate(
                [jnp.broadcast_to(jnp.exp(alast[0:1, DT_COL + g * SSM_HPG + r:DT_COL + g * SSM_HPG + r + 1]),
                                  (SSM_HEADDIM, SSM_STATE)) for r in range(SSM_HPG)], axis=0)
            h_new = h_old * scale + _dot_tn(xwe_g.astype(BF16), bg)
            h_scr[hl] = h_new.reshape(SSM_HPG, SSM_HEADDIM, SSM_STATE)
        else:
            offs = []
            for s in range(n_seq):
                h_old = h0_ref[s, hl].reshape(SSM_HPG * SSM_HEADDIM, SSM_STATE)
                offs.append(_dot_nt(cg, h_old.astype(BF16))[s * q_len:(s + 1) * q_len, :])
                a_row = alast[s * q_len:s * q_len + 1, :]
                scale = jnp.concatenate(
                    [jnp.broadcast_to(jnp.exp(a_row[:, DT_COL + g * SSM_HPG + r:DT_COL + g * SSM_HPG + r + 1]),
                                      (SSM_HEADDIM, SSM_STATE)) for r in range(SSM_HPG)], axis=0)
                x_s = jnp.where(rowseq == s, xwe_g, 0.0).astype(BF16)
                h_new = h_old * scale + _dot_tn(x_s, bg)
                hout_ref[s, hl] = h_new.reshape(SSM_HPG, SSM_HEADDIM, SSM_STATE)
            y_off = jnp.concatenate(offs, axis=0)
        for r in range(SSM_HPG):
            h = g * SSM_HPG + r
            ys[h] = ys[h] + y_off[:, r * SSM_HEADDIM:(r + 1) * SSM_HEADDIM] * grow[:, h * SSM_HEADDIM:(h + 1) * SSM_HEADDIM]
    if carry:
        @pl.when(pl.program_id(1) == pl.num_programs(1) - 1)
        def _():
            hout_ref[...] = h_scr[...]

    y = jnp.concatenate(ys, axis=1) + xs * dexp_ref[...]
    y = y * _silu(z_ref[...])
    gw = SSM_INNER // SSM_GROUPS
    parts = []
    for g in range(SSM_GROUPS):
        blk = y[:, g * gw:(g + 1) * gw]
        parts.append(blk * lax.rsqrt(jnp.mean(blk * blk, axis=-1, keepdims=True) + EPS))
    y_ref[...] = (jnp.concatenate(parts, axis=1) * ng_ref[...]).astype(BF16)


def _ssd(p3, prev, h0, cw, cb, dtb, alog, dexp, ng, e16, *, q_len):
    n_grp, rows, _ = p3.shape
    carry = prev is None
    n_chunks = rows // TILE
    n_seq = TILE // q_len
    zspec = pl.BlockSpec((None, TILE, SSM_INNER), lambda i, c: (i, c, P_Z // SSM_INNER))
    xspec = pl.BlockSpec((None, TILE, CONV_DIM), lambda i, c: (i, c, P_XBC // CONV_DIM))
    gspec = pl.BlockSpec((None, TILE, LANE), lambda i, c: (i, c, P_GD // LANE))
    const = lambda arr: pl.BlockSpec(arr.shape, lambda i, c: (0,) * arr.ndim)
    params = [cw, cb, dtb, alog, dexp, ng, e16]
    in_specs = [zspec, xspec, gspec]
    args = [p3, p3, p3]
    if carry:
        n_out_seq = n_grp
        hspec = pl.BlockSpec((None, SSM_HEADS, SSM_HEADDIM, SSM_STATE), lambda i, c: (i, 0, 0, 0))
        scratch = [pltpu.VMEM((TILE, CONV_DIM), F32), pltpu.VMEM((SSM_HEADS, SSM_HEADDIM, SSM_STATE), F32)]
    else:
        n_out_seq = n_grp * n_chunks * n_seq
        hspec = pl.BlockSpec((n_seq, SSM_HEADS, SSM_HEADDIM, SSM_STATE), lambda i, c: (i * n_chunks + c, 0, 0, 0))
        in_specs += [pl.BlockSpec((None, TILE, CONV_DIM), lambda i, c: (i, c, 0)), hspec]
        args += [prev, h0]
        scratch = []
    in_specs += [const(a) for a in params]
    return pl.pallas_call(
        functools.partial(_ssd_body, q_len=q_len, carry=carry),
        grid=(n_grp, n_chunks),
        in_specs=in_specs,
        out_specs=[pl.BlockSpec((None, TILE, SSM_INNER), lambda i, c: (i, c, 0)), hspec],
        out_shape=[jax.ShapeDtypeStruct((n_grp, rows, SSM_INNER), BF16),
                   jax.ShapeDtypeStruct((n_out_seq, SSM_HEADS, SSM_HEADDIM, SSM_STATE), F32)],
        scratch_shapes=scratch,
        compiler_params=_cparams("parallel", "arbitrary"),
        name="ssd",
    )(*args, *params)


def _importance_map_t(n_sel):
    n_cmp = TILE - 1
    c0 = np.arange(n_cmp) * CMP_STRIDE
    s0 = np.arange(n_sel) * SEL_BLOCK
    ov = np.minimum(c0[None, :] + CMP_BLOCK, s0[:, None] + SEL_BLOCK) - np.maximum(c0[None, :], s0[:, None])
    m = np.zeros((-(-n_sel // SUBLANE) * SUBLANE, TILE), np.float32)
    m[:n_sel, :n_cmp] = np.clip(ov, 0, None).astype(np.float32) / CMP_BLOCK
    return jnp.asarray(m, BF16)


def _dt_expand():
    e = np.zeros((LANE, SSM_INNER), np.float32)
    for h in range(SSM_HEADS):
        e[DT_COL + h, h * SSM_HEADDIM:(h + 1) * SSM_HEADDIM] = 1.0
    return jnp.asarray(e, BF16)


def _pad_to(x, axis, size):
    pad = [(0, 0)] * x.ndim
    pad[axis] = (0, size - x.shape[axis])
    return jnp.pad(x, pad)


def _history_rows(state, n_rows, width):
    b, k, _ = state.shape
    tile = jnp.pad(state, ((0, 0), (n_rows - k, 0), (0, width - state.shape[2])))
    return tile.reshape(b * n_rows, width)


def kernel(x_prompt, x_sample, cache_kv, cache_win_kv, state_ssm_conv, state_ssm, state_ffn_conv, page_table, rel_table, ln_mix_pre, w_in, cmp_pe, cmp_w1, cmp_b1, cmp_w2, cmp_b2, ssm_conv_w, ssm_conv_b, ssm_dt_bias, ssm_a_log, ssm_d, ssm_norm_g, w_out, ln_mix_post, ln_ffn_pre, ffn_w_gate, ffn_w_up, ffn_dw_w, ffn_dw_b, ffn_w_down, ln_ffn_post):
    bp, tp, d = x_prompt.shape
    bs, ts, _ = x_sample.shape
    n_pool = cache_kv.shape[1]
    n_pages = page_table.shape[1]
    past_len = n_pages * PAGE_SIZE
    win_len = cache_win_kv.shape[2]
    assert d == D_MODEL and tp == N_CHUNK_TILES * TILE and past_len == N_CHUNK_TILES * TILE
    assert N_HEADS * ts == LANE and ts == SUBLANE and tp >= WINDOW and win_len == WINDOW

    rel = rel_table.astype(F32)
    tab_p = jnp.repeat(rel.reshape(REL_BUCKETS, KV_HEADS, Q_PER_KV).transpose(1, 0, 2), TILE, axis=-1)
    tab_s = jnp.repeat(rel, ts, axis=1)
    cmpb_p, near, edge = _bias_prompt(tab_p, tp // TILE)
    cmpbt_s, selb, winb = _bias_sample(tab_s, past_len, ts, win_len)
    impt_p = _importance_map_t(-(-tp // SEL_BLOCK))
    impt_s = _importance_map_t(-(-(past_len + ts) // SEL_BLOCK))
    rsum, gsel, expand, place = _sample_selectors(ts, past_len + TILE)
    e16 = _dt_expand()
    dummy_pt = jnp.zeros((1, 1), jnp.int32)

    xp = x_prompt.reshape(bp * tp, d)
    xs = x_sample.reshape(bs * ts, d)
    outs = [[] for _ in range(10)]
    for l in range(w_in.shape[0]):
        o1 = ATTN_WIDTH
        o2 = o1 + N_KV_SLOTS * KV_WIDTH
        o3 = o2 + 3 * N_HEADS
        o4 = o3 + SSM_INNER
        o5 = o4 + CONV_DIM
        w = w_in[l].T
        wp = jnp.concatenate([w[:o1], w[o3:o4], w[o4:o5], w[o1:o2], w[o2:o3], w[o5:],
                              jnp.zeros((P_WIDTH - w.shape[0], d), w.dtype)], axis=0).astype(BF16)
        g_pre = ln_mix_pre[l].reshape(1, d)
        cw = _compress_weights(cmp_pe[l], cmp_w1[l], cmp_b1[l], cmp_w2[l], cmp_b2[l])
        ssm_w = (_pad_to(ssm_conv_w[l], 0, SUBLANE), ssm_conv_b[l].reshape(1, CONV_DIM),
                 _pad_to(jnp.pad(ssm_dt_bias[l].astype(F32), (DT_COL, 0)), 0, LANE).reshape(1, LANE),
                 _pad_to(jnp.pad(ssm_a_log[l].astype(F32), (DT_COL, 0)), 0, LANE).reshape(1, LANE),
                 jnp.repeat(ssm_d[l].astype(F32), SSM_HEADDIM).reshape(1, SSM_INNER),
                 ssm_norm_g[l].astype(F32).reshape(1, SSM_INNER), e16)
        w_attn = w_out[l, :ATTN_WIDTH].astype(BF16)
        w_ssm = w_out[l, ATTN_WIDTH:].astype(BF16)
        g_post = ln_mix_post[l].reshape(1, d)
        g_pre2 = ln_ffn_pre[l].reshape(1, d)
        wg = _cast_pad_cols(ffn_w_gate[l], FF_PAD)
        wu = _cast_pad_cols(ffn_w_up[l], FF_PAD)
        dw = _pad_to(_pad_to(ffn_dw_w[l], 1, FF_PAD), 0, SUBLANE)
        db = _pad_to(ffn_dw_b[l].reshape(1, D_FF), 1, FF_PAD)
        wd = _cast_pad_rows(ffn_w_down[l], FF_PAD)
        g_post2 = ln_ffn_post[l].reshape(1, d)

        p3 = _proj(xp, g_pre, wp, tm=1024).reshape(bp, tp, P_WIDTH)
        kc, vc = _compress(p3, lambda j, cb, b, pt: (b, j, P_KV // LANE + cb), bp, dummy_pt, *cw)
        attn = _nsa_prompt(p3, kc, vc, cmpb_p, near, edge, tab_p, impt_p)
        ssm, h_new = _ssd(p3, None, None, *ssm_w, q_len=TILE)
        x1, hn = _outproj(attn.reshape(bp * tp, ATTN_WIDTH), ssm.reshape(bp * tp, SSM_INNER), xp,
                          w_attn, w_ssm, g_post, g_pre2)
        act, tail = _ffn_up(hn, wg, wu, dw, db, None, seq_len=tp, tm=tp)
        xp = _ffn_down(act, wd, x1, g_post2)
        kv = p3[:, :, P_KV:P_KV + N_KV_SLOTS * KV_WIDTH]
        outs[0].append(kv[:, :, :N_CACHE_SLOTS * KV_WIDTH].reshape(bp, tp, N_CACHE_SLOTS, KV_HEADS, HEAD_DIM))
        outs[2].append(kv[:, tp - WINDOW:, N_CACHE_SLOTS * KV_WIDTH:].reshape(bp, WINDOW, 2, KV_HEADS, HEAD_DIM))
        outs[4].append(p3[:, tp - (SSM_CONV - 1):, P_XBC:P_XBC + CONV_DIM])
        outs[6].append(h_new)
        outs[8].append(tail[:, SUBLANE - (FFN_CONV - 1):, :D_FF])

        p3 = _proj(xs, g_pre, wp, tm=bs * ts).reshape(bs, ts, P_WIDTH)
        cache_t = cache_kv[l].transpose(0, 2, 3, 4, 1).reshape(n_pool, N_CACHE_SLOTS * KV_WIDTH, PAGE_SIZE)
        win_t = cache_win_kv[l].transpose(0, 2, 3, 4, 1).reshape(bs, 2 * KV_WIDTH, win_len)
        w1p, perm = _compress_paged_weights(cmp_w1[l])
        kc, vc = _compress_paged(cache_t, page_table, perm, w1p, _pe_bias(cw[0], w1p), cw[2], cw[3], cw[4])
        attn, win_new = _nsa_sample(p3, kc, vc, cache_t, win_t, page_table, cmpbt_s, selb, winb,
                                    impt_s, rsum, gsel, expand, place)
        win_new = win_new.reshape(bs, 2, KV_HEADS, HEAD_DIM, win_len).transpose(0, 4, 1, 2, 3)
        prev = _history_rows(state_ssm_conv[l], ts, CONV_DIM).reshape(1, bs * ts, CONV_DIM)
        ssm, h_new = _ssd(p3.reshape(1, bs * ts, P_WIDTH), prev, state_ssm[l], *ssm_w, q_len=ts)
        x1, hn = _outproj(attn.reshape(bs * ts, ATTN_WIDTH), ssm.reshape(bs * ts, SSM_INNER), xs,
                          w_attn, w_ssm, g_post, g_pre2)
        prev_f = _history_rows(state_ffn_conv[l], ts, FF_PAD)
        act, gp = _ffn_up(hn, wg, wu, dw, db, prev_f, seq_len=ts, tm=bs * ts)
        xs = _ffn_down(act, wd, x1, g_post2, tm=bs * ts)
        kv = p3[:, :, P_KV:P_KV + N_KV_SLOTS * KV_WIDTH]
        outs[1].append(kv[:, :, :N_CACHE_SLOTS * KV_WIDTH].reshape(bs, ts, N_CACHE_SLOTS, KV_HEADS, HEAD_DIM))
        outs[3].append(win_new)
        outs[5].append(p3[:, ts - (SSM_CONV - 1):, P_XBC:P_XBC + CONV_DIM])
        outs[7].append(h_new)
        outs[9].append(gp.reshape(bs, ts, FF_PAD)[:, ts - (FFN_CONV - 1):, :D_FF])
    return (xp.reshape(bp, tp, d), xs.reshape(bs, ts, d)) + tuple(jnp.stack(o) for o in outs)
```

```python
import functools
import math

import numpy as np
import jax
import jax.numpy as jnp
from jax import lax
from jax.experimental import pallas as pl
from jax.experimental.pallas import tpu as pltpu

F32 = jnp.float32
BF16 = jnp.bfloat16

D_MODEL = 2048
PAGE_SIZE = 128
HEAD_DIM = 64
ATTN_WIDTH = D_MODEL // 2
N_HEADS = ATTN_WIDTH // HEAD_DIM
KV_HEADS = 4
Q_PER_KV = N_HEADS // KV_HEADS
CMP_STRIDE = 16
CMP_BLOCK = 2 * CMP_STRIDE
CMP_HIDDEN = 2 * HEAD_DIM
SEL_BLOCK = 64
N_SELECT = 8
WINDOW = 512
N_KV_SLOTS = 6
N_CACHE_SLOTS = 4
KV_WIDTH = KV_HEADS * HEAD_DIM
SSM_INNER = D_MODEL - ATTN_WIDTH
SSM_HEADDIM = 64
SSM_HEADS = SSM_INNER // SSM_HEADDIM
SSM_GROUPS = 4
SSM_HPG = SSM_HEADS // SSM_GROUPS
SSM_STATE = 128
SSM_CONV = 4
CONV_DIM = SSM_INNER + 2 * SSM_GROUPS * SSM_STATE
D_FF = ((8 * D_MODEL // 3 + 127) // 128) * 128
FFN_CONV = 3
REL_BUCKETS = 32
REL_MAX_DIST = 128
EPS = 1e-6
NEG = -1e30
TINY = 1e-30
FORCE_BONUS = 1e4
SCALE = HEAD_DIM ** -0.5
LOG2E = math.log2(math.e)

LANE = 128
SUBLANE = 8
TILE = 128
VMEM_LIMIT = 56 * 1024 * 1024

P_Q = 0
P_Z = ATTN_WIDTH
P_XBC = P_Z + SSM_INNER
P_KV = P_XBC + CONV_DIM
P_GD = P_KV + N_KV_SLOTS * KV_WIDTH
GD_WIDTH = 3 * N_HEADS + SSM_HEADS
P_WIDTH = 6144
DT_COL = 3 * N_HEADS
FF_PAD = 5632
FFN_SUB = 256


def _cparams(*sem):
    return pltpu.CompilerParams(dimension_semantics=sem, vmem_limit_bytes=VMEM_LIMIT)


def _rms(x, g):
    return x * lax.rsqrt(jnp.mean(x * x, axis=-1, keepdims=True) + EPS) * g


def _split3(x):
    hi = x.astype(BF16)
    r1 = x - hi.astype(F32)
    mid = r1.astype(BF16)
    lo = (r1 - mid.astype(F32)).astype(BF16)
    return hi, mid, lo


def _dot(a, b):
    return jnp.dot(a, b, preferred_element_type=F32)


def _dot_nt(a, b):
    return lax.dot_general(a, b, (((1,), (1,)), ((), ())), preferred_element_type=F32)


def _dot_exact_rhs(x, sel):
    hi, mid, lo = _split3(x)
    return _dot(hi, sel) + _dot(mid, sel) + _dot(lo, sel)


def _dot_exact_lhs(sel, x):
    hi, mid, lo = _split3(x)
    return _dot(sel, hi) + _dot(sel, mid) + _dot(sel, lo)


def _gelu_tanh(x):
    return 0.5 * x * (1.0 + jnp.tanh(math.sqrt(2.0 / math.pi) * (x + 0.044715 * (x * x * x))))


def _silu(x):
    return x * jax.nn.sigmoid(x)


CAST_BLOCK = 256


def _cast_pad_cols_body(w_ref, o_ref):
    n = w_ref.shape[1]
    o_ref[:, 0:n] = w_ref[...].astype(BF16)
    o_ref[:, n:] = jnp.zeros((o_ref.shape[0], o_ref.shape[1] - n), BF16)


def _cast_pad_cols(w, n_out):
    m, n = w.shape
    return pl.pallas_call(
        _cast_pad_cols_body,
        grid=(m // CAST_BLOCK,),
        in_specs=[pl.BlockSpec((CAST_BLOCK, n), lambda i: (i, 0))],
        out_specs=pl.BlockSpec((CAST_BLOCK, n_out), lambda i: (i, 0)),
        out_shape=jax.ShapeDtypeStruct((m, n_out), BF16),
        compiler_params=_cparams("parallel"),
        name="cast_pad_cols",
    )(w)


def _cast_pad_rows_body(w_ref, o_ref):
    m = w_ref.shape[0]
    o_ref[0:m, :] = w_ref[...].astype(BF16)
    o_ref[m:, :] = jnp.zeros((o_ref.shape[0] - m, o_ref.shape[1]), BF16)


def _cast_pad_rows(w, m_out):
    m, n = w.shape
    return pl.pallas_call(
        _cast_pad_rows_body,
        grid=(n // CAST_BLOCK,),
        in_specs=[pl.BlockSpec((m, CAST_BLOCK), lambda j: (0, j))],
        out_specs=pl.BlockSpec((m_out, CAST_BLOCK), lambda j: (0, j)),
        out_shape=jax.ShapeDtypeStruct((m_out, n), BF16),
        compiler_params=_cparams("parallel"),
        name="cast_pad_rows",
    )(w)


def _proj_body(x_ref, g_ref, w_ref, o_ref, xn_ref):
    @pl.when(pl.program_id(1) == 0)
    def _():
        xn_ref[...] = _rms(x_ref[...], g_ref[...]).astype(BF16)

    o_ref[...] = _dot_nt(xn_ref[...], w_ref[...])


def _proj(x2d, g, w_t, tm, tn=1024):
    m, k = x2d.shape
    n = w_t.shape[0]
    w = w_t
    return pl.pallas_call(
        _proj_body,
        grid=(m // tm, n // tn),
        in_specs=[pl.BlockSpec((tm, k), lambda i, j: (i, 0)),
                  pl.BlockSpec((1, k), lambda i, j: (0, 0)),
                  pl.BlockSpec((tn, k), lambda i, j: (j, 0))],
        out_specs=pl.BlockSpec((tm, tn), lambda i, j: (i, j)),
        out_shape=jax.ShapeDtypeStruct((m, n), F32),
        scratch_shapes=[pltpu.VMEM((tm, k), BF16)],
        compiler_params=_cparams("parallel", "arbitrary"),
        name="proj",
    )(x2d, g, w)


def _outproj_body(a_ref, s_ref, x_ref, wa_ref, ws_ref, g1_ref, g2_ref, x1_ref, hn_ref):
    mix = _dot(a_ref[...], wa_ref[...]) + _dot(s_ref[...], ws_ref[...])
    x1 = x_ref[...] + _rms(mix, g1_ref[...])
    x1_ref[...] = x1
    hn_ref[...] = _rms(x1, g2_ref[...]).astype(BF16)


def _outproj(attn, ssm, x2d, w_attn, w_ssm, g_post, g_pre2, tm=512):
    m, d = x2d.shape
    ka = attn.shape[1]
    const = lambda i: (0, 0)
    return pl.pallas_call(
        _outproj_body,
        grid=(m // tm,),
        in_specs=[pl.BlockSpec((tm, ka), lambda i: (i, 0)),
                  pl.BlockSpec((tm, ka), lambda i: (i, 0)),
                  pl.BlockSpec((tm, d), lambda i: (i, 0)),
                  pl.BlockSpec((ka, d), const),
                  pl.BlockSpec((ka, d), const),
                  pl.BlockSpec((1, d), const),
                  pl.BlockSpec((1, d), const)],
        out_specs=[pl.BlockSpec((tm, d), lambda i: (i, 0)),
                   pl.BlockSpec((tm, d), lambda i: (i, 0))],
        out_shape=[jax.ShapeDtypeStruct((m, d), F32), jax.ShapeDtypeStruct((m, d), BF16)],
        compiler_params=_cparams("parallel"),
        name="outproj",
    )(attn, ssm, x2d, w_attn, w_ssm, g_post, g_pre2)


def _shift_rows(x, prev, k, seq_len):
    rows, cols = x.shape
    if seq_len >= rows:
        t = lax.broadcasted_iota(jnp.int32, (rows, 1), 0)
        return jnp.where(t >= k, pltpu.roll(x, k, 0), 0.0 if prev is None else pltpu.roll(prev, k, 0))
    assert seq_len == SUBLANE and prev is not None
    shape = (rows // seq_len, seq_len, cols)
    t = lax.broadcasted_iota(jnp.int32, shape, 1)
    out = jnp.where(t >= k, pltpu.roll(x.reshape(shape), k, 1), pltpu.roll(prev.reshape(shape), k, 1))
    return out.reshape(rows, cols)


def _ffn_up_body(*refs, seq_len, has_prev):
    if has_prev:
        h_ref, wg_ref, wu_ref, dw_ref, db_ref, prev_ref, act_ref, tail_ref = refs
    else:
        h_ref, wg_ref, wu_ref, dw_ref, db_ref, act_ref, tail_ref = refs
    tm, tn = act_ref.shape
    h = h_ref[...]

    def project(cs):
        return _dot(h, wg_ref[:, cs]), _dot(h, wu_ref[:, cs])

    def gate(cs, gp, up):
        prev = prev_ref[:, cs] if has_prev else None
        acc = db_ref[:, cs] + dw_ref[FFN_CONV - 1:FFN_CONV, cs] * gp
        for k in range(1, FFN_CONV):
            acc = acc + dw_ref[FFN_CONV - 1 - k:FFN_CONV - k, cs] * _shift_rows(gp, prev, k, seq_len)
        act_ref[:, cs] = (_gelu_tanh(acc) * up).astype(BF16)
        tail_ref[:, cs] = gp if has_prev else gp[tm - SUBLANE:, :]

    pending = None
    for c in range(tn // FFN_SUB):
        cs = slice(c * FFN_SUB, (c + 1) * FFN_SUB)
        cur = (cs,) + project(cs)
        if pending is not None:
            gate(*pending)
        pending = cur
    gate(*pending)


def _ffn_up(hn, wg, wu, dw, db, prev, *, seq_len, tm, tn=512):
    m, d = hn.shape
    ni = m // tm
    nj = FF_PAD // tn
    has_prev = prev is not None
    in_specs = [pl.BlockSpec((tm, d), lambda i, j: (i, 0)),
                pl.BlockSpec((d, tn), lambda i, j: (0, j)),
                pl.BlockSpec((d, tn), lambda i, j: (0, j)),
                pl.BlockSpec((SUBLANE, tn), lambda i, j: (0, j)),
                pl.BlockSpec((1, tn), lambda i, j: (0, j))]
    args = [hn, wg, wu, dw, db]
    if has_prev:
        in_specs.append(pl.BlockSpec((tm, tn), lambda i, j: (i, j)))
        args.append(prev)
        tail_shape = jax.ShapeDtypeStruct((m, FF_PAD), F32)
        tail_spec = pl.BlockSpec((tm, tn), lambda i, j: (i, j))
    else:
        tail_shape = jax.ShapeDtypeStruct((ni, SUBLANE, FF_PAD), F32)
        tail_spec = pl.BlockSpec((None, SUBLANE, tn), lambda i, j: (i, 0, j))
    return pl.pallas_call(
        functools.partial(_ffn_up_body, seq_len=seq_len, has_prev=has_prev),
        grid=(ni, nj),
        in_specs=in_specs,
        out_specs=[pl.BlockSpec((tm, tn), lambda i, j: (i, j)), tail_spec],
        out_shape=[jax.ShapeDtypeStruct((m, FF_PAD), BF16), tail_shape],
        compiler_params=_cparams("parallel", "parallel"),
        name="ffn_up",
    )(*args)


def _ffn_down_body(a_ref, w_ref, x_ref, g_ref, o_ref, acc_ref):
    k = pl.program_id(1)

    @pl.when(k == 0)
    def _():
        acc_ref[...] = jnp.zeros_like(acc_ref)

    acc_ref[...] += _dot(a_ref[...], w_ref[...])

    @pl.when(k == pl.num_programs(1) - 1)
    def _():
        o_ref[...] = x_ref[...] + _rms(acc_ref[...], g_ref[...])


def _ffn_down(act, wd, x1, g, tm=1024, tk=512):
    m, d = x1.shape
    return pl.pallas_call(
        _ffn_down_body,
        grid=(m // tm, FF_PAD // tk),
        in_specs=[pl.BlockSpec((tm, tk), lambda i, k: (i, k)),
                  pl.BlockSpec((tk, d), lambda i, k: (k, 0)),
                  pl.BlockSpec((tm, d), lambda i, k: (i, 0)),
                  pl.BlockSpec((1, d), lambda i, k: (0, 0))],
        out_specs=pl.BlockSpec((tm, d), lambda i, k: (i, 0)),
        out_shape=jax.ShapeDtypeStruct((m, d), F32),
        scratch_shapes=[pltpu.VMEM((tm, d), F32)],
        compiler_params=_cparams("parallel", "arbitrary"),
        name="ffn_down",
    )(act, wd, x1, g)


N_CHUNK_TILES = 16
CHUNKS_PER_TILE = TILE // CMP_STRIDE
CMP_COL_BLOCKS = 2 * KV_WIDTH // LANE


def _compress_body(*refs):
    pt_ref = refs[0]
    del pt_ref
    n_in = N_CHUNK_TILES * CMP_COL_BLOCKS
    tiles = refs[1:1 + n_in]
    pe_ref, w1_ref, b1_ref, w2_ref, b2_ref, kc_ref, vc_ref, lhs_a, lhs_b = refs[1 + n_in:]
    out_refs = (kc_ref, vc_ref)
    for slot in range(2):
        outs = []
        for gp in range(KV_HEADS // 2):
            cb = slot * (KV_HEADS // 2) + gp
            for s in range(CMP_STRIDE):
                x = jnp.concatenate(
                    [tiles[j * CMP_COL_BLOCKS + cb][pl.ds(s, CHUNKS_PER_TILE, stride=CMP_STRIDE), :]
                     for j in range(N_CHUNK_TILES)], axis=0)
                lhs_a[:, s * LANE:(s + 1) * LANE] = (x + pe_ref[slot, 0, s:s + 1, :]).astype(BF16)
                lhs_b[:, s * LANE:(s + 1) * LANE] = (x + pe_ref[slot, 1, s:s + 1, :]).astype(BF16)
            first = _dot(lhs_a[...], w1_ref[slot, 0])
            second = _dot(lhs_b[...], w1_ref[slot, 1])
            hid = first + pltpu.roll(second, TILE - 1, 0) + b1_ref[slot]
            act = _gelu_tanh(hid).astype(BF16)
            for gj in range(2):
                outs.append(_dot(act[:, gj * CMP_HIDDEN:(gj + 1) * CMP_HIDDEN], w2_ref[slot]) + b2_ref[slot])
        out_refs[slot][...] = jnp.concatenate(outs, axis=1)


def _compress(src, tile_map, n_seq, page_table, pe, w1, b1, w2, b2):
    in_specs = [pl.BlockSpec((None, TILE, LANE), functools.partial(tile_map, j, cb))
                for j in range(N_CHUNK_TILES) for cb in range(CMP_COL_BLOCKS)]
    full = lambda shape: pl.BlockSpec(shape, lambda b, pt: (0,) * len(shape))
    in_specs += [full(pe.shape), full(w1.shape), full(b1.shape), full(w2.shape), full(b2.shape)]
    out_spec = pl.BlockSpec((None, TILE, KV_WIDTH), lambda b, pt: (b, 0, 0))
    return pl.pallas_call(
        _compress_body,
        grid_spec=pltpu.PrefetchScalarGridSpec(
            num_scalar_prefetch=1, grid=(n_seq,), in_specs=in_specs, out_specs=[out_spec, out_spec],
            scratch_shapes=[pltpu.VMEM((TILE, CMP_STRIDE * LANE), BF16), pltpu.VMEM((TILE, CMP_STRIDE * LANE), BF16)]),
        out_shape=[jax.ShapeDtypeStruct((n_seq, TILE, KV_WIDTH), F32)] * 2,
        compiler_params=_cparams("parallel"),
        name="compress",
    )(page_table, *([src] * (N_CHUNK_TILES * CMP_COL_BLOCKS)), pe, w1, b1, w2, b2)


def _compress_weights(cmp_pe, cmp_w1, cmp_b1, cmp_w2, cmp_b2):
    pe = cmp_pe.reshape(2, 2, CMP_STRIDE, HEAD_DIM)
    pe = jnp.concatenate([pe, pe], axis=-1)
    w = cmp_w1.reshape(2, 2, CMP_STRIDE, HEAD_DIM, CMP_HIDDEN)
    eye = jnp.eye(2, dtype=w.dtype)
    w1 = jnp.einsum('absdh,ij->absidjh', w, eye).reshape(2, 2, CMP_STRIDE * LANE, 2 * CMP_HIDDEN).astype(BF16)
    b1 = jnp.concatenate([cmp_b1, cmp_b1], axis=-1).reshape(2, 1, 2 * CMP_HIDDEN)
    return pe, w1, b1, cmp_w2.astype(BF16), cmp_b2.reshape(2, 1, HEAD_DIM)


PAGES_PER_PERM = 2


def _pe_bias_body(pe_ref, w1_ref, o_ref):
    for slot in range(2):
        cols = []
        for half in range(2):
            acc = jnp.zeros((2 * SUBLANE, 2 * CMP_HIDDEN), F32)
            for s in range(CMP_STRIDE):
                row = jnp.broadcast_to(pe_ref[slot, half, s:s + 1, :], (2 * SUBLANE, LANE))
                w = w1_ref[slot, s * LANE:(s + 1) * LANE, half * 2 * CMP_HIDDEN:(half + 1) * 2 * CMP_HIDDEN]
                acc = acc + _dot_exact_rhs(row, w)
            cols.append(acc)
        o_ref[slot] = jnp.concatenate(cols, axis=1)


def _pe_bias(pe, w1):
    return pl.pallas_call(
        _pe_bias_body,
        out_shape=jax.ShapeDtypeStruct((2, 2 * SUBLANE, 4 * CMP_HIDDEN), F32),
        name="pe_bias",
    )(pe, w1)


def _compress_pages(pages, perm_ref, w1_ref, peb_ref, b1_ref, w2_ref, b2_ref, kc_ref, vc_ref, xp_ref, hid_ref):
    n_perm = N_CHUNK_TILES // PAGES_PER_PERM
    slab = PAGES_PER_PERM * CHUNKS_PER_TILE
    for jp in range(n_perm):
        xt = jnp.concatenate([pages[PAGES_PER_PERM * jp + jj][0:2 * KV_WIDTH, :] for jj in range(PAGES_PER_PERM)],
                             axis=1)
        xp_ref[jp] = _dot_nt(perm_ref[...], xt.astype(BF16)).astype(BF16)
    out_refs = (kc_ref, vc_ref)

    def first_layer(slot, gp):
        cb = slot * (KV_HEADS // 2) + gp
        lhs = jnp.concatenate(
            [jnp.concatenate([xp_ref[jp, s * slab:(s + 1) * slab, cb * LANE:(cb + 1) * LANE]
                              for jp in range(n_perm)], axis=0)
             for s in range(CMP_STRIDE)], axis=1)
        hid_ref[gp] = _dot(lhs, w1_ref[slot])
        return gp

    def second_layer(slot, gp, buf):
        acc = hid_ref[buf] + peb_ref[slot, 0:1, :]
        hid = (acc[:, 0:2 * CMP_HIDDEN] + pltpu.roll(acc[:, 2 * CMP_HIDDEN:], TILE - 1, 0) + b1_ref[slot])
        act = _gelu_tanh(hid).astype(BF16)
        for gj in range(2):
            g = 2 * gp + gj
            out_refs[slot][:, g * HEAD_DIM:(g + 1) * HEAD_DIM] = (
                _dot(act[:, gj * CMP_HIDDEN:(gj + 1) * CMP_HIDDEN], w2_ref[slot]) + b2_ref[slot])

    pending = None
    for slot in range(2):
        for gp in range(KV_HEADS // 2):
            acc = first_layer(slot, gp)
            if pending is not None:
                second_layer(*pending)
            pending = (slot, gp, acc)
    second_layer(*pending)


def _compress_paged_weights(cmp_w1):
    w = cmp_w1.reshape(2, 2, CMP_STRIDE, HEAD_DIM, CMP_HIDDEN)
    eye = jnp.eye(2, dtype=w.dtype)
    w1 = jnp.einsum('absdh,ij->asidbjh', w, eye).reshape(2, CMP_STRIDE * LANE, 4 * CMP_HIDDEN).astype(BF16)
    rows = PAGES_PER_PERM * PAGE_SIZE
    r = np.arange(rows)
    s, page, chunk = r // (PAGES_PER_PERM * CHUNKS_PER_TILE), (r // CHUNKS_PER_TILE) % PAGES_PER_PERM, r % CHUNKS_PER_TILE
    src = page * PAGE_SIZE + chunk * CMP_STRIDE + s
    perm = (np.arange(rows)[None, :] == src[:, None]).astype(np.float32)
    return w1, jnp.asarray(perm, BF16)


def _bucket_thresholds():
    n = np.arange(0, 2 * REL_MAX_DIST)
    max_exact = REL_BUCKETS // 2
    nf = np.maximum(n, 1).astype(np.float32)
    large = max_exact + (np.log(nf / np.float32(max_exact)) / np.float32(math.log(REL_MAX_DIST / max_exact))
                         * np.float32(REL_BUCKETS - max_exact)).astype(np.int32)
    bucket = np.where(n < max_exact, n, np.minimum(large, REL_BUCKETS - 1))
    assert (np.diff(bucket) >= 0).all() and (np.diff(bucket) <= 1).all()
    return [int(np.argmax(bucket >= k)) for k in range(1, REL_BUCKETS)]


BUCKET_START = _bucket_thresholds()
CMP_BAND = 2 * SUBLANE
assert (BUCKET_START[-1] - 1 + CMP_BLOCK - 1) // CMP_STRIDE <= SUBLANE
assert (TILE - CMP_BLOCK) // CMP_STRIDE < SUBLANE


def _rel_bias(dist, tab_ref):
    bias = jnp.broadcast_to(tab_ref[0:1, :], dist.shape)
    for k in range(1, REL_BUCKETS):
        bias = jnp.where(dist >= BUCKET_START[k - 1], tab_ref[k:k + 1, :], bias)
    return bias


def _bias_prompt_body(tab_ref, cmpb_ref, near_ref, edge_ref):
    qb = pl.program_id(0)
    shape = (TILE, Q_PER_KV * TILE)
    row = lax.broadcasted_iota(jnp.int32, shape, 0)
    qi = lax.broadcasted_iota(jnp.int32, shape, 1) % TILE
    dist_c = qb * TILE + qi - (row * CMP_STRIDE + CMP_BLOCK - 1)
    band0 = pl.multiple_of(jnp.maximum(qb * (TILE // CMP_STRIDE) - SUBLANE, 0), SUBLANE)
    band_shape = (CMP_BAND, Q_PER_KV * TILE)
    band_row = band0 + lax.broadcasted_iota(jnp.int32, band_shape, 0)
    band_qi = lax.broadcasted_iota(jnp.int32, band_shape, 1) % TILE
    dist_b = qb * TILE + band_qi - (band_row * CMP_STRIDE + CMP_BLOCK - 1)
    for g in range(KV_HEADS):
        cmpb_ref[g] = jnp.where(dist_c >= 0, tab_ref[g, REL_BUCKETS - 1:REL_BUCKETS, :], NEG)
        cmpb_ref[g, pl.ds(band0, CMP_BAND), :] = jnp.where(dist_b >= 0, _rel_bias(dist_b, tab_ref.at[g]), NEG)

    @pl.when(qb == 0)
    def _():
        for g in range(KV_HEADS):
            for delta in range(2):
                dist = delta * TILE + qi - row
                b = _rel_bias(dist, tab_ref.at[g])
                near_ref[g, delta] = jnp.where(dist >= 0, b * LOG2E, NEG)
        edge_ref[...] = jnp.where(qi <= row, 0.0, NEG)


def _bias_prompt(tab, n_qb):
    lanes = Q_PER_KV * TILE
    return pl.pallas_call(
        _bias_prompt_body,
        grid=(n_qb,),
        in_specs=[pl.BlockSpec(tab.shape, lambda i: (0, 0, 0))],
        out_specs=[pl.BlockSpec((None, KV_HEADS, TILE, lanes), lambda i: (i, 0, 0, 0)),
                   pl.BlockSpec((KV_HEADS, 2, TILE, lanes), lambda i: (0, 0, 0, 0)),
                   pl.BlockSpec((TILE, lanes), lambda i: (0, 0))],
        out_shape=[jax.ShapeDtypeStruct((n_qb, KV_HEADS, TILE, lanes), F32),
                   jax.ShapeDtypeStruct((KV_HEADS, 2, TILE, lanes), F32),
                   jax.ShapeDtypeStruct((TILE, lanes), F32)],
        compiler_params=_cparams("arbitrary"),
        name="bias_prompt",
    )(tab)


def _bias_sample_body(tab_ref, cmpbt_ref, selb_ref, winb_ref, *, past_len, n_new, win_len):
    qt = lax.broadcasted_iota(jnp.int32, (TILE, LANE), 1) % n_new
    row = lax.broadcasted_iota(jnp.int32, (TILE, LANE), 0)
    dist = past_len + qt - (row * CMP_STRIDE + CMP_BLOCK - 1)
    cmpbt_ref[...] = jnp.where(dist >= 0, _rel_bias(dist, tab_ref), NEG)
    for j in range(selb_ref.shape[1] // TILE):
        key = row + j * TILE
        dist = past_len + qt - key
        ok = (dist >= 0) & (key < past_len + n_new)
        selb_ref[:, j * TILE:(j + 1) * TILE] = jnp.where(ok, _rel_bias(dist, tab_ref), NEG).T
    for j in range(winb_ref.shape[1] // TILE):
        key = row + j * TILE
        dist = win_len + qt - key
        ok = (dist >= 0) & (dist <= WINDOW) & (key < win_len + n_new)
        winb_ref[:, j * TILE:(j + 1) * TILE] = jnp.where(ok, _rel_bias(dist, tab_ref), NEG).T


def _bias_sample(tab, past_len, n_new, win_len):
    return pl.pallas_call(
        functools.partial(_bias_sample_body, past_len=past_len, n_new=n_new, win_len=win_len),
        out_shape=[jax.ShapeDtypeStruct((TILE, LANE), F32),
                   jax.ShapeDtypeStruct((LANE, past_len + TILE), F32),
                   jax.ShapeDtypeStruct((LANE, win_len + TILE), F32)],
        name="bias_sample",
    )(tab)


def _select_blocks(imp, blk, q_blk):
    valid = blk <= q_blk
    forced = (blk == 0) | (blk == q_blk) | (blk == q_blk - 1)
    score = jnp.where(valid, imp + jnp.where(forced, FORCE_BONUS, 0.0), -jnp.inf)
    rank = jnp.zeros(score.shape, jnp.int32)
    for i in range(score.shape[0]):
        row = score[i:i + 1, :]
        beats = (row > score) | ((row == score) & (blk > i))
        rank = rank + beats.astype(jnp.int32)
    return jnp.where(valid & (rank < N_SELECT), 0.0, NEG)


def _nsa_prompt_body(q_ref, gd_ref, kc_ref, vc_ref, sk_ref, sv_ref, wk_ref, wv_ref, cmpb_ref, near_ref,
                     edge_ref, tab_ref, impt_ref, o_ref, qs_ref, seladd_ref, selfar_ref, m_ref, l_ref, acc_ref,
                     out_ref):
    qb = pl.program_id(1)
    lanes = Q_PER_KV * TILE
    q = q_ref[...] * SCALE
    gate_t = jax.nn.sigmoid(gd_ref[...]).T
    n_sel = seladd_ref.shape[1]
    blk = lax.broadcasted_iota(jnp.int32, (n_sel, TILE), 0)
    q_blk = (qb * TILE + lax.broadcasted_iota(jnp.int32, (n_sel, TILE), 1)) // SEL_BLOCK
    groups = range(KV_HEADS)

    def hs(g):
        return slice(g * HEAD_DIM, (g + 1) * HEAD_DIM)

    def gate_row(branch, g):
        c0 = branch * N_HEADS + g * Q_PER_KV
        return jnp.concatenate([gate_t[c0 + r:c0 + r + 1, :] for r in range(Q_PER_KV)], axis=1)

    def far_bias(g):
        return tab_ref[g, REL_BUCKETS - 1:REL_BUCKETS, :] * LOG2E

    def stack_heads(x, g):
        return jnp.concatenate(
            [x[:, (g * Q_PER_KV + r) * HEAD_DIM:(g * Q_PER_KV + r + 1) * HEAD_DIM] for r in range(Q_PER_KV)],
            axis=0).astype(BF16)

    q2 = q * LOG2E
    for g in groups:
        qs = stack_heads(q, g)
        qs_ref[g] = stack_heads(q2, g)
        s = _dot_nt(kc_ref[:, hs(g)].astype(BF16), qs) + cmpb_ref[g]
        m = jnp.max(s, axis=0, keepdims=True)
        e = jnp.exp(s - m)
        norm = jnp.where(m > 0.5 * NEG, 1.0 / jnp.maximum(jnp.sum(e, axis=0, keepdims=True), TINY), 0.0)
        p = e * norm
        out_ref[g] = gate_row(0, g) * _dot(vc_ref[:, hs(g)].T.astype(BF16), p.astype(BF16))
        psum = p[:, 0:TILE]
        for r in range(1, Q_PER_KV):
            psum = psum + p[:, r * TILE:(r + 1) * TILE]
        sa = _select_blocks(_dot_exact_lhs(impt_ref[...], psum), blk, q_blk)
        sa = jnp.concatenate([sa] * Q_PER_KV, axis=1)
        seladd_ref[g] = sa
        selfar_ref[g] = sa + far_bias(g)

    def sel_rows(ref, g, kt):
        half = SEL_BLOCK
        return jnp.concatenate(
            [jnp.broadcast_to(ref[g, pl.ds(2 * kt, 1), :], (half, lanes)),
             jnp.broadcast_to(ref[g, pl.ds(2 * kt + 1, 1), :], (half, lanes))], axis=0)

    def reset():
        m_ref[...] = jnp.full(m_ref.shape, -jnp.inf, F32)
        l_ref[...] = jnp.zeros(l_ref.shape, F32)
        acc_ref[...] = jnp.zeros(acc_ref.shape, F32)

    def tiles(k_ref, v_ref, parts):
        starts = [pl.multiple_of(kt * TILE, TILE) for kt, _ in parts]
        k = jnp.concatenate([k_ref[pl.ds(r0, TILE), :] for r0 in starts], axis=0).astype(BF16)
        vt = jnp.concatenate([v_ref[pl.ds(r0, TILE), :] for r0 in starts], axis=0).T.astype(BF16)
        scores = []
        for g in groups:
            s = _dot_nt(k[:, hs(g)], qs_ref[g])
            scores.append([s[i * TILE:(i + 1) * TILE] + bias_of(g) for i, (_, bias_of) in enumerate(parts)])
        probs, alphas = [], []
        for g in groups:
            m = m_ref[g]
            m_new = m
            for s in scores[g]:
                m_new = jnp.maximum(m_new, jnp.max(s, axis=0, keepdims=True))
            alpha = jnp.exp2(m - m_new)
            ps = [jnp.exp2(s - m_new) for s in scores[g]]
            total = jnp.sum(ps[0], axis=0, keepdims=True)
            for p in ps[1:]:
                total = total + jnp.sum(p, axis=0, keepdims=True)
            m_ref[g] = m_new
            l_ref[g] = alpha * l_ref[g] + total
            probs.append(jnp.concatenate([p.astype(BF16) for p in ps], axis=0))
            alphas.append(alpha)
        pvs = [_dot(vt[hs(g), :], probs[g]) for g in groups]
        for g in groups:
            acc_ref[g] = alphas[g] * acc_ref[g] + pvs[g]

    def finish(branch, g):
        return gate_row(branch, g) * (acc_ref[g] / jnp.maximum(l_ref[g], TINY))

    reset()

    n_far = jnp.maximum(qb - 1, 0)

    def far_body(i, carry):
        hi = n_far - 1 - 2 * i
        lo = jnp.maximum(hi - 1, 0)
        lo_off = jnp.where(hi >= 1, 0.0, NEG)
        tiles(sk_ref, sv_ref, [(lo, lambda g: sel_rows(selfar_ref, g, lo) + lo_off),
                               (hi, lambda g: sel_rows(selfar_ref, g, hi))])
        return carry

    lax.fori_loop(0, (n_far + 1) // 2, far_body, 0)
    kt1 = jnp.maximum(qb - 1, 0)
    off1 = jnp.where(qb >= 1, 0.0, NEG)
    tiles(sk_ref, sv_ref, [(kt1, lambda g: near_ref[g, 1] + (sel_rows(seladd_ref, g, kt1) + off1)),
                           (qb, lambda g: near_ref[g, 0] + sel_rows(seladd_ref, g, qb))])
    for g in groups:
        out_ref[g] = out_ref[g] + finish(1, g)

    reset()
    parts = []
    for delta in range(WINDOW // TILE, -1, -1):
        kt = qb - delta
        off = jnp.where(kt >= 0, 0.0, NEG)
        if delta == WINDOW // TILE:
            bias_of = lambda g, off=off: edge_ref[...] + (far_bias(g) + off)
        elif delta >= 2:
            bias_of = lambda g, off=off: far_bias(g) + off
        else:
            bias_of = lambda g, off=off, delta=delta: near_ref[g, delta] + off
        parts.append((jnp.maximum(kt, 0), bias_of))
    tiles(wk_ref, wv_ref, parts[:3])
    tiles(wk_ref, wv_ref, parts[3:])
    outs = []
    for g in groups:
        out_t = out_ref[g] + finish(2, g)
        outs += [out_t[:, r * TILE:(r + 1) * TILE].T for r in range(Q_PER_KV)]
    o_ref[...] = jnp.concatenate(outs, axis=1).astype(BF16)


def _nsa_prompt(p3, kc, vc, cmpb, near, edge, tab, impt):
    b, t, _ = p3.shape
    n_qb = t // TILE
    lanes = Q_PER_KV * TILE
    kvcol = lambda slot: (P_KV + slot * KV_WIDTH) // KV_WIDTH
    seq_spec = lambda slot: pl.BlockSpec((None, t, KV_WIDTH), lambda i, j: (i, 0, kvcol(slot)))
    const = lambda shape: pl.BlockSpec(shape, lambda i, j: (0,) * len(shape))
    return pl.pallas_call(
        _nsa_prompt_body,
        grid=(b, n_qb),
        in_specs=[pl.BlockSpec((None, TILE, ATTN_WIDTH), lambda i, j: (i, j, 0)),
                  pl.BlockSpec((None, TILE, LANE), lambda i, j: (i, j, P_GD // LANE)),
                  pl.BlockSpec((None, TILE, KV_WIDTH), lambda i, j: (i, 0, 0)),
                  pl.BlockSpec((None, TILE, KV_WIDTH), lambda i, j: (i, 0, 0)),
                  seq_spec(2), seq_spec(3), seq_spec(4), seq_spec(5),
                  pl.BlockSpec((None, KV_HEADS, TILE, lanes), lambda i, j: (j, 0, 0, 0)),
                  const(near.shape), const(edge.shape), const(tab.shape), const(impt.shape)],
        out_specs=pl.BlockSpec((None, TILE, ATTN_WIDTH), lambda i, j: (i, j, 0)),
        out_shape=jax.ShapeDtypeStruct((b, t, ATTN_WIDTH), BF16),
        scratch_shapes=[pltpu.VMEM((KV_HEADS, lanes, HEAD_DIM), BF16),
                        pltpu.VMEM((KV_HEADS, impt.shape[0], lanes), F32),
                        pltpu.VMEM((KV_HEADS, impt.shape[0], lanes), F32),
                        pltpu.VMEM((KV_HEADS, 1, lanes), F32),
                        pltpu.VMEM((KV_HEADS, 1, lanes), F32),
                        pltpu.VMEM((KV_HEADS, HEAD_DIM, lanes), F32),
                        pltpu.VMEM((KV_HEADS, HEAD_DIM, lanes), F32)],
        compiler_params=_cparams("parallel", "arbitrary"),
        name="nsa_prompt",
    )(p3, p3, kc, vc, p3, p3, p3, p3, cmpb, near, edge, tab, impt)


def _nsa_sample_body(*refs, past_len, n_new):
    n_pages = past_len // TILE
    pt_ref, q_ref, gd_ref, newkv_ref, newwin_ref = refs[:5]
    pages = refs[5:5 + n_pages]
    (win_ref, cmpbt_ref, selb_ref, winb_ref, impt_ref, rsum_ref, gsel_ref, expand_ref, place_ref,
     perm_ref, w1_ref, peb_ref, b1_ref, w2_ref, b2_ref,
     o_ref, winout_ref, s_scr, xp_ref, hid_ref, kc_ref, vc_ref) = refs[5 + n_pages:]
    del pt_ref
    win_len = win_ref.shape[1]
    sel_rows = 2 * KV_WIDTH
    _compress_pages(pages, perm_ref, w1_ref, peb_ref, b1_ref, w2_ref, b2_ref, kc_ref, vc_ref, xp_ref, hid_ref)
    rows_per_g = Q_PER_KV * n_new
    row_g = lax.broadcasted_iota(jnp.int32, (LANE, 1), 0) // rows_per_g

    q = q_ref[...] * SCALE
    qs = jnp.concatenate([q[:, h * HEAD_DIM:(h + 1) * HEAD_DIM] for h in range(N_HEADS)], axis=0)
    qs4 = jnp.concatenate([qs] * KV_HEADS, axis=1)
    col_g = lax.broadcasted_iota(jnp.int32, qs4.shape, 1) // HEAD_DIM
    qbd = jnp.where(row_g == col_g, qs4, 0.0).astype(BF16)

    sig = jax.nn.sigmoid(gd_ref[...])
    t_row = lax.broadcasted_iota(jnp.int32, (n_new, LANE), 0)
    t_lane = lax.broadcasted_iota(jnp.int32, (n_new, LANE), 1) % n_new
    lane_g = lax.broadcasted_iota(jnp.int32, (1, LANE), 1) // rows_per_g
    ones_rows = jnp.ones((2 * SUBLANE, TILE), BF16)

    def gate_row(branch):
        spread = _dot_exact_rhs(sig, gsel_ref[branch])
        return jnp.sum(jnp.where(t_row == t_lane, spread, 0.0), axis=0, keepdims=True)

    def pick(full_t):
        out = jnp.zeros((HEAD_DIM, LANE), F32)
        for g in range(KV_HEADS):
            out = out + jnp.where(lane_g == g, full_t[g * HEAD_DIM:(g + 1) * HEAD_DIM, :], 0.0)
        return out

    kc = kc_ref[...].astype(BF16)
    s_t = _dot_nt(kc, qbd) + cmpbt_ref[...]
    m_t = jnp.max(s_t, axis=0, keepdims=True)
    e_t = jnp.exp(s_t - m_t)
    p_t = e_t / jnp.maximum(jnp.sum(e_t, axis=0, keepdims=True), TINY)
    out_t = gate_row(0) * pick(_dot(vc_ref[...].T.astype(BF16), p_t.astype(BF16)))
    imp = _dot_exact_lhs(impt_ref[...], _dot_exact_rhs(p_t, rsum_ref[...]))
    blk = lax.broadcasted_iota(jnp.int32, imp.shape, 0)
    q_blk = (past_len + lax.broadcasted_iota(jnp.int32, imp.shape, 1) % n_new) // SEL_BLOCK
    chosen_t = jnp.where(_select_blocks(imp, blk, q_blk) == 0.0, 1.0, 0.0)
    chosen = jnp.concatenate([chosen_t, jnp.zeros((LANE - chosen_t.shape[0], LANE), F32)], axis=0).T
    sel_mask = (_dot(chosen.astype(BF16), expand_ref[...]) - 1.0) * (-NEG)

    def attend(tiles, bias_ref, mask):
        m = jnp.full((LANE, 1), -jnp.inf, F32)
        for j, (score, _) in enumerate(tiles):
            s = score() + bias_ref[:, j * TILE:(j + 1) * TILE]
            if mask is not None:
                s = s + mask[:, j * TILE:(j + 1) * TILE]
            s_scr[:, j * TILE:(j + 1) * TILE] = s
            m = jnp.maximum(m, jnp.max(s, axis=1, keepdims=True))
        acc = jnp.zeros((KV_WIDTH + 2 * SUBLANE, LANE), F32)
        for j, (_, vt) in enumerate(tiles):
            p = jnp.exp(s_scr[:, j * TILE:(j + 1) * TILE] - m).astype(BF16)
            acc = acc + _dot_nt(jnp.concatenate([vt(), ones_rows], axis=0), p)
        return pick(acc) / jnp.maximum(acc[KV_WIDTH:KV_WIDTH + 1, :], TINY)

    def stored_tile(ref, row0, lanes):
        return (lambda: _dot(qbd, ref[row0:row0 + KV_WIDTH, lanes].astype(BF16)),
                lambda: ref[row0 + KV_WIDTH:row0 + 2 * KV_WIDTH, lanes].astype(BF16))

    def new_tile(ref):
        rows = jnp.concatenate([ref[...], jnp.zeros((TILE - n_new, ref.shape[1]), F32)], axis=0)
        k = rows[:, 0:KV_WIDTH].astype(BF16)
        vt = rows[:, KV_WIDTH:2 * KV_WIDTH].T.astype(BF16)
        return rows, (lambda: _dot_nt(qbd, k), lambda: vt)

    _, new_kv = new_tile(newkv_ref)
    tiles = [stored_tile(pg, sel_rows, slice(None)) for pg in pages] + [new_kv]
    out_t = out_t + gate_row(1) * attend(tiles, selb_ref, sel_mask)

    new_rows, new_win = new_tile(newwin_ref)
    n_wt = win_len // TILE
    tiles = [stored_tile(win_ref, 0, slice(j * TILE, (j + 1) * TILE)) for j in range(n_wt)] + [new_win]
    out_t = out_t + gate_row(2) * attend(tiles, winb_ref, None)
    out = out_t.T
    o_ref[...] = jnp.concatenate([out[h * n_new:(h + 1) * n_new, :] for h in range(N_HEADS)], axis=1).astype(BF16)

    lane = lax.broadcasted_iota(jnp.int32, (1, LANE), 1)
    hi, mid, lo = _split3(new_rows)
    tail = _dot_tn(hi, place_ref[...]) + _dot_tn(mid, place_ref[...]) + _dot_tn(lo, place_ref[...])
    rolled = [pltpu.roll(win_ref[:, j * TILE:(j + 1) * TILE], TILE - n_new, 1) for j in range(n_wt)]
    for j in range(n_wt):
        nxt = rolled[j + 1] if j + 1 < n_wt else tail
        winout_ref[:, j * TILE:(j + 1) * TILE] = jnp.where(lane < TILE - n_new, rolled[j], nxt)


def _nsa_sample(p3, cache_t, win_t, page_table, consts, cmp_consts):
    b, n_new, _ = p3.shape
    n_pages = page_table.shape[1]
    past_len = n_pages * PAGE_SIZE
    win_len = win_t.shape[2]
    kvblk = lambda slot: (P_KV + slot * KV_WIDTH) // (2 * KV_WIDTH)
    const = lambda arr: pl.BlockSpec(arr.shape, lambda i, pt: (0,) * arr.ndim)
    in_specs = [pl.BlockSpec((None, n_new, ATTN_WIDTH), lambda i, pt: (i, 0, 0)),
                pl.BlockSpec((None, n_new, LANE), lambda i, pt: (i, 0, P_GD // LANE)),
                pl.BlockSpec((None, n_new, 2 * KV_WIDTH), lambda i, pt: (i, 0, kvblk(2))),
                pl.BlockSpec((None, n_new, 2 * KV_WIDTH), lambda i, pt: (i, 0, kvblk(4)))]
    in_specs += [pl.BlockSpec((None, N_CACHE_SLOTS * KV_WIDTH, PAGE_SIZE),
                              functools.partial(lambda j, i, pt: (pt[i, j], 0, 0), j)) for j in range(n_pages)]
    consts = list(consts) + list(cmp_consts)
    in_specs += [pl.BlockSpec((None, 2 * KV_WIDTH, win_len), lambda i, pt: (i, 0, 0))] + [const(a) for a in consts]
    perm_rows = PAGES_PER_PERM * PAGE_SIZE
    return pl.pallas_call(
        functools.partial(_nsa_sample_body, past_len=past_len, n_new=n_new),
        grid_spec=pltpu.PrefetchScalarGridSpec(
            num_scalar_prefetch=1, grid=(b,), in_specs=in_specs,
            out_specs=[pl.BlockSpec((None, n_new, ATTN_WIDTH), lambda i, pt: (i, 0, 0)),
                       pl.BlockSpec((None, 2 * KV_WIDTH, win_len), lambda i, pt: (i, 0, 0))],
            scratch_shapes=[pltpu.VMEM((LANE, past_len + TILE), F32),
                            pltpu.VMEM((n_pages // PAGES_PER_PERM, perm_rows, 2 * KV_WIDTH), BF16),
                            pltpu.VMEM((KV_HEADS // 2, TILE, 4 * CMP_HIDDEN), F32),
                            pltpu.VMEM((TILE, KV_WIDTH), F32),
                            pltpu.VMEM((TILE, KV_WIDTH), F32)]),
        out_shape=[jax.ShapeDtypeStruct((b, n_new, ATTN_WIDTH), BF16),
                   jax.ShapeDtypeStruct((b, 2 * KV_WIDTH, win_len), F32)],
        compiler_params=_cparams("parallel"),
        name="nsa_sample",
    )(page_table, p3, p3, p3, p3, *([cache_t] * n_pages), win_t, *consts)


def _sample_selectors(n_new, n_keys):
    lanes = np.arange(LANE)
    h, t = lanes // n_new, lanes % n_new
    rsum = ((h[:, None] // Q_PER_KV == h[None, :] // Q_PER_KV) & (t[:, None] == t[None, :])).astype(np.float32)
    gsel = np.stack([(lanes[:, None] == br * N_HEADS + h[None, :]).astype(np.float32) for br in range(3)])
    expand = (np.arange(n_keys)[None, :] // SEL_BLOCK == lanes[:, None]).astype(np.float32)
    place = (lanes[None, :] == LANE - n_new + lanes[:, None]).astype(np.float32)
    return tuple(jnp.asarray(a, BF16) for a in (rsum, gsel, expand, place))


def _dot_tn(a, b):
    return lax.dot_general(a, b, (((0,), (0,)), ((), ())), preferred_element_type=F32)


def _ssd_body(*refs, q_len, carry):
    if carry:
        (z_ref, xbc_ref, gd_ref, cw_ref, cb_ref, dtb_ref, alog_ref, dexp_ref, ng_ref, e16_ref,
         y_ref, hout_ref, prev_ref, h_scr) = refs
    else:
        (z_ref, xbc_ref, gd_ref, prev_ref, h0_ref, cw_ref, cb_ref, dtb_ref, alog_ref, dexp_ref, ng_ref, e16_ref,
         y_ref, hout_ref) = refs
    n_seq = TILE // q_len
    x = xbc_ref[...]
    if carry:
        @pl.when(pl.program_id(1) == 0)
        def _():
            prev_ref[...] = jnp.zeros_like(prev_ref)
            h_scr[...] = jnp.zeros_like(h_scr)
    prev = prev_ref[...]
    acc = cb_ref[...] + cw_ref[SSM_CONV - 1:SSM_CONV, :] * x
    for k in range(1, SSM_CONV):
        acc = acc + cw_ref[SSM_CONV - 1 - k:SSM_CONV - k, :] * _shift_rows(x, prev, k, q_len)
    if carry:
        prev_ref[...] = x
    xc = _silu(acc)
    xs = xc[:, 0:SSM_INNER]
    b_all = xc[:, SSM_INNER:SSM_INNER + SSM_GROUPS * SSM_STATE]
    c_all = xc[:, SSM_INNER + SSM_GROUPS * SSM_STATE:]

    col = lax.broadcasted_iota(jnp.int32, (1, LANE), 1)
    dt_cols = (col >= DT_COL) & (col < DT_COL + SSM_HEADS)
    dt = jnp.where(dt_cols, jax.nn.softplus(gd_ref[...] + dtb_ref[...]), 0.0)
    a = dt * jnp.where(dt_cols, -jnp.exp(alog_ref[...]), 0.0)
    ri = lax.broadcasted_iota(jnp.int32, (TILE, TILE), 0)
    ci = lax.broadcasted_iota(jnp.int32, (TILE, TILE), 1)
    same = (ri // q_len) == (ci // q_len)
    causal = same & (ci <= ri)
    acum = _dot_exact_lhs(causal.astype(BF16), a)
    last_sel = (ci == (ri // q_len) * q_len + (q_len - 1)).astype(BF16)
    alast = _dot_exact_lhs(last_sel, acum)
    acum_t = acum.T
    e16 = e16_ref[...]
    xw = xs * _dot_exact_rhs(dt, e16)
    xwe = xw * _dot_exact_rhs(jnp.exp(alast - acum), e16)
    grow = jnp.exp(_dot_exact_rhs(acum, e16))
    rowseq = lax.broadcasted_iota(jnp.int32, (TILE, 1), 0) // q_len

    ys = []
    for g in range(SSM_GROUPS):
        cg = c_all[:, g * SSM_STATE:(g + 1) * SSM_STATE].astype(BF16)
        bg = b_all[:, g * SSM_STATE:(g + 1) * SSM_STATE].astype(BF16)
        cb = _dot_nt(cg, bg)
        gl = slice(g * SSM_HPG * SSM_HEADDIM, (g + 1) * SSM_HPG * SSM_HEADDIM)
        hl = slice(g * SSM_HPG, (g + 1) * SSM_HPG)
        for r in range(SSM_HPG):
            h = g * SSM_HPG + r
            c = DT_COL + h
            seg = acum[:, c:c + 1] - acum_t[c:c + 1, :]
            decay = jnp.where(causal, jnp.exp(jnp.where(causal, seg, 0.0)), 0.0)
            ys.append(_dot((cb * decay).astype(BF16), xw[:, h * SSM_HEADDIM:(h + 1) * SSM_HEADDIM].astype(BF16)))
        xwe_g = xwe[:, gl]
        if carry:
            h_old = h_scr[hl].reshape(SSM_HPG * SSM_HEADDIM, SSM_STATE)
            y_off = _dot_nt(cg, h_old.astype(BF16))
            scale = jnp.concatenate(
                [jnp.broadcast_to(jnp.exp(alast[0:1, DT_COL + g * SSM_HPG + r:DT_COL + g * SSM_HPG + r + 1]),
                                  (SSM_HEADDIM, SSM_STATE)) for r in range(SSM_HPG)], axis=0)
            h_new = h_old * scale + _dot_tn(xwe_g.astype(BF16), bg)
            h_scr[hl] = h_new.reshape(SSM_HPG, SSM_HEADDIM, SSM_STATE)
        else:
            offs = []
            for s in range(n_seq):
                h_old = h0_ref[s, hl].reshape(SSM_HPG * SSM_HEADDIM, SSM_STATE)
                offs.append(_dot_nt(cg, h_old.astype(BF16))[s * q_len:(s + 1) * q_len, :])
                a_row = alast[s * q_len:s * q_len + 1, :]
                scale = jnp.concatenate(
                    [jnp.broadcast_to(jnp.exp(a_row[:, DT_COL + g * SSM_HPG + r:DT_COL + g * SSM_HPG + r + 1]),
                                      (SSM_HEADDIM, SSM_STATE)) for r in range(SSM_HPG)], axis=0)
                x_s = jnp.where(rowseq == s, xwe_g, 0.0).astype(BF16)
                h_new = h_old * scale + _dot_tn(x_s, bg)
                hout_ref[s, hl] = h_new.reshape(SSM_HPG, SSM_HEADDIM, SSM_STATE)
            y_off = jnp.concatenate(offs, axis=0)
        for r in range(SSM_HPG):
            h = g * SSM_HPG + r
            ys[h] = ys[h] + y_off[:, r * SSM_HEADDIM:(r + 1) * SSM_HEADDIM] * grow[:, h * SSM_HEADDIM:(h + 1) * SSM_HEADDIM]
    if carry:
        @pl.when(pl.program_id(1) == pl.num_programs(1) - 1)
        def _():
            hout_ref[...] = h_scr[...]

    y = jnp.concatenate(ys, axis=1) + xs * dexp_ref[...]
    y = y * _silu(z_ref[...])
    gw = SSM_INNER // SSM_GROUPS
    parts = []
    for g in range(SSM_GROUPS):
        blk = y[:, g * gw:(g + 1) * gw]
        parts.append(blk * lax.rsqrt(jnp.mean(blk * blk, axis=-1, keepdims=True) + EPS))
    y_ref[...] = (jnp.concatenate(parts, axis=1) * ng_ref[...]).astype(BF16)


def _ssd(p3, prev, h0, cw, cb, dtb, alog, dexp, ng, e16, *, q_len):
    n_grp, rows, _ = p3.shape
    carry = prev is None
    n_chunks = rows // TILE
    n_seq = TILE // q_len
    zspec = pl.BlockSpec((None, TILE, SSM_INNER), lambda i, c: (i, c, P_Z // SSM_INNER))
    xspec = pl.BlockSpec((None, TILE, CONV_DIM), lambda i, c: (i, c, P_XBC // CONV_DIM))
    gspec = pl.BlockSpec((None, TILE, LANE), lambda i, c: (i, c, P_GD // LANE))
    const = lambda arr: pl.BlockSpec(arr.shape, lambda i, c: (0,) * arr.ndim)
    params = [cw, cb, dtb, alog, dexp, ng, e16]
    in_specs = [zspec, xspec, gspec]
    args = [p3, p3, p3]
    if carry:
        n_out_seq = n_grp
        hspec = pl.BlockSpec((None, SSM_HEADS, SSM_HEADDIM, SSM_STATE), lambda i, c: (i, 0, 0, 0))
        scratch = [pltpu.VMEM((TILE, CONV_DIM), F32), pltpu.VMEM((SSM_HEADS, SSM_HEADDIM, SSM_STATE), F32)]
    else:
        n_out_seq = n_grp * n_chunks * n_seq
        hspec = pl.BlockSpec((n_seq, SSM_HEADS, SSM_HEADDIM, SSM_STATE), lambda i, c: (i * n_chunks + c, 0, 0, 0))
        in_specs += [pl.BlockSpec((None, TILE, CONV_DIM), lambda i, c: (i, c, 0)), hspec]
        args += [prev, h0]
        scratch = []
    in_specs += [const(a) for a in params]
    return pl.pallas_call(
        functools.partial(_ssd_body, q_len=q_len, carry=carry),
        grid=(n_grp, n_chunks),
        in_specs=in_specs,
        out_specs=[pl.BlockSpec((None, TILE, SSM_INNER), lambda i, c: (i, c, 0)), hspec],
        out_shape=[jax.ShapeDtypeStruct((n_grp, rows, SSM_INNER), BF16),
                   jax.ShapeDtypeStruct((n_out_seq, SSM_HEADS, SSM_HEADDIM, SSM_STATE), F32)],
        scratch_shapes=scratch,
        compiler_params=_cparams("parallel", "arbitrary"),
        name="ssd",
    )(*args, *params)


def _importance_map_t(n_sel):
    n_cmp = TILE - 1
    c0 = np.arange(n_cmp) * CMP_STRIDE
    s0 = np.arange(n_sel) * SEL_BLOCK
    ov = np.minimum(c0[None, :] + CMP_BLOCK, s0[:, None] + SEL_BLOCK) - np.maximum(c0[None, :], s0[:, None])
    m = np.zeros((-(-n_sel // SUBLANE) * SUBLANE, TILE), np.float32)
    m[:n_sel, :n_cmp] = np.clip(ov, 0, None).astype(np.float32) / CMP_BLOCK
    return jnp.asarray(m, BF16)


def _dt_expand():
    e = np.zeros((LANE, SSM_INNER), np.float32)
    for h in range(SSM_HEADS):
        e[DT_COL + h, h * SSM_HEADDIM:(h + 1) * SSM_HEADDIM] = 1.0
    return jnp.asarray(e, BF16)


def _pad_to(x, axis, size):
    pad = [(0, 0)] * x.ndim
    pad[axis] = (0, size - x.shape[axis])
    return jnp.pad(x, pad)


def _history_rows(state, n_rows, width):
    b, k, _ = state.shape
    tile = jnp.pad(state, ((0, 0), (n_rows - k, 0), (0, width - state.shape[2])))
    return tile.reshape(b * n_rows, width)


def kernel(x_prompt, x_sample, cache_kv, cache_win_kv, state_ssm_conv, state_ssm, state_ffn_conv, page_table, rel_table, ln_mix_pre, w_in, cmp_pe, cmp_w1, cmp_b1, cmp_w2, cmp_b2, ssm_conv_w, ssm_conv_b, ssm_dt_bias, ssm_a_log, ssm_d, ssm_norm_g, w_out, ln_mix_post, ln_ffn_pre, ffn_w_gate, ffn_w_up, ffn_dw_w, ffn_dw_b, ffn_w_down, ln_ffn_post):
    bp, tp, d = x_prompt.shape
    bs, ts, _ = x_sample.shape
    n_pool = cache_kv.shape[1]
    n_pages = page_table.shape[1]
    past_len = n_pages * PAGE_SIZE
    win_len = cache_win_kv.shape[2]
    assert d == D_MODEL and tp == N_CHUNK_TILES * TILE and past_len == N_CHUNK_TILES * TILE
    assert N_HEADS * ts == LANE and ts == SUBLANE and tp >= WINDOW and win_len == WINDOW

    rel = rel_table.astype(F32)
    tab_p = jnp.repeat(rel.reshape(REL_BUCKETS, KV_HEADS, Q_PER_KV).transpose(1, 0, 2), TILE, axis=-1)
    tab_s = jnp.repeat(rel, ts, axis=1)
    cmpb_p, near, edge = _bias_prompt(tab_p, tp // TILE)
    cmpbt_s, selb, winb = _bias_sample(tab_s, past_len, ts, win_len)
    impt_p = _importance_map_t(-(-tp // SEL_BLOCK))
    impt_s = _importance_map_t(-(-(past_len + ts) // SEL_BLOCK))
    rsum, gsel, expand, place = _sample_selectors(ts, past_len + TILE)
    e16 = _dt_expand()
    dummy_pt = jnp.zeros((1, 1), jnp.int32)

    xp = x_prompt.reshape(bp * tp, d)
    xs = x_sample.reshape(bs * ts, d)
    outs = [[] for _ in range(10)]
    for l in range(w_in.shape[0]):
        o1 = ATTN_WIDTH
        o2 = o1 + N_KV_SLOTS * KV_WIDTH
        o3 = o2 + 3 * N_HEADS
        o4 = o3 + SSM_INNER
        o5 = o4 + CONV_DIM
        w = w_in[l].T
        wp = jnp.concatenate([w[:o1], w[o3:o4], w[o4:o5], w[o1:o2], w[o2:o3], w[o5:],
                              jnp.zeros((P_WIDTH - w.shape[0], d), w.dtype)], axis=0).astype(BF16)
        g_pre = ln_mix_pre[l].reshape(1, d)
        cw = _compress_weights(cmp_pe[l], cmp_w1[l], cmp_b1[l], cmp_w2[l], cmp_b2[l])
        ssm_w = (_pad_to(ssm_conv_w[l], 0, SUBLANE), ssm_conv_b[l].reshape(1, CONV_DIM),
                 _pad_to(jnp.pad(ssm_dt_bias[l].astype(F32), (DT_COL, 0)), 0, LANE).reshape(1, LANE),
                 _pad_to(jnp.pad(ssm_a_log[l].astype(F32), (DT_COL, 0)), 0, LANE).reshape(1, LANE),
                 jnp.repeat(ssm_d[l].astype(F32), SSM_HEADDIM).reshape(1, SSM_INNER),
                 ssm_norm_g[l].astype(F32).reshape(1, SSM_INNER), e16)
        w_attn = w_out[l, :ATTN_WIDTH].astype(BF16)
        w_ssm = w_out[l, ATTN_WIDTH:].astype(BF16)
        g_post = ln_mix_post[l].reshape(1, d)
        g_pre2 = ln_ffn_pre[l].reshape(1, d)
        wg = _cast_pad_cols(ffn_w_gate[l], FF_PAD)
        wu = _cast_pad_cols(ffn_w_up[l], FF_PAD)
        dw = _pad_to(_pad_to(ffn_dw_w[l], 1, FF_PAD), 0, SUBLANE)
        db = _pad_to(ffn_dw_b[l].reshape(1, D_FF), 1, FF_PAD)
        wd = _cast_pad_rows(ffn_w_down[l], FF_PAD)
        g_post2 = ln_ffn_post[l].reshape(1, d)

        p3 = _proj(xp, g_pre, wp, tm=1024).reshape(bp, tp, P_WIDTH)
        kc, vc = _compress(p3, lambda j, cb, b, pt: (b, j, P_KV // LANE + cb), bp, dummy_pt, *cw)
        attn = _nsa_prompt(p3, kc, vc, cmpb_p, near, edge, tab_p, impt_p)
        ssm, h_new = _ssd(p3, None, None, *ssm_w, q_len=TILE)
        x1, hn = _outproj(attn.reshape(bp * tp, ATTN_WIDTH), ssm.reshape(bp * tp, SSM_INNER), xp,
                          w_attn, w_ssm, g_post, g_pre2)
        act, tail = _ffn_up(hn, wg, wu, dw, db, None, seq_len=tp, tm=tp)
        xp = _ffn_down(act, wd, x1, g_post2)
        kv = p3[:, :, P_KV:P_KV + N_KV_SLOTS * KV_WIDTH]
        outs[0].append(kv[:, :, :N_CACHE_SLOTS * KV_WIDTH].reshape(bp, tp, N_CACHE_SLOTS, KV_HEADS, HEAD_DIM))
        outs[2].append(kv[:, tp - WINDOW:, N_CACHE_SLOTS * KV_WIDTH:].reshape(bp, WINDOW, 2, KV_HEADS, HEAD_DIM))
        outs[4].append(p3[:, tp - (SSM_CONV - 1):, P_XBC:P_XBC + CONV_DIM])
        outs[6].append(h_new)
        outs[8].append(tail[:, SUBLANE - (FFN_CONV - 1):, :D_FF])

        p3 = _proj(xs, g_pre, wp, tm=bs * ts).reshape(bs, ts, P_WIDTH)
        cache_t = cache_kv[l].transpose(0, 2, 3, 4, 1).reshape(n_pool, N_CACHE_SLOTS * KV_WIDTH, PAGE_SIZE)
        win_t = cache_win_kv[l].transpose(0, 2, 3, 4, 1).reshape(bs, 2 * KV_WIDTH, win_len)
        w1p, perm = _compress_paged_weights(cmp_w1[l])
        attn, win_new = _nsa_sample(p3, cache_t, win_t, page_table,
                                    (cmpbt_s, selb, winb, impt_s, rsum, gsel, expand, place),
                                    (perm, w1p, _pe_bias(cw[0], w1p), cw[2], cw[3], cw[4]))
        win_new = win_new.reshape(bs, 2, KV_HEADS, HEAD_DIM, win_len).transpose(0, 4, 1, 2, 3)
        prev = _history_rows(state_ssm_conv[l], ts, CONV_DIM).reshape(1, bs * ts, CONV_DIM)
        ssm, h_new = _ssd(p3.reshape(1, bs * ts, P_WIDTH), prev, state_ssm[l], *ssm_w, q_len=ts)
        x1, hn = _outproj(attn.reshape(bs * ts, ATTN_WIDTH), ssm.reshape(bs * ts, SSM_INNER), xs,
                          w_attn, w_ssm, g_post, g_pre2)
        prev_f = _history_rows(state_ffn_conv[l], ts, FF_PAD)
        act, gp = _ffn_up(hn, wg, wu, dw, db, prev_f, seq_len=ts, tm=bs * ts)
        xs = _ffn_down(act, wd, x1, g_post2, tm=bs * ts)
        kv = p3[:, :, P_KV:P_KV + N_KV_SLOTS * KV_WIDTH]
        outs[1].append(kv[:, :, :N_CACHE_SLOTS * KV_WIDTH].reshape(bs, ts, N_CACHE_SLOTS, KV_HEADS, HEAD_DIM))
        outs[3].append(win_new)
        outs[5].append(p3[:, ts - (SSM_CONV - 1):, P_XBC:P_XBC + CONV_DIM])
        outs[7].append(h_new)
        outs[9].append(gp.reshape(bs, ts, FF_PAD)[:, ts - (FFN_CONV - 1):, :D_FF])
    return (xp.reshape(bp, tp, d), xs.reshape(bs, ts, d)) + tuple(jnp.stack(o) for o in outs)
```

```python
import functools
import math

import numpy as np
import jax
import jax.numpy as jnp
from jax import lax
from jax.experimental import pallas as pl
from jax.experimental.pallas import tpu as pltpu

F32 = jnp.float32
BF16 = jnp.bfloat16

D_MODEL = 2048
PAGE_SIZE = 128
HEAD_DIM = 64
ATTN_WIDTH = D_MODEL // 2
N_HEADS = ATTN_WIDTH // HEAD_DIM
KV_HEADS = 4
Q_PER_KV = N_HEADS // KV_HEADS
CMP_STRIDE = 16
CMP_BLOCK = 2 * CMP_STRIDE
CMP_HIDDEN = 2 * HEAD_DIM
SEL_BLOCK = 64
N_SELECT = 8
WINDOW = 512
N_KV_SLOTS = 6
N_CACHE_SLOTS = 4
KV_WIDTH = KV_HEADS * HEAD_DIM
SSM_INNER = D_MODEL - ATTN_WIDTH
SSM_HEADDIM = 64
SSM_HEADS = SSM_INNER // SSM_HEADDIM
SSM_GROUPS = 4
SSM_HPG = SSM_HEADS // SSM_GROUPS
SSM_STATE = 128
SSM_CONV = 4
CONV_DIM = SSM_INNER + 2 * SSM_GROUPS * SSM_STATE
D_FF = ((8 * D_MODEL // 3 + 127) // 128) * 128
FFN_CONV = 3
REL_BUCKETS = 32
REL_MAX_DIST = 128
EPS = 1e-6
NEG = -1e30
TINY = 1e-30
FORCE_BONUS = 1e4
SCALE = HEAD_DIM ** -0.5
LOG2E = math.log2(math.e)

LANE = 128
SUBLANE = 8
TILE = 128
VMEM_LIMIT = 56 * 1024 * 1024

P_Q = 0
P_Z = ATTN_WIDTH
P_XBC = P_Z + SSM_INNER
P_KV = P_XBC + CONV_DIM
P_GD = P_KV + N_KV_SLOTS * KV_WIDTH
GD_WIDTH = 3 * N_HEADS + SSM_HEADS
P_WIDTH = 6144
DT_COL = 3 * N_HEADS
FF_PAD = 5632
FFN_SUB = 256


def _cparams(*sem):
    return pltpu.CompilerParams(dimension_semantics=sem, vmem_limit_bytes=VMEM_LIMIT)


def _rms(x, g):
    return x * lax.rsqrt(jnp.mean(x * x, axis=-1, keepdims=True) + EPS) * g


def _split3(x):
    hi = x.astype(BF16)
    r1 = x - hi.astype(F32)
    mid = r1.astype(BF16)
    lo = (r1 - mid.astype(F32)).astype(BF16)
    return hi, mid, lo


def _dot(a, b):
    return jnp.dot(a, b, preferred_element_type=F32)


def _dot_nt(a, b):
    return lax.dot_general(a, b, (((1,), (1,)), ((), ())), preferred_element_type=F32)


def _dot_exact_rhs(x, sel):
    hi, mid, lo = _split3(x)
    return _dot(hi, sel) + _dot(mid, sel) + _dot(lo, sel)


def _dot_exact_lhs(sel, x):
    hi, mid, lo = _split3(x)
    return _dot(sel, hi) + _dot(sel, mid) + _dot(sel, lo)


def _gelu_tanh(x):
    return 0.5 * x * (1.0 + jnp.tanh(math.sqrt(2.0 / math.pi) * (x + 0.044715 * (x * x * x))))


def _silu(x):
    return x * jax.nn.sigmoid(x)


CAST_BLOCK = 256


def _cast_pad_cols_body(w_ref, o_ref):
    n = w_ref.shape[1]
    o_ref[:, 0:n] = w_ref[...].astype(BF16)
    o_ref[:, n:] = jnp.zeros((o_ref.shape[0], o_ref.shape[1] - n), BF16)


def _cast_pad_cols(w, n_out):
    m, n = w.shape
    return pl.pallas_call(
        _cast_pad_cols_body,
        grid=(m // CAST_BLOCK,),
        in_specs=[pl.BlockSpec((CAST_BLOCK, n), lambda i: (i, 0))],
        out_specs=pl.BlockSpec((CAST_BLOCK, n_out), lambda i: (i, 0)),
        out_shape=jax.ShapeDtypeStruct((m, n_out), BF16),
        compiler_params=_cparams("parallel"),
        name="cast_pad_cols",
    )(w)


def _cast_pad_rows_body(w_ref, o_ref):
    m = w_ref.shape[0]
    o_ref[0:m, :] = w_ref[...].astype(BF16)
    o_ref[m:, :] = jnp.zeros((o_ref.shape[0] - m, o_ref.shape[1]), BF16)


def _cast_pad_rows(w, m_out):
    m, n = w.shape
    return pl.pallas_call(
        _cast_pad_rows_body,
        grid=(n // CAST_BLOCK,),
        in_specs=[pl.BlockSpec((m, CAST_BLOCK), lambda j: (0, j))],
        out_specs=pl.BlockSpec((m_out, CAST_BLOCK), lambda j: (0, j)),
        out_shape=jax.ShapeDtypeStruct((m_out, n), BF16),
        compiler_params=_cparams("parallel"),
        name="cast_pad_rows",
    )(w)


def _proj_body(x_ref, g_ref, w_ref, o_ref, xn_ref):
    @pl.when(pl.program_id(1) == 0)
    def _():
        xn_ref[...] = _rms(x_ref[...], g_ref[...]).astype(BF16)

    o_ref[...] = _dot_nt(xn_ref[...], w_ref[...])


def _proj(x2d, g, w_t, tm, tn=1024):
    m, k = x2d.shape
    n = w_t.shape[0]
    w = w_t
    return pl.pallas_call(
        _proj_body,
        grid=(m // tm, n // tn),
        in_specs=[pl.BlockSpec((tm, k), lambda i, j: (i, 0)),
                  pl.BlockSpec((1, k), lambda i, j: (0, 0)),
                  pl.BlockSpec((tn, k), lambda i, j: (j, 0))],
        out_specs=pl.BlockSpec((tm, tn), lambda i, j: (i, j)),
        out_shape=jax.ShapeDtypeStruct((m, n), F32),
        scratch_shapes=[pltpu.VMEM((tm, k), BF16)],
        compiler_params=_cparams("parallel", "arbitrary"),
        name="proj",
    )(x2d, g, w)


NORM_ROWS = 128


def _outproj_body(a_ref, s_ref, x_ref, wa_ref, ws_ref, g1_ref, g2_ref, x1_ref, hn_ref):
    def project(rs):
        return _dot(a_ref[rs, :], wa_ref[...]) + _dot(s_ref[rs, :], ws_ref[...])

    def finish(rs, mix):
        x1 = x_ref[rs, :] + _rms(mix, g1_ref[...])
        x1_ref[rs, :] = x1
        hn_ref[rs, :] = _rms(x1, g2_ref[...]).astype(BF16)

    tm = x_ref.shape[0]
    rows = min(NORM_ROWS, tm)
    pending = None
    for r in range(tm // rows):
        rs = slice(r * rows, (r + 1) * rows)
        cur = (rs, project(rs))
        if pending is not None:
            finish(*pending)
        pending = cur
    finish(*pending)


def _outproj(attn, ssm, x2d, w_attn, w_ssm, g_post, g_pre2, tm=512):
    m, d = x2d.shape
    ka = attn.shape[1]
    const = lambda i: (0, 0)
    return pl.pallas_call(
        _outproj_body,
        grid=(m // tm,),
        in_specs=[pl.BlockSpec((tm, ka), lambda i: (i, 0)),
                  pl.BlockSpec((tm, ka), lambda i: (i, 0)),
                  pl.BlockSpec((tm, d), lambda i: (i, 0)),
                  pl.BlockSpec((ka, d), const),
                  pl.BlockSpec((ka, d), const),
                  pl.BlockSpec((1, d), const),
                  pl.BlockSpec((1, d), const)],
        out_specs=[pl.BlockSpec((tm, d), lambda i: (i, 0)),
                   pl.BlockSpec((tm, d), lambda i: (i, 0))],
        out_shape=[jax.ShapeDtypeStruct((m, d), F32), jax.ShapeDtypeStruct((m, d), BF16)],
        compiler_params=_cparams("parallel"),
        name="outproj",
    )(attn, ssm, x2d, w_attn, w_ssm, g_post, g_pre2)


def _shift_rows(x, prev, k, seq_len):
    rows, cols = x.shape
    if seq_len >= rows:
        t = lax.broadcasted_iota(jnp.int32, (rows, 1), 0)
        return jnp.where(t >= k, pltpu.roll(x, k, 0), 0.0 if prev is None else pltpu.roll(prev, k, 0))
    assert seq_len == SUBLANE and prev is not None
    shape = (rows // seq_len, seq_len, cols)
    t = lax.broadcasted_iota(jnp.int32, shape, 1)
    out = jnp.where(t >= k, pltpu.roll(x.reshape(shape), k, 1), pltpu.roll(prev.reshape(shape), k, 1))
    return out.reshape(rows, cols)


def _ffn_up_body(*refs, seq_len, has_prev):
    if has_prev:
        h_ref, wg_ref, wu_ref, dw_ref, db_ref, prev_ref, act_ref, tail_ref = refs
    else:
        h_ref, wg_ref, wu_ref, dw_ref, db_ref, act_ref, tail_ref = refs
    tm, tn = act_ref.shape
    h = h_ref[...]

    def project(cs):
        return _dot(h, wg_ref[:, cs]), _dot(h, wu_ref[:, cs])

    def gate(cs, gp, up):
        prev = prev_ref[:, cs] if has_prev else None
        acc = db_ref[:, cs] + dw_ref[FFN_CONV - 1:FFN_CONV, cs] * gp
        for k in range(1, FFN_CONV):
            acc = acc + dw_ref[FFN_CONV - 1 - k:FFN_CONV - k, cs] * _shift_rows(gp, prev, k, seq_len)
        act_ref[:, cs] = (_gelu_tanh(acc) * up).astype(BF16)
        tail_ref[:, cs] = gp if has_prev else gp[tm - SUBLANE:, :]

    pending = None
    for c in range(tn // FFN_SUB):
        cs = slice(c * FFN_SUB, (c + 1) * FFN_SUB)
        cur = (cs,) + project(cs)
        if pending is not None:
            gate(*pending)
        pending = cur
    gate(*pending)


def _ffn_up(hn, wg, wu, dw, db, prev, *, seq_len, tm, tn=512):
    m, d = hn.shape
    ni = m // tm
    nj = FF_PAD // tn
    has_prev = prev is not None
    in_specs = [pl.BlockSpec((tm, d), lambda i, j: (i, 0)),
                pl.BlockSpec((d, tn), lambda i, j: (0, j)),
                pl.BlockSpec((d, tn), lambda i, j: (0, j)),
                pl.BlockSpec((SUBLANE, tn), lambda i, j: (0, j)),
                pl.BlockSpec((1, tn), lambda i, j: (0, j))]
    args = [hn, wg, wu, dw, db]
    if has_prev:
        in_specs.append(pl.BlockSpec((tm, tn), lambda i, j: (i, j)))
        args.append(prev)
        tail_shape = jax.ShapeDtypeStruct((m, FF_PAD), F32)
        tail_spec = pl.BlockSpec((tm, tn), lambda i, j: (i, j))
    else:
        tail_shape = jax.ShapeDtypeStruct((ni, SUBLANE, FF_PAD), F32)
        tail_spec = pl.BlockSpec((None, SUBLANE, tn), lambda i, j: (i, 0, j))
    return pl.pallas_call(
        functools.partial(_ffn_up_body, seq_len=seq_len, has_prev=has_prev),
        grid=(ni, nj),
        in_specs=in_specs,
        out_specs=[pl.BlockSpec((tm, tn), lambda i, j: (i, j)), tail_spec],
        out_shape=[jax.ShapeDtypeStruct((m, FF_PAD), BF16), tail_shape],
        compiler_params=_cparams("parallel", "parallel"),
        name="ffn_up",
    )(*args)


def _ffn_down_body(a_ref, w_ref, x_ref, g_ref, o_ref, acc_ref):
    k = pl.program_id(1)
    last = pl.num_programs(1) - 1

    @pl.when(k == 0)
    def _():
        acc_ref[...] = _dot(a_ref[...], w_ref[...])

    @pl.when((k > 0) & (k < last))
    def _():
        acc_ref[...] += _dot(a_ref[...], w_ref[...])

    @pl.when(k == last)
    def _():
        tm = acc_ref.shape[0]
        rows = min(NORM_ROWS, tm)
        pending = None
        for r in range(tm // rows):
            rs = slice(r * rows, (r + 1) * rows)
            cur = (rs, acc_ref[rs, :] + _dot(a_ref[rs, :], w_ref[...]))
            if pending is not None:
                o_ref[pending[0], :] = x_ref[pending[0], :] + _rms(pending[1], g_ref[...])
            pending = cur
        o_ref[pending[0], :] = x_ref[pending[0], :] + _rms(pending[1], g_ref[...])


def _ffn_down(act, wd, x1, g, tm=1024, tk=512):
    m, d = x1.shape
    return pl.pallas_call(
        _ffn_down_body,
        grid=(m // tm, FF_PAD // tk),
        in_specs=[pl.BlockSpec((tm, tk), lambda i, k: (i, k)),
                  pl.BlockSpec((tk, d), lambda i, k: (k, 0)),
                  pl.BlockSpec((tm, d), lambda i, k: (i, 0)),
                  pl.BlockSpec((1, d), lambda i, k: (0, 0))],
        out_specs=pl.BlockSpec((tm, d), lambda i, k: (i, 0)),
        out_shape=jax.ShapeDtypeStruct((m, d), F32),
        scratch_shapes=[pltpu.VMEM((tm, d), F32)],
        compiler_params=_cparams("parallel", "arbitrary"),
        name="ffn_down",
    )(act, wd, x1, g)


N_CHUNK_TILES = 16
CHUNKS_PER_TILE = TILE // CMP_STRIDE
CMP_COL_BLOCKS = 2 * KV_WIDTH // LANE


def _compress_body(*refs):
    pt_ref = refs[0]
    del pt_ref
    n_in = N_CHUNK_TILES * CMP_COL_BLOCKS
    tiles = refs[1:1 + n_in]
    pe_ref, w1_ref, b1_ref, w2_ref, b2_ref, kc_ref, vc_ref, lhs_a, lhs_b = refs[1 + n_in:]
    out_refs = (kc_ref, vc_ref)
    for slot in range(2):
        outs = []
        for gp in range(KV_HEADS // 2):
            cb = slot * (KV_HEADS // 2) + gp
            for s in range(CMP_STRIDE):
                x = jnp.concatenate(
                    [tiles[j * CMP_COL_BLOCKS + cb][pl.ds(s, CHUNKS_PER_TILE, stride=CMP_STRIDE), :]
                     for j in range(N_CHUNK_TILES)], axis=0)
                lhs_a[:, s * LANE:(s + 1) * LANE] = (x + pe_ref[slot, 0, s:s + 1, :]).astype(BF16)
                lhs_b[:, s * LANE:(s + 1) * LANE] = (x + pe_ref[slot, 1, s:s + 1, :]).astype(BF16)
            first = _dot(lhs_a[...], w1_ref[slot, 0])
            second = _dot(lhs_b[...], w1_ref[slot, 1])
            hid = first + pltpu.roll(second, TILE - 1, 0) + b1_ref[slot]
            act = _gelu_tanh(hid).astype(BF16)
            for gj in range(2):
                outs.append(_dot(act[:, gj * CMP_HIDDEN:(gj + 1) * CMP_HIDDEN], w2_ref[slot]) + b2_ref[slot])
        out_refs[slot][...] = jnp.concatenate(outs, axis=1)


def _compress(src, tile_map, n_seq, page_table, pe, w1, b1, w2, b2):
    in_specs = [pl.BlockSpec((None, TILE, LANE), functools.partial(tile_map, j, cb))
                for j in range(N_CHUNK_TILES) for cb in range(CMP_COL_BLOCKS)]
    full = lambda shape: pl.BlockSpec(shape, lambda b, pt: (0,) * len(shape))
    in_specs += [full(pe.shape), full(w1.shape), full(b1.shape), full(w2.shape), full(b2.shape)]
    out_spec = pl.BlockSpec((None, TILE, KV_WIDTH), lambda b, pt: (b, 0, 0))
    return pl.pallas_call(
        _compress_body,
        grid_spec=pltpu.PrefetchScalarGridSpec(
            num_scalar_prefetch=1, grid=(n_seq,), in_specs=in_specs, out_specs=[out_spec, out_spec],
            scratch_shapes=[pltpu.VMEM((TILE, CMP_STRIDE * LANE), BF16), pltpu.VMEM((TILE, CMP_STRIDE * LANE), BF16)]),
        out_shape=[jax.ShapeDtypeStruct((n_seq, TILE, KV_WIDTH), F32)] * 2,
        compiler_params=_cparams("parallel"),
        name="compress",
    )(page_table, *([src] * (N_CHUNK_TILES * CMP_COL_BLOCKS)), pe, w1, b1, w2, b2)


def _compress_weights(cmp_pe, cmp_w1, cmp_b1, cmp_w2, cmp_b2):
    pe = cmp_pe.reshape(2, 2, CMP_STRIDE, HEAD_DIM)
    pe = jnp.concatenate([pe, pe], axis=-1)
    w = cmp_w1.reshape(2, 2, CMP_STRIDE, HEAD_DIM, CMP_HIDDEN)
    eye = jnp.eye(2, dtype=w.dtype)
    w1 = jnp.einsum('absdh,ij->absidjh', w, eye).reshape(2, 2, CMP_STRIDE * LANE, 2 * CMP_HIDDEN).astype(BF16)
    b1 = jnp.concatenate([cmp_b1, cmp_b1], axis=-1).reshape(2, 1, 2 * CMP_HIDDEN)
    return pe, w1, b1, cmp_w2.astype(BF16), cmp_b2.reshape(2, 1, HEAD_DIM)


PAGES_PER_PERM = 2


def _pe_bias_body(pe_ref, w1_ref, o_ref):
    for slot in range(2):
        cols = []
        for half in range(2):
            acc = jnp.zeros((2 * SUBLANE, 2 * CMP_HIDDEN), F32)
            for s in range(CMP_STRIDE):
                row = jnp.broadcast_to(pe_ref[slot, half, s:s + 1, :], (2 * SUBLANE, LANE))
                w = w1_ref[slot, s * LANE:(s + 1) * LANE, half * 2 * CMP_HIDDEN:(half + 1) * 2 * CMP_HIDDEN]
                acc = acc + _dot_exact_rhs(row, w)
            cols.append(acc)
        o_ref[slot] = jnp.concatenate(cols, axis=1)


def _pe_bias(pe, w1):
    return pl.pallas_call(
        _pe_bias_body,
        out_shape=jax.ShapeDtypeStruct((2, 2 * SUBLANE, 4 * CMP_HIDDEN), F32),
        name="pe_bias",
    )(pe, w1)


def _permute_pages(pages, perm_ref, xp_ref):
    for jp in range(N_CHUNK_TILES // PAGES_PER_PERM):
        xt = jnp.concatenate([pages[PAGES_PER_PERM * jp + jj][0:2 * KV_WIDTH, :] for jj in range(PAGES_PER_PERM)],
                             axis=1)
        xp_ref[jp] = _dot_nt(perm_ref[...], xt.astype(BF16)).astype(BF16)


def _compress_mlp(xp_ref, w1_ref, peb_ref, b1_ref, w2_ref, b2_ref, kc_ref, vc_ref, hid_ref):
    n_perm = N_CHUNK_TILES // PAGES_PER_PERM
    slab = PAGES_PER_PERM * CHUNKS_PER_TILE
    out_refs = (kc_ref, vc_ref)

    def first_layer(slot, gp):
        cb = slot * (KV_HEADS // 2) + gp
        lhs = jnp.concatenate(
            [jnp.concatenate([xp_ref[jp, s * slab:(s + 1) * slab, cb * LANE:(cb + 1) * LANE]
                              for jp in range(n_perm)], axis=0)
             for s in range(CMP_STRIDE)], axis=1)
        hid_ref[gp] = _dot(lhs, w1_ref[slot])
        return gp

    def second_layer(slot, gp, buf):
        acc = hid_ref[buf] + peb_ref[slot, 0:1, :]
        hid = (acc[:, 0:2 * CMP_HIDDEN] + pltpu.roll(acc[:, 2 * CMP_HIDDEN:], TILE - 1, 0) + b1_ref[slot])
        act = _gelu_tanh(hid).astype(BF16)
        for gj in range(2):
            g = 2 * gp + gj
            out_refs[slot][:, g * HEAD_DIM:(g + 1) * HEAD_DIM] = (
                _dot(act[:, gj * CMP_HIDDEN:(gj + 1) * CMP_HIDDEN], w2_ref[slot]) + b2_ref[slot])

    pending = None
    for slot in range(2):
        for gp in range(KV_HEADS // 2):
            acc = first_layer(slot, gp)
            if pending is not None:
                second_layer(*pending)
            pending = (slot, gp, acc)
    second_layer(*pending)


def _compress_paged_weights(cmp_w1):
    w = cmp_w1.reshape(2, 2, CMP_STRIDE, HEAD_DIM, CMP_HIDDEN)
    eye = jnp.eye(2, dtype=w.dtype)
    w1 = jnp.einsum('absdh,ij->asidbjh', w, eye).reshape(2, CMP_STRIDE * LANE, 4 * CMP_HIDDEN).astype(BF16)
    rows = PAGES_PER_PERM * PAGE_SIZE
    r = np.arange(rows)
    s, page, chunk = r // (PAGES_PER_PERM * CHUNKS_PER_TILE), (r // CHUNKS_PER_TILE) % PAGES_PER_PERM, r % CHUNKS_PER_TILE
    src = page * PAGE_SIZE + chunk * CMP_STRIDE + s
    perm = (np.arange(rows)[None, :] == src[:, None]).astype(np.float32)
    return w1, jnp.asarray(perm, BF16)


def _bucket_thresholds():
    n = np.arange(0, 2 * REL_MAX_DIST)
    max_exact = REL_BUCKETS // 2
    nf = np.maximum(n, 1).astype(np.float32)
    large = max_exact + (np.log(nf / np.float32(max_exact)) / np.float32(math.log(REL_MAX_DIST / max_exact))
                         * np.float32(REL_BUCKETS - max_exact)).astype(np.int32)
    bucket = np.where(n < max_exact, n, np.minimum(large, REL_BUCKETS - 1))
    assert (np.diff(bucket) >= 0).all() and (np.diff(bucket) <= 1).all()
    return [int(np.argmax(bucket >= k)) for k in range(1, REL_BUCKETS)]


BUCKET_START = _bucket_thresholds()
CMP_BAND = 2 * SUBLANE
assert (BUCKET_START[-1] - 1 + CMP_BLOCK - 1) // CMP_STRIDE <= SUBLANE
assert (TILE - CMP_BLOCK) // CMP_STRIDE < SUBLANE


def _rel_bias(dist, tab_ref):
    bias = jnp.broadcast_to(tab_ref[0:1, :], dist.shape)
    for k in range(1, REL_BUCKETS):
        bias = jnp.where(dist >= BUCKET_START[k - 1], tab_ref[k:k + 1, :], bias)
    return bias


def _bias_prompt_body(tab_ref, cmpb_ref, near_ref, edge_ref):
    qb = pl.program_id(0)
    shape = (TILE, Q_PER_KV * TILE)
    row = lax.broadcasted_iota(jnp.int32, shape, 0)
    qi = lax.broadcasted_iota(jnp.int32, shape, 1) % TILE
    dist_c = qb * TILE + qi - (row * CMP_STRIDE + CMP_BLOCK - 1)
    band0 = pl.multiple_of(jnp.maximum(qb * (TILE // CMP_STRIDE) - SUBLANE, 0), SUBLANE)
    band_shape = (CMP_BAND, Q_PER_KV * TILE)
    band_row = band0 + lax.broadcasted_iota(jnp.int32, band_shape, 0)
    band_qi = lax.broadcasted_iota(jnp.int32, band_shape, 1) % TILE
    dist_b = qb * TILE + band_qi - (band_row * CMP_STRIDE + CMP_BLOCK - 1)
    for g in range(KV_HEADS):
        cmpb_ref[g] = jnp.where(dist_c >= 0, tab_ref[g, REL_BUCKETS - 1:REL_BUCKETS, :], NEG)
        cmpb_ref[g, pl.ds(band0, CMP_BAND), :] = jnp.where(dist_b >= 0, _rel_bias(dist_b, tab_ref.at[g]), NEG)

    @pl.when(qb == 0)
    def _():
        for g in range(KV_HEADS):
            for delta in range(2):
                dist = delta * TILE + qi - row
                b = _rel_bias(dist, tab_ref.at[g])
                near_ref[g, delta] = jnp.where(dist >= 0, b * LOG2E, NEG)
        edge_ref[...] = jnp.where(qi <= row, 0.0, NEG)


def _bias_prompt(tab, n_qb):
    lanes = Q_PER_KV * TILE
    return pl.pallas_call(
        _bias_prompt_body,
        grid=(n_qb,),
        in_specs=[pl.BlockSpec(tab.shape, lambda i: (0, 0, 0))],
        out_specs=[pl.BlockSpec((None, KV_HEADS, TILE, lanes), lambda i: (i, 0, 0, 0)),
                   pl.BlockSpec((KV_HEADS, 2, TILE, lanes), lambda i: (0, 0, 0, 0)),
                   pl.BlockSpec((TILE, lanes), lambda i: (0, 0))],
        out_shape=[jax.ShapeDtypeStruct((n_qb, KV_HEADS, TILE, lanes), F32),
                   jax.ShapeDtypeStruct((KV_HEADS, 2, TILE, lanes), F32),
                   jax.ShapeDtypeStruct((TILE, lanes), F32)],
        compiler_params=_cparams("arbitrary"),
        name="bias_prompt",
    )(tab)


def _bias_sample_body(tab_ref, cmpbt_ref, selb_ref, winb_ref, *, past_len, n_new, win_len):
    qt = lax.broadcasted_iota(jnp.int32, (TILE, LANE), 1) % n_new
    row = lax.broadcasted_iota(jnp.int32, (TILE, LANE), 0)
    dist = past_len + qt - (row * CMP_STRIDE + CMP_BLOCK - 1)
    cmpbt_ref[...] = jnp.where(dist >= 0, _rel_bias(dist, tab_ref), NEG)
    for j in range(selb_ref.shape[1] // TILE):
        key = row + j * TILE
        dist = past_len + qt - key
        ok = (dist >= 0) & (key < past_len + n_new)
        selb_ref[:, j * TILE:(j + 1) * TILE] = jnp.where(ok, _rel_bias(dist, tab_ref), NEG).T
    for j in range(winb_ref.shape[1] // TILE):
        key = row + j * TILE
        dist = win_len + qt - key
        ok = (dist >= 0) & (dist <= WINDOW) & (key < win_len + n_new)
        winb_ref[:, j * TILE:(j + 1) * TILE] = jnp.where(ok, _rel_bias(dist, tab_ref), NEG).T


def _bias_sample(tab, past_len, n_new, win_len):
    return pl.pallas_call(
        functools.partial(_bias_sample_body, past_len=past_len, n_new=n_new, win_len=win_len),
        out_shape=[jax.ShapeDtypeStruct((TILE, LANE), F32),
                   jax.ShapeDtypeStruct((LANE, past_len + TILE), F32),
                   jax.ShapeDtypeStruct((LANE, win_len + TILE), F32)],
        name="bias_sample",
    )(tab)


def _select_blocks(imp, blk, q_blk):
    valid = blk <= q_blk
    forced = (blk == 0) | (blk == q_blk) | (blk == q_blk - 1)
    score = jnp.where(valid, imp + jnp.where(forced, FORCE_BONUS, 0.0), -jnp.inf)
    rank = jnp.zeros(score.shape, jnp.int32)
    for i in range(score.shape[0]):
        row = score[i:i + 1, :]
        beats = (row > score) | ((row == score) & (blk > i))
        rank = rank + beats.astype(jnp.int32)
    return jnp.where(valid & (rank < N_SELECT), 0.0, NEG)


FAR_RUN = 2


def _nsa_prompt_body(q_ref, gd_ref, kc_ref, vc_ref, sk_ref, sv_ref, wk_ref, wv_ref, cmpb_ref, near_ref,
                     edge_ref, tab_ref, impt_ref, o_ref, qs_ref, seladd_ref, selfar_ref, m_ref, l_ref, acc_ref,
                     out_ref):
    qb = pl.program_id(1)
    lanes = Q_PER_KV * TILE
    q = q_ref[...] * SCALE
    gate_t = jax.nn.sigmoid(gd_ref[...]).T
    n_sel = seladd_ref.shape[1]
    blk = lax.broadcasted_iota(jnp.int32, (n_sel, TILE), 0)
    q_blk = (qb * TILE + lax.broadcasted_iota(jnp.int32, (n_sel, TILE), 1)) // SEL_BLOCK
    groups = range(KV_HEADS)

    def hs(g):
        return slice(g * HEAD_DIM, (g + 1) * HEAD_DIM)

    def gate_row(branch, g):
        c0 = branch * N_HEADS + g * Q_PER_KV
        return jnp.concatenate([gate_t[c0 + r:c0 + r + 1, :] for r in range(Q_PER_KV)], axis=1)

    def far_bias(g):
        return tab_ref[g, REL_BUCKETS - 1:REL_BUCKETS, :] * LOG2E

    def stack_heads(x, g):
        return jnp.concatenate(
            [x[:, (g * Q_PER_KV + r) * HEAD_DIM:(g * Q_PER_KV + r + 1) * HEAD_DIM] for r in range(Q_PER_KV)],
            axis=0).astype(BF16)

    q2 = q * LOG2E
    kc = kc_ref[...].astype(BF16)
    vct = vc_ref[...].T.astype(BF16)
    scores = []
    for g in groups:
        qs_ref[g] = stack_heads(q2, g)
        scores.append(_dot_nt(kc[:, hs(g)], stack_heads(q, g)) + cmpb_ref[g])
    probs = []
    for g in groups:
        m = jnp.max(scores[g], axis=0, keepdims=True)
        e = jnp.exp(scores[g] - m)
        norm = jnp.where(m > 0.5 * NEG, 1.0 / jnp.maximum(jnp.sum(e, axis=0, keepdims=True), TINY), 0.0)
        probs.append(e * norm)
    imps = []
    for g in groups:
        p = probs[g]
        out_ref[g] = gate_row(0, g) * _dot(vct[hs(g), :], p.astype(BF16))
        psum = p[:, 0:TILE]
        for r in range(1, Q_PER_KV):
            psum = psum + p[:, r * TILE:(r + 1) * TILE]
        imps.append(_dot_exact_lhs(impt_ref[...], psum))
    for g in groups:
        sa = _select_blocks(imps[g], blk, q_blk)
        sa = jnp.concatenate([sa] * Q_PER_KV, axis=1)
        seladd_ref[g] = sa
        selfar_ref[g] = sa + far_bias(g)

    def sel_rows(ref, g, kt):
        half = SEL_BLOCK
        return jnp.concatenate(
            [jnp.broadcast_to(ref[g, pl.ds(2 * kt, 1), :], (half, lanes)),
             jnp.broadcast_to(ref[g, pl.ds(2 * kt + 1, 1), :], (half, lanes))], axis=0)

    def reset():
        m_ref[...] = jnp.full(m_ref.shape, -jnp.inf, F32)
        l_ref[...] = jnp.zeros(l_ref.shape, F32)
        acc_ref[...] = jnp.zeros(acc_ref.shape, F32)

    def tiles(k_ref, v_ref, parts):
        starts = [pl.multiple_of(kt * TILE, TILE) for kt, _ in parts]
        k = jnp.concatenate([k_ref[pl.ds(r0, TILE), :] for r0 in starts], axis=0).astype(BF16)
        vt = jnp.concatenate([v_ref[pl.ds(r0, TILE), :] for r0 in starts], axis=0).T.astype(BF16)
        scores = []
        for g in groups:
            s = _dot_nt(k[:, hs(g)], qs_ref[g])
            scores.append([s[i * TILE:(i + 1) * TILE] + bias_of(g) for i, (_, bias_of) in enumerate(parts)])
        probs, alphas = [], []
        for g in groups:
            m = m_ref[g]
            m_new = m
            for s in scores[g]:
                m_new = jnp.maximum(m_new, jnp.max(s, axis=0, keepdims=True))
            alpha = jnp.exp2(m - m_new)
            ps = [jnp.exp2(s - m_new) for s in scores[g]]
            total = jnp.sum(ps[0], axis=0, keepdims=True)
            for p in ps[1:]:
                total = total + jnp.sum(p, axis=0, keepdims=True)
            m_ref[g] = m_new
            l_ref[g] = alpha * l_ref[g] + total
            probs.append(jnp.concatenate([p.astype(BF16) for p in ps], axis=0))
            alphas.append(alpha)
        pvs = [_dot(vt[hs(g), :], probs[g]) for g in groups]
        for g in groups:
            acc_ref[g] = alphas[g] * acc_ref[g] + pvs[g]

    def finish(branch, g):
        return gate_row(branch, g) * (acc_ref[g] / jnp.maximum(l_ref[g], TINY))

    reset()

    n_far = jnp.maximum(qb - 1, 0)

    def far_body(i, carry):
        hi = n_far - 1 - FAR_RUN * i
        parts = []
        for j in range(FAR_RUN - 1, -1, -1):
            kt = jnp.maximum(hi - j, 0)
            off = jnp.where(hi - j >= 0, 0.0, NEG)
            parts.append((kt, lambda g, kt=kt, off=off: sel_rows(selfar_ref, g, kt) + off))
        tiles(sk_ref, sv_ref, parts)
        return carry

    lax.fori_loop(0, (n_far + FAR_RUN - 1) // FAR_RUN, far_body, 0)
    kt1 = jnp.maximum(qb - 1, 0)
    off1 = jnp.where(qb >= 1, 0.0, NEG)
    tiles(sk_ref, sv_ref, [(kt1, lambda g: near_ref[g, 1] + (sel_rows(seladd_ref, g, kt1) + off1)),
                           (qb, lambda g: near_ref[g, 0] + sel_rows(seladd_ref, g, qb))])
    for g in groups:
        out_ref[g] = out_ref[g] + finish(1, g)

    reset()
    parts = []
    for delta in range(WINDOW // TILE, -1, -1):
        kt = qb - delta
        off = jnp.where(kt >= 0, 0.0, NEG)
        if delta == WINDOW // TILE:
            bias_of = lambda g, off=off: edge_ref[...] + (far_bias(g) + off)
        elif delta >= 2:
            bias_of = lambda g, off=off: far_bias(g) + off
        else:
            bias_of = lambda g, off=off, delta=delta: near_ref[g, delta] + off
        parts.append((jnp.maximum(kt, 0), bias_of))
    tiles(wk_ref, wv_ref, parts[:3])
    tiles(wk_ref, wv_ref, parts[3:])
    outs = []
    for g in groups:
        out_t = out_ref[g] + finish(2, g)
        outs += [out_t[:, r * TILE:(r + 1) * TILE].T for r in range(Q_PER_KV)]
    o_ref[...] = jnp.concatenate(outs, axis=1).astype(BF16)


def _nsa_prompt(p3, kc, vc, cmpb, near, edge, tab, impt):
    b, t, _ = p3.shape
    n_qb = t // TILE
    lanes = Q_PER_KV * TILE
    kvcol = lambda slot: (P_KV + slot * KV_WIDTH) // KV_WIDTH
    seq_spec = lambda slot: pl.BlockSpec((None, t, KV_WIDTH), lambda i, j: (i, 0, kvcol(slot)))
    const = lambda shape: pl.BlockSpec(shape, lambda i, j: (0,) * len(shape))
    return pl.pallas_call(
        _nsa_prompt_body,
        grid=(b, n_qb),
        in_specs=[pl.BlockSpec((None, TILE, ATTN_WIDTH), lambda i, j: (i, j, 0)),
                  pl.BlockSpec((None, TILE, LANE), lambda i, j: (i, j, P_GD // LANE)),
                  pl.BlockSpec((None, TILE, KV_WIDTH), lambda i, j: (i, 0, 0)),
                  pl.BlockSpec((None, TILE, KV_WIDTH), lambda i, j: (i, 0, 0)),
                  seq_spec(2), seq_spec(3), seq_spec(4), seq_spec(5),
                  pl.BlockSpec((None, KV_HEADS, TILE, lanes), lambda i, j: (j, 0, 0, 0)),
                  const(near.shape), const(edge.shape), const(tab.shape), const(impt.shape)],
        out_specs=pl.BlockSpec((None, TILE, ATTN_WIDTH), lambda i, j: (i, j, 0)),
        out_shape=jax.ShapeDtypeStruct((b, t, ATTN_WIDTH), BF16),
        scratch_shapes=[pltpu.VMEM((KV_HEADS, lanes, HEAD_DIM), BF16),
                        pltpu.VMEM((KV_HEADS, impt.shape[0], lanes), F32),
                        pltpu.VMEM((KV_HEADS, impt.shape[0], lanes), F32),
                        pltpu.VMEM((KV_HEADS, 1, lanes), F32),
                        pltpu.VMEM((KV_HEADS, 1, lanes), F32),
                        pltpu.VMEM((KV_HEADS, HEAD_DIM, lanes), F32),
                        pltpu.VMEM((KV_HEADS, HEAD_DIM, lanes), F32)],
        compiler_params=_cparams("parallel", "arbitrary"),
        name="nsa_prompt",
    )(p3, p3, kc, vc, p3, p3, p3, p3, cmpb, near, edge, tab, impt)


def _shift_window(win_ref, new_rows, place_ref, winout_ref, n_new):
    n_wt = win_ref.shape[1] // TILE
    lane = lax.broadcasted_iota(jnp.int32, (1, LANE), 1)
    hi, mid, lo = _split3(new_rows)
    tail = _dot_tn(hi, place_ref[...]) + _dot_tn(mid, place_ref[...]) + _dot_tn(lo, place_ref[...])
    rolled = [pltpu.roll(win_ref[:, j * TILE:(j + 1) * TILE], TILE - n_new, 1) for j in range(n_wt)]
    for j in range(n_wt):
        nxt = rolled[j + 1] if j + 1 < n_wt else tail
        winout_ref[:, j * TILE:(j + 1) * TILE] = jnp.where(lane < TILE - n_new, rolled[j], nxt)


def _nsa_sample_body(*refs, past_len, n_new):
    n_pages = past_len // TILE
    pt_ref, q_ref, gd_ref, newkv_ref, newwin_ref = refs[:5]
    pages = refs[5:5 + n_pages]
    (win_ref, cmpbt_ref, selb_ref, winb_ref, impt_ref, rsum_ref, gsel_ref, expand_ref, place_ref,
     perm_ref, w1_ref, peb_ref, b1_ref, w2_ref, b2_ref,
     o_ref, winout_ref, s_scr, xp_ref, hid_ref, kc_ref, vc_ref) = refs[5 + n_pages:]
    del pt_ref
    win_len = win_ref.shape[1]
    sel_rows = 2 * KV_WIDTH
    _permute_pages(pages, perm_ref, xp_ref)
    _compress_mlp(xp_ref, w1_ref, peb_ref, b1_ref, w2_ref, b2_ref, kc_ref, vc_ref, hid_ref)
    rows_per_g = Q_PER_KV * n_new
    row_g = lax.broadcasted_iota(jnp.int32, (LANE, 1), 0) // rows_per_g

    q = q_ref[...] * SCALE
    qs = jnp.concatenate([q[:, h * HEAD_DIM:(h + 1) * HEAD_DIM] for h in range(N_HEADS)], axis=0)
    qs4 = jnp.concatenate([qs] * KV_HEADS, axis=1)
    col_g = lax.broadcasted_iota(jnp.int32, qs4.shape, 1) // HEAD_DIM
    qbd = jnp.where(row_g == col_g, qs4, 0.0).astype(BF16)

    sig = jax.nn.sigmoid(gd_ref[...])
    t_row = lax.broadcasted_iota(jnp.int32, (n_new, LANE), 0)
    t_lane = lax.broadcasted_iota(jnp.int32, (n_new, LANE), 1) % n_new
    lane_g = lax.broadcasted_iota(jnp.int32, (1, LANE), 1) // rows_per_g
    ones_rows = jnp.ones((2 * SUBLANE, TILE), BF16)

    def gate_row(branch):
        spread = _dot_exact_rhs(sig, gsel_ref[branch])
        return jnp.sum(jnp.where(t_row == t_lane, spread, 0.0), axis=0, keepdims=True)

    def pick(full_t):
        out = jnp.zeros((HEAD_DIM, LANE), F32)
        for g in range(KV_HEADS):
            out = out + jnp.where(lane_g == g, full_t[g * HEAD_DIM:(g + 1) * HEAD_DIM, :], 0.0)
        return out

    def compressed():
        kc = kc_ref[...].astype(BF16)
        s_t = _dot_nt(kc, qbd) + cmpbt_ref[...]
        m_t = jnp.max(s_t, axis=0, keepdims=True)
        e_t = jnp.exp(s_t - m_t)
        p_t = e_t / jnp.maximum(jnp.sum(e_t, axis=0, keepdims=True), TINY)
        out_c = gate_row(0) * pick(_dot(vc_ref[...].T.astype(BF16), p_t.astype(BF16)))
        imp = _dot_exact_lhs(impt_ref[...], _dot_exact_rhs(p_t, rsum_ref[...]))
        blk = lax.broadcasted_iota(jnp.int32, imp.shape, 0)
        q_blk = (past_len + lax.broadcasted_iota(jnp.int32, imp.shape, 1) % n_new) // SEL_BLOCK
        chosen_t = jnp.where(_select_blocks(imp, blk, q_blk) == 0.0, 1.0, 0.0)
        chosen = jnp.concatenate([chosen_t, jnp.zeros((LANE - chosen_t.shape[0], LANE), F32)], axis=0).T
        return out_c, (_dot(chosen.astype(BF16), expand_ref[...]) - 1.0) * (-NEG)

    def attend(tiles, bias_ref, mask):
        m = jnp.full((LANE, 1), -jnp.inf, F32)
        for j, (score, _) in enumerate(tiles):
            s = score() + bias_ref[:, j * TILE:(j + 1) * TILE]
            if mask is not None:
                s = s + mask[:, j * TILE:(j + 1) * TILE]
            s_scr[:, j * TILE:(j + 1) * TILE] = s
            m = jnp.maximum(m, jnp.max(s, axis=1, keepdims=True))
        acc = jnp.zeros((KV_WIDTH + 2 * SUBLANE, LANE), F32)
        for j, (_, vt) in enumerate(tiles):
            p = jnp.exp(s_scr[:, j * TILE:(j + 1) * TILE] - m).astype(BF16)
            acc = acc + _dot_nt(jnp.concatenate([vt(), ones_rows], axis=0), p)
        return pick(acc) / jnp.maximum(acc[KV_WIDTH:KV_WIDTH + 1, :], TINY)

    def stored_tile(ref, row0, lanes):
        return (lambda: _dot(qbd, ref[row0:row0 + KV_WIDTH, lanes].astype(BF16)),
                lambda: ref[row0 + KV_WIDTH:row0 + 2 * KV_WIDTH, lanes].astype(BF16))

    def new_tile(ref):
        rows = jnp.concatenate([ref[...], jnp.zeros((TILE - n_new, ref.shape[1]), F32)], axis=0)
        k = rows[:, 0:KV_WIDTH].astype(BF16)
        vt = rows[:, KV_WIDTH:2 * KV_WIDTH].T.astype(BF16)
        return rows, (lambda: _dot_nt(qbd, k), lambda: vt)

    out_t, sel_mask = compressed()

    _, new_kv = new_tile(newkv_ref)
    tiles = [stored_tile(pg, sel_rows, slice(None)) for pg in pages] + [new_kv]
    out_t = out_t + gate_row(1) * attend(tiles, selb_ref, sel_mask)

    new_rows, new_win = new_tile(newwin_ref)
    n_wt = win_len // TILE
    tiles = [stored_tile(win_ref, 0, slice(j * TILE, (j + 1) * TILE)) for j in range(n_wt)] + [new_win]
    out_t = out_t + gate_row(2) * attend(tiles, winb_ref, None)
    out = out_t.T
    o_ref[...] = jnp.concatenate([out[h * n_new:(h + 1) * n_new, :] for h in range(N_HEADS)], axis=1).astype(BF16)
    _shift_window(win_ref, new_rows, place_ref, winout_ref, n_new)


def _nsa_sample(p3, cache_t, win_t, page_table, consts, cmp_consts):
    b, n_new, _ = p3.shape
    n_pages = page_table.shape[1]
    past_len = n_pages * PAGE_SIZE
    win_len = win_t.shape[2]
    kvblk = lambda slot: (P_KV + slot * KV_WIDTH) // (2 * KV_WIDTH)
    const = lambda arr: pl.BlockSpec(arr.shape, lambda i, pt: (0,) * arr.ndim)
    in_specs = [pl.BlockSpec((None, n_new, ATTN_WIDTH), lambda i, pt: (i, 0, 0)),
                pl.BlockSpec((None, n_new, LANE), lambda i, pt: (i, 0, P_GD // LANE)),
                pl.BlockSpec((None, n_new, 2 * KV_WIDTH), lambda i, pt: (i, 0, kvblk(2))),
                pl.BlockSpec((None, n_new, 2 * KV_WIDTH), lambda i, pt: (i, 0, kvblk(4)))]
    in_specs += [pl.BlockSpec((None, N_CACHE_SLOTS * KV_WIDTH, PAGE_SIZE),
                              functools.partial(lambda j, i, pt: (pt[i, j], 0, 0), j)) for j in range(n_pages)]
    consts = list(consts) + list(cmp_consts)
    in_specs += [pl.BlockSpec((None, 2 * KV_WIDTH, win_len), lambda i, pt: (i, 0, 0))] + [const(a) for a in consts]
    perm_rows = PAGES_PER_PERM * PAGE_SIZE
    return pl.pallas_call(
        functools.partial(_nsa_sample_body, past_len=past_len, n_new=n_new),
        grid_spec=pltpu.PrefetchScalarGridSpec(
            num_scalar_prefetch=1, grid=(b,), in_specs=in_specs,
            out_specs=[pl.BlockSpec((None, n_new, ATTN_WIDTH), lambda i, pt: (i, 0, 0)),
                       pl.BlockSpec((None, 2 * KV_WIDTH, win_len), lambda i, pt: (i, 0, 0))],
            scratch_shapes=[pltpu.VMEM((LANE, past_len + TILE), F32),
                            pltpu.VMEM((n_pages // PAGES_PER_PERM, perm_rows, 2 * KV_WIDTH), BF16),
                            pltpu.VMEM((KV_HEADS // 2, TILE, 4 * CMP_HIDDEN), F32),
                            pltpu.VMEM((TILE, KV_WIDTH), F32),
                            pltpu.VMEM((TILE, KV_WIDTH), F32)]),
        out_shape=[jax.ShapeDtypeStruct((b, n_new, ATTN_WIDTH), BF16),
                   jax.ShapeDtypeStruct((b, 2 * KV_WIDTH, win_len), F32)],
        compiler_params=_cparams("parallel"),
        name="nsa_sample",
    )(page_table, p3, p3, p3, p3, *([cache_t] * n_pages), win_t, *consts)


def _sample_selectors(n_new, n_keys):
    lanes = np.arange(LANE)
    h, t = lanes // n_new, lanes % n_new
    rsum = ((h[:, None] // Q_PER_KV == h[None, :] // Q_PER_KV) & (t[:, None] == t[None, :])).astype(np.float32)
    gsel = np.stack([(lanes[:, None] == br * N_HEADS + h[None, :]).astype(np.float32) for br in range(3)])
    expand = (np.arange(n_keys)[None, :] // SEL_BLOCK == lanes[:, None]).astype(np.float32)
    place = (lanes[None, :] == LANE - n_new + lanes[:, None]).astype(np.float32)
    return tuple(jnp.asarray(a, BF16) for a in (rsum, gsel, expand, place))


def _dot_tn(a, b):
    return lax.dot_general(a, b, (((0,), (0,)), ((), ())), preferred_element_type=F32)


def _ssd_body(*refs, q_len, carry):
    if carry:
        (z_ref, xbc_ref, gd_ref, cw_ref, cb_ref, dtb_ref, alog_ref, dexp_ref, ng_ref, e16_ref,
         y_ref, hout_ref, prev_ref, h_scr) = refs
    else:
        (z_ref, xbc_ref, gd_ref, prev_ref, h0_ref, cw_ref, cb_ref, dtb_ref, alog_ref, dexp_ref, ng_ref, e16_ref,
         y_ref, hout_ref) = refs
    n_seq = TILE // q_len
    x = xbc_ref[...]
    if carry:
        @pl.when(pl.program_id(1) == 0)
        def _():
            prev_ref[...] = jnp.zeros_like(prev_ref)
            h_scr[...] = jnp.zeros_like(h_scr)
    prev = prev_ref[...]
    acc = cb_ref[...] + cw_ref[SSM_CONV - 1:SSM_CONV, :] * x
    for k in range(1, SSM_CONV):
        acc = acc + cw_ref[SSM_CONV - 1 - k:SSM_CONV - k, :] * _shift_rows(x, prev, k, q_len)
    if carry:
        prev_ref[...] = x
    xc = _silu(acc)
    xs = xc[:, 0:SSM_INNER]
    b_all = xc[:, SSM_INNER:SSM_INNER + SSM_GROUPS * SSM_STATE]
    c_all = xc[:, SSM_INNER + SSM_GROUPS * SSM_STATE:]

    col = lax.broadcasted_iota(jnp.int32, (1, LANE), 1)
    dt_cols = (col >= DT_COL) & (col < DT_COL + SSM_HEADS)
    dt = jnp.where(dt_cols, jax.nn.softplus(gd_ref[...] + dtb_ref[...]), 0.0)
    a = dt * jnp.where(dt_cols, -jnp.exp(alog_ref[...]), 0.0)
    ri = lax.broadcasted_iota(jnp.int32, (TILE, TILE), 0)
    ci = lax.broadcasted_iota(jnp.int32, (TILE, TILE), 1)
    same = (ri // q_len) == (ci // q_len)
    causal = same & (ci <= ri)
    acum = _dot_exact_lhs(causal.astype(BF16), a)
    last_sel = (ci == (ri // q_len) * q_len + (q_len - 1)).astype(BF16)
    alast = _dot_exact_lhs(last_sel, acum)
    acum_t = acum.T
    e16 = e16_ref[...]
    xw = xs * _dot_exact_rhs(dt, e16)
    xwe = xw * _dot_exact_rhs(jnp.exp(alast - acum), e16)
    grow = jnp.exp(_dot_exact_rhs(acum, e16))
    rowseq = lax.broadcasted_iota(jnp.int32, (TILE, 1), 0) // q_len

    ys = []
    for g in range(SSM_GROUPS):
        cg = c_all[:, g * SSM_STATE:(g + 1) * SSM_STATE].astype(BF16)
        bg = b_all[:, g * SSM_STATE:(g + 1) * SSM_STATE].astype(BF16)
        cb = _dot_nt(cg, bg)
        gl = slice(g * SSM_HPG * SSM_HEADDIM, (g + 1) * SSM_HPG * SSM_HEADDIM)
        hl = slice(g * SSM_HPG, (g + 1) * SSM_HPG)
        for r in range(SSM_HPG):
            h = g * SSM_HPG + r
            c = DT_COL + h
            seg = acum[:, c:c + 1] - acum_t[c:c + 1, :]
            decay = jnp.where(causal, jnp.exp(jnp.where(causal, seg, 0.0)), 0.0)
            ys.append(_dot((cb * decay).astype(BF16), xw[:, h * SSM_HEADDIM:(h + 1) * SSM_HEADDIM].astype(BF16)))
        xwe_g = xwe[:, gl]
        if carry:
            h_old = h_scr[hl].reshape(SSM_HPG * SSM_HEADDIM, SSM_STATE)
            y_off = _dot_nt(cg, h_old.astype(BF16))
            scale = jnp.concatenate(
                [jnp.broadcast_to(jnp.exp(alast[0:1, DT_COL + g * SSM_HPG + r:DT_COL + g * SSM_HPG + r + 1]),
                                  (SSM_HEADDIM, SSM_STATE)) for r in range(SSM_HPG)], axis=0)
            h_new = h_old * scale + _dot_tn(xwe_g.astype(BF16), bg)
            h_scr[hl] = h_new.reshape(SSM_HPG, SSM_HEADDIM, SSM_STATE)
        else:
            offs = []
            for s in range(n_seq):
                h_old = h0_ref[s, hl].reshape(SSM_HPG * SSM_HEADDIM, SSM_STATE)
                offs.append(_dot_nt(cg, h_old.astype(BF16))[s * q_len:(s + 1) * q_len, :])
                a_row = alast[s * q_len:s * q_len + 1, :]
                scale = jnp.concatenate(
                    [jnp.broadcast_to(jnp.exp(a_row[:, DT_COL + g * SSM_HPG + r:DT_COL + g * SSM_HPG + r + 1]),
                                      (SSM_HEADDIM, SSM_STATE)) for r in range(SSM_HPG)], axis=0)
                x_s = jnp.where(rowseq == s, xwe_g, 0.0).astype(BF16)
                h_new = h_old * scale + _dot_tn(x_s, bg)
                hout_ref[s, hl] = h_new.reshape(SSM_HPG, SSM_HEADDIM, SSM_STATE)
            y_off = jnp.concatenate(offs, axis=0)
        for r in range(SSM_HPG):
            h = g * SSM_HPG + r
            ys[h] = ys[h] + y_off[:, r * SSM_HEADDIM:(r + 1) * SSM_HEADDIM] * grow[:, h * SSM_HEADDIM:(h + 1) * SSM_HEADDIM]
    if carry:
        @pl.when(pl.program_id(1) == pl.num_programs(1) - 1)
        def _():
            hout_ref[...] = h_scr[...]

    y = jnp.concatenate(ys, axis=1) + xs * dexp_ref[...]
    y = y * _silu(z_ref[...])
    gw = SSM_INNER // SSM_GROUPS
    parts = []
    for g in range(SSM_GROUPS):
        blk = y[:, g * gw:(g + 1) * gw]
        parts.append(blk * lax.rsqrt(jnp.mean(blk * blk, axis=-1, keepdims=True) + EPS))
    y_ref[...] = (jnp.concatenate(parts, axis=1) * ng_ref[...]).astype(BF16)


def _ssd(p3, prev, h0, cw, cb, dtb, alog, dexp, ng, e16, *, q_len):
    n_grp, rows, _ = p3.shape
    carry = prev is None
    n_chunks = rows // TILE
    n_seq = TILE // q_len
    zspec = pl.BlockSpec((None, TILE, SSM_INNER), lambda i, c: (i, c, P_Z // SSM_INNER))
    xspec = pl.BlockSpec((None, TILE, CONV_DIM), lambda i, c: (i, c, P_XBC // CONV_DIM))
    gspec = pl.BlockSpec((None, TILE, LANE), lambda i, c: (i, c, P_GD // LANE))
    const = lambda arr: pl.BlockSpec(arr.shape, lambda i, c: (0,) * arr.ndim)
    params = [cw, cb, dtb, alog, dexp, ng, e16]
    in_specs = [zspec, xspec, gspec]
    args = [p3, p3, p3]
    if carry:
        n_out_seq = n_grp
        hspec = pl.BlockSpec((None, SSM_HEADS, SSM_HEADDIM, SSM_STATE), lambda i, c: (i, 0, 0, 0))
        scratch = [pltpu.VMEM((TILE, CONV_DIM), F32), pltpu.VMEM((SSM_HEADS, SSM_HEADDIM, SSM_STATE), F32)]
    else:
        n_out_seq = n_grp * n_chunks * n_seq
        hspec = pl.BlockSpec((n_seq, SSM_HEADS, SSM_HEADDIM, SSM_STATE), lambda i, c: (i * n_chunks + c, 0, 0, 0))
        in_specs += [pl.BlockSpec((None, TILE, CONV_DIM), lambda i, c: (i, c, 0)), hspec]
        args += [prev, h0]
        scratch = []
    in_specs += [const(a) for a in params]
    return pl.pallas_call(
        functools.partial(_ssd_body, q_len=q_len, carry=carry),
        grid=(n_grp, n_chunks),
        in_specs=in_specs,
        out_specs=[pl.BlockSpec((None, TILE, SSM_INNER), lambda i, c: (i, c, 0)), hspec],
        out_shape=[jax.ShapeDtypeStruct((n_grp, rows, SSM_INNER), BF16),
                   jax.ShapeDtypeStruct((n_out_seq, SSM_HEADS, SSM_HEADDIM, SSM_STATE), F32)],
        scratch_shapes=scratch,
        compiler_params=_cparams("parallel", "arbitrary"),
        name="ssd",
    )(*args, *params)


def _importance_map_t(n_sel):
    n_cmp = TILE - 1
    c0 = np.arange(n_cmp) * CMP_STRIDE
    s0 = np.arange(n_sel) * SEL_BLOCK
    ov = np.minimum(c0[None, :] + CMP_BLOCK, s0[:, None] + SEL_BLOCK) - np.maximum(c0[None, :], s0[:, None])
    m = np.zeros((-(-n_sel // SUBLANE) * SUBLANE, TILE), np.float32)
    m[:n_sel, :n_cmp] = np.clip(ov, 0, None).astype(np.float32) / CMP_BLOCK
    return jnp.asarray(m, BF16)


def _dt_expand():
    e = np.zeros((LANE, SSM_INNER), np.float32)
    for h in range(SSM_HEADS):
        e[DT_COL + h, h * SSM_HEADDIM:(h + 1) * SSM_HEADDIM] = 1.0
    return jnp.asarray(e, BF16)


def _pad_to(x, axis, size):
    pad = [(0, 0)] * x.ndim
    pad[axis] = (0, size - x.shape[axis])
    return jnp.pad(x, pad)


def _history_rows(state, n_rows, width):
    b, k, _ = state.shape
    tile = jnp.pad(state, ((0, 0), (n_rows - k, 0), (0, width - state.shape[2])))
    return tile.reshape(b * n_rows, width)


def kernel(x_prompt, x_sample, cache_kv, cache_win_kv, state_ssm_conv, state_ssm, state_ffn_conv, page_table, rel_table, ln_mix_pre, w_in, cmp_pe, cmp_w1, cmp_b1, cmp_w2, cmp_b2, ssm_conv_w, ssm_conv_b, ssm_dt_bias, ssm_a_log, ssm_d, ssm_norm_g, w_out, ln_mix_post, ln_ffn_pre, ffn_w_gate, ffn_w_up, ffn_dw_w, ffn_dw_b, ffn_w_down, ln_ffn_post):
    bp, tp, d = x_prompt.shape
    bs, ts, _ = x_sample.shape
    n_pool = cache_kv.shape[1]
    n_pages = page_table.shape[1]
    past_len = n_pages * PAGE_SIZE
    win_len = cache_win_kv.shape[2]
    assert d == D_MODEL and tp == N_CHUNK_TILES * TILE and past_len == N_CHUNK_TILES * TILE
    assert N_HEADS * ts == LANE and ts == SUBLANE and tp >= WINDOW and win_len == WINDOW

    rel = rel_table.astype(F32)
    tab_p = jnp.repeat(rel.reshape(REL_BUCKETS, KV_HEADS, Q_PER_KV).transpose(1, 0, 2), TILE, axis=-1)
    tab_s = jnp.repeat(rel, ts, axis=1)
    cmpb_p, near, edge = _bias_prompt(tab_p, tp // TILE)
    cmpbt_s, selb, winb = _bias_sample(tab_s, past_len, ts, win_len)
    impt_p = _importance_map_t(-(-tp // SEL_BLOCK))
    impt_s = _importance_map_t(-(-(past_len + ts) // SEL_BLOCK))
    rsum, gsel, expand, place = _sample_selectors(ts, past_len + TILE)
    e16 = _dt_expand()
    dummy_pt = jnp.zeros((1, 1), jnp.int32)

    xp = x_prompt.reshape(bp * tp, d)
    xs = x_sample.reshape(bs * ts, d)
    outs = [[] for _ in range(10)]
    for l in range(w_in.shape[0]):
        o1 = ATTN_WIDTH
        o2 = o1 + N_KV_SLOTS * KV_WIDTH
        o3 = o2 + 3 * N_HEADS
        o4 = o3 + SSM_INNER
        o5 = o4 + CONV_DIM
        w = w_in[l].T
        wp = jnp.concatenate([w[:o1], w[o3:o4], w[o4:o5], w[o1:o2], w[o2:o3], w[o5:],
                              jnp.zeros((P_WIDTH - w.shape[0], d), w.dtype)], axis=0).astype(BF16)
        g_pre = ln_mix_pre[l].reshape(1, d)
        cw = _compress_weights(cmp_pe[l], cmp_w1[l], cmp_b1[l], cmp_w2[l], cmp_b2[l])
        ssm_w = (_pad_to(ssm_conv_w[l], 0, SUBLANE), ssm_conv_b[l].reshape(1, CONV_DIM),
                 _pad_to(jnp.pad(ssm_dt_bias[l].astype(F32), (DT_COL, 0)), 0, LANE).reshape(1, LANE),
                 _pad_to(jnp.pad(ssm_a_log[l].astype(F32), (DT_COL, 0)), 0, LANE).reshape(1, LANE),
                 jnp.repeat(ssm_d[l].astype(F32), SSM_HEADDIM).reshape(1, SSM_INNER),
                 ssm_norm_g[l].astype(F32).reshape(1, SSM_INNER), e16)
        w_attn = w_out[l, :ATTN_WIDTH].astype(BF16)
        w_ssm = w_out[l, ATTN_WIDTH:].astype(BF16)
        g_post = ln_mix_post[l].reshape(1, d)
        g_pre2 = ln_ffn_pre[l].reshape(1, d)
        wg = _cast_pad_cols(ffn_w_gate[l], FF_PAD)
        wu = _cast_pad_cols(ffn_w_up[l], FF_PAD)
        dw = _pad_to(_pad_to(ffn_dw_w[l], 1, FF_PAD), 0, SUBLANE)
        db = _pad_to(ffn_dw_b[l].reshape(1, D_FF), 1, FF_PAD)
        wd = _cast_pad_rows(ffn_w_down[l], FF_PAD)
        g_post2 = ln_ffn_post[l].reshape(1, d)

        p3 = _proj(xp, g_pre, wp, tm=1024).reshape(bp, tp, P_WIDTH)
        kc, vc = _compress(p3, lambda j, cb, b, pt: (b, j, P_KV // LANE + cb), bp, dummy_pt, *cw)
        attn = _nsa_prompt(p3, kc, vc, cmpb_p, near, edge, tab_p, impt_p)
        ssm, h_new = _ssd(p3, None, None, *ssm_w, q_len=TILE)
        x1, hn = _outproj(attn.reshape(bp * tp, ATTN_WIDTH), ssm.reshape(bp * tp, SSM_INNER), xp,
                          w_attn, w_ssm, g_post, g_pre2)
        act, tail = _ffn_up(hn, wg, wu, dw, db, None, seq_len=tp, tm=tp)
        xp = _ffn_down(act, wd, x1, g_post2)
        kv = p3[:, :, P_KV:P_KV + N_KV_SLOTS * KV_WIDTH]
        outs[0].append(kv[:, :, :N_CACHE_SLOTS * KV_WIDTH].reshape(bp, tp, N_CACHE_SLOTS, KV_HEADS, HEAD_DIM))
        outs[2].append(kv[:, tp - WINDOW:, N_CACHE_SLOTS * KV_WIDTH:].reshape(bp, WINDOW, 2, KV_HEADS, HEAD_DIM))
        outs[4].append(p3[:, tp - (SSM_CONV - 1):, P_XBC:P_XBC + CONV_DIM])
        outs[6].append(h_new)
        outs[8].append(tail[:, SUBLANE - (FFN_CONV - 1):, :D_FF])

        p3 = _proj(xs, g_pre, wp, tm=bs * ts).reshape(bs, ts, P_WIDTH)
        cache_t = cache_kv[l].transpose(0, 2, 3, 4, 1).reshape(n_pool, N_CACHE_SLOTS * KV_WIDTH, PAGE_SIZE)
        win_t = cache_win_kv[l].transpose(0, 2, 3, 4, 1).reshape(bs, 2 * KV_WIDTH, win_len)
        w1p, perm = _compress_paged_weights(cmp_w1[l])
        attn, win_new = _nsa_sample(p3, cache_t, win_t, page_table,
                                    (cmpbt_s, selb, winb, impt_s, rsum, gsel, expand, place),
                                    (perm, w1p, _pe_bias(cw[0], w1p), cw[2], cw[3], cw[4]))
        win_new = win_new.reshape(bs, 2, KV_HEADS, HEAD_DIM, win_len).transpose(0, 4, 1, 2, 3)
        prev = _history_rows(state_ssm_conv[l], ts, CONV_DIM).reshape(1, bs * ts, CONV_DIM)
        ssm, h_new = _ssd(p3.reshape(1, bs * ts, P_WIDTH), prev, state_ssm[l], *ssm_w, q_len=ts)
        x1, hn = _outproj(attn.reshape(bs * ts, ATTN_WIDTH), ssm.reshape(bs * ts, SSM_INNER), xs,
                          w_attn, w_ssm, g_post, g_pre2)
        prev_f = _history_rows(state_ffn_conv[l], ts, FF_PAD)
        act, gp = _ffn_up(hn, wg, wu, dw, db, prev_f, seq_len=ts, tm=bs * ts)
        xs = _ffn_down(act, wd, x1, g_post2, tm=bs * ts)
        kv = p3[:, :, P_KV:P_KV + N_KV_SLOTS * KV_WIDTH]
        outs[1].append(kv[:, :, :N_CACHE_SLOTS * KV_WIDTH].reshape(bs, ts, N_CACHE_SLOTS, KV_HEADS, HEAD_DIM))
        outs[3].append(win_new)
        outs[5].append(p3[:, ts - (SSM_CONV - 1):, P_XBC:P_XBC + CONV_DIM])
        outs[7].append(h_new)
        outs[9].append(gp.reshape(bs, ts, FF_PAD)[:, ts - (FFN_CONV - 1):, :D_FF])
    return (xp.reshape(bp, tp, d), xs.reshape(bs, ts, d)) + tuple(jnp.stack(o) for o in outs)
```

```python
import functools
import math

import numpy as np
import jax
import jax.numpy as jnp
from jax import lax
from jax.experimental import pallas as pl
from jax.experimental.pallas import tpu as pltpu

F32 = jnp.float32
BF16 = jnp.bfloat16

D_MODEL = 2048
PAGE_SIZE = 128
HEAD_DIM = 64
ATTN_WIDTH = D_MODEL // 2
N_HEADS = ATTN_WIDTH // HEAD_DIM
KV_HEADS = 4
Q_PER_KV = N_HEADS // KV_HEADS
CMP_STRIDE = 16
CMP_BLOCK = 2 * CMP_STRIDE
CMP_HIDDEN = 2 * HEAD_DIM
SEL_BLOCK = 64
N_SELECT = 8
WINDOW = 512
N_KV_SLOTS = 6
N_CACHE_SLOTS = 4
KV_WIDTH = KV_HEADS * HEAD_DIM
SSM_INNER = D_MODEL - ATTN_WIDTH
SSM_HEADDIM = 64
SSM_HEADS = SSM_INNER // SSM_HEADDIM
SSM_GROUPS = 4
SSM_HPG = SSM_HEADS // SSM_GROUPS
SSM_STATE = 128
SSM_CONV = 4
CONV_DIM = SSM_INNER + 2 * SSM_GROUPS * SSM_STATE
D_FF = ((8 * D_MODEL // 3 + 127) // 128) * 128
FFN_CONV = 3
REL_BUCKETS = 32
REL_MAX_DIST = 128
EPS = 1e-6
NEG = -1e30
TINY = 1e-30
FORCE_BONUS = 1e4
SCALE = HEAD_DIM ** -0.5
LOG2E = math.log2(math.e)

LANE = 128
SUBLANE = 8
TILE = 128
VMEM_LIMIT = 56 * 1024 * 1024

P_Q = 0
P_Z = ATTN_WIDTH
P_XBC = P_Z + SSM_INNER
P_KV = P_XBC + CONV_DIM
P_GD = P_KV + N_KV_SLOTS * KV_WIDTH
GD_WIDTH = 3 * N_HEADS + SSM_HEADS
P_WIDTH = 6144
DT_COL = 3 * N_HEADS
FF_PAD = 5632
FFN_SUB = 256


def _cparams(*sem):
    return pltpu.CompilerParams(dimension_semantics=sem, vmem_limit_bytes=VMEM_LIMIT)


def _rms(x, g):
    return x * lax.rsqrt(jnp.mean(x * x, axis=-1, keepdims=True) + EPS) * g


def _split3(x):
    hi = x.astype(BF16)
    r1 = x - hi.astype(F32)
    mid = r1.astype(BF16)
    lo = (r1 - mid.astype(F32)).astype(BF16)
    return hi, mid, lo


def _dot(a, b):
    return jnp.dot(a, b, preferred_element_type=F32)


def _dot_nt(a, b):
    return lax.dot_general(a, b, (((1,), (1,)), ((), ())), preferred_element_type=F32)


def _dot_exact_rhs(x, sel):
    hi, mid, lo = _split3(x)
    return _dot(hi, sel) + _dot(mid, sel) + _dot(lo, sel)


def _dot_exact_lhs(sel, x):
    hi, mid, lo = _split3(x)
    return _dot(sel, hi) + _dot(sel, mid) + _dot(sel, lo)


def _gelu_tanh(x):
    return 0.5 * x * (1.0 + jnp.tanh(math.sqrt(2.0 / math.pi) * (x + 0.044715 * (x * x * x))))


def _silu(x):
    return x * jax.nn.sigmoid(x)


CAST_BLOCK = 256


def _cast_pad_cols_body(w_ref, o_ref):
    n = w_ref.shape[1]
    o_ref[:, 0:n] = w_ref[...].astype(BF16)
    o_ref[:, n:] = jnp.zeros((o_ref.shape[0], o_ref.shape[1] - n), BF16)


def _cast_pad_cols(w, n_out):
    m, n = w.shape
    return pl.pallas_call(
        _cast_pad_cols_body,
        grid=(m // CAST_BLOCK,),
        in_specs=[pl.BlockSpec((CAST_BLOCK, n), lambda i: (i, 0))],
        out_specs=pl.BlockSpec((CAST_BLOCK, n_out), lambda i: (i, 0)),
        out_shape=jax.ShapeDtypeStruct((m, n_out), BF16),
        compiler_params=_cparams("parallel"),
        name="cast_pad_cols",
    )(w)


def _cast_pad_rows_body(w_ref, o_ref):
    m = w_ref.shape[0]
    o_ref[0:m, :] = w_ref[...].astype(BF16)
    o_ref[m:, :] = jnp.zeros((o_ref.shape[0] - m, o_ref.shape[1]), BF16)


def _cast_pad_rows(w, m_out):
    m, n = w.shape
    return pl.pallas_call(
        _cast_pad_rows_body,
        grid=(n // CAST_BLOCK,),
        in_specs=[pl.BlockSpec((m, CAST_BLOCK), lambda j: (0, j))],
        out_specs=pl.BlockSpec((m_out, CAST_BLOCK), lambda j: (0, j)),
        out_shape=jax.ShapeDtypeStruct((m_out, n), BF16),
        compiler_params=_cparams("parallel"),
        name="cast_pad_rows",
    )(w)


def _proj_body(x_ref, g_ref, w_ref, o_ref, xn_ref):
    @pl.when(pl.program_id(1) == 0)
    def _():
        xn_ref[...] = _rms(x_ref[...], g_ref[...]).astype(BF16)

    o_ref[...] = _dot_nt(xn_ref[...], w_ref[...])


def _proj(x2d, g, w_t, tm, tn=1024):
    m, k = x2d.shape
    n = w_t.shape[0]
    w = w_t
    return pl.pallas_call(
        _proj_body,
        grid=(m // tm, n // tn),
        in_specs=[pl.BlockSpec((tm, k), lambda i, j: (i, 0)),
                  pl.BlockSpec((1, k), lambda i, j: (0, 0)),
                  pl.BlockSpec((tn, k), lambda i, j: (j, 0))],
        out_specs=pl.BlockSpec((tm, tn), lambda i, j: (i, j)),
        out_shape=jax.ShapeDtypeStruct((m, n), F32),
        scratch_shapes=[pltpu.VMEM((tm, k), BF16)],
        compiler_params=_cparams("parallel", "arbitrary"),
        name="proj",
    )(x2d, g, w)


NORM_ROWS = 128


def _outproj_body(a_ref, s_ref, x_ref, wa_ref, ws_ref, g1_ref, g2_ref, x1_ref, hn_ref):
    def project(rs):
        return _dot(a_ref[rs, :], wa_ref[...]) + _dot(s_ref[rs, :], ws_ref[...])

    def finish(rs, mix):
        x1 = x_ref[rs, :] + _rms(mix, g1_ref[...])
        x1_ref[rs, :] = x1
        hn_ref[rs, :] = _rms(x1, g2_ref[...]).astype(BF16)

    tm = x_ref.shape[0]
    rows = min(NORM_ROWS, tm)
    pending = None
    for r in range(tm // rows):
        rs = slice(r * rows, (r + 1) * rows)
        cur = (rs, project(rs))
        if pending is not None:
            finish(*pending)
        pending = cur
    finish(*pending)


def _outproj(attn, ssm, x2d, w_attn, w_ssm, g_post, g_pre2, tm=512):
    m, d = x2d.shape
    ka = attn.shape[1]
    const = lambda i: (0, 0)
    return pl.pallas_call(
        _outproj_body,
        grid=(m // tm,),
        in_specs=[pl.BlockSpec((tm, ka), lambda i: (i, 0)),
                  pl.BlockSpec((tm, ka), lambda i: (i, 0)),
                  pl.BlockSpec((tm, d), lambda i: (i, 0)),
                  pl.BlockSpec((ka, d), const),
                  pl.BlockSpec((ka, d), const),
                  pl.BlockSpec((1, d), const),
                  pl.BlockSpec((1, d), const)],
        out_specs=[pl.BlockSpec((tm, d), lambda i: (i, 0)),
                   pl.BlockSpec((tm, d), lambda i: (i, 0))],
        out_shape=[jax.ShapeDtypeStruct((m, d), F32), jax.ShapeDtypeStruct((m, d), BF16)],
        compiler_params=_cparams("parallel"),
        name="outproj",
    )(attn, ssm, x2d, w_attn, w_ssm, g_post, g_pre2)


def _shift_rows(x, prev, k, seq_len):
    rows, cols = x.shape
    if seq_len >= rows:
        t = lax.broadcasted_iota(jnp.int32, (rows, 1), 0)
        return jnp.where(t >= k, pltpu.roll(x, k, 0), 0.0 if prev is None else pltpu.roll(prev, k, 0))
    assert seq_len == SUBLANE and prev is not None
    shape = (rows // seq_len, seq_len, cols)
    t = lax.broadcasted_iota(jnp.int32, shape, 1)
    out = jnp.where(t >= k, pltpu.roll(x.reshape(shape), k, 1), pltpu.roll(prev.reshape(shape), k, 1))
    return out.reshape(rows, cols)


def _ffn_up_body(*refs, seq_len, has_prev):
    if has_prev:
        h_ref, wg_ref, wu_ref, dw_ref, db_ref, prev_ref, act_ref, tail_ref = refs
    else:
        h_ref, wg_ref, wu_ref, dw_ref, db_ref, act_ref, tail_ref = refs
    tm, tn = act_ref.shape
    h = h_ref[...]

    def project(cs):
        return _dot(h, wg_ref[:, cs]), _dot(h, wu_ref[:, cs])

    def gate(cs, gp, up):
        prev = prev_ref[:, cs] if has_prev else None
        acc = db_ref[:, cs] + dw_ref[FFN_CONV - 1:FFN_CONV, cs] * gp
        for k in range(1, FFN_CONV):
            acc = acc + dw_ref[FFN_CONV - 1 - k:FFN_CONV - k, cs] * _shift_rows(gp, prev, k, seq_len)
        act_ref[:, cs] = (_gelu_tanh(acc) * up).astype(BF16)
        tail_ref[:, cs] = gp if has_prev else gp[tm - SUBLANE:, :]

    pending = None
    for c in range(tn // FFN_SUB):
        cs = slice(c * FFN_SUB, (c + 1) * FFN_SUB)
        cur = (cs,) + project(cs)
        if pending is not None:
            gate(*pending)
        pending = cur
    gate(*pending)


def _ffn_up(hn, wg, wu, dw, db, prev, *, seq_len, tm, tn=512):
    m, d = hn.shape
    ni = m // tm
    nj = FF_PAD // tn
    has_prev = prev is not None
    in_specs = [pl.BlockSpec((tm, d), lambda i, j: (i, 0)),
                pl.BlockSpec((d, tn), lambda i, j: (0, j)),
                pl.BlockSpec((d, tn), lambda i, j: (0, j)),
                pl.BlockSpec((SUBLANE, tn), lambda i, j: (0, j)),
                pl.BlockSpec((1, tn), lambda i, j: (0, j))]
    args = [hn, wg, wu, dw, db]
    if has_prev:
        in_specs.append(pl.BlockSpec((tm, tn), lambda i, j: (i, j)))
        args.append(prev)
        tail_shape = jax.ShapeDtypeStruct((m, FF_PAD), F32)
        tail_spec = pl.BlockSpec((tm, tn), lambda i, j: (i, j))
    else:
        tail_shape = jax.ShapeDtypeStruct((ni, SUBLANE, FF_PAD), F32)
        tail_spec = pl.BlockSpec((None, SUBLANE, tn), lambda i, j: (i, 0, j))
    return pl.pallas_call(
        functools.partial(_ffn_up_body, seq_len=seq_len, has_prev=has_prev),
        grid=(ni, nj),
        in_specs=in_specs,
        out_specs=[pl.BlockSpec((tm, tn), lambda i, j: (i, j)), tail_spec],
        out_shape=[jax.ShapeDtypeStruct((m, FF_PAD), BF16), tail_shape],
        compiler_params=_cparams("parallel", "parallel"),
        name="ffn_up",
    )(*args)


def _ffn_down_body(a_ref, w_ref, x_ref, g_ref, o_ref, acc_ref):
    k = pl.program_id(1)
    last = pl.num_programs(1) - 1

    @pl.when(k == 0)
    def _():
        acc_ref[...] = _dot(a_ref[...], w_ref[...])

    @pl.when((k > 0) & (k < last))
    def _():
        acc_ref[...] += _dot(a_ref[...], w_ref[...])

    @pl.when(k == last)
    def _():
        tm = acc_ref.shape[0]
        rows = min(NORM_ROWS, tm)
        pending = None
        for r in range(tm // rows):
            rs = slice(r * rows, (r + 1) * rows)
            cur = (rs, acc_ref[rs, :] + _dot(a_ref[rs, :], w_ref[...]))
            if pending is not None:
                o_ref[pending[0], :] = x_ref[pending[0], :] + _rms(pending[1], g_ref[...])
            pending = cur
        o_ref[pending[0], :] = x_ref[pending[0], :] + _rms(pending[1], g_ref[...])


def _ffn_down(act, wd, x1, g, tm=1024, tk=512):
    m, d = x1.shape
    return pl.pallas_call(
        _ffn_down_body,
        grid=(m // tm, FF_PAD // tk),
        in_specs=[pl.BlockSpec((tm, tk), lambda i, k: (i, k)),
                  pl.BlockSpec((tk, d), lambda i, k: (k, 0)),
                  pl.BlockSpec((tm, d), lambda i, k: (i, 0)),
                  pl.BlockSpec((1, d), lambda i, k: (0, 0))],
        out_specs=pl.BlockSpec((tm, d), lambda i, k: (i, 0)),
        out_shape=jax.ShapeDtypeStruct((m, d), F32),
        scratch_shapes=[pltpu.VMEM((tm, d), F32)],
        compiler_params=_cparams("parallel", "arbitrary"),
        name="ffn_down",
    )(act, wd, x1, g)


N_CHUNK_TILES = 16
CHUNKS_PER_TILE = TILE // CMP_STRIDE
CMP_COL_BLOCKS = 2 * KV_WIDTH // LANE


def _compress_body(*refs):
    pt_ref = refs[0]
    del pt_ref
    n_in = N_CHUNK_TILES * CMP_COL_BLOCKS
    tiles = refs[1:1 + n_in]
    pe_ref, w1_ref, b1_ref, w2_ref, b2_ref, kc_ref, vc_ref, lhs_a, lhs_b = refs[1 + n_in:]
    out_refs = (kc_ref, vc_ref)
    for slot in range(2):
        outs = []
        for gp in range(KV_HEADS // 2):
            cb = slot * (KV_HEADS // 2) + gp
            for s in range(CMP_STRIDE):
                x = jnp.concatenate(
                    [tiles[j * CMP_COL_BLOCKS + cb][pl.ds(s, CHUNKS_PER_TILE, stride=CMP_STRIDE), :]
                     for j in range(N_CHUNK_TILES)], axis=0)
                lhs_a[:, s * LANE:(s + 1) * LANE] = (x + pe_ref[slot, 0, s:s + 1, :]).astype(BF16)
                lhs_b[:, s * LANE:(s + 1) * LANE] = (x + pe_ref[slot, 1, s:s + 1, :]).astype(BF16)
            first = _dot(lhs_a[...], w1_ref[slot, 0])
            second = _dot(lhs_b[...], w1_ref[slot, 1])
            hid = first + pltpu.roll(second, TILE - 1, 0) + b1_ref[slot]
            act = _gelu_tanh(hid).astype(BF16)
            for gj in range(2):
                outs.append(_dot(act[:, gj * CMP_HIDDEN:(gj + 1) * CMP_HIDDEN], w2_ref[slot]) + b2_ref[slot])
        out_refs[slot][...] = jnp.concatenate(outs, axis=1)


def _compress(src, tile_map, n_seq, page_table, pe, w1, b1, w2, b2):
    in_specs = [pl.BlockSpec((None, TILE, LANE), functools.partial(tile_map, j, cb))
                for j in range(N_CHUNK_TILES) for cb in range(CMP_COL_BLOCKS)]
    full = lambda shape: pl.BlockSpec(shape, lambda b, pt: (0,) * len(shape))
    in_specs += [full(pe.shape), full(w1.shape), full(b1.shape), full(w2.shape), full(b2.shape)]
    out_spec = pl.BlockSpec((None, TILE, KV_WIDTH), lambda b, pt: (b, 0, 0))
    return pl.pallas_call(
        _compress_body,
        grid_spec=pltpu.PrefetchScalarGridSpec(
            num_scalar_prefetch=1, grid=(n_seq,), in_specs=in_specs, out_specs=[out_spec, out_spec],
            scratch_shapes=[pltpu.VMEM((TILE, CMP_STRIDE * LANE), BF16), pltpu.VMEM((TILE, CMP_STRIDE * LANE), BF16)]),
        out_shape=[jax.ShapeDtypeStruct((n_seq, TILE, KV_WIDTH), F32)] * 2,
        compiler_params=_cparams("parallel"),
        name="compress",
    )(page_table, *([src] * (N_CHUNK_TILES * CMP_COL_BLOCKS)), pe, w1, b1, w2, b2)


def _compress_weights(cmp_pe, cmp_w1, cmp_b1, cmp_w2, cmp_b2):
    pe = cmp_pe.reshape(2, 2, CMP_STRIDE, HEAD_DIM)
    pe = jnp.concatenate([pe, pe], axis=-1)
    w = cmp_w1.reshape(2, 2, CMP_STRIDE, HEAD_DIM, CMP_HIDDEN)
    eye = jnp.eye(2, dtype=w.dtype)
    w1 = jnp.einsum('absdh,ij->absidjh', w, eye).reshape(2, 2, CMP_STRIDE * LANE, 2 * CMP_HIDDEN).astype(BF16)
    b1 = jnp.concatenate([cmp_b1, cmp_b1], axis=-1).reshape(2, 1, 2 * CMP_HIDDEN)
    return pe, w1, b1, cmp_w2.astype(BF16), cmp_b2.reshape(2, 1, HEAD_DIM)


PAGES_PER_PERM = 2


def _pe_bias_body(pe_ref, w1_ref, o_ref):
    for slot in range(2):
        cols = []
        for half in range(2):
            acc = jnp.zeros((2 * SUBLANE, 2 * CMP_HIDDEN), F32)
            for s in range(CMP_STRIDE):
                row = jnp.broadcast_to(pe_ref[slot, half, s:s + 1, :], (2 * SUBLANE, LANE))
                w = w1_ref[slot, s * LANE:(s + 1) * LANE, half * 2 * CMP_HIDDEN:(half + 1) * 2 * CMP_HIDDEN]
                acc = acc + _dot_exact_rhs(row, w)
            cols.append(acc)
        o_ref[slot] = jnp.concatenate(cols, axis=1)


def _pe_bias(pe, w1):
    return pl.pallas_call(
        _pe_bias_body,
        out_shape=jax.ShapeDtypeStruct((2, 2 * SUBLANE, 4 * CMP_HIDDEN), F32),
        name="pe_bias",
    )(pe, w1)


def _permute_pages(pages, perm_ref, xp_ref):
    for jp in range(N_CHUNK_TILES // PAGES_PER_PERM):
        xt = jnp.concatenate([pages[PAGES_PER_PERM * jp + jj][0:2 * KV_WIDTH, :] for jj in range(PAGES_PER_PERM)],
                             axis=1)
        xp_ref[jp] = _dot_nt(perm_ref[...], xt.astype(BF16)).astype(BF16)


def _compress_mlp(xp_ref, w1_ref, peb_ref, b1_ref, w2_ref, b2_ref, kc_ref, vc_ref, hid_ref):
    n_perm = N_CHUNK_TILES // PAGES_PER_PERM
    slab = PAGES_PER_PERM * CHUNKS_PER_TILE
    out_refs = (kc_ref, vc_ref)

    def first_layer(slot, gp):
        cb = slot * (KV_HEADS // 2) + gp
        lhs = jnp.concatenate(
            [jnp.concatenate([xp_ref[jp, s * slab:(s + 1) * slab, cb * LANE:(cb + 1) * LANE]
                              for jp in range(n_perm)], axis=0)
             for s in range(CMP_STRIDE)], axis=1)
        hid_ref[gp] = _dot(lhs, w1_ref[slot])
        return gp

    def second_layer(slot, gp, buf):
        acc = hid_ref[buf] + peb_ref[slot, 0:1, :]
        hid = (acc[:, 0:2 * CMP_HIDDEN] + pltpu.roll(acc[:, 2 * CMP_HIDDEN:], TILE - 1, 0) + b1_ref[slot])
        act = _gelu_tanh(hid).astype(BF16)
        for gj in range(2):
            g = 2 * gp + gj
            out_refs[slot][:, g * HEAD_DIM:(g + 1) * HEAD_DIM] = (
                _dot(act[:, gj * CMP_HIDDEN:(gj + 1) * CMP_HIDDEN], w2_ref[slot]) + b2_ref[slot])

    pending = None
    for slot in range(2):
        for gp in range(KV_HEADS // 2):
            acc = first_layer(slot, gp)
            if pending is not None:
                second_layer(*pending)
            pending = (slot, gp, acc)
    second_layer(*pending)


def _compress_paged_weights(cmp_w1):
    w = cmp_w1.reshape(2, 2, CMP_STRIDE, HEAD_DIM, CMP_HIDDEN)
    eye = jnp.eye(2, dtype=w.dtype)
    w1 = jnp.einsum('absdh,ij->asidbjh', w, eye).reshape(2, CMP_STRIDE * LANE, 4 * CMP_HIDDEN).astype(BF16)
    rows = PAGES_PER_PERM * PAGE_SIZE
    r = np.arange(rows)
    s, page, chunk = r // (PAGES_PER_PERM * CHUNKS_PER_TILE), (r // CHUNKS_PER_TILE) % PAGES_PER_PERM, r % CHUNKS_PER_TILE
    src = page * PAGE_SIZE + chunk * CMP_STRIDE + s
    perm = (np.arange(rows)[None, :] == src[:, None]).astype(np.float32)
    return w1, jnp.asarray(perm, BF16)


def _bucket_thresholds():
    n = np.arange(0, 2 * REL_MAX_DIST)
    max_exact = REL_BUCKETS // 2
    nf = np.maximum(n, 1).astype(np.float32)
    large = max_exact + (np.log(nf / np.float32(max_exact)) / np.float32(math.log(REL_MAX_DIST / max_exact))
                         * np.float32(REL_BUCKETS - max_exact)).astype(np.int32)
    bucket = np.where(n < max_exact, n, np.minimum(large, REL_BUCKETS - 1))
    assert (np.diff(bucket) >= 0).all() and (np.diff(bucket) <= 1).all()
    return [int(np.argmax(bucket >= k)) for k in range(1, REL_BUCKETS)]


BUCKET_START = _bucket_thresholds()
CMP_BAND = 2 * SUBLANE
assert (BUCKET_START[-1] - 1 + CMP_BLOCK - 1) // CMP_STRIDE <= SUBLANE
assert (TILE - CMP_BLOCK) // CMP_STRIDE < SUBLANE


def _rel_bias(dist, tab_ref):
    bias = jnp.broadcast_to(tab_ref[0:1, :], dist.shape)
    for k in range(1, REL_BUCKETS):
        bias = jnp.where(dist >= BUCKET_START[k - 1], tab_ref[k:k + 1, :], bias)
    return bias


def _bias_prompt_body(tab_ref, cmpb_ref, near_ref, edge_ref):
    qb = pl.program_id(0)
    shape = (TILE, Q_PER_KV * TILE)
    row = lax.broadcasted_iota(jnp.int32, shape, 0)
    qi = lax.broadcasted_iota(jnp.int32, shape, 1) % TILE
    dist_c = qb * TILE + qi - (row * CMP_STRIDE + CMP_BLOCK - 1)
    band0 = pl.multiple_of(jnp.maximum(qb * (TILE // CMP_STRIDE) - SUBLANE, 0), SUBLANE)
    band_shape = (CMP_BAND, Q_PER_KV * TILE)
    band_row = band0 + lax.broadcasted_iota(jnp.int32, band_shape, 0)
    band_qi = lax.broadcasted_iota(jnp.int32, band_shape, 1) % TILE
    dist_b = qb * TILE + band_qi - (band_row * CMP_STRIDE + CMP_BLOCK - 1)
    for g in range(KV_HEADS):
        cmpb_ref[g] = jnp.where(dist_c >= 0, tab_ref[g, REL_BUCKETS - 1:REL_BUCKETS, :], NEG)
        cmpb_ref[g, pl.ds(band0, CMP_BAND), :] = jnp.where(dist_b >= 0, _rel_bias(dist_b, tab_ref.at[g]), NEG)

    @pl.when(qb == 0)
    def _():
        for g in range(KV_HEADS):
            for delta in range(2):
                dist = delta * TILE + qi - row
                b = _rel_bias(dist, tab_ref.at[g])
                near_ref[g, delta] = jnp.where(dist >= 0, b * LOG2E, NEG)
        edge_ref[...] = jnp.where(qi <= row, 0.0, NEG)


def _bias_prompt(tab, n_qb):
    lanes = Q_PER_KV * TILE
    return pl.pallas_call(
        _bias_prompt_body,
        grid=(n_qb,),
        in_specs=[pl.BlockSpec(tab.shape, lambda i: (0, 0, 0))],
        out_specs=[pl.BlockSpec((None, KV_HEADS, TILE, lanes), lambda i: (i, 0, 0, 0)),
                   pl.BlockSpec((KV_HEADS, 2, TILE, lanes), lambda i: (0, 0, 0, 0)),
                   pl.BlockSpec((TILE, lanes), lambda i: (0, 0))],
        out_shape=[jax.ShapeDtypeStruct((n_qb, KV_HEADS, TILE, lanes), F32),
                   jax.ShapeDtypeStruct((KV_HEADS, 2, TILE, lanes), F32),
                   jax.ShapeDtypeStruct((TILE, lanes), F32)],
        compiler_params=_cparams("arbitrary"),
        name="bias_prompt",
    )(tab)


def _bias_sample_body(tab_ref, cmpbt_ref, selb_ref, winb_ref, *, past_len, n_new, win_len):
    qt = lax.broadcasted_iota(jnp.int32, (TILE, LANE), 1) % n_new
    row = lax.broadcasted_iota(jnp.int32, (TILE, LANE), 0)
    dist = past_len + qt - (row * CMP_STRIDE + CMP_BLOCK - 1)
    cmpbt_ref[...] = jnp.where(dist >= 0, _rel_bias(dist, tab_ref), NEG)
    for j in range(selb_ref.shape[1] // TILE):
        key = row + j * TILE
        dist = past_len + qt - key
        ok = (dist >= 0) & (key < past_len + n_new)
        selb_ref[:, j * TILE:(j + 1) * TILE] = jnp.where(ok, _rel_bias(dist, tab_ref), NEG).T
    for j in range(winb_ref.shape[1] // TILE):
        key = row + j * TILE
        dist = win_len + qt - key
        ok = (dist >= 0) & (dist <= WINDOW) & (key < win_len + n_new)
        winb_ref[:, j * TILE:(j + 1) * TILE] = jnp.where(ok, _rel_bias(dist, tab_ref), NEG).T


def _bias_sample(tab, past_len, n_new, win_len):
    return pl.pallas_call(
        functools.partial(_bias_sample_body, past_len=past_len, n_new=n_new, win_len=win_len),
        out_shape=[jax.ShapeDtypeStruct((TILE, LANE), F32),
                   jax.ShapeDtypeStruct((LANE, past_len + TILE), F32),
                   jax.ShapeDtypeStruct((LANE, win_len + TILE), F32)],
        name="bias_sample",
    )(tab)


def _select_blocks(imp, blk, q_blk):
    valid = blk <= q_blk
    forced = (blk == 0) | (blk == q_blk) | (blk == q_blk - 1)
    score = jnp.where(valid, imp + jnp.where(forced, FORCE_BONUS, 0.0), -jnp.inf)
    rank = jnp.zeros(score.shape, jnp.int32)
    for i in range(score.shape[0]):
        row = score[i:i + 1, :]
        beats = (row > score) | ((row == score) & (blk > i))
        rank = rank + beats.astype(jnp.int32)
    return jnp.where(valid & (rank < N_SELECT), 0.0, NEG)


FAR_RUN = 2


def _nsa_prompt_body(q_ref, gd_ref, kc_ref, vc_ref, sk_ref, sv_ref, wk_ref, wv_ref, cmpb_ref, near_ref,
                     edge_ref, tab_ref, impt_ref, o_ref, qs_ref, seladd_ref, selfar_ref, m_ref, l_ref, acc_ref,
                     out_ref):
    qb = pl.program_id(1)
    lanes = Q_PER_KV * TILE
    q = q_ref[...] * SCALE
    gate_t = jax.nn.sigmoid(gd_ref[...]).T
    n_sel = seladd_ref.shape[1]
    blk = lax.broadcasted_iota(jnp.int32, (n_sel, TILE), 0)
    q_blk = (qb * TILE + lax.broadcasted_iota(jnp.int32, (n_sel, TILE), 1)) // SEL_BLOCK
    groups = range(KV_HEADS)

    def hs(g):
        return slice(g * HEAD_DIM, (g + 1) * HEAD_DIM)

    def gate_row(branch, g):
        c0 = branch * N_HEADS + g * Q_PER_KV
        return jnp.concatenate([gate_t[c0 + r:c0 + r + 1, :] for r in range(Q_PER_KV)], axis=1)

    def far_bias(g):
        return tab_ref[g, REL_BUCKETS - 1:REL_BUCKETS, :] * LOG2E

    def stack_heads(x, g):
        return jnp.concatenate(
            [x[:, (g * Q_PER_KV + r) * HEAD_DIM:(g * Q_PER_KV + r + 1) * HEAD_DIM] for r in range(Q_PER_KV)],
            axis=0).astype(BF16)

    q2 = q * LOG2E
    kc = kc_ref[...].astype(BF16)
    vct = vc_ref[...].T.astype(BF16)
    scores = []
    for g in groups:
        qs_ref[g] = stack_heads(q2, g)
        scores.append(_dot_nt(kc[:, hs(g)], stack_heads(q, g)) + cmpb_ref[g])
    probs = []
    for g in groups:
        m = jnp.max(scores[g], axis=0, keepdims=True)
        e = jnp.exp(scores[g] - m)
        norm = jnp.where(m > 0.5 * NEG, 1.0 / jnp.maximum(jnp.sum(e, axis=0, keepdims=True), TINY), 0.0)
        probs.append(e * norm)
    imps = []
    for g in groups:
        p = probs[g]
        out_ref[g] = gate_row(0, g) * _dot(vct[hs(g), :], p.astype(BF16))
        psum = p[:, 0:TILE]
        for r in range(1, Q_PER_KV):
            psum = psum + p[:, r * TILE:(r + 1) * TILE]
        imps.append(_dot_exact_lhs(impt_ref[...], psum))
    for g in groups:
        sa = _select_blocks(imps[g], blk, q_blk)
        sa = jnp.concatenate([sa] * Q_PER_KV, axis=1)
        seladd_ref[g] = sa
        selfar_ref[g] = sa + far_bias(g)

    def sel_rows(ref, g, kt):
        half = SEL_BLOCK
        return jnp.concatenate(
            [jnp.broadcast_to(ref[g, pl.ds(2 * kt, 1), :], (half, lanes)),
             jnp.broadcast_to(ref[g, pl.ds(2 * kt + 1, 1), :], (half, lanes))], axis=0)

    def reset():
        m_ref[...] = jnp.full(m_ref.shape, -jnp.inf, F32)
        l_ref[...] = jnp.zeros(l_ref.shape, F32)
        acc_ref[...] = jnp.zeros(acc_ref.shape, F32)

    def tiles(k_ref, v_ref, parts):
        starts = [pl.multiple_of(kt * TILE, TILE) for kt, _ in parts]
        k = jnp.concatenate([k_ref[pl.ds(r0, TILE), :] for r0 in starts], axis=0).astype(BF16)
        vt = jnp.concatenate([v_ref[pl.ds(r0, TILE), :] for r0 in starts], axis=0).T.astype(BF16)
        scores = []
        for g in groups:
            s = _dot_nt(k[:, hs(g)], qs_ref[g])
            scores.append([s[i * TILE:(i + 1) * TILE] + bias_of(g) for i, (_, bias_of) in enumerate(parts)])
        probs, alphas = [], []
        for g in groups:
            m = m_ref[g]
            m_new = m
            for s in scores[g]:
                m_new = jnp.maximum(m_new, jnp.max(s, axis=0, keepdims=True))
            alpha = jnp.exp2(m - m_new)
            ps = [jnp.exp2(s - m_new) for s in scores[g]]
            total = jnp.sum(ps[0], axis=0, keepdims=True)
            for p in ps[1:]:
                total = total + jnp.sum(p, axis=0, keepdims=True)
            m_ref[g] = m_new
            l_ref[g] = alpha * l_ref[g] + total
            probs.append(jnp.concatenate([p.astype(BF16) for p in ps], axis=0))
            alphas.append(alpha)
        pvs = [_dot(vt[hs(g), :], probs[g]) for g in groups]
        for g in groups:
            acc_ref[g] = alphas[g] * acc_ref[g] + pvs[g]

    def finish(branch, g):
        return gate_row(branch, g) * (acc_ref[g] / jnp.maximum(l_ref[g], TINY))

    reset()

    n_far = jnp.maximum(qb - 1, 0)

    def far_body(i, carry):
        hi = n_far - 1 - FAR_RUN * i
        parts = []
        for j in range(FAR_RUN - 1, -1, -1):
            kt = jnp.maximum(hi - j, 0)
            off = jnp.where(hi - j >= 0, 0.0, NEG)
            parts.append((kt, lambda g, kt=kt, off=off: sel_rows(selfar_ref, g, kt) + off))
        tiles(sk_ref, sv_ref, parts)
        return carry

    lax.fori_loop(0, (n_far + FAR_RUN - 1) // FAR_RUN, far_body, 0)
    kt1 = jnp.maximum(qb - 1, 0)
    off1 = jnp.where(qb >= 1, 0.0, NEG)
    tiles(sk_ref, sv_ref, [(kt1, lambda g: near_ref[g, 1] + (sel_rows(seladd_ref, g, kt1) + off1)),
                           (qb, lambda g: near_ref[g, 0] + sel_rows(seladd_ref, g, qb))])
    for g in groups:
        out_ref[g] = out_ref[g] + finish(1, g)

    reset()
    parts = []
    for delta in range(WINDOW // TILE, -1, -1):
        kt = qb - delta
        off = jnp.where(kt >= 0, 0.0, NEG)
        if delta == WINDOW // TILE:
            bias_of = lambda g, off=off: edge_ref[...] + (far_bias(g) + off)
        elif delta >= 2:
            bias_of = lambda g, off=off: far_bias(g) + off
        else:
            bias_of = lambda g, off=off, delta=delta: near_ref[g, delta] + off
        parts.append((jnp.maximum(kt, 0), bias_of))
    tiles(wk_ref, wv_ref, parts[:3])
    tiles(wk_ref, wv_ref, parts[3:])
    outs = []
    for g in groups:
        out_t = out_ref[g] + finish(2, g)
        outs += [out_t[:, r * TILE:(r + 1) * TILE].T for r in range(Q_PER_KV)]
    o_ref[...] = jnp.concatenate(outs, axis=1).astype(BF16)


def _nsa_prompt(p3, kc, vc, cmpb, near, edge, tab, impt):
    b, t, _ = p3.shape
    n_qb = t // TILE
    lanes = Q_PER_KV * TILE
    kvcol = lambda slot: (P_KV + slot * KV_WIDTH) // KV_WIDTH
    seq_spec = lambda slot: pl.BlockSpec((None, t, KV_WIDTH), lambda i, j: (i, 0, kvcol(slot)))
    const = lambda shape: pl.BlockSpec(shape, lambda i, j: (0,) * len(shape))
    return pl.pallas_call(
        _nsa_prompt_body,
        grid=(b, n_qb),
        in_specs=[pl.BlockSpec((None, TILE, ATTN_WIDTH), lambda i, j: (i, j, 0)),
                  pl.BlockSpec((None, TILE, LANE), lambda i, j: (i, j, P_GD // LANE)),
                  pl.BlockSpec((None, TILE, KV_WIDTH), lambda i, j: (i, 0, 0)),
                  pl.BlockSpec((None, TILE, KV_WIDTH), lambda i, j: (i, 0, 0)),
                  seq_spec(2), seq_spec(3), seq_spec(4), seq_spec(5),
                  pl.BlockSpec((None, KV_HEADS, TILE, lanes), lambda i, j: (j, 0, 0, 0)),
                  const(near.shape), const(edge.shape), const(tab.shape), const(impt.shape)],
        out_specs=pl.BlockSpec((None, TILE, ATTN_WIDTH), lambda i, j: (i, j, 0)),
        out_shape=jax.ShapeDtypeStruct((b, t, ATTN_WIDTH), BF16),
        scratch_shapes=[pltpu.VMEM((KV_HEADS, lanes, HEAD_DIM), BF16),
                        pltpu.VMEM((KV_HEADS, impt.shape[0], lanes), F32),
                        pltpu.VMEM((KV_HEADS, impt.shape[0], lanes), F32),
                        pltpu.VMEM((KV_HEADS, 1, lanes), F32),
                        pltpu.VMEM((KV_HEADS, 1, lanes), F32),
                        pltpu.VMEM((KV_HEADS, HEAD_DIM, lanes), F32),
                        pltpu.VMEM((KV_HEADS, HEAD_DIM, lanes), F32)],
        compiler_params=_cparams("parallel", "arbitrary"),
        name="nsa_prompt",
    )(p3, p3, kc, vc, p3, p3, p3, p3, cmpb, near, edge, tab, impt)


def _shift_window(win_ref, new_rows, place_ref, winout_ref, n_new):
    n_wt = win_ref.shape[1] // TILE
    lane = lax.broadcasted_iota(jnp.int32, (1, LANE), 1)
    hi, mid, lo = _split3(new_rows)
    tail = _dot_tn(hi, place_ref[...]) + _dot_tn(mid, place_ref[...]) + _dot_tn(lo, place_ref[...])
    rolled = [pltpu.roll(win_ref[:, j * TILE:(j + 1) * TILE], TILE - n_new, 1) for j in range(n_wt)]
    for j in range(n_wt):
        nxt = rolled[j + 1] if j + 1 < n_wt else tail
        winout_ref[:, j * TILE:(j + 1) * TILE] = jnp.where(lane < TILE - n_new, rolled[j], nxt)


def _nsa_sample_body(*refs, past_len, n_new):
    n_pages = past_len // TILE
    pt_ref, q_ref, gd_ref, newkv_ref, newwin_ref = refs[:5]
    pages = refs[5:5 + n_pages]
    (win_ref, cmpbt_ref, selb_ref, winb_ref, impt_ref, rsum_ref, gsel_ref, expand_ref, place_ref,
     perm_ref, w1_ref, peb_ref, b1_ref, w2_ref, b2_ref,
     o_ref, winout_ref, s_scr, xp_ref, hid_ref, kc_ref, vc_ref) = refs[5 + n_pages:]
    del pt_ref
    win_len = win_ref.shape[1]
    sel_rows = 2 * KV_WIDTH
    _permute_pages(pages, perm_ref, xp_ref)
    _compress_mlp(xp_ref, w1_ref, peb_ref, b1_ref, w2_ref, b2_ref, kc_ref, vc_ref, hid_ref)
    rows_per_g = Q_PER_KV * n_new
    row_g = lax.broadcasted_iota(jnp.int32, (LANE, 1), 0) // rows_per_g

    q = q_ref[...] * SCALE
    qs = jnp.concatenate([q[:, h * HEAD_DIM:(h + 1) * HEAD_DIM] for h in range(N_HEADS)], axis=0)
    qs4 = jnp.concatenate([qs] * KV_HEADS, axis=1)
    col_g = lax.broadcasted_iota(jnp.int32, qs4.shape, 1) // HEAD_DIM
    qbd = jnp.where(row_g == col_g, qs4, 0.0).astype(BF16)

    sig = jax.nn.sigmoid(gd_ref[...])
    t_row = lax.broadcasted_iota(jnp.int32, (n_new, LANE), 0)
    t_lane = lax.broadcasted_iota(jnp.int32, (n_new, LANE), 1) % n_new
    lane_g = lax.broadcasted_iota(jnp.int32, (1, LANE), 1) // rows_per_g

    def gate_row(branch):
        spread = _dot_exact_rhs(sig, gsel_ref[branch])
        return jnp.sum(jnp.where(t_row == t_lane, spread, 0.0), axis=0, keepdims=True)

    def pick(full_t):
        out = jnp.zeros((HEAD_DIM, LANE), F32)
        for g in range(KV_HEADS):
            out = out + jnp.where(lane_g == g, full_t[g * HEAD_DIM:(g + 1) * HEAD_DIM, :], 0.0)
        return out

    def compressed():
        kc = kc_ref[...].astype(BF16)
        s_t = _dot_nt(kc, qbd) + cmpbt_ref[...]
        m_t = jnp.max(s_t, axis=0, keepdims=True)
        e_t = jnp.exp(s_t - m_t)
        p_t = e_t / jnp.maximum(jnp.sum(e_t, axis=0, keepdims=True), TINY)
        out_c = gate_row(0) * pick(_dot(vc_ref[...].T.astype(BF16), p_t.astype(BF16)))
        imp = _dot_exact_lhs(impt_ref[...], _dot_exact_rhs(p_t, rsum_ref[...]))
        blk = lax.broadcasted_iota(jnp.int32, imp.shape, 0)
        q_blk = (past_len + lax.broadcasted_iota(jnp.int32, imp.shape, 1) % n_new) // SEL_BLOCK
        chosen_t = jnp.where(_select_blocks(imp, blk, q_blk) == 0.0, 1.0, 0.0)
        chosen = jnp.concatenate([chosen_t, jnp.zeros((LANE - chosen_t.shape[0], LANE), F32)], axis=0).T
        return out_c, (_dot(chosen.astype(BF16), expand_ref[...]) - 1.0) * (-NEG)

    def attend(tiles, bias_ref, mask):
        m = jnp.full((LANE, 1), -jnp.inf, F32)
        for j, (score, _) in enumerate(tiles):
            s = score() + bias_ref[:, j * TILE:(j + 1) * TILE]
            if mask is not None:
                s = s + mask[:, j * TILE:(j + 1) * TILE]
            s_scr[:, j * TILE:(j + 1) * TILE] = s
            m = jnp.maximum(m, jnp.max(s, axis=1, keepdims=True))
        l = jnp.zeros((LANE, 1), F32)
        acc = jnp.zeros((LANE, KV_WIDTH), F32)
        for j, (_, vt) in enumerate(tiles):
            p = jnp.exp(s_scr[:, j * TILE:(j + 1) * TILE] - m)
            l = l + jnp.sum(p, axis=1, keepdims=True)
            acc = acc + _dot_nt(p.astype(BF16), vt())
        own = jnp.zeros((LANE, HEAD_DIM), F32)
        for g in range(KV_HEADS):
            own = own + jnp.where(row_g == g, acc[:, g * HEAD_DIM:(g + 1) * HEAD_DIM], 0.0)
        return (own / jnp.maximum(l, TINY)).T

    def stored_tile(ref, row0, lanes):
        return (lambda: _dot(qbd, ref[row0:row0 + KV_WIDTH, lanes].astype(BF16)),
                lambda: ref[row0 + KV_WIDTH:row0 + 2 * KV_WIDTH, lanes].astype(BF16))

    def new_tile(ref):
        rows = jnp.concatenate([ref[...], jnp.zeros((TILE - n_new, ref.shape[1]), F32)], axis=0)
        k = rows[:, 0:KV_WIDTH].astype(BF16)
        vt = rows[:, KV_WIDTH:2 * KV_WIDTH].T.astype(BF16)
        return rows, (lambda: _dot_nt(qbd, k), lambda: vt)

    out_t, sel_mask = compressed()

    _, new_kv = new_tile(newkv_ref)
    tiles = [stored_tile(pg, sel_rows, slice(None)) for pg in pages] + [new_kv]
    out_t = out_t + gate_row(1) * attend(tiles, selb_ref, sel_mask)

    new_rows, new_win = new_tile(newwin_ref)
    n_wt = win_len // TILE
    tiles = [stored_tile(win_ref, 0, slice(j * TILE, (j + 1) * TILE)) for j in range(n_wt)] + [new_win]
    out_t = out_t + gate_row(2) * attend(tiles, winb_ref, None)
    out = out_t.T
    o_ref[...] = jnp.concatenate([out[h * n_new:(h + 1) * n_new, :] for h in range(N_HEADS)], axis=1).astype(BF16)
    _shift_window(win_ref, new_rows, place_ref, winout_ref, n_new)


def _nsa_sample(p3, cache_t, win_t, page_table, consts, cmp_consts):
    b, n_new, _ = p3.shape
    n_pages = page_table.shape[1]
    past_len = n_pages * PAGE_SIZE
    win_len = win_t.shape[2]
    kvblk = lambda slot: (P_KV + slot * KV_WIDTH) // (2 * KV_WIDTH)
    const = lambda arr: pl.BlockSpec(arr.shape, lambda i, pt: (0,) * arr.ndim)
    in_specs = [pl.BlockSpec((None, n_new, ATTN_WIDTH), lambda i, pt: (i, 0, 0)),
                pl.BlockSpec((None, n_new, LANE), lambda i, pt: (i, 0, P_GD // LANE)),
                pl.BlockSpec((None, n_new, 2 * KV_WIDTH), lambda i, pt: (i, 0, kvblk(2))),
                pl.BlockSpec((None, n_new, 2 * KV_WIDTH), lambda i, pt: (i, 0, kvblk(4)))]
    in_specs += [pl.BlockSpec((None, N_CACHE_SLOTS * KV_WIDTH, PAGE_SIZE),
                              functools.partial(lambda j, i, pt: (pt[i, j], 0, 0), j)) for j in range(n_pages)]
    consts = list(consts) + list(cmp_consts)
    in_specs += [pl.BlockSpec((None, 2 * KV_WIDTH, win_len), lambda i, pt: (i, 0, 0))] + [const(a) for a in consts]
    perm_rows = PAGES_PER_PERM * PAGE_SIZE
    return pl.pallas_call(
        functools.partial(_nsa_sample_body, past_len=past_len, n_new=n_new),
        grid_spec=pltpu.PrefetchScalarGridSpec(
            num_scalar_prefetch=1, grid=(b,), in_specs=in_specs,
            out_specs=[pl.BlockSpec((None, n_new, ATTN_WIDTH), lambda i, pt: (i, 0, 0)),
                       pl.BlockSpec((None, 2 * KV_WIDTH, win_len), lambda i, pt: (i, 0, 0))],
            scratch_shapes=[pltpu.VMEM((LANE, past_len + TILE), F32),
                            pltpu.VMEM((n_pages // PAGES_PER_PERM, perm_rows, 2 * KV_WIDTH), BF16),
                            pltpu.VMEM((KV_HEADS // 2, TILE, 4 * CMP_HIDDEN), F32),
                            pltpu.VMEM((TILE, KV_WIDTH), F32),
                            pltpu.VMEM((TILE, KV_WIDTH), F32)]),
        out_shape=[jax.ShapeDtypeStruct((b, n_new, ATTN_WIDTH), BF16),
                   jax.ShapeDtypeStruct((b, 2 * KV_WIDTH, win_len), F32)],
        compiler_params=_cparams("parallel"),
        name="nsa_sample",
    )(page_table, p3, p3, p3, p3, *([cache_t] * n_pages), win_t, *consts)


def _sample_selectors(n_new, n_keys):
    lanes = np.arange(LANE)
    h, t = lanes // n_new, lanes % n_new
    rsum = ((h[:, None] // Q_PER_KV == h[None, :] // Q_PER_KV) & (t[:, None] == t[None, :])).astype(np.float32)
    gsel = np.stack([(lanes[:, None] == br * N_HEADS + h[None, :]).astype(np.float32) for br in range(3)])
    expand = (np.arange(n_keys)[None, :] // SEL_BLOCK == lanes[:, None]).astype(np.float32)
    place = (lanes[None, :] == LANE - n_new + lanes[:, None]).astype(np.float32)
    return tuple(jnp.asarray(a, BF16) for a in (rsum, gsel, expand, place))


def _dot_tn(a, b):
    return lax.dot_general(a, b, (((0,), (0,)), ((), ())), preferred_element_type=F32)


def _ssd_body(*refs, q_len, carry):
    if carry:
        (z_ref, xbc_ref, gd_ref, cw_ref, cb_ref, dtb_ref, alog_ref, dexp_ref, ng_ref, e16_ref,
         y_ref, hout_ref, prev_ref, h_scr) = refs
    else:
        (z_ref, xbc_ref, gd_ref, prev_ref, h0_ref, cw_ref, cb_ref, dtb_ref, alog_ref, dexp_ref, ng_ref, e16_ref,
         y_ref, hout_ref) = refs
    n_seq = TILE // q_len
    x = xbc_ref[...]
    if carry:
        @pl.when(pl.program_id(1) == 0)
        def _():
            prev_ref[...] = jnp.zeros_like(prev_ref)
            h_scr[...] = jnp.zeros_like(h_scr)
    prev = prev_ref[...]
    acc = cb_ref[...] + cw_ref[SSM_CONV - 1:SSM_CONV, :] * x
    for k in range(1, SSM_CONV):
        acc = acc + cw_ref[SSM_CONV - 1 - k:SSM_CONV - k, :] * _shift_rows(x, prev, k, q_len)
    if carry:
        prev_ref[...] = x
    xc = _silu(acc)
    xs = xc[:, 0:SSM_INNER]
    b_all = xc[:, SSM_INNER:SSM_INNER + SSM_GROUPS * SSM_STATE]
    c_all = xc[:, SSM_INNER + SSM_GROUPS * SSM_STATE:]

    col = lax.broadcasted_iota(jnp.int32, (1, LANE), 1)
    dt_cols = (col >= DT_COL) & (col < DT_COL + SSM_HEADS)
    dt = jnp.where(dt_cols, jax.nn.softplus(gd_ref[...] + dtb_ref[...]), 0.0)
    a = dt * jnp.where(dt_cols, -jnp.exp(alog_ref[...]), 0.0)
    ri = lax.broadcasted_iota(jnp.int32, (TILE, TILE), 0)
    ci = lax.broadcasted_iota(jnp.int32, (TILE, TILE), 1)
    same = (ri // q_len) == (ci // q_len)
    causal = same & (ci <= ri)
    acum = _dot_exact_lhs(causal.astype(BF16), a)
    last_sel = (ci == (ri // q_len) * q_len + (q_len - 1)).astype(BF16)
    alast = _dot_exact_lhs(last_sel, acum)
    acum_t = acum.T
    e16 = e16_ref[...]
    xw = xs * _dot_exact_rhs(dt, e16)
    xwe = xw * _dot_exact_rhs(jnp.exp(alast - acum), e16)
    grow = jnp.exp(_dot_exact_rhs(acum, e16))
    rowseq = lax.broadcasted_iota(jnp.int32, (TILE, 1), 0) // q_len

    cgs = [c_all[:, g * SSM_STATE:(g + 1) * SSM_STATE].astype(BF16) for g in range(SSM_GROUPS)]
    bgs = [b_all[:, g * SSM_STATE:(g + 1) * SSM_STATE].astype(BF16) for g in range(SSM_GROUPS)]
    cbs = [_dot_nt(cgs[g], bgs[g]) for g in range(SSM_GROUPS)]
    weights = []
    for h in range(SSM_HEADS):
        c = DT_COL + h
        seg = acum[:, c:c + 1] - acum_t[c:c + 1, :]
        decay = jnp.where(causal, jnp.exp(jnp.where(causal, seg, 0.0)), 0.0)
        weights.append((cbs[h // SSM_HPG] * decay).astype(BF16))
    ys = [_dot(weights[h], xw[:, h * SSM_HEADDIM:(h + 1) * SSM_HEADDIM].astype(BF16)) for h in range(SSM_HEADS)]
    for g in range(SSM_GROUPS):
        cg, bg = cgs[g], bgs[g]
        gl = slice(g * SSM_HPG * SSM_HEADDIM, (g + 1) * SSM_HPG * SSM_HEADDIM)
        hl = slice(g * SSM_HPG, (g + 1) * SSM_HPG)
        xwe_g = xwe[:, gl]
        if carry:
            h_old = h_scr[hl].reshape(SSM_HPG * SSM_HEADDIM, SSM_STATE)
            y_off = _dot_nt(cg, h_old.astype(BF16))
            scale = jnp.concatenate(
                [jnp.broadcast_to(jnp.exp(alast[0:1, DT_COL + g * SSM_HPG + r:DT_COL + g * SSM_HPG + r + 1]),
                                  (SSM_HEADDIM, SSM_STATE)) for r in range(SSM_HPG)], axis=0)
            h_new = h_old * scale + _dot_tn(xwe_g.astype(BF16), bg)
            h_scr[hl] = h_new.reshape(SSM_HPG, SSM_HEADDIM, SSM_STATE)
        else:
            offs = []
            for s in range(n_seq):
                h_old = h0_ref[s, hl].reshape(SSM_HPG * SSM_HEADDIM, SSM_STATE)
                offs.append(_dot_nt(cg, h_old.astype(BF16))[s * q_len:(s + 1) * q_len, :])
                a_row = alast[s * q_len:s * q_len + 1, :]
                scale = jnp.concatenate(
                    [jnp.broadcast_to(jnp.exp(a_row[:, DT_COL + g * SSM_HPG + r:DT_COL + g * SSM_HPG + r + 1]),
                                      (SSM_HEADDIM, SSM_STATE)) for r in range(SSM_HPG)], axis=0)
                x_s = jnp.where(rowseq == s, xwe_g, 0.0).astype(BF16)
                h_new = h_old * scale + _dot_tn(x_s, bg)
                hout_ref[s, hl] = h_new.reshape(SSM_HPG, SSM_HEADDIM, SSM_STATE)
            y_off = jnp.concatenate(offs, axis=0)
        for r in range(SSM_HPG):
            h = g * SSM_HPG + r
            ys[h] = ys[h] + y_off[:, r * SSM_HEADDIM:(r + 1) * SSM_HEADDIM] * grow[:, h * SSM_HEADDIM:(h + 1) * SSM_HEADDIM]
    if carry:
        @pl.when(pl.program_id(1) == pl.num_programs(1) - 1)
        def _():
            hout_ref[...] = h_scr[...]

    y = jnp.concatenate(ys, axis=1) + xs * dexp_ref[...]
    y = y * _silu(z_ref[...])
    gw = SSM_INNER // SSM_GROUPS
    parts = []
    for g in range(SSM_GROUPS):
        blk = y[:, g * gw:(g + 1) * gw]
        parts.append(blk * lax.rsqrt(jnp.mean(blk * blk, axis=-1, keepdims=True) + EPS))
    y_ref[...] = (jnp.concatenate(parts, axis=1) * ng_ref[...]).astype(BF16)


def _ssd(p3, prev, h0, cw, cb, dtb, alog, dexp, ng, e16, *, q_len):
    n_grp, rows, _ = p3.shape
    carry = prev is None
    n_chunks = rows // TILE
    n_seq = TILE // q_len
    zspec = pl.BlockSpec((None, TILE, SSM_INNER), lambda i, c: (i, c, P_Z // SSM_INNER))
    xspec = pl.BlockSpec((None, TILE, CONV_DIM), lambda i, c: (i, c, P_XBC // CONV_DIM))
    gspec = pl.BlockSpec((None, TILE, LANE), lambda i, c: (i, c, P_GD // LANE))
    const = lambda arr: pl.BlockSpec(arr.shape, lambda i, c: (0,) * arr.ndim)
    params = [cw, cb, dtb, alog, dexp, ng, e16]
    in_specs = [zspec, xspec, gspec]
    args = [p3, p3, p3]
    if carry:
        n_out_seq = n_grp
        hspec = pl.BlockSpec((None, SSM_HEADS, SSM_HEADDIM, SSM_STATE), lambda i, c: (i, 0, 0, 0))
        scratch = [pltpu.VMEM((TILE, CONV_DIM), F32), pltpu.VMEM((SSM_HEADS, SSM_HEADDIM, SSM_STATE), F32)]
    else:
        n_out_seq = n_grp * n_chunks * n_seq
        hspec = pl.BlockSpec((n_seq, SSM_HEADS, SSM_HEADDIM, SSM_STATE), lambda i, c: (i * n_chunks + c, 0, 0, 0))
        in_specs += [pl.BlockSpec((None, TILE, CONV_DIM), lambda i, c: (i, c, 0)), hspec]
        args += [prev, h0]
        scratch = []
    in_specs += [const(a) for a in params]
    return pl.pallas_call(
        functools.partial(_ssd_body, q_len=q_len, carry=carry),
        grid=(n_grp, n_chunks),
        in_specs=in_specs,
        out_specs=[pl.BlockSpec((None, TILE, SSM_INNER), lambda i, c: (i, c, 0)), hspec],
        out_shape=[jax.ShapeDtypeStruct((n_grp, rows, SSM_INNER), BF16),
                   jax.ShapeDtypeStruct((n_out_seq, SSM_HEADS, SSM_HEADDIM, SSM_STATE), F32)],
        scratch_shapes=scratch,
        compiler_params=_cparams("parallel", "arbitrary"),
        name="ssd",
    )(*args, *params)


def _importance_map_t(n_sel):
    n_cmp = TILE - 1
    c0 = np.arange(n_cmp) * CMP_STRIDE
    s0 = np.arange(n_sel) * SEL_BLOCK
    ov = np.minimum(c0[None, :] + CMP_BLOCK, s0[:, None] + SEL_BLOCK) - np.maximum(c0[None, :], s0[:, None])
    m = np.zeros((-(-n_sel // SUBLANE) * SUBLANE, TILE), np.float32)
    m[:n_sel, :n_cmp] = np.clip(ov, 0, None).astype(np.float32) / CMP_BLOCK
    return jnp.asarray(m, BF16)


def _dt_expand():
    e = np.zeros((LANE, SSM_INNER), np.float32)
    for h in range(SSM_HEADS):
        e[DT_COL + h, h * SSM_HEADDIM:(h + 1) * SSM_HEADDIM] = 1.0
    return jnp.asarray(e, BF16)


def _pad_to(x, axis, size):
    pad = [(0, 0)] * x.ndim
    pad[axis] = (0, size - x.shape[axis])
    return jnp.pad(x, pad)


def _history_rows(state, n_rows, width):
    b, k, _ = state.shape
    tile = jnp.pad(state, ((0, 0), (n_rows - k, 0), (0, width - state.shape[2])))
    return tile.reshape(b * n_rows, width)


def kernel(x_prompt, x_sample, cache_kv, cache_win_kv, state_ssm_conv, state_ssm, state_ffn_conv, page_table, rel_table, ln_mix_pre, w_in, cmp_pe, cmp_w1, cmp_b1, cmp_w2, cmp_b2, ssm_conv_w, ssm_conv_b, ssm_dt_bias, ssm_a_log, ssm_d, ssm_norm_g, w_out, ln_mix_post, ln_ffn_pre, ffn_w_gate, ffn_w_up, ffn_dw_w, ffn_dw_b, ffn_w_down, ln_ffn_post):
    bp, tp, d = x_prompt.shape
    bs, ts, _ = x_sample.shape
    n_pool = cache_kv.shape[1]
    n_pages = page_table.shape[1]
    past_len = n_pages * PAGE_SIZE
    win_len = cache_win_kv.shape[2]
    assert d == D_MODEL and tp == N_CHUNK_TILES * TILE and past_len == N_CHUNK_TILES * TILE
    assert N_HEADS * ts == LANE and ts == SUBLANE and tp >= WINDOW and win_len == WINDOW

    rel = rel_table.astype(F32)
    tab_p = jnp.repeat(rel.reshape(REL_BUCKETS, KV_HEADS, Q_PER_KV).transpose(1, 0, 2), TILE, axis=-1)
    tab_s = jnp.repeat(rel, ts, axis=1)
    cmpb_p, near, edge = _bias_prompt(tab_p, tp // TILE)
    cmpbt_s, selb, winb = _bias_sample(tab_s, past_len, ts, win_len)
    impt_p = _importance_map_t(-(-tp // SEL_BLOCK))
    impt_s = _importance_map_t(-(-(past_len + ts) // SEL_BLOCK))
    rsum, gsel, expand, place = _sample_selectors(ts, past_len + TILE)
    e16 = _dt_expand()
    dummy_pt = jnp.zeros((1, 1), jnp.int32)

    xp = x_prompt.reshape(bp * tp, d)
    xs = x_sample.reshape(bs * ts, d)
    outs = [[] for _ in range(10)]
    for l in range(w_in.shape[0]):
        o1 = ATTN_WIDTH
        o2 = o1 + N_KV_SLOTS * KV_WIDTH
        o3 = o2 + 3 * N_HEADS
        o4 = o3 + SSM_INNER
        o5 = o4 + CONV_DIM
        w = w_in[l].T
        wp = jnp.concatenate([w[:o1], w[o3:o4], w[o4:o5], w[o1:o2], w[o2:o3], w[o5:],
                              jnp.zeros((P_WIDTH - w.shape[0], d), w.dtype)], axis=0).astype(BF16)
        g_pre = ln_mix_pre[l].reshape(1, d)
        cw = _compress_weights(cmp_pe[l], cmp_w1[l], cmp_b1[l], cmp_w2[l], cmp_b2[l])
        ssm_w = (_pad_to(ssm_conv_w[l], 0, SUBLANE), ssm_conv_b[l].reshape(1, CONV_DIM),
                 _pad_to(jnp.pad(ssm_dt_bias[l].astype(F32), (DT_COL, 0)), 0, LANE).reshape(1, LANE),
                 _pad_to(jnp.pad(ssm_a_log[l].astype(F32), (DT_COL, 0)), 0, LANE).reshape(1, LANE),
                 jnp.repeat(ssm_d[l].astype(F32), SSM_HEADDIM).reshape(1, SSM_INNER),
                 ssm_norm_g[l].astype(F32).reshape(1, SSM_INNER), e16)
        w_attn = w_out[l, :ATTN_WIDTH].astype(BF16)
        w_ssm = w_out[l, ATTN_WIDTH:].astype(BF16)
        g_post = ln_mix_post[l].reshape(1, d)
        g_pre2 = ln_ffn_pre[l].reshape(1, d)
        wg = _cast_pad_cols(ffn_w_gate[l], FF_PAD)
        wu = _cast_pad_cols(ffn_w_up[l], FF_PAD)
        dw = _pad_to(_pad_to(ffn_dw_w[l], 1, FF_PAD), 0, SUBLANE)
        db = _pad_to(ffn_dw_b[l].reshape(1, D_FF), 1, FF_PAD)
        wd = _cast_pad_rows(ffn_w_down[l], FF_PAD)
        g_post2 = ln_ffn_post[l].reshape(1, d)

        p3 = _proj(xp, g_pre, wp, tm=1024).reshape(bp, tp, P_WIDTH)
        kc, vc = _compress(p3, lambda j, cb, b, pt: (b, j, P_KV // LANE + cb), bp, dummy_pt, *cw)
        attn = _nsa_prompt(p3, kc, vc, cmpb_p, near, edge, tab_p, impt_p)
        ssm, h_new = _ssd(p3, None, None, *ssm_w, q_len=TILE)
        x1, hn = _outproj(attn.reshape(bp * tp, ATTN_WIDTH), ssm.reshape(bp * tp, SSM_INNER), xp,
                          w_attn, w_ssm, g_post, g_pre2)
        act, tail = _ffn_up(hn, wg, wu, dw, db, None, seq_len=tp, tm=tp)
        xp = _ffn_down(act, wd, x1, g_post2)
        kv = p3[:, :, P_KV:P_KV + N_KV_SLOTS * KV_WIDTH]
        outs[0].append(kv[:, :, :N_CACHE_SLOTS * KV_WIDTH].reshape(bp, tp, N_CACHE_SLOTS, KV_HEADS, HEAD_DIM))
        outs[2].append(kv[:, tp - WINDOW:, N_CACHE_SLOTS * KV_WIDTH:].reshape(bp, WINDOW, 2, KV_HEADS, HEAD_DIM))
        outs[4].append(p3[:, tp - (SSM_CONV - 1):, P_XBC:P_XBC + CONV_DIM])
        outs[6].append(h_new)
        outs[8].append(tail[:, SUBLANE - (FFN_CONV - 1):, :D_FF])

        p3 = _proj(xs, g_pre, wp, tm=bs * ts).reshape(bs, ts, P_WIDTH)
        cache_t = cache_kv[l].transpose(0, 2, 3, 4, 1).reshape(n_pool, N_CACHE_SLOTS * KV_WIDTH, PAGE_SIZE)
        win_t = cache_win_kv[l].transpose(0, 2, 3, 4, 1).reshape(bs, 2 * KV_WIDTH, win_len)
        w1p, perm = _compress_paged_weights(cmp_w1[l])
        attn, win_new = _nsa_sample(p3, cache_t, win_t, page_table,
                                    (cmpbt_s, selb, winb, impt_s, rsum, gsel, expand, place),
                                    (perm, w1p, _pe_bias(cw[0], w1p), cw[2], cw[3], cw[4]))
        win_new = win_new.reshape(bs, 2, KV_HEADS, HEAD_DIM, win_len).transpose(0, 4, 1, 2, 3)
        prev = _history_rows(state_ssm_conv[l], ts, CONV_DIM).reshape(1, bs * ts, CONV_DIM)
        ssm, h_new = _ssd(p3.reshape(1, bs * ts, P_WIDTH), prev, state_ssm[l], *ssm_w, q_len=ts)
        x1, hn = _outproj(attn.reshape(bs * ts, ATTN_WIDTH), ssm.reshape(bs * ts, SSM_INNER), xs,
                          w_attn, w_ssm, g_post, g_pre2)
        prev_f = _history_rows(state_ffn_conv[l], ts, FF_PAD)
        act, gp = _ffn_up(hn, wg, wu, dw, db, prev_f, seq_len=ts, tm=bs * ts)
        xs = _ffn_down(act, wd, x1, g_post2, tm=bs * ts)
        kv = p3[:, :, P_KV:P_KV + N_KV_SLOTS * KV_WIDTH]
        outs[1].append(kv[:, :, :N_CACHE_SLOTS * KV_WIDTH].reshape(bs, ts, N_CACHE_SLOTS, KV_HEADS, HEAD_DIM))
        outs[3].append(win_new)
        outs[5].append(p3[:, ts - (SSM_CONV - 1):, P_XBC:P_XBC + CONV_DIM])
        outs[7].append(h_new)
        outs[9].append(gp.reshape(bs, ts, FF_PAD)[:, ts - (FFN_CONV - 1):, :D_FF])
    return (xp.reshape(bp, tp, d), xs.reshape(bs, ts, d)) + tuple(jnp.stack(o) for o in outs)
```

```python
import functools
import math

import numpy as np
import jax
import jax.numpy as jnp
from jax import lax
from jax.experimental import pallas as pl
from jax.experimental.pallas import tpu as pltpu

F32 = jnp.float32
BF16 = jnp.bfloat16

D_MODEL = 2048
PAGE_SIZE = 128
HEAD_DIM = 64
ATTN_WIDTH = D_MODEL // 2
N_HEADS = ATTN_WIDTH // HEAD_DIM
KV_HEADS = 4
Q_PER_KV = N_HEADS // KV_HEADS
CMP_STRIDE = 16
CMP_BLOCK = 2 * CMP_STRIDE
CMP_HIDDEN = 2 * HEAD_DIM
SEL_BLOCK = 64
N_SELECT = 8
WINDOW = 512
N_KV_SLOTS = 6
N_CACHE_SLOTS = 4
KV_WIDTH = KV_HEADS * HEAD_DIM
SSM_INNER = D_MODEL - ATTN_WIDTH
SSM_HEADDIM = 64
SSM_HEADS = SSM_INNER // SSM_HEADDIM
SSM_GROUPS = 4
SSM_HPG = SSM_HEADS // SSM_GROUPS
SSM_STATE = 128
SSM_CONV = 4
CONV_DIM = SSM_INNER + 2 * SSM_GROUPS * SSM_STATE
D_FF = ((8 * D_MODEL // 3 + 127) // 128) * 128
FFN_CONV = 3
REL_BUCKETS = 32
REL_MAX_DIST = 128
EPS = 1e-6
NEG = -1e30
TINY = 1e-30
FORCE_BONUS = 1e4
SCALE = HEAD_DIM ** -0.5
LOG2E = math.log2(math.e)

LANE = 128
SUBLANE = 8
TILE = 128
VMEM_LIMIT = 56 * 1024 * 1024

P_Q = 0
P_Z = ATTN_WIDTH
P_XBC = P_Z + SSM_INNER
P_KV = P_XBC + CONV_DIM
P_GD = P_KV + N_KV_SLOTS * KV_WIDTH
GD_WIDTH = 3 * N_HEADS + SSM_HEADS
P_WIDTH = 6144
DT_COL = 3 * N_HEADS
FF_PAD = 5632
FFN_SUB = 256


def _cparams(*sem):
    return pltpu.CompilerParams(dimension_semantics=sem, vmem_limit_bytes=VMEM_LIMIT)


def _rms(x, g):
    return x * lax.rsqrt(jnp.mean(x * x, axis=-1, keepdims=True) + EPS) * g


def _split3(x):
    hi = x.astype(BF16)
    r1 = x - hi.astype(F32)
    mid = r1.astype(BF16)
    lo = (r1 - mid.astype(F32)).astype(BF16)
    return hi, mid, lo


def _dot(a, b):
    return jnp.dot(a, b, preferred_element_type=F32)


def _dot_nt(a, b):
    return lax.dot_general(a, b, (((1,), (1,)), ((), ())), preferred_element_type=F32)


def _dot_exact_rhs(x, sel):
    hi, mid, lo = _split3(x)
    return _dot(hi, sel) + _dot(mid, sel) + _dot(lo, sel)


def _dot_exact_lhs(sel, x):
    hi, mid, lo = _split3(x)
    return _dot(sel, hi) + _dot(sel, mid) + _dot(sel, lo)


def _gelu_tanh(x):
    return 0.5 * x * (1.0 + jnp.tanh(math.sqrt(2.0 / math.pi) * (x + 0.044715 * (x * x * x))))


def _silu(x):
    return x * jax.nn.sigmoid(x)


CAST_BLOCK = 256


def _cast_pad_cols_body(w_ref, o_ref):
    n = w_ref.shape[1]
    o_ref[:, 0:n] = w_ref[...].astype(BF16)
    o_ref[:, n:] = jnp.zeros((o_ref.shape[0], o_ref.shape[1] - n), BF16)


def _cast_pad_cols(w, n_out):
    m, n = w.shape
    return pl.pallas_call(
        _cast_pad_cols_body,
        grid=(m // CAST_BLOCK,),
        in_specs=[pl.BlockSpec((CAST_BLOCK, n), lambda i: (i, 0))],
        out_specs=pl.BlockSpec((CAST_BLOCK, n_out), lambda i: (i, 0)),
        out_shape=jax.ShapeDtypeStruct((m, n_out), BF16),
        compiler_params=_cparams("parallel"),
        name="cast_pad_cols",
    )(w)


def _cast_pad_rows_body(w_ref, o_ref):
    m = w_ref.shape[0]
    o_ref[0:m, :] = w_ref[...].astype(BF16)
    o_ref[m:, :] = jnp.zeros((o_ref.shape[0] - m, o_ref.shape[1]), BF16)


def _cast_pad_rows(w, m_out):
    m, n = w.shape
    return pl.pallas_call(
        _cast_pad_rows_body,
        grid=(n // CAST_BLOCK,),
        in_specs=[pl.BlockSpec((m, CAST_BLOCK), lambda j: (0, j))],
        out_specs=pl.BlockSpec((m_out, CAST_BLOCK), lambda j: (0, j)),
        out_shape=jax.ShapeDtypeStruct((m_out, n), BF16),
        compiler_params=_cparams("parallel"),
        name="cast_pad_rows",
    )(w)


def _proj_body(x_ref, g_ref, w_ref, o_ref, xn_ref):
    @pl.when(pl.program_id(1) == 0)
    def _():
        xn_ref[...] = _rms(x_ref[...], g_ref[...]).astype(BF16)

    o_ref[...] = _dot_nt(xn_ref[...], w_ref[...])


def _proj(x2d, g, w_t, tm, tn=1024):
    m, k = x2d.shape
    n = w_t.shape[0]
    w = w_t
    return pl.pallas_call(
        _proj_body,
        grid=(m // tm, n // tn),
        in_specs=[pl.BlockSpec((tm, k), lambda i, j: (i, 0)),
                  pl.BlockSpec((1, k), lambda i, j: (0, 0)),
                  pl.BlockSpec((tn, k), lambda i, j: (j, 0))],
        out_specs=pl.BlockSpec((tm, tn), lambda i, j: (i, j)),
        out_shape=jax.ShapeDtypeStruct((m, n), F32),
        scratch_shapes=[pltpu.VMEM((tm, k), BF16)],
        compiler_params=_cparams("parallel", "arbitrary"),
        name="proj",
    )(x2d, g, w)


NORM_ROWS = 128


def _outproj_body(a_ref, s_ref, x_ref, wa_ref, ws_ref, g1_ref, g2_ref, x1_ref, hn_ref):
    def project(rs):
        return _dot(a_ref[rs, :], wa_ref[...]) + _dot(s_ref[rs, :], ws_ref[...])

    def finish(rs, mix):
        x1 = x_ref[rs, :] + _rms(mix, g1_ref[...])
        x1_ref[rs, :] = x1
        hn_ref[rs, :] = _rms(x1, g2_ref[...]).astype(BF16)

    tm = x_ref.shape[0]
    rows = min(NORM_ROWS, tm)
    pending = None
    for r in range(tm // rows):
        rs = slice(r * rows, (r + 1) * rows)
        cur = (rs, project(rs))
        if pending is not None:
            finish(*pending)
        pending = cur
    finish(*pending)


def _outproj(attn, ssm, x2d, w_attn, w_ssm, g_post, g_pre2, tm=512):
    m, d = x2d.shape
    ka = attn.shape[1]
    const = lambda i: (0, 0)
    return pl.pallas_call(
        _outproj_body,
        grid=(m // tm,),
        in_specs=[pl.BlockSpec((tm, ka), lambda i: (i, 0)),
                  pl.BlockSpec((tm, ka), lambda i: (i, 0)),
                  pl.BlockSpec((tm, d), lambda i: (i, 0)),
                  pl.BlockSpec((ka, d), const),
                  pl.BlockSpec((ka, d), const),
                  pl.BlockSpec((1, d), const),
                  pl.BlockSpec((1, d), const)],
        out_specs=[pl.BlockSpec((tm, d), lambda i: (i, 0)),
                   pl.BlockSpec((tm, d), lambda i: (i, 0))],
        out_shape=[jax.ShapeDtypeStruct((m, d), F32), jax.ShapeDtypeStruct((m, d), BF16)],
        compiler_params=_cparams("parallel"),
        name="outproj",
    )(attn, ssm, x2d, w_attn, w_ssm, g_post, g_pre2)


def _shift_rows(x, prev, k, seq_len):
    rows, cols = x.shape
    if seq_len >= rows:
        t = lax.broadcasted_iota(jnp.int32, (rows, 1), 0)
        return jnp.where(t >= k, pltpu.roll(x, k, 0), 0.0 if prev is None else pltpu.roll(prev, k, 0))
    assert seq_len == SUBLANE and prev is not None
    shape = (rows // seq_len, seq_len, cols)
    t = lax.broadcasted_iota(jnp.int32, shape, 1)
    out = jnp.where(t >= k, pltpu.roll(x.reshape(shape), k, 1), pltpu.roll(prev.reshape(shape), k, 1))
    return out.reshape(rows, cols)


def _ffn_up_body(*refs, seq_len, has_prev):
    if has_prev:
        h_ref, wg_ref, wu_ref, dw_ref, db_ref, prev_ref, act_ref, tail_ref = refs
    else:
        h_ref, wg_ref, wu_ref, dw_ref, db_ref, act_ref, tail_ref = refs
    tm, tn = act_ref.shape
    h = h_ref[...]

    def project(cs):
        return _dot(h, wg_ref[:, cs]), _dot(h, wu_ref[:, cs])

    def gate(cs, gp, up):
        prev = prev_ref[:, cs] if has_prev else None
        acc = db_ref[:, cs] + dw_ref[FFN_CONV - 1:FFN_CONV, cs] * gp
        for k in range(1, FFN_CONV):
            acc = acc + dw_ref[FFN_CONV - 1 - k:FFN_CONV - k, cs] * _shift_rows(gp, prev, k, seq_len)
        act_ref[:, cs] = (_gelu_tanh(acc) * up).astype(BF16)
        tail_ref[:, cs] = gp if has_prev else gp[tm - SUBLANE:, :]

    pending = None
    for c in range(tn // FFN_SUB):
        cs = slice(c * FFN_SUB, (c + 1) * FFN_SUB)
        cur = (cs,) + project(cs)
        if pending is not None:
            gate(*pending)
        pending = cur
    gate(*pending)


def _ffn_up(hn, wg, wu, dw, db, prev, *, seq_len, tm, tn=512):
    m, d = hn.shape
    ni = m // tm
    nj = FF_PAD // tn
    has_prev = prev is not None
    in_specs = [pl.BlockSpec((tm, d), lambda i, j: (i, 0)),
                pl.BlockSpec((d, tn), lambda i, j: (0, j)),
                pl.BlockSpec((d, tn), lambda i, j: (0, j)),
                pl.BlockSpec((SUBLANE, tn), lambda i, j: (0, j)),
                pl.BlockSpec((1, tn), lambda i, j: (0, j))]
    args = [hn, wg, wu, dw, db]
    if has_prev:
        in_specs.append(pl.BlockSpec((tm, tn), lambda i, j: (i, j)))
        args.append(prev)
        tail_shape = jax.ShapeDtypeStruct((m, FF_PAD), F32)
        tail_spec = pl.BlockSpec((tm, tn), lambda i, j: (i, j))
    else:
        tail_shape = jax.ShapeDtypeStruct((ni, SUBLANE, FF_PAD), F32)
        tail_spec = pl.BlockSpec((None, SUBLANE, tn), lambda i, j: (i, 0, j))
    return pl.pallas_call(
        functools.partial(_ffn_up_body, seq_len=seq_len, has_prev=has_prev),
        grid=(ni, nj),
        in_specs=in_specs,
        out_specs=[pl.BlockSpec((tm, tn), lambda i, j: (i, j)), tail_spec],
        out_shape=[jax.ShapeDtypeStruct((m, FF_PAD), BF16), tail_shape],
        compiler_params=_cparams("parallel", "parallel"),
        name="ffn_up",
    )(*args)


def _ffn_down_body(a_ref, w_ref, x_ref, g_ref, o_ref, acc_ref):
    k = pl.program_id(1)
    last = pl.num_programs(1) - 1

    @pl.when(k == 0)
    def _():
        acc_ref[...] = _dot(a_ref[...], w_ref[...])

    @pl.when((k > 0) & (k < last))
    def _():
        acc_ref[...] += _dot(a_ref[...], w_ref[...])

    @pl.when(k == last)
    def _():
        tm = acc_ref.shape[0]
        rows = min(NORM_ROWS, tm)
        pending = None
        for r in range(tm // rows):
            rs = slice(r * rows, (r + 1) * rows)
            cur = (rs, acc_ref[rs, :] + _dot(a_ref[rs, :], w_ref[...]))
            if pending is not None:
                o_ref[pending[0], :] = x_ref[pending[0], :] + _rms(pending[1], g_ref[...])
            pending = cur
        o_ref[pending[0], :] = x_ref[pending[0], :] + _rms(pending[1], g_ref[...])


def _ffn_down(act, wd, x1, g, tm=1024, tk=512):
    m, d = x1.shape
    return pl.pallas_call(
        _ffn_down_body,
        grid=(m // tm, FF_PAD // tk),
        in_specs=[pl.BlockSpec((tm, tk), lambda i, k: (i, k)),
                  pl.BlockSpec((tk, d), lambda i, k: (k, 0)),
                  pl.BlockSpec((tm, d), lambda i, k: (i, 0)),
                  pl.BlockSpec((1, d), lambda i, k: (0, 0))],
        out_specs=pl.BlockSpec((tm, d), lambda i, k: (i, 0)),
        out_shape=jax.ShapeDtypeStruct((m, d), F32),
        scratch_shapes=[pltpu.VMEM((tm, d), F32)],
        compiler_params=_cparams("parallel", "arbitrary"),
        name="ffn_down",
    )(act, wd, x1, g)


N_CHUNK_TILES = 16
CHUNKS_PER_TILE = TILE // CMP_STRIDE
CMP_COL_BLOCKS = 2 * KV_WIDTH // LANE


def _compress_body(*refs):
    pt_ref = refs[0]
    del pt_ref
    n_in = N_CHUNK_TILES * CMP_COL_BLOCKS
    tiles = refs[1:1 + n_in]
    pe_ref, w1_ref, b1_ref, w2_ref, b2_ref, kc_ref, vc_ref, lhs_a, lhs_b = refs[1 + n_in:]
    out_refs = (kc_ref, vc_ref)
    for slot in range(2):
        outs = []
        for gp in range(KV_HEADS // 2):
            cb = slot * (KV_HEADS // 2) + gp
            for s in range(CMP_STRIDE):
                x = jnp.concatenate(
                    [tiles[j * CMP_COL_BLOCKS + cb][pl.ds(s, CHUNKS_PER_TILE, stride=CMP_STRIDE), :]
                     for j in range(N_CHUNK_TILES)], axis=0)
                lhs_a[:, s * LANE:(s + 1) * LANE] = (x + pe_ref[slot, 0, s:s + 1, :]).astype(BF16)
                lhs_b[:, s * LANE:(s + 1) * LANE] = (x + pe_ref[slot, 1, s:s + 1, :]).astype(BF16)
            first = _dot(lhs_a[...], w1_ref[slot, 0])
            second = _dot(lhs_b[...], w1_ref[slot, 1])
            hid = first + pltpu.roll(second, TILE - 1, 0) + b1_ref[slot]
            act = _gelu_tanh(hid).astype(BF16)
            for gj in range(2):
                outs.append(_dot(act[:, gj * CMP_HIDDEN:(gj + 1) * CMP_HIDDEN], w2_ref[slot]) + b2_ref[slot])
        out_refs[slot][...] = jnp.concatenate(outs, axis=1)


def _compress(src, tile_map, n_seq, page_table, pe, w1, b1, w2, b2):
    in_specs = [pl.BlockSpec((None, TILE, LANE), functools.partial(tile_map, j, cb))
                for j in range(N_CHUNK_TILES) for cb in range(CMP_COL_BLOCKS)]
    full = lambda shape: pl.BlockSpec(shape, lambda b, pt: (0,) * len(shape))
    in_specs += [full(pe.shape), full(w1.shape), full(b1.shape), full(w2.shape), full(b2.shape)]
    out_spec = pl.BlockSpec((None, TILE, KV_WIDTH), lambda b, pt: (b, 0, 0))
    return pl.pallas_call(
        _compress_body,
        grid_spec=pltpu.PrefetchScalarGridSpec(
            num_scalar_prefetch=1, grid=(n_seq,), in_specs=in_specs, out_specs=[out_spec, out_spec],
            scratch_shapes=[pltpu.VMEM((TILE, CMP_STRIDE * LANE), BF16), pltpu.VMEM((TILE, CMP_STRIDE * LANE), BF16)]),
        out_shape=[jax.ShapeDtypeStruct((n_seq, TILE, KV_WIDTH), F32)] * 2,
        compiler_params=_cparams("parallel"),
        name="compress",
    )(page_table, *([src] * (N_CHUNK_TILES * CMP_COL_BLOCKS)), pe, w1, b1, w2, b2)


def _compress_weights(cmp_pe, cmp_w1, cmp_b1, cmp_w2, cmp_b2):
    pe = cmp_pe.reshape(2, 2, CMP_STRIDE, HEAD_DIM)
    pe = jnp.concatenate([pe, pe], axis=-1)
    w = cmp_w1.reshape(2, 2, CMP_STRIDE, HEAD_DIM, CMP_HIDDEN)
    eye = jnp.eye(2, dtype=w.dtype)
    w1 = jnp.einsum('absdh,ij->absidjh', w, eye).reshape(2, 2, CMP_STRIDE * LANE, 2 * CMP_HIDDEN).astype(BF16)
    b1 = jnp.concatenate([cmp_b1, cmp_b1], axis=-1).reshape(2, 1, 2 * CMP_HIDDEN)
    return pe, w1, b1, cmp_w2.astype(BF16), cmp_b2.reshape(2, 1, HEAD_DIM)


PAGES_PER_PERM = 2


def _pe_bias_body(pe_ref, w1_ref, o_ref):
    for slot in range(2):
        cols = []
        for half in range(2):
            acc = jnp.zeros((2 * SUBLANE, 2 * CMP_HIDDEN), F32)
            for s in range(CMP_STRIDE):
                row = jnp.broadcast_to(pe_ref[slot, half, s:s + 1, :], (2 * SUBLANE, LANE))
                w = w1_ref[slot, s * LANE:(s + 1) * LANE, half * 2 * CMP_HIDDEN:(half + 1) * 2 * CMP_HIDDEN]
                acc = acc + _dot_exact_rhs(row, w)
            cols.append(acc)
        o_ref[slot] = jnp.concatenate(cols, axis=1)


def _pe_bias(pe, w1):
    return pl.pallas_call(
        _pe_bias_body,
        out_shape=jax.ShapeDtypeStruct((2, 2 * SUBLANE, 4 * CMP_HIDDEN), F32),
        name="pe_bias",
    )(pe, w1)


def _permute_pages(pages, perm_ref, xp_ref):
    for jp in range(N_CHUNK_TILES // PAGES_PER_PERM):
        xt = jnp.concatenate([pages[PAGES_PER_PERM * jp + jj][0:2 * KV_WIDTH, :] for jj in range(PAGES_PER_PERM)],
                             axis=1)
        xp_ref[jp] = _dot_nt(perm_ref[...], xt.astype(BF16)).astype(BF16)


def _compress_mlp(xp_ref, w1_ref, peb_ref, b1_ref, w2_ref, b2_ref, kc_ref, vc_ref, hid_ref):
    n_perm = N_CHUNK_TILES // PAGES_PER_PERM
    slab = PAGES_PER_PERM * CHUNKS_PER_TILE
    out_refs = (kc_ref, vc_ref)

    def first_layer(slot, gp):
        cb = slot * (KV_HEADS // 2) + gp
        lhs = jnp.concatenate(
            [jnp.concatenate([xp_ref[jp, s * slab:(s + 1) * slab, cb * LANE:(cb + 1) * LANE]
                              for jp in range(n_perm)], axis=0)
             for s in range(CMP_STRIDE)], axis=1)
        hid_ref[gp] = _dot(lhs, w1_ref[slot])
        return gp

    def second_layer(slot, gp, buf):
        acc = hid_ref[buf] + peb_ref[slot, 0:1, :]
        hid = (acc[:, 0:2 * CMP_HIDDEN] + pltpu.roll(acc[:, 2 * CMP_HIDDEN:], TILE - 1, 0) + b1_ref[slot])
        act = _gelu_tanh(hid).astype(BF16)
        for gj in range(2):
            g = 2 * gp + gj
            out_refs[slot][:, g * HEAD_DIM:(g + 1) * HEAD_DIM] = (
                _dot(act[:, gj * CMP_HIDDEN:(gj + 1) * CMP_HIDDEN], w2_ref[slot]) + b2_ref[slot])

    pending = None
    for slot in range(2):
        for gp in range(KV_HEADS // 2):
            acc = first_layer(slot, gp)
            if pending is not None:
                second_layer(*pending)
            pending = (slot, gp, acc)
    second_layer(*pending)


def _compress_paged_weights(cmp_w1):
    w = cmp_w1.reshape(2, 2, CMP_STRIDE, HEAD_DIM, CMP_HIDDEN)
    eye = jnp.eye(2, dtype=w.dtype)
    w1 = jnp.einsum('absdh,ij->asidbjh', w, eye).reshape(2, CMP_STRIDE * LANE, 4 * CMP_HIDDEN).astype(BF16)
    rows = PAGES_PER_PERM * PAGE_SIZE
    r = np.arange(rows)
    s, page, chunk = r // (PAGES_PER_PERM * CHUNKS_PER_TILE), (r // CHUNKS_PER_TILE) % PAGES_PER_PERM, r % CHUNKS_PER_TILE
    src = page * PAGE_SIZE + chunk * CMP_STRIDE + s
    perm = (np.arange(rows)[None, :] == src[:, None]).astype(np.float32)
    return w1, jnp.asarray(perm, BF16)


def _bucket_thresholds():
    n = np.arange(0, 2 * REL_MAX_DIST)
    max_exact = REL_BUCKETS // 2
    nf = np.maximum(n, 1).astype(np.float32)
    large = max_exact + (np.log(nf / np.float32(max_exact)) / np.float32(math.log(REL_MAX_DIST / max_exact))
                         * np.float32(REL_BUCKETS - max_exact)).astype(np.int32)
    bucket = np.where(n < max_exact, n, np.minimum(large, REL_BUCKETS - 1))
    assert (np.diff(bucket) >= 0).all() and (np.diff(bucket) <= 1).all()
    return [int(np.argmax(bucket >= k)) for k in range(1, REL_BUCKETS)]


BUCKET_START = _bucket_thresholds()
CMP_BAND = 2 * SUBLANE
assert (BUCKET_START[-1] - 1 + CMP_BLOCK - 1) // CMP_STRIDE <= SUBLANE
assert (TILE - CMP_BLOCK) // CMP_STRIDE < SUBLANE


def _rel_bias(dist, tab_ref):
    bias = jnp.broadcast_to(tab_ref[0:1, :], dist.shape)
    for k in range(1, REL_BUCKETS):
        bias = jnp.where(dist >= BUCKET_START[k - 1], tab_ref[k:k + 1, :], bias)
    return bias


def _bias_prompt_body(tab_ref, cmpb_ref, near_ref, edge_ref):
    qb = pl.program_id(0)
    shape = (TILE, Q_PER_KV * TILE)
    row = lax.broadcasted_iota(jnp.int32, shape, 0)
    qi = lax.broadcasted_iota(jnp.int32, shape, 1) % TILE
    dist_c = qb * TILE + qi - (row * CMP_STRIDE + CMP_BLOCK - 1)
    band0 = pl.multiple_of(jnp.maximum(qb * (TILE // CMP_STRIDE) - SUBLANE, 0), SUBLANE)
    band_shape = (CMP_BAND, Q_PER_KV * TILE)
    band_row = band0 + lax.broadcasted_iota(jnp.int32, band_shape, 0)
    band_qi = lax.broadcasted_iota(jnp.int32, band_shape, 1) % TILE
    dist_b = qb * TILE + band_qi - (band_row * CMP_STRIDE + CMP_BLOCK - 1)
    for g in range(KV_HEADS):
        cmpb_ref[g] = jnp.where(dist_c >= 0, tab_ref[g, REL_BUCKETS - 1:REL_BUCKETS, :], NEG)
        cmpb_ref[g, pl.ds(band0, CMP_BAND), :] = jnp.where(dist_b >= 0, _rel_bias(dist_b, tab_ref.at[g]), NEG)

    @pl.when(qb == 0)
    def _():
        for g in range(KV_HEADS):
            for delta in range(2):
                dist = delta * TILE + qi - row
                b = _rel_bias(dist, tab_ref.at[g])
                near_ref[g, delta] = jnp.where(dist >= 0, b * LOG2E, NEG)
        edge_ref[...] = jnp.where(qi <= row, 0.0, NEG)


def _bias_prompt(tab, n_qb):
    lanes = Q_PER_KV * TILE
    return pl.pallas_call(
        _bias_prompt_body,
        grid=(n_qb,),
        in_specs=[pl.BlockSpec(tab.shape, lambda i: (0, 0, 0))],
        out_specs=[pl.BlockSpec((None, KV_HEADS, TILE, lanes), lambda i: (i, 0, 0, 0)),
                   pl.BlockSpec((KV_HEADS, 2, TILE, lanes), lambda i: (0, 0, 0, 0)),
                   pl.BlockSpec((TILE, lanes), lambda i: (0, 0))],
        out_shape=[jax.ShapeDtypeStruct((n_qb, KV_HEADS, TILE, lanes), F32),
                   jax.ShapeDtypeStruct((KV_HEADS, 2, TILE, lanes), F32),
                   jax.ShapeDtypeStruct((TILE, lanes), F32)],
        compiler_params=_cparams("arbitrary"),
        name="bias_prompt",
    )(tab)


def _bias_sample_body(tab_ref, cmpbt_ref, selb_ref, winb_ref, *, past_len, n_new, win_len):
    qt = lax.broadcasted_iota(jnp.int32, (TILE, LANE), 1) % n_new
    row = lax.broadcasted_iota(jnp.int32, (TILE, LANE), 0)
    dist = past_len + qt - (row * CMP_STRIDE + CMP_BLOCK - 1)
    cmpbt_ref[...] = jnp.where(dist >= 0, _rel_bias(dist, tab_ref), NEG)
    for j in range(selb_ref.shape[1] // TILE):
        key = row + j * TILE
        dist = past_len + qt - key
        ok = (dist >= 0) & (key < past_len + n_new)
        selb_ref[:, j * TILE:(j + 1) * TILE] = jnp.where(ok, _rel_bias(dist, tab_ref), NEG).T
    for j in range(winb_ref.shape[1] // TILE):
        key = row + j * TILE
        dist = win_len + qt - key
        ok = (dist >= 0) & (dist <= WINDOW) & (key < win_len + n_new)
        winb_ref[:, j * TILE:(j + 1) * TILE] = jnp.where(ok, _rel_bias(dist, tab_ref), NEG).T


def _bias_sample(tab, past_len, n_new, win_len):
    return pl.pallas_call(
        functools.partial(_bias_sample_body, past_len=past_len, n_new=n_new, win_len=win_len),
        out_shape=[jax.ShapeDtypeStruct((TILE, LANE), F32),
                   jax.ShapeDtypeStruct((LANE, past_len + TILE), F32),
                   jax.ShapeDtypeStruct((LANE, win_len + TILE), F32)],
        name="bias_sample",
    )(tab)


def _select_blocks(imp, blk, q_blk):
    valid = blk <= q_blk
    forced = (blk == 0) | (blk == q_blk) | (blk == q_blk - 1)
    score = jnp.where(valid, imp + jnp.where(forced, FORCE_BONUS, 0.0), -jnp.inf)
    rank = jnp.zeros(score.shape, jnp.int32)
    for i in range(score.shape[0]):
        row = score[i:i + 1, :]
        beats = (row > score) | ((row == score) & (blk > i))
        rank = rank + beats.astype(jnp.int32)
    return jnp.where(valid & (rank < N_SELECT), 0.0, NEG)


FAR_RUN = 2


def _nsa_prompt_body(q_ref, gd_ref, kc_ref, vc_ref, sk_ref, sv_ref, wk_ref, wv_ref, cmpb_ref, near_ref,
                     edge_ref, tab_ref, impt_ref, o_ref, qs_ref, seladd_ref, selfar_ref, m_ref, l_ref, acc_ref,
                     out_ref):
    qb = pl.program_id(1)
    lanes = Q_PER_KV * TILE
    q = q_ref[...] * SCALE
    gate_t = jax.nn.sigmoid(gd_ref[...]).T
    n_sel = seladd_ref.shape[1]
    blk = lax.broadcasted_iota(jnp.int32, (n_sel, TILE), 0)
    q_blk = (qb * TILE + lax.broadcasted_iota(jnp.int32, (n_sel, TILE), 1)) // SEL_BLOCK
    groups = range(KV_HEADS)

    def hs(g):
        return slice(g * HEAD_DIM, (g + 1) * HEAD_DIM)

    def gate_row(branch, g):
        c0 = branch * N_HEADS + g * Q_PER_KV
        return jnp.concatenate([gate_t[c0 + r:c0 + r + 1, :] for r in range(Q_PER_KV)], axis=1)

    def far_bias(g):
        return tab_ref[g, REL_BUCKETS - 1:REL_BUCKETS, :] * LOG2E

    def stack_heads(x, g):
        return jnp.concatenate(
            [x[:, (g * Q_PER_KV + r) * HEAD_DIM:(g * Q_PER_KV + r + 1) * HEAD_DIM] for r in range(Q_PER_KV)],
            axis=0).astype(BF16)

    q2 = q * LOG2E
    kc = kc_ref[...].astype(BF16)
    vct = vc_ref[...].T.astype(BF16)
    scores = []
    for g in groups:
        qs_ref[g] = stack_heads(q2, g)
        scores.append(_dot_nt(kc[:, hs(g)], stack_heads(q, g)) + cmpb_ref[g])
    probs = []
    for g in groups:
        m = jnp.max(scores[g], axis=0, keepdims=True)
        e = jnp.exp(scores[g] - m)
        norm = jnp.where(m > 0.5 * NEG, 1.0 / jnp.maximum(jnp.sum(e, axis=0, keepdims=True), TINY), 0.0)
        probs.append(e * norm)
    imps = []
    for g in groups:
        p = probs[g]
        out_ref[g] = gate_row(0, g) * _dot(vct[hs(g), :], p.astype(BF16))
        psum = p[:, 0:TILE]
        for r in range(1, Q_PER_KV):
            psum = psum + p[:, r * TILE:(r + 1) * TILE]
        imps.append(_dot_exact_lhs(impt_ref[...], psum))
    for g in groups:
        sa = _select_blocks(imps[g], blk, q_blk)
        sa = jnp.concatenate([sa] * Q_PER_KV, axis=1)
        seladd_ref[g] = sa
        selfar_ref[g] = sa + far_bias(g)

    def sel_rows(ref, g, kt):
        half = SEL_BLOCK
        return jnp.concatenate(
            [jnp.broadcast_to(ref[g, pl.ds(2 * kt, 1), :], (half, lanes)),
             jnp.broadcast_to(ref[g, pl.ds(2 * kt + 1, 1), :], (half, lanes))], axis=0)

    def reset():
        m_ref[...] = jnp.full(m_ref.shape, -jnp.inf, F32)
        l_ref[...] = jnp.zeros(l_ref.shape, F32)
        acc_ref[...] = jnp.zeros(acc_ref.shape, F32)

    def tiles(k_ref, v_ref, parts):
        starts = [pl.multiple_of(kt * TILE, TILE) for kt, _ in parts]
        k = jnp.concatenate([k_ref[pl.ds(r0, TILE), :] for r0 in starts], axis=0).astype(BF16)
        vt = jnp.concatenate([v_ref[pl.ds(r0, TILE), :] for r0 in starts], axis=0).T.astype(BF16)
        scores = []
        for g in groups:
            s = _dot_nt(k[:, hs(g)], qs_ref[g])
            scores.append([s[i * TILE:(i + 1) * TILE] + bias_of(g) for i, (_, bias_of) in enumerate(parts)])
        probs, alphas = [], []
        for g in groups:
            m = m_ref[g]
            m_new = m
            for s in scores[g]:
                m_new = jnp.maximum(m_new, jnp.max(s, axis=0, keepdims=True))
            alpha = jnp.exp2(m - m_new)
            ps = [jnp.exp2(s - m_new) for s in scores[g]]
            total = jnp.sum(ps[0], axis=0, keepdims=True)
            for p in ps[1:]:
                total = total + jnp.sum(p, axis=0, keepdims=True)
            m_ref[g] = m_new
            l_ref[g] = alpha * l_ref[g] + total
            probs.append(jnp.concatenate([p.astype(BF16) for p in ps], axis=0))
            alphas.append(alpha)
        pvs = [_dot(vt[hs(g), :], probs[g]) for g in groups]
        for g in groups:
            acc_ref[g] = alphas[g] * acc_ref[g] + pvs[g]

    def finish(branch, g):
        return gate_row(branch, g) * (acc_ref[g] / jnp.maximum(l_ref[g], TINY))

    reset()

    n_far = jnp.maximum(qb - 1, 0)

    def far_body(i, carry):
        hi = n_far - 1 - FAR_RUN * i
        parts = []
        for j in range(FAR_RUN - 1, -1, -1):
            kt = jnp.maximum(hi - j, 0)
            off = jnp.where(hi - j >= 0, 0.0, NEG)
            parts.append((kt, lambda g, kt=kt, off=off: sel_rows(selfar_ref, g, kt) + off))
        tiles(sk_ref, sv_ref, parts)
        return carry

    lax.fori_loop(0, (n_far + FAR_RUN - 1) // FAR_RUN, far_body, 0)
    kt1 = jnp.maximum(qb - 1, 0)
    off1 = jnp.where(qb >= 1, 0.0, NEG)
    tiles(sk_ref, sv_ref, [(kt1, lambda g: near_ref[g, 1] + (sel_rows(seladd_ref, g, kt1) + off1)),
                           (qb, lambda g: near_ref[g, 0] + sel_rows(seladd_ref, g, qb))])
    for g in groups:
        out_ref[g] = out_ref[g] + finish(1, g)

    reset()
    parts = []
    for delta in range(WINDOW // TILE, -1, -1):
        kt = qb - delta
        off = jnp.where(kt >= 0, 0.0, NEG)
        if delta == WINDOW // TILE:
            bias_of = lambda g, off=off: edge_ref[...] + (far_bias(g) + off)
        elif delta >= 2:
            bias_of = lambda g, off=off: far_bias(g) + off
        else:
            bias_of = lambda g, off=off, delta=delta: near_ref[g, delta] + off
        parts.append((jnp.maximum(kt, 0), bias_of))
    tiles(wk_ref, wv_ref, parts[:3])
    tiles(wk_ref, wv_ref, parts[3:])
    outs = []
    for g in groups:
        out_t = out_ref[g] + finish(2, g)
        outs += [out_t[:, r * TILE:(r + 1) * TILE].T for r in range(Q_PER_KV)]
    o_ref[...] = jnp.concatenate(outs, axis=1).astype(BF16)


def _nsa_prompt(p3, kc, vc, cmpb, near, edge, tab, impt):
    b, t, _ = p3.shape
    n_qb = t // TILE
    lanes = Q_PER_KV * TILE
    kvcol = lambda slot: (P_KV + slot * KV_WIDTH) // KV_WIDTH
    seq_spec = lambda slot: pl.BlockSpec((None, t, KV_WIDTH), lambda i, j: (i, 0, kvcol(slot)))
    const = lambda shape: pl.BlockSpec(shape, lambda i, j: (0,) * len(shape))
    return pl.pallas_call(
        _nsa_prompt_body,
        grid=(b, n_qb),
        in_specs=[pl.BlockSpec((None, TILE, ATTN_WIDTH), lambda i, j: (i, j, 0)),
                  pl.BlockSpec((None, TILE, LANE), lambda i, j: (i, j, P_GD // LANE)),
                  pl.BlockSpec((None, TILE, KV_WIDTH), lambda i, j: (i, 0, 0)),
                  pl.BlockSpec((None, TILE, KV_WIDTH), lambda i, j: (i, 0, 0)),
                  seq_spec(2), seq_spec(3), seq_spec(4), seq_spec(5),
                  pl.BlockSpec((None, KV_HEADS, TILE, lanes), lambda i, j: (j, 0, 0, 0)),
                  const(near.shape), const(edge.shape), const(tab.shape), const(impt.shape)],
        out_specs=pl.BlockSpec((None, TILE, ATTN_WIDTH), lambda i, j: (i, j, 0)),
        out_shape=jax.ShapeDtypeStruct((b, t, ATTN_WIDTH), BF16),
        scratch_shapes=[pltpu.VMEM((KV_HEADS, lanes, HEAD_DIM), BF16),
                        pltpu.VMEM((KV_HEADS, impt.shape[0], lanes), F32),
                        pltpu.VMEM((KV_HEADS, impt.shape[0], lanes), F32),
                        pltpu.VMEM((KV_HEADS, 1, lanes), F32),
                        pltpu.VMEM((KV_HEADS, 1, lanes), F32),
                        pltpu.VMEM((KV_HEADS, HEAD_DIM, lanes), F32),
                        pltpu.VMEM((KV_HEADS, HEAD_DIM, lanes), F32)],
        compiler_params=_cparams("parallel", "arbitrary"),
        name="nsa_prompt",
    )(p3, p3, kc, vc, p3, p3, p3, p3, cmpb, near, edge, tab, impt)


def _shift_window(win_ref, new_rows, place_ref, winout_ref, n_new):
    n_wt = win_ref.shape[1] // TILE
    lane = lax.broadcasted_iota(jnp.int32, (1, LANE), 1)
    hi, mid, lo = _split3(new_rows)
    tail = _dot_tn(hi, place_ref[...]) + _dot_tn(mid, place_ref[...]) + _dot_tn(lo, place_ref[...])
    rolled = [pltpu.roll(win_ref[:, j * TILE:(j + 1) * TILE], TILE - n_new, 1) for j in range(n_wt)]
    for j in range(n_wt):
        nxt = rolled[j + 1] if j + 1 < n_wt else tail
        winout_ref[:, j * TILE:(j + 1) * TILE] = jnp.where(lane < TILE - n_new, rolled[j], nxt)


def _nsa_sample_body(*refs, past_len, n_new):
    n_pages = past_len // TILE
    pt_ref, q_ref, gd_ref, newkv_ref, newwin_ref = refs[:5]
    pages = refs[5:5 + n_pages]
    (win_ref, cmpbt_ref, selb_ref, winb_ref, impt_ref, rsum_ref, gsel_ref, expand_ref, place_ref,
     perm_ref, w1_ref, peb_ref, b1_ref, w2_ref, b2_ref,
     o_ref, winout_ref, s_scr, xp_ref, hid_ref, kc_ref, vc_ref) = refs[5 + n_pages:]
    del pt_ref
    win_len = win_ref.shape[1]
    sel_rows = 2 * KV_WIDTH
    _permute_pages(pages, perm_ref, xp_ref)
    _compress_mlp(xp_ref, w1_ref, peb_ref, b1_ref, w2_ref, b2_ref, kc_ref, vc_ref, hid_ref)
    rows_per_g = Q_PER_KV * n_new
    row_g = lax.broadcasted_iota(jnp.int32, (LANE, 1), 0) // rows_per_g

    q = q_ref[...] * SCALE
    qs = jnp.concatenate([q[:, h * HEAD_DIM:(h + 1) * HEAD_DIM] for h in range(N_HEADS)], axis=0)
    qs4 = jnp.concatenate([qs] * KV_HEADS, axis=1)
    col_g = lax.broadcasted_iota(jnp.int32, qs4.shape, 1) // HEAD_DIM
    qbd = jnp.where(row_g == col_g, qs4, 0.0).astype(BF16)

    sig = jax.nn.sigmoid(gd_ref[...])
    t_row = lax.broadcasted_iota(jnp.int32, (n_new, LANE), 0)
    t_lane = lax.broadcasted_iota(jnp.int32, (n_new, LANE), 1) % n_new
    lane_g = lax.broadcasted_iota(jnp.int32, (1, LANE), 1) // rows_per_g

    def gate_row(branch):
        spread = _dot_exact_rhs(sig, gsel_ref[branch])
        return jnp.sum(jnp.where(t_row == t_lane, spread, 0.0), axis=0, keepdims=True)

    def pick(full_t):
        out = jnp.zeros((HEAD_DIM, LANE), F32)
        for g in range(KV_HEADS):
            out = out + jnp.where(lane_g == g, full_t[g * HEAD_DIM:(g + 1) * HEAD_DIM, :], 0.0)
        return out

    def compressed():
        kc = kc_ref[...].astype(BF16)
        s_t = _dot_nt(kc, qbd) + cmpbt_ref[...]
        m_t = jnp.max(s_t, axis=0, keepdims=True)
        e_t = jnp.exp(s_t - m_t)
        p_t = e_t / jnp.maximum(jnp.sum(e_t, axis=0, keepdims=True), TINY)
        out_c = gate_row(0) * pick(_dot(vc_ref[...].T.astype(BF16), p_t.astype(BF16)))
        imp = _dot_exact_lhs(impt_ref[...], _dot_exact_rhs(p_t, rsum_ref[...]))
        blk = lax.broadcasted_iota(jnp.int32, imp.shape, 0)
        q_blk = (past_len + lax.broadcasted_iota(jnp.int32, imp.shape, 1) % n_new) // SEL_BLOCK
        chosen_t = jnp.where(_select_blocks(imp, blk, q_blk) == 0.0, 1.0, 0.0)
        chosen = jnp.concatenate([chosen_t, jnp.zeros((LANE - chosen_t.shape[0], LANE), F32)], axis=0).T
        return out_c, (_dot(chosen.astype(BF16), expand_ref[...]) - 1.0) * (-NEG)

    def attend(tiles, bias_ref, mask):
        m = jnp.full((LANE, 1), -jnp.inf, F32)
        for j, (score, _) in enumerate(tiles):
            s = score() + bias_ref[:, j * TILE:(j + 1) * TILE]
            if mask is not None:
                s = s + mask[:, j * TILE:(j + 1) * TILE]
            s_scr[:, j * TILE:(j + 1) * TILE] = s
            m = jnp.maximum(m, jnp.max(s, axis=1, keepdims=True))
        l = jnp.zeros((LANE, 1), F32)
        acc = jnp.zeros((LANE, KV_WIDTH), F32)
        for j, (_, vt) in enumerate(tiles):
            p = jnp.exp(s_scr[:, j * TILE:(j + 1) * TILE] - m)
            l = l + jnp.sum(p, axis=1, keepdims=True)
            acc = acc + _dot_nt(p.astype(BF16), vt())
        own = jnp.zeros((LANE, HEAD_DIM), F32)
        for g in range(KV_HEADS):
            own = own + jnp.where(row_g == g, acc[:, g * HEAD_DIM:(g + 1) * HEAD_DIM], 0.0)
        return (own / jnp.maximum(l, TINY)).T

    def stored_tile(ref, row0, lanes):
        return (lambda: _dot(qbd, ref[row0:row0 + KV_WIDTH, lanes].astype(BF16)),
                lambda: ref[row0 + KV_WIDTH:row0 + 2 * KV_WIDTH, lanes].astype(BF16))

    def new_tile(ref):
        rows = jnp.concatenate([ref[...], jnp.zeros((TILE - n_new, ref.shape[1]), F32)], axis=0)
        k = rows[:, 0:KV_WIDTH].astype(BF16)
        vt = rows[:, KV_WIDTH:2 * KV_WIDTH].T.astype(BF16)
        return rows, (lambda: _dot_nt(qbd, k), lambda: vt)

    out_t, sel_mask = compressed()

    _, new_kv = new_tile(newkv_ref)
    tiles = [stored_tile(pg, sel_rows, slice(None)) for pg in pages] + [new_kv]
    out_t = out_t + gate_row(1) * attend(tiles, selb_ref, sel_mask)

    new_rows, new_win = new_tile(newwin_ref)
    n_wt = win_len // TILE
    tiles = [stored_tile(win_ref, 0, slice(j * TILE, (j + 1) * TILE)) for j in range(n_wt)] + [new_win]
    out_t = out_t + gate_row(2) * attend(tiles, winb_ref, None)
    out = out_t.T
    o_ref[...] = jnp.concatenate([out[h * n_new:(h + 1) * n_new, :] for h in range(N_HEADS)], axis=1).astype(BF16)
    _shift_window(win_ref, new_rows, place_ref, winout_ref, n_new)


def _nsa_sample(p3, cache_t, win_t, page_table, consts, cmp_consts):
    b, n_new, _ = p3.shape
    n_pages = page_table.shape[1]
    past_len = n_pages * PAGE_SIZE
    win_len = win_t.shape[2]
    kvblk = lambda slot: (P_KV + slot * KV_WIDTH) // (2 * KV_WIDTH)
    const = lambda arr: pl.BlockSpec(arr.shape, lambda i, pt: (0,) * arr.ndim)
    in_specs = [pl.BlockSpec((None, n_new, ATTN_WIDTH), lambda i, pt: (i, 0, 0)),
                pl.BlockSpec((None, n_new, LANE), lambda i, pt: (i, 0, P_GD // LANE)),
                pl.BlockSpec((None, n_new, 2 * KV_WIDTH), lambda i, pt: (i, 0, kvblk(2))),
                pl.BlockSpec((None, n_new, 2 * KV_WIDTH), lambda i, pt: (i, 0, kvblk(4)))]
    in_specs += [pl.BlockSpec((None, N_CACHE_SLOTS * KV_WIDTH, PAGE_SIZE),
                              functools.partial(lambda j, i, pt: (pt[i, j], 0, 0), j)) for j in range(n_pages)]
    consts = list(consts) + list(cmp_consts)
    in_specs += [pl.BlockSpec((None, 2 * KV_WIDTH, win_len), lambda i, pt: (i, 0, 0))] + [const(a) for a in consts]
    perm_rows = PAGES_PER_PERM * PAGE_SIZE
    return pl.pallas_call(
        functools.partial(_nsa_sample_body, past_len=past_len, n_new=n_new),
        grid_spec=pltpu.PrefetchScalarGridSpec(
            num_scalar_prefetch=1, grid=(b,), in_specs=in_specs,
            out_specs=[pl.BlockSpec((None, n_new, ATTN_WIDTH), lambda i, pt: (i, 0, 0)),
                       pl.BlockSpec((None, 2 * KV_WIDTH, win_len), lambda i, pt: (i, 0, 0))],
            scratch_shapes=[pltpu.VMEM((LANE, past_len + TILE), F32),
                            pltpu.VMEM((n_pages // PAGES_PER_PERM, perm_rows, 2 * KV_WIDTH), BF16),
                            pltpu.VMEM((KV_HEADS // 2, TILE, 4 * CMP_HIDDEN), F32),
                            pltpu.VMEM((TILE, KV_WIDTH), F32),
                            pltpu.VMEM((TILE, KV_WIDTH), F32)]),
        out_shape=[jax.ShapeDtypeStruct((b, n_new, ATTN_WIDTH), BF16),
                   jax.ShapeDtypeStruct((b, 2 * KV_WIDTH, win_len), F32)],
        compiler_params=_cparams("parallel"),
        name="nsa_sample",
    )(page_table, p3, p3, p3, p3, *([cache_t] * n_pages), win_t, *consts)


def _sample_selectors(n_new, n_keys):
    lanes = np.arange(LANE)
    h, t = lanes // n_new, lanes % n_new
    rsum = ((h[:, None] // Q_PER_KV == h[None, :] // Q_PER_KV) & (t[:, None] == t[None, :])).astype(np.float32)
    gsel = np.stack([(lanes[:, None] == br * N_HEADS + h[None, :]).astype(np.float32) for br in range(3)])
    expand = (np.arange(n_keys)[None, :] // SEL_BLOCK == lanes[:, None]).astype(np.float32)
    place = (lanes[None, :] == LANE - n_new + lanes[:, None]).astype(np.float32)
    return tuple(jnp.asarray(a, BF16) for a in (rsum, gsel, expand, place))


def _dot_tn(a, b):
    return lax.dot_general(a, b, (((0,), (0,)), ((), ())), preferred_element_type=F32)


def _ssd_body(*refs, q_len, carry):
    if carry:
        (z_ref, xbc_ref, gd_ref, cw_ref, cb_ref, dtb_ref, alog_ref, dexp_ref, ng_ref, e16_ref,
         y_ref, hout_ref, prev_ref, h_scr) = refs
    else:
        (z_ref, xbc_ref, gd_ref, prev_ref, h0_ref, cw_ref, cb_ref, dtb_ref, alog_ref, dexp_ref, ng_ref, e16_ref,
         y_ref, hout_ref) = refs
    n_seq = TILE // q_len
    x = xbc_ref[...]
    if carry:
        @pl.when(pl.program_id(1) == 0)
        def _():
            prev_ref[...] = jnp.zeros_like(prev_ref)
            h_scr[...] = jnp.zeros_like(h_scr)
    prev = prev_ref[...]
    acc = cb_ref[...] + cw_ref[SSM_CONV - 1:SSM_CONV, :] * x
    for k in range(1, SSM_CONV):
        acc = acc + cw_ref[SSM_CONV - 1 - k:SSM_CONV - k, :] * _shift_rows(x, prev, k, q_len)
    if carry:
        prev_ref[...] = x
    xc = _silu(acc)
    xs = xc[:, 0:SSM_INNER]
    b_all = xc[:, SSM_INNER:SSM_INNER + SSM_GROUPS * SSM_STATE]
    c_all = xc[:, SSM_INNER + SSM_GROUPS * SSM_STATE:]

    col = lax.broadcasted_iota(jnp.int32, (1, LANE), 1)
    dt_cols = (col >= DT_COL) & (col < DT_COL + SSM_HEADS)
    dt = jnp.where(dt_cols, jax.nn.softplus(gd_ref[...] + dtb_ref[...]), 0.0)
    a = dt * jnp.where(dt_cols, -jnp.exp(alog_ref[...]), 0.0)
    ri = lax.broadcasted_iota(jnp.int32, (TILE, TILE), 0)
    ci = lax.broadcasted_iota(jnp.int32, (TILE, TILE), 1)
    same = (ri // q_len) == (ci // q_len)
    causal = same & (ci <= ri)
    acum = _dot_exact_lhs(causal.astype(BF16), a)
    last_sel = (ci == (ri // q_len) * q_len + (q_len - 1)).astype(BF16)
    alast = _dot_exact_lhs(last_sel, acum)
    acum_t = acum.T
    e16 = e16_ref[...]
    xw = xs * _dot_exact_rhs(dt, e16)
    xwe = xw * _dot_exact_rhs(jnp.exp(alast - acum), e16)
    grow = jnp.exp(_dot_exact_rhs(acum, e16))
    rowseq = lax.broadcasted_iota(jnp.int32, (TILE, 1), 0) // q_len

    cgs = [c_all[:, g * SSM_STATE:(g + 1) * SSM_STATE].astype(BF16) for g in range(SSM_GROUPS)]
    bgs = [b_all[:, g * SSM_STATE:(g + 1) * SSM_STATE].astype(BF16) for g in range(SSM_GROUPS)]
    cbs = [_dot_nt(cgs[g], bgs[g]) for g in range(SSM_GROUPS)]
    weights = []
    for h in range(SSM_HEADS):
        c = DT_COL + h
        seg = acum[:, c:c + 1] - acum_t[c:c + 1, :]
        decay = jnp.where(causal, jnp.exp(jnp.where(causal, seg, 0.0)), 0.0)
        weights.append((cbs[h // SSM_HPG] * decay).astype(BF16))
    ys = [_dot(weights[h], xw[:, h * SSM_HEADDIM:(h + 1) * SSM_HEADDIM].astype(BF16)) for h in range(SSM_HEADS)]
    for g in range(SSM_GROUPS):
        cg, bg = cgs[g], bgs[g]
        gl = slice(g * SSM_HPG * SSM_HEADDIM, (g + 1) * SSM_HPG * SSM_HEADDIM)
        hl = slice(g * SSM_HPG, (g + 1) * SSM_HPG)
        xwe_g = xwe[:, gl]
        if carry:
            h_old = h_scr[hl].reshape(SSM_HPG * SSM_HEADDIM, SSM_STATE)
            y_off = _dot_nt(cg, h_old.astype(BF16))
            scale = jnp.concatenate(
                [jnp.broadcast_to(jnp.exp(alast[0:1, DT_COL + g * SSM_HPG + r:DT_COL + g * SSM_HPG + r + 1]),
                                  (SSM_HEADDIM, SSM_STATE)) for r in range(SSM_HPG)], axis=0)
            h_new = h_old * scale + _dot_tn(xwe_g.astype(BF16), bg)
            h_scr[hl] = h_new.reshape(SSM_HPG, SSM_HEADDIM, SSM_STATE)
        else:
            offs = []
            for s in range(n_seq):
                h_old = h0_ref[s, hl].reshape(SSM_HPG * SSM_HEADDIM, SSM_STATE)
                c_rows = c_all[s * q_len:(s + 1) * q_len, g * SSM_STATE:(g + 1) * SSM_STATE].astype(BF16)
                offs.append(_dot_nt(c_rows, h_old.astype(BF16)))
                a_row = alast[s * q_len:s * q_len + 1, :]
                scale = jnp.concatenate(
                    [jnp.broadcast_to(jnp.exp(a_row[:, DT_COL + g * SSM_HPG + r:DT_COL + g * SSM_HPG + r + 1]),
                                      (SSM_HEADDIM, SSM_STATE)) for r in range(SSM_HPG)], axis=0)
                x_s = jnp.where(rowseq == s, xwe_g, 0.0).astype(BF16)
                h_new = h_old * scale + _dot_tn(x_s, bg)
                hout_ref[s, hl] = h_new.reshape(SSM_HPG, SSM_HEADDIM, SSM_STATE)
            y_off = jnp.concatenate(offs, axis=0)
        for r in range(SSM_HPG):
            h = g * SSM_HPG + r
            ys[h] = ys[h] + y_off[:, r * SSM_HEADDIM:(r + 1) * SSM_HEADDIM] * grow[:, h * SSM_HEADDIM:(h + 1) * SSM_HEADDIM]
    if carry:
        @pl.when(pl.program_id(1) == pl.num_programs(1) - 1)
        def _():
            hout_ref[...] = h_scr[...]

    y = jnp.concatenate(ys, axis=1) + xs * dexp_ref[...]
    y = y * _silu(z_ref[...])
    gw = SSM_INNER // SSM_GROUPS
    parts = []
    for g in range(SSM_GROUPS):
        blk = y[:, g * gw:(g + 1) * gw]
        parts.append(blk * lax.rsqrt(jnp.mean(blk * blk, axis=-1, keepdims=True) + EPS))
    y_ref[...] = (jnp.concatenate(parts, axis=1) * ng_ref[...]).astype(BF16)


def _ssd(p3, prev, h0, cw, cb, dtb, alog, dexp, ng, e16, *, q_len):
    n_grp, rows, _ = p3.shape
    carry = prev is None
    n_chunks = rows // TILE
    n_seq = TILE // q_len
    zspec = pl.BlockSpec((None, TILE, SSM_INNER), lambda i, c: (i, c, P_Z // SSM_INNER))
    xspec = pl.BlockSpec((None, TILE, CONV_DIM), lambda i, c: (i, c, P_XBC // CONV_DIM))
    gspec = pl.BlockSpec((None, TILE, LANE), lambda i, c: (i, c, P_GD // LANE))
    const = lambda arr: pl.BlockSpec(arr.shape, lambda i, c: (0,) * arr.ndim)
    params = [cw, cb, dtb, alog, dexp, ng, e16]
    in_specs = [zspec, xspec, gspec]
    args = [p3, p3, p3]
    if carry:
        n_out_seq = n_grp
        hspec = pl.BlockSpec((None, SSM_HEADS, SSM_HEADDIM, SSM_STATE), lambda i, c: (i, 0, 0, 0))
        scratch = [pltpu.VMEM((TILE, CONV_DIM), F32), pltpu.VMEM((SSM_HEADS, SSM_HEADDIM, SSM_STATE), F32)]
    else:
        n_out_seq = n_grp * n_chunks * n_seq
        hspec = pl.BlockSpec((n_seq, SSM_HEADS, SSM_HEADDIM, SSM_STATE), lambda i, c: (i * n_chunks + c, 0, 0, 0))
        in_specs += [pl.BlockSpec((None, TILE, CONV_DIM), lambda i, c: (i, c, 0)), hspec]
        args += [prev, h0]
        scratch = []
    in_specs += [const(a) for a in params]
    return pl.pallas_call(
        functools.partial(_ssd_body, q_len=q_len, carry=carry),
        grid=(n_grp, n_chunks),
        in_specs=in_specs,
        out_specs=[pl.BlockSpec((None, TILE, SSM_INNER), lambda i, c: (i, c, 0)), hspec],
        out_shape=[jax.ShapeDtypeStruct((n_grp, rows, SSM_INNER), BF16),
                   jax.ShapeDtypeStruct((n_out_seq, SSM_HEADS, SSM_HEADDIM, SSM_STATE), F32)],
        scratch_shapes=scratch,
        compiler_params=_cparams("parallel", "arbitrary"),
        name="ssd",
    )(*args, *params)


def _importance_map_t(n_sel):
    n_cmp = TILE - 1
    c0 = np.arange(n_cmp) * CMP_STRIDE
    s0 = np.arange(n_sel) * SEL_BLOCK
    ov = np.minimum(c0[None, :] + CMP_BLOCK, s0[:, None] + SEL_BLOCK) - np.maximum(c0[None, :], s0[:, None])
    m = np.zeros((-(-n_sel // SUBLANE) * SUBLANE, TILE), np.float32)
    m[:n_sel, :n_cmp] = np.clip(ov, 0, None).astype(np.float32) / CMP_BLOCK
    return jnp.asarray(m, BF16)


def _dt_expand():
    e = np.zeros((LANE, SSM_INNER), np.float32)
    for h in range(SSM_HEADS):
        e[DT_COL + h, h * SSM_HEADDIM:(h + 1) * SSM_HEADDIM] = 1.0
    return jnp.asarray(e, BF16)


def _pad_to(x, axis, size):
    pad = [(0, 0)] * x.ndim
    pad[axis] = (0, size - x.shape[axis])
    return jnp.pad(x, pad)


def _history_rows(state, n_rows, width):
    b, k, _ = state.shape
    tile = jnp.pad(state, ((0, 0), (n_rows - k, 0), (0, width - state.shape[2])))
    return tile.reshape(b * n_rows, width)


def kernel(x_prompt, x_sample, cache_kv, cache_win_kv, state_ssm_conv, state_ssm, state_ffn_conv, page_table, rel_table, ln_mix_pre, w_in, cmp_pe, cmp_w1, cmp_b1, cmp_w2, cmp_b2, ssm_conv_w, ssm_conv_b, ssm_dt_bias, ssm_a_log, ssm_d, ssm_norm_g, w_out, ln_mix_post, ln_ffn_pre, ffn_w_gate, ffn_w_up, ffn_dw_w, ffn_dw_b, ffn_w_down, ln_ffn_post):
    bp, tp, d = x_prompt.shape
    bs, ts, _ = x_sample.shape
    n_pool = cache_kv.shape[1]
    n_pages = page_table.shape[1]
    past_len = n_pages * PAGE_SIZE
    win_len = cache_win_kv.shape[2]
    assert d == D_MODEL and tp == N_CHUNK_TILES * TILE and past_len == N_CHUNK_TILES * TILE
    assert N_HEADS * ts == LANE and ts == SUBLANE and tp >= WINDOW and win_len == WINDOW

    rel = rel_table.astype(F32)
    tab_p = jnp.repeat(rel.reshape(REL_BUCKETS, KV_HEADS, Q_PER_KV).transpose(1, 0, 2), TILE, axis=-1)
    tab_s = jnp.repeat(rel, ts, axis=1)
    cmpb_p, near, edge = _bias_prompt(tab_p, tp // TILE)
    cmpbt_s, selb, winb = _bias_sample(tab_s, past_len, ts, win_len)
    impt_p = _importance_map_t(-(-tp // SEL_BLOCK))
    impt_s = _importance_map_t(-(-(past_len + ts) // SEL_BLOCK))
    rsum, gsel, expand, place = _sample_selectors(ts, past_len + TILE)
    e16 = _dt_expand()
    dummy_pt = jnp.zeros((1, 1), jnp.int32)

    xp = x_prompt.reshape(bp * tp, d)
    xs = x_sample.reshape(bs * ts, d)
    outs = [[] for _ in range(10)]
    for l in range(w_in.shape[0]):
        o1 = ATTN_WIDTH
        o2 = o1 + N_KV_SLOTS * KV_WIDTH
        o3 = o2 + 3 * N_HEADS
        o4 = o3 + SSM_INNER
        o5 = o4 + CONV_DIM
        w = w_in[l].T
        wp = jnp.concatenate([w[:o1], w[o3:o4], w[o4:o5], w[o1:o2], w[o2:o3], w[o5:],
                              jnp.zeros((P_WIDTH - w.shape[0], d), w.dtype)], axis=0).astype(BF16)
        g_pre = ln_mix_pre[l].reshape(1, d)
        cw = _compress_weights(cmp_pe[l], cmp_w1[l], cmp_b1[l], cmp_w2[l], cmp_b2[l])
        ssm_w = (_pad_to(ssm_conv_w[l], 0, SUBLANE), ssm_conv_b[l].reshape(1, CONV_DIM),
                 _pad_to(jnp.pad(ssm_dt_bias[l].astype(F32), (DT_COL, 0)), 0, LANE).reshape(1, LANE),
                 _pad_to(jnp.pad(ssm_a_log[l].astype(F32), (DT_COL, 0)), 0, LANE).reshape(1, LANE),
                 jnp.repeat(ssm_d[l].astype(F32), SSM_HEADDIM).reshape(1, SSM_INNER),
                 ssm_norm_g[l].astype(F32).reshape(1, SSM_INNER), e16)
        w_attn = w_out[l, :ATTN_WIDTH].astype(BF16)
        w_ssm = w_out[l, ATTN_WIDTH:].astype(BF16)
        g_post = ln_mix_post[l].reshape(1, d)
        g_pre2 = ln_ffn_pre[l].reshape(1, d)
        wg = _cast_pad_cols(ffn_w_gate[l], FF_PAD)
        wu = _cast_pad_cols(ffn_w_up[l], FF_PAD)
        dw = _pad_to(_pad_to(ffn_dw_w[l], 1, FF_PAD), 0, SUBLANE)
        db = _pad_to(ffn_dw_b[l].reshape(1, D_FF), 1, FF_PAD)
        wd = _cast_pad_rows(ffn_w_down[l], FF_PAD)
        g_post2 = ln_ffn_post[l].reshape(1, d)

        p3 = _proj(xp, g_pre, wp, tm=1024).reshape(bp, tp, P_WIDTH)
        kc, vc = _compress(p3, lambda j, cb, b, pt: (b, j, P_KV // LANE + cb), bp, dummy_pt, *cw)
        attn = _nsa_prompt(p3, kc, vc, cmpb_p, near, edge, tab_p, impt_p)
        ssm, h_new = _ssd(p3, None, None, *ssm_w, q_len=TILE)
        x1, hn = _outproj(attn.reshape(bp * tp, ATTN_WIDTH), ssm.reshape(bp * tp, SSM_INNER), xp,
                          w_attn, w_ssm, g_post, g_pre2)
        act, tail = _ffn_up(hn, wg, wu, dw, db, None, seq_len=tp, tm=tp)
        xp = _ffn_down(act, wd, x1, g_post2)
        kv = p3[:, :, P_KV:P_KV + N_KV_SLOTS * KV_WIDTH]
        outs[0].append(kv[:, :, :N_CACHE_SLOTS * KV_WIDTH].reshape(bp, tp, N_CACHE_SLOTS, KV_HEADS, HEAD_DIM))
        outs[2].append(kv[:, tp - WINDOW:, N_CACHE_SLOTS * KV_WIDTH:].reshape(bp, WINDOW, 2, KV_HEADS, HEAD_DIM))
        outs[4].append(p3[:, tp - (SSM_CONV - 1):, P_XBC:P_XBC + CONV_DIM])
        outs[6].append(h_new)
        outs[8].append(tail[:, SUBLANE - (FFN_CONV - 1):, :D_FF])

        p3 = _proj(xs, g_pre, wp, tm=bs * ts).reshape(bs, ts, P_WIDTH)
        cache_t = cache_kv[l].transpose(0, 2, 3, 4, 1).reshape(n_pool, N_CACHE_SLOTS * KV_WIDTH, PAGE_SIZE)
        win_t = cache_win_kv[l].transpose(0, 2, 3, 4, 1).reshape(bs, 2 * KV_WIDTH, win_len)
        w1p, perm = _compress_paged_weights(cmp_w1[l])
        attn, win_new = _nsa_sample(p3, cache_t, win_t, page_table,
                                    (cmpbt_s, selb, winb, impt_s, rsum, gsel, expand, place),
                                    (perm, w1p, _pe_bias(cw[0], w1p), cw[2], cw[3], cw[4]))
        win_new = win_new.reshape(bs, 2, KV_HEADS, HEAD_DIM, win_len).transpose(0, 4, 1, 2, 3)
        prev = _history_rows(state_ssm_conv[l], ts, CONV_DIM).reshape(1, bs * ts, CONV_DIM)
        ssm, h_new = _ssd(p3.reshape(1, bs * ts, P_WIDTH), prev, state_ssm[l], *ssm_w, q_len=ts)
        x1, hn = _outproj(attn.reshape(bs * ts, ATTN_WIDTH), ssm.reshape(bs * ts, SSM_INNER), xs,
                          w_attn, w_ssm, g_post, g_pre2)
        prev_f = _history_rows(state_ffn_conv[l], ts, FF_PAD)
        act, gp = _ffn_up(hn, wg, wu, dw, db, prev_f, seq_len=ts, tm=bs * ts)
        xs = _ffn_down(act, wd, x1, g_post2, tm=bs * ts)
        kv = p3[:, :, P_KV:P_KV + N_KV_SLOTS * KV_WIDTH]
        outs[1].append(kv[:, :, :N_CACHE_SLOTS * KV_WIDTH].reshape(bs, ts, N_CACHE_SLOTS, KV_HEADS, HEAD_DIM))
        outs[3].append(win_new)
        outs[5].append(p3[:, ts - (SSM_CONV - 1):, P_XBC:P_XBC + CONV_DIM])
        outs[7].append(h_new)
        outs[9].append(gp.reshape(bs, ts, FF_PAD)[:, ts - (FFN_CONV - 1):, :D_FF])
    return (xp.reshape(bp, tp, d), xs.reshape(bs, ts, d)) + tuple(jnp.stack(o) for o in outs)
```
